```python
import math
import jax, jax.numpy as jnp
from jax import lax
import numpy as np

D_MODEL = 2048
BATCH = 4
SEQ = 2048
DEPTH = 1
DEC_BATCH = 128
DEC_SEQ = 4
PAST_LEN = 16384
PAGE_SIZE = 128

N_META = 16
CHUNK = 64
CONV_W = 4
EPS = 1e-6
GDN_DK = 128
GDN_DV = 128
GDN_HEADS = (D_MODEL // 2) // GDN_DV
GDN_QK = GDN_HEADS * GDN_DK
GDN_V = GDN_HEADS * GDN_DV
GDN_CONV_CH = 2 * GDN_QK + GDN_V
HGRN_DK = 128
HGRN_DV = 128
HGRN_HEADS = (D_MODEL // 2) // HGRN_DV
HGRN_K = HGRN_HEADS * HGRN_DK
HGRN_V = HGRN_HEADS * HGRN_DV
MIX_WIDTH = GDN_V + HGRN_V
IN_COLS = GDN_CONV_CH + GDN_V + 2 * GDN_HEADS + 2 * HGRN_K + 2 * HGRN_V
N_GROUPS = 4
EXPERTS_PER_GROUP = 8
N_EXPERTS = N_GROUPS * EXPERTS_PER_GROUP
TOP_K = 2
D_EXPERT = D_MODEL // 4

kernel_name = 'hymba_gdn_hgrn2_hmoe_step'


def rmsnorm(x, w):
    xf = x.astype(jnp.float32)
    y = xf * lax.rsqrt(jnp.mean(xf * xf, axis=-1, keepdims=True) + EPS)
    return (y * w.astype(jnp.float32)).astype(x.dtype)


def gated_rmsnorm(o, w, gate):
    o = o * lax.rsqrt(jnp.mean(o * o, axis=-1, keepdims=True) + EPS)
    return o * w.astype(jnp.float32) * jax.nn.silu(gate.astype(jnp.float32))


def l2norm(x):
    return x * lax.rsqrt(jnp.sum(x * x, axis=-1, keepdims=True) + EPS)


def split_cols(proj):
    sizes = (GDN_CONV_CH, GDN_V, GDN_HEADS, GDN_HEADS, HGRN_K, HGRN_K, HGRN_V, HGRN_V)
    idx, acc = [], 0
    for s in sizes[:-1]:
        acc += s
        idx.append(acc)
    return jnp.split(proj, idx, axis=-1)


def causal_conv(u, buf, w):
    L = u.shape[1]
    full = jnp.concatenate([buf.astype(u.dtype), u], axis=1)
    out = sum(full[:, j:j + L] * w[j] for j in range(CONV_W))
    return jax.nn.silu(out), full[:, L:]


def to_chunks(x, c):
    b, L, h, d = x.shape
    return x.reshape(b, L // c, c, h, d).transpose(1, 0, 3, 2, 4)


def from_chunks(x):
    n, b, h, c, d = x.shape
    return x.transpose(1, 0, 3, 2, 4).reshape(b, n * c, h, d)


def gdn_segment(q, k, v, g, beta, S):
    c = math.gcd(q.shape[1], CHUNK)
    incl = jnp.tril(jnp.ones((c, c), bool))
    strict = jnp.tril(jnp.ones((c, c), bool), -1)
    eye = jnp.eye(c, dtype=jnp.float32)

    def step(S, inp):
        qc, kc, vc, gc, bc = inp
        G = jnp.cumsum(gc, axis=-1)
        decay = jnp.exp(jnp.where(incl, G[..., :, None] - G[..., None, :], -jnp.inf))
        kb = kc * bc[..., None]
        m = jnp.where(strict, jnp.einsum('bhik,bhjk->bhij', kb, kc) * decay, 0.0)
        rhs = jnp.concatenate([vc * bc[..., None], kb * jnp.exp(G)[..., None]], axis=-1)
        uw = lax.linalg.triangular_solve(eye + m, rhs, left_side=True, lower=True)
        u, w = uw[..., :GDN_DV], uw[..., GDN_DV:]
        v_new = u - jnp.einsum('bhik,bhkv->bhiv', w, S)
        attn = jnp.einsum('bhik,bhjk->bhij', qc, kc) * decay
        o = (jnp.einsum('bhik,bhkv->bhiv', qc * jnp.exp(G)[..., None], S)
             + jnp.einsum('bhij,bhjv->bhiv', attn, v_new))
        gl = G[..., -1:]
        S = S * jnp.exp(gl)[..., None] + jnp.einsum('bhjk,bhjv->bhkv', kc * jnp.exp(gl - G)[..., None], v_new)
        return S, o

    xs = (to_chunks(q, c), to_chunks(k, c), to_chunks(v, c),
          to_chunks(g[..., None], c)[..., 0], to_chunks(beta[..., None], c)[..., 0])
    S, o = lax.scan(step, S, xs)
    return S, from_chunks(o)


def hgrn_segment(q, lf, k, v, S):
    c = math.gcd(q.shape[1], CHUNK)
    incl = jnp.tril(jnp.ones((c, c), bool))

    def step(S, inp):
        qc, lfc, kc, vc = inp
        Bc = jnp.cumsum(lfc, axis=2)
        decay = jnp.exp(jnp.where(incl[..., None], Bc[:, :, :, None, :] - Bc[:, :, None, :, :], -jnp.inf))
        attn = jnp.einsum('bhid,bhjd,bhijd->bhij', qc, kc, decay)
        o = (jnp.einsum('bhid,bhde->bhie', qc * jnp.exp(Bc), S)
             + jnp.einsum('bhij,bhje->bhie', attn, vc))
        bl = Bc[:, :, -1:, :]
        S = S * jnp.exp(bl)[:, :, 0, :, None] + jnp.einsum('bhjd,bhje->bhde', kc * jnp.exp(bl - Bc), vc)
        return S, o

    xs = (to_chunks(q, c), to_chunks(lf, c), to_chunks(k, c), to_chunks(v, c))
    S, o = lax.scan(step, S, xs)
    return S, from_chunks(o)


def token_mixer(xn, conv_buf, s_gdn, s_hgrn, seg_lens, lb, w_in, conv_w, a_log, dt_bias,
                gdn_norm_w, hgrn_norm_w, w_out):
    f32 = jnp.float32
    bsz, L, _ = xn.shape
    qkv, z, b_lin, a_lin, q_h, f_lin, i_h, g_h = split_cols(xn @ w_in)
    qkv, conv_new = causal_conv(qkv, conv_buf, conv_w)
    q_a, k_a, v_a = jnp.split(qkv.astype(f32), [GDN_QK, 2 * GDN_QK], axis=-1)
    q_a = l2norm(q_a.reshape(bsz, L, GDN_HEADS, GDN_DK)) * (GDN_DK ** -0.5)
    k_a = l2norm(k_a.reshape(bsz, L, GDN_HEADS, GDN_DK))
    v_a = v_a.reshape(bsz, L, GDN_HEADS, GDN_DV)
    beta = jax.nn.sigmoid(b_lin.astype(f32))
    g = -jnp.exp(a_log.astype(f32)) * jax.nn.softplus(a_lin.astype(f32) + dt_bias.astype(f32))
    q_b = (jax.nn.silu(q_h.astype(f32)) * (HGRN_DK ** -0.5)).reshape(bsz, L, HGRN_HEADS, HGRN_DK)
    f = lb + (1.0 - lb) * jax.nn.sigmoid(f_lin.astype(f32))
    lf_b = jnp.log(f).reshape(bsz, L, HGRN_HEADS, HGRN_DK)
    k_b = (1.0 - f).reshape(bsz, L, HGRN_HEADS, HGRN_DK)
    v_b = i_h.astype(f32).reshape(bsz, L, HGRN_HEADS, HGRN_DV)
    s_gdn = s_gdn.astype(f32)
    s_hgrn = s_hgrn.astype(f32)
    o_a, o_b, start = [], [], 0
    for n in seg_lens:
        sl = slice(start, start + n)
        s_gdn, oa = gdn_segment(q_a[:, sl], k_a[:, sl], v_a[:, sl], g[:, sl], beta[:, sl], s_gdn)
        s_hgrn, ob = hgrn_segment(q_b[:, sl], lf_b[:, sl], k_b[:, sl], v_b[:, sl], s_hgrn)
        o_a.append(oa)
        o_b.append(ob)
        start += n
    o_a = gated_rmsnorm(jnp.concatenate(o_a, axis=1), gdn_norm_w,
                        z.reshape(bsz, L, GDN_HEADS, GDN_DV))
    o_b = gated_rmsnorm(jnp.concatenate(o_b, axis=1), hgrn_norm_w,
                        g_h.reshape(bsz, L, HGRN_HEADS, HGRN_DV))
    o = jnp.concatenate([o_a.reshape(bsz, L, GDN_V), o_b.reshape(bsz, L, HGRN_V)], axis=-1)
    return o.astype(xn.dtype) @ w_out, conv_new, s_gdn, s_hgrn


def hier_moe(xn, w_rg, b_rg, w_re, b_re, w_gate, w_up, w_down):
    f32 = jnp.float32
    shp = xn.shape
    h = xn.reshape(-1, D_MODEL)
    T = h.shape[0]
    lg = (h @ w_rg + b_rg).astype(f32)
    pg = jax.nn.softmax(lg, axis=-1)
    gsel = jnp.argmax(lg, axis=-1)
    p_top = jnp.take_along_axis(pg, gsel[:, None], axis=-1)
    le = (h @ w_re + b_re).astype(f32).reshape(T, N_GROUPS, EXPERTS_PER_GROUP)
    idx = jnp.broadcast_to(gsel[:, None, None], (T, 1, EXPERTS_PER_GROUP))
    le = jnp.take_along_axis(le, idx, axis=1)[:, 0]
    top_l, top_i = lax.top_k(le, TOP_K)
    wts = jax.nn.softmax(top_l, axis=-1) * p_top
    eid = gsel[:, None] * EXPERTS_PER_GROUP + top_i
    combine = jnp.sum(jax.nn.one_hot(eid, N_EXPERTS, dtype=f32) * wts[..., None], axis=1)
    gate = jnp.einsum('td,edf->tef', h, w_gate)
    up = jnp.einsum('td,edf->tef', h, w_up)
    act = jax.nn.silu(gate) * up * combine[..., None].astype(h.dtype)
    y = jnp.einsum('tef,efd->td', act, w_down)
    return y.reshape(shp)


def setup_inputs(seed: int = 0) -> dict:
    key = jax.random.key(seed)
    ks = jax.random.split(key, 32)
    f32 = jnp.float32

    def nrm(k, shape, s):
        return jax.random.normal(k, shape, f32) * s

    dt = jnp.exp(jax.random.uniform(ks[9], (DEPTH, GDN_HEADS), f32, math.log(1e-3), math.log(1e-1)))
    return {
        'x_prompt': nrm(ks[0], (BATCH, SEQ, D_MODEL), 1.0),
        'x_sample': nrm(ks[1], (DEC_BATCH, DEC_SEQ, D_MODEL), 1.0),
        'state_gdn': nrm(ks[2], (DEPTH, DEC_BATCH, GDN_HEADS, GDN_DK, GDN_DV), 0.1),
        'state_conv': nrm(ks[3], (DEPTH, DEC_BATCH, CONV_W - 1, GDN_CONV_CH), 1.0),
        'state_hgrn': nrm(ks[4], (DEPTH, DEC_BATCH, HGRN_HEADS, HGRN_DK, HGRN_DV), 0.5),
        'meta_tokens': nrm(ks[5], (N_META, D_MODEL), 1.0),
        'norm1_w': 1.0 + nrm(ks[6], (DEPTH, D_MODEL), 0.02),
        'w_in': nrm(ks[7], (DEPTH, D_MODEL, IN_COLS), D_MODEL ** -0.5),
        'conv_w': nrm(ks[8], (DEPTH, CONV_W, GDN_CONV_CH), CONV_W ** -0.5),
        'a_log': jnp.log(jax.random.uniform(ks[10], (DEPTH, GDN_HEADS), f32, 1.0, 16.0)),
        'dt_bias': dt + jnp.log(-jnp.expm1(-dt)),
        'gdn_norm_w': 1.0 + nrm(ks[11], (DEPTH, GDN_DV), 0.02),
        'lb_logits': nrm(ks[12], (DEPTH + 1, HGRN_K), 0.5),
        'hgrn_norm_w': 1.0 + nrm(ks[13], (DEPTH, HGRN_DV), 0.02),
        'w_out': nrm(ks[14], (DEPTH, MIX_WIDTH, D_MODEL), MIX_WIDTH ** -0.5),
        'norm2_w': 1.0 + nrm(ks[15], (DEPTH, D_MODEL), 0.02),
        'w_router_group': nrm(ks[16], (DEPTH, D_MODEL, N_GROUPS), D_MODEL ** -0.5),
        'b_router_group': nrm(ks[17], (DEPTH, N_GROUPS), 0.01),
        'w_router_expert': nrm(ks[18], (DEPTH, D_MODEL, N_EXPERTS), D_MODEL ** -0.5),
        'b_router_expert': nrm(ks[19], (DEPTH, N_EXPERTS), 0.01),
        'w_gate': nrm(ks[20], (DEPTH, N_EXPERTS, D_MODEL, D_EXPERT), D_MODEL ** -0.5),
        'w_up': nrm(ks[21], (DEPTH, N_EXPERTS, D_MODEL, D_EXPERT), D_MODEL ** -0.5),
        'w_down': nrm(ks[22], (DEPTH, N_EXPERTS, D_EXPERT, D_MODEL), D_EXPERT ** -0.5),
        'final_norm_w': 1.0 + nrm(ks[23], (D_MODEL,), 0.02),
    }


def reference(x_prompt, x_sample, state_gdn, state_conv, state_hgrn, meta_tokens, norm1_w, w_in,
              conv_w, a_log, dt_bias, gdn_norm_w, lb_logits, hgrn_norm_w, w_out, norm2_w,
              w_router_group, b_router_group, w_router_expert, b_router_expert, w_gate, w_up,
              w_down, final_norm_w):
    f32 = jnp.float32
    bp, seq, _ = x_prompt.shape
    lb_all = jnp.cumsum(jax.nn.softmax(lb_logits.astype(f32), axis=0), axis=0)
    hp = jnp.concatenate([jnp.broadcast_to(meta_tokens[None].astype(x_prompt.dtype),
                                           (bp, N_META, D_MODEL)), x_prompt], axis=1)
    hs = x_sample
    gdn_p, conv_p, hgrn_p, gdn_s, conv_s, hgrn_s = [], [], [], [], [], []
    for l in range(DEPTH):
        lw = (lb_all[l], w_in[l], conv_w[l], a_log[l], dt_bias[l], gdn_norm_w[l], hgrn_norm_w[l], w_out[l])
        mw = (w_router_group[l], b_router_group[l], w_router_expert[l], b_router_expert[l],
              w_gate[l], w_up[l], w_down[l])
        mix, c_new, sg, sh = token_mixer(
            rmsnorm(hp, norm1_w[l]), jnp.zeros((bp, CONV_W - 1, GDN_CONV_CH), hp.dtype),
            jnp.zeros((bp, GDN_HEADS, GDN_DK, GDN_DV), f32),
            jnp.zeros((bp, HGRN_HEADS, HGRN_DK, HGRN_DV), f32), (N_META, seq), *lw)
        hp = hp + mix
        hp = hp + hier_moe(rmsnorm(hp, norm2_w[l]), *mw)
        gdn_p.append(sg.astype(state_gdn.dtype))
        conv_p.append(c_new.astype(state_conv.dtype))
        hgrn_p.append(sh.astype(state_hgrn.dtype))
        mix, c_new, sg, sh = token_mixer(
            rmsnorm(hs, norm1_w[l]), state_conv[l], state_gdn[l], state_hgrn[l],
            (hs.shape[1],), *lw)
        hs = hs + mix
        hs = hs + hier_moe(rmsnorm(hs, norm2_w[l]), *mw)
        gdn_s.append(sg.astype(state_gdn.dtype))
        conv_s.append(c_new.astype(state_conv.dtype))
        hgrn_s.append(sh.astype(state_hgrn.dtype))
    y_prompt = rmsnorm(hp[:, N_META:], final_norm_w)
    y_sample = rmsnorm(hs, final_norm_w)
    return (y_prompt, y_sample, jnp.stack(gdn_p), jnp.stack(conv_p), jnp.stack(hgrn_p),
            jnp.stack(gdn_s), jnp.stack(conv_s), jnp.stack(hgrn_s))
```

```python
import functools
import math

import jax
import jax.numpy as jnp
from jax import lax
from jax.experimental import pallas as pl
from jax.experimental.pallas import tpu as pltpu

F32 = jnp.float32
BF16 = jnp.bfloat16
HIGHEST = lax.Precision.HIGHEST

EPS = 1e-6
N_META = 16
CONV_W = 4
N_HEADS = 8
D_HEAD = 128
N_GROUPS = 4
EXPERTS_PER_GROUP = 8
N_EXPERTS = N_GROUPS * EXPERTS_PER_GROUP

LANES = 128
SUBLANES = 8
CHUNK = 64
SUB = 16
VMEM_LIMIT = 56 * 1024 * 1024
NEG_BIG = -1e30


def _sigmoid(x):
    return 1.0 / (1.0 + jnp.exp(-x))


def _silu(x):
    return x * _sigmoid(x)


def _softplus(x):
    return jnp.maximum(x, 0.0) + jnp.log1p(jnp.exp(-jnp.abs(x)))


def _dot(a, b, precision=None):
    return jnp.dot(a, b, preferred_element_type=F32, precision=precision)


def _dot_nt(a, b, precision=None):
    return lax.dot_general(a, b, (((1,), (1,)), ((), ())), preferred_element_type=F32,
                           precision=precision)


def _dot_tn(a, b, precision=None):
    return lax.dot_general(a, b, (((0,), (0,)), ((), ())), preferred_element_type=F32,
                           precision=precision)


def _params(sem):
    return pltpu.CompilerParams(dimension_semantics=sem, vmem_limit_bytes=VMEM_LIMIT)


def _rmsnorm_kernel(x_ref, w_ref, o_ref):
    x = x_ref[...]
    ms = jnp.mean(x * x, axis=-1, keepdims=True)
    o_ref[...] = (x * lax.rsqrt(ms + EPS) * w_ref[...]).astype(o_ref.dtype)


def _rmsnorm(x, w, out_dtype, tm):
    t, d = x.shape
    return pl.pallas_call(
        _rmsnorm_kernel,
        grid=(t // tm,),
        in_specs=[pl.BlockSpec((tm, d), lambda i: (i, 0)), pl.BlockSpec((1, d), lambda i: (0, 0))],
        out_specs=pl.BlockSpec((tm, d), lambda i: (i, 0)),
        out_shape=jax.ShapeDtypeStruct((t, d), out_dtype),
        compiler_params=_params(("parallel",)),
    )(x, w.reshape(1, d))


def _matmul_kernel(x_ref, w_ref, o_ref):
    o_ref[...] = _dot(x_ref[...], w_ref[...])


def _matmul(x, w, tm, tn):
    t, k = x.shape
    n = w.shape[1]
    return pl.pallas_call(
        _matmul_kernel,
        grid=(n // tn, t // tm),
        in_specs=[pl.BlockSpec((tm, k), lambda j, i: (i, 0)), pl.BlockSpec((k, tn), lambda j, i: (0, j))],
        out_specs=pl.BlockSpec((tm, tn), lambda j, i: (i, j)),
        out_shape=jax.ShapeDtypeStruct((t, n), F32),
        compiler_params=_params(("parallel", "arbitrary")),
    )(x, w)


def _chunk_masks(nb, sl):
    r = nb * sl
    shift = int(math.log2(sl))
    ri = lax.broadcasted_iota(jnp.int32, (r, r), 0)
    ci = lax.broadcasted_iota(jnp.int32, (r, r), 1)
    same = lax.shift_right_logical(ri, shift) == lax.shift_right_logical(ci, shift)
    return same & (ci <= ri), same & (ci < ri)


def _row_valid(nb, sl, n_valid):
    rowid = lax.broadcasted_iota(jnp.int32, (nb * sl, 1), 0)
    return (rowid & (sl - 1)) < n_valid


def _last_row_bcast(x, nb, sl):
    c = x.shape[-1]
    x3 = x.reshape(nb, sl, c)
    return jnp.broadcast_to(x3[:, sl - 1:sl, :], (nb, sl, c)).reshape(nb * sl, c)


def _gated_rmsnorm(o, w, gate):
    return o * lax.rsqrt(jnp.mean(o * o, axis=-1, keepdims=True) + EPS) * w * _silu(gate)


def _gdn_kernel(*refs, nb, sl, n_valid, has_init, precision):
    if has_init:
        (qkv_ref, z_ref, ba_ref, cw_ref, pv_ref, nw_ref, s0_ref, c0_ref,
         o_ref, sout_ref, s_scr, carry_scr) = refs
    else:
        (qkv_ref, z_ref, ba_ref, cw_ref, pv_ref, nw_ref,
         o_ref, sout_ref, s_scr, carry_scr) = refs
    c = pl.program_id(1)
    r = nb * sl
    dh = D_HEAD

    @pl.when(c == 0)
    def _():
        if has_init:
            s_scr[...] = s0_ref[...]
            carry_scr[...] = c0_ref[...]
        else:
            s_scr[...] = jnp.zeros_like(s_scr)
            carry_scr[...] = jnp.zeros_like(carry_scr)

    incl, strict = _chunk_masks(nb, sl)
    lmask = incl.astype(F32)
    eye = (lax.broadcasted_iota(jnp.int32, (r, r), 0) == lax.broadcasted_iota(jnp.int32, (r, r), 1)).astype(F32)
    valid = _row_valid(nb, sl, n_valid)
    masked = n_valid < sl
    rowid = lax.broadcasted_iota(jnp.int32, (r, 1), 0)
    row_seq = lax.shift_right_logical(rowid, int(math.log2(sl)))
    n_sq = int(math.log2(sl)) - 1

    ba = ba_ref[...].reshape(r, LANES)
    pv = pv_ref[...]
    beta_all = _sigmoid(ba)
    g_all = -jnp.exp(pv[0:1]) * _softplus(ba + pv[1:2])
    if masked:
        g_all = jnp.where(valid, g_all, 0.0)
    gcum = _dot(lmask, g_all, HIGHEST)
    gcum_t = gcum.T
    glast = _last_row_bcast(gcum, nb, sl)
    cw = cw_ref[...]
    nw = nw_ref[...]

    def conv_slice(c0):
        u = qkv_ref[:, :, c0:c0 + dh]
        prev = carry_scr[:, :, c0:c0 + dh]
        full = jnp.concatenate([prev, u], axis=1)
        acc = None
        for j in range(CONV_W):
            off = SUBLANES - (CONV_W - 1) + j
            term = full[:, off:off + sl, :] * cw[j:j + 1, c0:c0 + dh]
            acc = term if acc is None else acc + term
        return _silu(acc).reshape(r, dh)

    for h in range(N_HEADS):
        q = conv_slice(h * dh)
        k = conv_slice(N_HEADS * dh + h * dh)
        v = conv_slice(2 * N_HEADS * dh + h * dh)
        q = q * lax.rsqrt(jnp.sum(q * q, axis=-1, keepdims=True) + EPS) * (dh ** -0.5)
        k = k * lax.rsqrt(jnp.sum(k * k, axis=-1, keepdims=True) + EPS)
        if masked:
            q = jnp.where(valid, q, 0.0)
            k = jnp.where(valid, k, 0.0)
            v = jnp.where(valid, v, 0.0)
        bc = beta_all[:, h:h + 1]
        gc = gcum[:, N_HEADS + h:N_HEADS + h + 1]
        gr = gcum_t[N_HEADS + h:N_HEADS + h + 1, :]
        gl = glast[:, N_HEADS + h:N_HEADS + h + 1]
        eg = jnp.exp(gc)
        decay = jnp.exp(jnp.where(incl, gc - gr, NEG_BIG))
        qk_kk = _dot_nt(jnp.concatenate([q, k], axis=0), k, precision)
        qk = qk_kk[:r]
        kk = qk_kk[r:]
        a = jnp.where(strict, -(bc * kk * decay), 0.0)
        tinv = eye + a
        pw = a
        for _ in range(n_sq):
            pw = _dot(pw, pw, precision)
            tinv = tinv + _dot(tinv, pw, precision)
        kb = k * bc
        uw = _dot(tinv, jnp.concatenate([v * bc, kb * eg], axis=1), precision)
        u = uw[:, :dh]
        w = uw[:, dh:]
        qe = q * eg
        vnew_parts, ointer_parts = [], []
        for b in range(nb):
            rows = slice(b * sl, (b + 1) * sl)
            ws = _dot(jnp.concatenate([w[rows], qe[rows]], axis=0), s_scr[b, h], precision)
            vnew_parts.append(u[rows] - ws[:sl])
            ointer_parts.append(ws[sl:])
        vnew = vnew_parts[0] if nb == 1 else jnp.concatenate(vnew_parts, axis=0)
        ointer = ointer_parts[0] if nb == 1 else jnp.concatenate(ointer_parts, axis=0)
        attn = jnp.where(incl, qk * decay, 0.0)
        o = ointer + _dot(attn, vnew, precision)
        ktil = k * jnp.exp(gl - gc)
        for b in range(nb):
            kt_b = ktil if nb == 1 else jnp.where(row_seq == b, ktil, 0.0)
            gl_b = gl[b * sl:b * sl + 1, :]
            s_scr[b, h] = s_scr[b, h] * jnp.exp(gl_b) + _dot_tn(kt_b, vnew, precision)
        z = z_ref[:, :, h * dh:(h + 1) * dh].reshape(r, dh)
        o_ref[:, :, h * dh:(h + 1) * dh] = _gated_rmsnorm(o, nw, z).reshape(nb, sl, dh)

    carry_scr[...] = qkv_ref[:, sl - SUBLANES:sl, :]

    @pl.when(c == pl.num_programs(1) - 1)
    def _():
        sout_ref[...] = s_scr[...]


def _hgrn_kernel(*refs, nb, sl, n_valid, has_init, precision):
    if has_init:
        (q_ref, f_ref, i_ref, g_ref, lb_ref, nw_ref, s0_ref, o_ref, sout_ref, s_scr) = refs
    else:
        (q_ref, f_ref, i_ref, g_ref, lb_ref, nw_ref, o_ref, sout_ref, s_scr) = refs
    c = pl.program_id(1)
    r = nb * sl
    dh = D_HEAD
    width = N_HEADS * dh

    @pl.when(c == 0)
    def _():
        if has_init:
            s_scr[...] = s0_ref[...]
        else:
            s_scr[...] = jnp.zeros_like(s_scr)

    incl, _ = _chunk_masks(nb, sl)
    lmask = incl.astype(F32)
    valid = _row_valid(nb, sl, n_valid)
    masked = n_valid < sl
    rowid = lax.broadcasted_iota(jnp.int32, (r, 1), 0)
    row_seq = lax.shift_right_logical(rowid, int(math.log2(sl)))
    sub = min(SUB, sl)
    nblk = r // sub
    sub_shift = int(math.log2(sub))
    ri = lax.broadcasted_iota(jnp.int32, (r, r), 0)
    ci = lax.broadcasted_iota(jnp.int32, (r, r), 1)
    same_blk = lax.shift_right_logical(ri, sub_shift) == lax.shift_right_logical(ci, sub_shift)
    diag_mask = incl & same_blk
    cross_mask = incl & jnp.logical_not(same_blk)

    lb = lb_ref[...]
    f = lb + (1.0 - lb) * _sigmoid(f_ref[...].reshape(r, width))
    lf = jnp.log(f)
    k_all = 1.0 - f
    if masked:
        lf = jnp.where(valid, lf, 0.0)
        k_all = jnp.where(valid, k_all, 0.0)
    bcum = _dot(lmask, lf, HIGHEST)
    nw = nw_ref[...]

    for h in range(N_HEADS):
        cols = slice(h * dh, (h + 1) * dh)
        bh = bcum[:, cols]
        q = _silu(q_ref[:, :, cols].reshape(r, dh)) * (dh ** -0.5)
        k = k_all[:, cols]
        v = i_ref[:, :, cols].reshape(r, dh)
        if masked:
            v = jnp.where(valid, v, 0.0)
        bmid = jnp.broadcast_to(bh.reshape(nblk, sub, dh)[:, sub // 2:sub // 2 + 1, :],
                                (nblk, sub, dh)).reshape(r, dh)
        attn = jnp.where(diag_mask, _dot_nt(q * jnp.exp(bh - bmid), k * jnp.exp(bmid - bh), precision), 0.0)
        if sl > sub:
            parts = [jnp.zeros((sub, r), F32)]
            for blk in range(1, nblk):
                start = blk * sub
                bref = bh[start - 1:start, :]
                qc = q[start:start + sub] * jnp.exp(bh[start:start + sub] - bref)
                kc = k * jnp.exp(jnp.minimum(bref - bh, 0.0))
                parts.append(_dot_nt(qc, kc, precision))
            attn = attn + jnp.where(cross_mask, jnp.concatenate(parts, axis=0), 0.0)
        qe = q * jnp.exp(bh)
        ointer_parts = []
        for b in range(nb):
            rows = slice(b * sl, (b + 1) * sl)
            ointer_parts.append(_dot(qe[rows], s_scr[b, h], precision))
        ointer = ointer_parts[0] if nb == 1 else jnp.concatenate(ointer_parts, axis=0)
        o = ointer + _dot(attn, v, precision)
        blast = _last_row_bcast(bh, nb, sl)
        ktil = k * jnp.exp(blast - bh)
        tr = jnp.concatenate([ktil, blast] + ([jnp.zeros((LANES - 2 * r, dh), F32)] if 2 * r < LANES else []),
                             axis=0).T
        for b in range(nb):
            kt_b = ktil if nb == 1 else jnp.where(row_seq == b, ktil, 0.0)
            dec_col = jnp.exp(tr[:, r + b * sl:r + b * sl + 1])
            s_scr[b, h] = s_scr[b, h] * dec_col + _dot_tn(kt_b, v, precision)
        gate = g_ref[:, :, cols].reshape(r, dh)
        o_ref[:, :, cols] = _gated_rmsnorm(o, nw, gate).reshape(nb, sl, dh)

    @pl.when(c == pl.num_programs(1) - 1)
    def _():
        sout_ref[...] = s_scr[...]


def _mixers(proj3, nb, sl, n_valid, conv_w, pvec, gdn_nw, lb, hgrn_nw, s_gdn0, conv0, s_hgrn0, precision):
    nseq, length, _ = proj3.shape
    has_init = s_gdn0 is not None
    width = N_HEADS * D_HEAD
    conv_ch = 3 * width
    grid = (nseq // nb, length // sl)
    state_spec = pl.BlockSpec((nb, N_HEADS, D_HEAD, D_HEAD), lambda g, c: (g, 0, 0, 0))
    state_shape = jax.ShapeDtypeStruct((nseq, N_HEADS, D_HEAD, D_HEAD), F32)

    def col_spec(w, idx):
        return pl.BlockSpec((nb, sl, w), lambda g, c: (g, c, idx))

    def const_spec(shape):
        return pl.BlockSpec(shape, lambda g, c: (0,) * len(shape))

    gdn_in = [proj3, proj3, proj3, conv_w, pvec, gdn_nw]
    gdn_specs = [col_spec(conv_ch, 0), col_spec(width, 3), col_spec(LANES, 8 * width // LANES),
                 const_spec(conv_w.shape), const_spec(pvec.shape), const_spec(gdn_nw.shape)]
    if has_init:
        gdn_in += [s_gdn0, conv0]
        gdn_specs += [state_spec, pl.BlockSpec((nb, SUBLANES, conv_ch), lambda g, c: (g, 0, 0))]
    o_gdn, s_gdn = pl.pallas_call(
        functools.partial(_gdn_kernel, nb=nb, sl=sl, n_valid=n_valid, has_init=has_init, precision=precision),
        grid=grid,
        in_specs=gdn_specs,
        out_specs=[col_spec(width, 0), state_spec],
        out_shape=[jax.ShapeDtypeStruct((nseq, length, width), F32), state_shape],
        scratch_shapes=[pltpu.VMEM((nb, N_HEADS, D_HEAD, D_HEAD), F32), pltpu.VMEM((nb, SUBLANES, conv_ch), F32)],
        compiler_params=_params(("parallel", "arbitrary")),
    )(*gdn_in)

    hgrn_in = [proj3, proj3, proj3, proj3, lb, hgrn_nw]
    hgrn_specs = [col_spec(width, 4), col_spec(width, 5), col_spec(width, 6), col_spec(width, 7),
                  const_spec(lb.shape), const_spec(hgrn_nw.shape)]
    if has_init:
        hgrn_in += [s_hgrn0]
        hgrn_specs += [state_spec]
    o_hgrn, s_hgrn = pl.pallas_call(
        functools.partial(_hgrn_kernel, nb=nb, sl=sl, n_valid=n_valid, has_init=has_init, precision=precision),
        grid=grid,
        in_specs=hgrn_specs,
        out_specs=[col_spec(width, 0), state_spec],
        out_shape=[jax.ShapeDtypeStruct((nseq, length, width), F32), state_shape],
        scratch_shapes=[pltpu.VMEM((nb, N_HEADS, D_HEAD, D_HEAD), F32)],
        compiler_params=_params(("parallel", "arbitrary")),
    )(*hgrn_in)
    return o_gdn, o_hgrn, s_gdn, s_hgrn


def _outproj_router_kernel(oa_ref, ob_ref, hp_ref, wo_ref, n2_ref, wr_ref, br_ref,
                           hp2_ref, xn2_ref, ri_ref, rw_ref, cnt_ref, carry_scr, *, tm):
    i = pl.program_id(0)

    @pl.when(i == 0)
    def _():
        carry_scr[...] = jnp.zeros_like(carry_scr)

    half = oa_ref.shape[-1]
    mix = _dot(oa_ref[...].astype(BF16), wo_ref[:half, :]) + _dot(ob_ref[...].astype(BF16), wo_ref[half:, :])
    hp2 = hp_ref[...] + mix
    hp2_ref[...] = hp2
    xn2 = hp2 * lax.rsqrt(jnp.mean(hp2 * hp2, axis=-1, keepdims=True) + EPS) * n2_ref[...]
    xn2_ref[...] = xn2
    logits = _dot(xn2, wr_ref[...], HIGHEST) + br_ref[...]

    lane = lax.broadcasted_iota(jnp.int32, (tm, LANES), 1)
    lane_f = lane.astype(F32)
    far = float(4 * LANES)
    is_g = (lane >= N_EXPERTS) & (lane < N_EXPERTS + N_GROUPS)
    lg = jnp.where(is_g, logits, -jnp.inf)
    gmax = jnp.max(lg, axis=-1, keepdims=True)
    gsel = jnp.min(jnp.where(lg == gmax, lane_f, far), axis=-1, keepdims=True).astype(jnp.int32) - N_EXPERTS
    p_top = 1.0 / jnp.sum(jnp.where(is_g, jnp.exp(logits - gmax), 0.0), axis=-1, keepdims=True)
    in_grp = (lane < N_EXPERTS) & (lax.shift_right_logical(lane, 3) == gsel)
    le = jnp.where(in_grp, logits, -jnp.inf)
    m1 = jnp.max(le, axis=-1, keepdims=True)
    i1 = jnp.min(jnp.where(le == m1, lane_f, far), axis=-1, keepdims=True).astype(jnp.int32)
    le2 = jnp.where(lane == i1, -jnp.inf, le)
    m2 = jnp.max(le2, axis=-1, keepdims=True)
    i2 = jnp.min(jnp.where(le2 == m2, lane_f, far), axis=-1, keepdims=True).astype(jnp.int32)
    e2 = jnp.exp(m2 - m1)
    w1 = p_top / (1.0 + e2)
    w2 = p_top * e2 / (1.0 + e2)

    onehot = (lane == i1) | (lane == i2)
    onehot_f = jnp.where(onehot, 1.0, 0.0)
    tri = (lax.broadcasted_iota(jnp.int32, (tm, tm), 1) < lax.broadcasted_iota(jnp.int32, (tm, tm), 0))
    before = _dot(jnp.where(tri, 1.0, 0.0).astype(BF16), onehot_f.astype(BF16)) + carry_scr[...]
    r1 = jnp.sum(jnp.where(lane == i1, before, 0.0), axis=-1, keepdims=True).astype(jnp.int32)
    r2 = jnp.sum(jnp.where(lane == i2, before, 0.0), axis=-1, keepdims=True).astype(jnp.int32)
    carry_scr[...] = carry_scr[...] + jnp.sum(onehot_f, axis=0, keepdims=True)

    ri_ref[...] = jnp.where(lane == 0, i1, jnp.where(lane == 1, i2, jnp.where(lane == 2, r1, jnp.where(lane == 3, r2, 0))))
    rw_ref[...] = jnp.where(lane == 0, w1, jnp.where(lane == 1, w2, 0.0))

    @pl.when(i == pl.num_programs(0) - 1)
    def _():
        cnt_ref[...] = carry_scr[...]


def _outproj_router(o_a, o_b, hp, w_out, norm2_w, w_r, b_r, tm):
    t, d = hp.shape
    half = o_a.shape[1]
    row = lambda w: pl.BlockSpec((tm, w), lambda i: (i, 0))
    const = lambda shape: pl.BlockSpec(shape, lambda i: (0,) * len(shape))
    return pl.pallas_call(
        functools.partial(_outproj_router_kernel, tm=tm),
        grid=(t // tm,),
        in_specs=[row(half), row(half), row(d), const(w_out.shape), const((1, d)), const(w_r.shape), const((1, LANES))],
        out_specs=[row(d), row(d), row(LANES), row(LANES), const((1, LANES))],
        out_shape=[jax.ShapeDtypeStruct((t, d), F32), jax.ShapeDtypeStruct((t, d), F32),
                   jax.ShapeDtypeStruct((t, LANES), jnp.int32), jax.ShapeDtypeStruct((t, LANES), F32),
                   jax.ShapeDtypeStruct((1, LANES), F32)],
        scratch_shapes=[pltpu.VMEM((1, LANES), F32)],
        compiler_params=_params(("arbitrary",)),
    )(o_a, o_b, hp, w_out, norm2_w.reshape(1, d), w_r, b_r)


def _gather_rows(idx_ref, base, n, src_hbm, dst, sem):
    def issue(r, carry):
        pltpu.make_async_copy(src_hbm.at[pl.ds(idx_ref[base + r], 1), :], dst.at[pl.ds(r, 1), :], sem).start()
        return carry
    lax.fori_loop(0, n, issue, 0)
    pltpu.make_async_copy(src_hbm.at[pl.ds(0, n), :], dst, sem).wait()


def _moe_kernel(te_ref, st_ref, nu_ref, x_hbm, wg_ref, wu_ref, wd_ref, o_ref, xbuf, sem, *, tile):
    i = pl.program_id(0)

    @pl.when(i < nu_ref[0])
    def _():
        _gather_rows(st_ref, i * tile, tile, x_hbm, xbuf, sem)
        x = xbuf[...].astype(BF16)
        g = _dot(x, wg_ref[0])
        u = _dot(x, wu_ref[0])
        o_ref[...] = _dot((_silu(g) * u).astype(BF16), wd_ref[0])

    @pl.when(i >= nu_ref[0])
    def _():
        o_ref[...] = jnp.zeros_like(o_ref)


def _moe(xn2, w_gate, w_up, w_down, tile_expert, slot_token, n_used, tile):
    d = xn2.shape[1]
    n_tiles = tile_expert.shape[0]
    de = w_gate.shape[2]
    grid_spec = pltpu.PrefetchScalarGridSpec(
        num_scalar_prefetch=3,
        grid=(n_tiles,),
        in_specs=[pl.BlockSpec(memory_space=pl.ANY),
                  pl.BlockSpec((1, d, de), lambda i, te, st, nu: (te[i], 0, 0)),
                  pl.BlockSpec((1, d, de), lambda i, te, st, nu: (te[i], 0, 0)),
                  pl.BlockSpec((1, de, d), lambda i, te, st, nu: (te[i], 0, 0))],
        out_specs=pl.BlockSpec((tile, d), lambda i, te, st, nu: (i, 0)),
        scratch_shapes=[pltpu.VMEM((tile, d), F32), pltpu.SemaphoreType.DMA(())],
    )
    return pl.pallas_call(
        functools.partial(_moe_kernel, tile=tile),
        grid_spec=grid_spec,
        out_shape=jax.ShapeDtypeStruct((n_tiles * tile, d), F32),
        compiler_params=_params(("arbitrary",)),
    )(tile_expert, slot_token, n_used, xn2, w_gate, w_up, w_down)


def _combine_kernel(p0_ref, p1_ref, ys_hbm, hp2_ref, rw_ref, fw_ref, o_ref, buf0, buf1, sem0, sem1, *, tm):
    i = pl.program_id(0)
    _gather_rows(p0_ref, i * tm, tm, ys_hbm, buf0, sem0)
    _gather_rows(p1_ref, i * tm, tm, ys_hbm, buf1, sem1)
    rw = rw_ref[...]
    y = hp2_ref[...] + rw[:, 0:1] * buf0[...] + rw[:, 1:2] * buf1[...]
    o_ref[...] = y * lax.rsqrt(jnp.mean(y * y, axis=-1, keepdims=True) + EPS) * fw_ref[...]


def _combine(ys, hp2, route_w, final_w, pos0, pos1, tm):
    t, d = hp2.shape
    grid_spec = pltpu.PrefetchScalarGridSpec(
        num_scalar_prefetch=2,
        grid=(t // tm,),
        in_specs=[pl.BlockSpec(memory_space=pl.ANY),
                  pl.BlockSpec((tm, d), lambda i, p0, p1: (i, 0)),
                  pl.BlockSpec((tm, LANES), lambda i, p0, p1: (i, 0)),
                  pl.BlockSpec((1, d), lambda i, p0, p1: (0, 0))],
        out_specs=pl.BlockSpec((tm, d), lambda i, p0, p1: (i, 0)),
        scratch_shapes=[pltpu.VMEM((tm, d), F32), pltpu.VMEM((tm, d), F32),
                        pltpu.SemaphoreType.DMA(()), pltpu.SemaphoreType.DMA(())],
    )
    return pl.pallas_call(
        functools.partial(_combine_kernel, tm=tm),
        grid_spec=grid_spec,
        out_shape=jax.ShapeDtypeStruct((t, d), F32),
        compiler_params=_params(("arbitrary",)),
    )(pos0, pos1, ys, hp2, route_w, final_w.reshape(1, d))


def kernel(x_prompt, x_sample, state_gdn, state_conv, state_hgrn, meta_tokens, norm1_w, w_in, conv_w, a_log,
           dt_bias, gdn_norm_w, lb_logits, hgrn_norm_w, w_out, norm2_w, w_router_group, b_router_group,
           w_router_expert, b_router_expert, w_gate, w_up, w_down, final_norm_w):
    bp, seq, d = x_prompt.shape
    bs, dec_seq, _ = x_sample.shape
    depth = w_in.shape[0]
    assert depth == 1, "single-layer trunk"
    width = N_HEADS * D_HEAD
    conv_ch = 3 * width
    tm = 256
    tile = 256

    lp = N_META + seq
    front = (-lp) % CHUNK
    lp_pad = lp + front
    sl_s = SUBLANES
    assert dec_seq <= sl_s
    hp = jnp.concatenate([jnp.zeros((bp, front, d), F32),
                          jnp.broadcast_to(meta_tokens[None].astype(F32), (bp, N_META, d)), x_prompt], axis=1)
    hs = jnp.concatenate([x_sample, jnp.zeros((bs, sl_s - dec_seq, d), F32)], axis=1)
    tp, ts = bp * lp_pad, bs * sl_s
    x_all = jnp.concatenate([hp.reshape(tp, d), hs.reshape(ts, d)], axis=0)
    t = tp + ts
    assert t % tm == 0 and bs % (CHUNK // sl_s) == 0

    wi = w_in[0]
    small = wi[:, 4 * width:4 * width + 2 * N_HEADS]
    w_cat = jnp.concatenate([wi[:, :4 * width], wi[:, 4 * width + 2 * N_HEADS:], small,
                             jnp.zeros((d, LANES - 2 * N_HEADS), F32)], axis=1).astype(BF16)
    n_cols = w_cat.shape[1]

    xn = _rmsnorm(x_all, norm1_w[0], BF16, tm)
    proj = _matmul(xn, w_cat, tm, 5 * LANES)

    pvec = jnp.zeros((2, LANES), F32)
    pvec = pvec.at[0, N_HEADS:2 * N_HEADS].set(a_log[0]).at[1, N_HEADS:2 * N_HEADS].set(dt_bias[0])
    lb = jnp.cumsum(jax.nn.softmax(lb_logits.astype(F32), axis=0), axis=0)[0].reshape(1, width)
    gdn_nw = gdn_norm_w[0].reshape(1, D_HEAD)
    hgrn_nw = hgrn_norm_w[0].reshape(1, D_HEAD)
    cw = conv_w[0]

    proj_p = proj[:tp].reshape(bp, lp_pad, n_cols)
    proj_s = proj[tp:].reshape(bs, sl_s, n_cols)
    oa_p, ob_p, sg_p, sh_p = _mixers(proj_p, 1, CHUNK, CHUNK, cw, pvec, gdn_nw, lb, hgrn_nw,
                                     None, None, None, HIGHEST)
    conv0 = jnp.concatenate([jnp.zeros((bs, SUBLANES - (CONV_W - 1), conv_ch), F32), state_conv[0]], axis=1)
    oa_s, ob_s, sg_s, sh_s = _mixers(proj_s, CHUNK // sl_s, sl_s, dec_seq, cw, pvec, gdn_nw, lb, hgrn_nw,
                                     state_gdn[0], conv0, state_hgrn[0], HIGHEST)
    o_a = jnp.concatenate([oa_p.reshape(tp, width), oa_s.reshape(ts, width)], axis=0)
    o_b = jnp.concatenate([ob_p.reshape(tp, width), ob_s.reshape(ts, width)], axis=0)

    w_r = jnp.concatenate([w_router_expert[0], w_router_group[0],
                           jnp.zeros((d, LANES - N_EXPERTS - N_GROUPS), F32)], axis=1)
    b_r = jnp.concatenate([b_router_expert[0], b_router_group[0],
                           jnp.zeros((LANES - N_EXPERTS - N_GROUPS,), F32)]).reshape(1, LANES)
    hp2, xn2, route_i, route_w, counts = _outproj_router(o_a, o_b, x_all, w_out[0].astype(BF16), norm2_w[0],
                                                         w_r, b_r, tm)

    cnt = counts[0, :N_EXPERTS].astype(jnp.int32)
    padded = (cnt + tile - 1) // tile * tile
    ends = jnp.cumsum(padded)
    offs = ends - padded
    eid = route_i[:, 0:2]
    pos = offs[eid] + route_i[:, 2:4]
    n_tiles = (2 * t) // tile + N_EXPERTS
    tile_start = jnp.arange(n_tiles, dtype=jnp.int32) * tile
    tile_expert = jnp.minimum(jnp.sum((ends[None, :] <= tile_start[:, None]).astype(jnp.int32), axis=1),
                              N_EXPERTS - 1)
    n_used = (ends[-1] // tile).astype(jnp.int32).reshape(1)
    tok = jnp.broadcast_to(jnp.arange(t, dtype=jnp.int32)[:, None], (t, 2))
    slot_token = jnp.zeros((n_tiles * tile,), jnp.int32).at[pos.reshape(-1)].set(tok.reshape(-1))

    ys = _moe(xn2, w_gate[0].astype(BF16), w_up[0].astype(BF16), w_down[0].astype(BF16),
              tile_expert, slot_token, n_used, tile)
    y = _combine(ys, hp2, route_w, final_norm_w, pos[:, 0], pos[:, 1], tm)

    y_prompt = y[:tp].reshape(bp, lp_pad, d)[:, front + N_META:]
    y_sample = y[tp:].reshape(bs, sl_s, d)[:, :dec_seq]
    conv_p = proj_p[:, lp_pad - (CONV_W - 1):, :conv_ch]
    conv_s = jnp.concatenate([state_conv[0], proj_s[:, :dec_seq, :conv_ch]], axis=1)[:, dec_seq:]
    return (y_prompt, y_sample, sg_p[None], conv_p[None], sh_p[None], sg_s[None], conv_s[None], sh_s[None])
```

```python
import functools
import math

import jax
import jax.numpy as jnp
from jax import lax
from jax.experimental import pallas as pl
from jax.experimental.pallas import tpu as pltpu

F32 = jnp.float32
BF16 = jnp.bfloat16
HIGHEST = lax.Precision.HIGHEST

EPS = 1e-6
N_META = 16
CONV_W = 4
N_HEADS = 8
D_HEAD = 128
N_GROUPS = 4
EXPERTS_PER_GROUP = 8
N_EXPERTS = N_GROUPS * EXPERTS_PER_GROUP

LANES = 128
SUBLANES = 8
CHUNK = 64
SUB = 16
VMEM_LIMIT = 56 * 1024 * 1024
NEG_BIG = -1e30


def _sigmoid(x):
    return 1.0 / (1.0 + jnp.exp(-x))


def _silu(x):
    return x * _sigmoid(x)


def _softplus(x):
    return jnp.maximum(x, 0.0) + jnp.log1p(jnp.exp(-jnp.abs(x)))


def _split_bf16(a, pieces):
    out = []
    for _ in range(pieces - 1):
        hi = a.astype(BF16)
        out.append(hi)
        a = a - hi.astype(F32)
    out.append(a.astype(BF16))
    return out


def _mm(a, b, dims, mode):
    dg = functools.partial(lax.dot_general, dimension_numbers=(dims, ((), ())), preferred_element_type=F32)
    if mode == "f32":
        return dg(a, b, precision=HIGHEST)
    if mode == "bf16":
        return dg(a.astype(BF16), b.astype(BF16))
    assert mode == "bf16x3"
    ah, al = _split_bf16(a, 2)
    bh, bl = _split_bf16(b, 2)
    return dg(ah, bh) + dg(ah, bl) + dg(al, bh)


def _dot(a, b, mode="bf16"):
    return _mm(a, b, ((1,), (0,)), mode)


def _dot_nt(a, b, mode="bf16"):
    return _mm(a, b, ((1,), (1,)), mode)


def _dot_tn(a, b, mode="bf16"):
    return _mm(a, b, ((0,), (0,)), mode)


def _masked_cumsum(lmask, x):
    lm = lmask.astype(BF16)
    return sum(lax.dot_general(lm, p, (((1,), (0,)), ((), ())), preferred_element_type=F32)
               for p in _split_bf16(x, 3))


def _params(sem):
    return pltpu.CompilerParams(dimension_semantics=sem, vmem_limit_bytes=VMEM_LIMIT)


def _rmsnorm_kernel(x_ref, w_ref, o_ref):
    x = x_ref[...]
    ms = jnp.mean(x * x, axis=-1, keepdims=True)
    o_ref[...] = (x * lax.rsqrt(ms + EPS) * w_ref[...]).astype(o_ref.dtype)


def _rmsnorm(x, w, out_dtype, tm):
    t, d = x.shape
    return pl.pallas_call(
        _rmsnorm_kernel,
        grid=(t // tm,),
        in_specs=[pl.BlockSpec((tm, d), lambda i: (i, 0)), pl.BlockSpec((1, d), lambda i: (0, 0))],
        out_specs=pl.BlockSpec((tm, d), lambda i: (i, 0)),
        out_shape=jax.ShapeDtypeStruct((t, d), out_dtype),
        compiler_params=_params(("parallel",)),
    )(x, w.reshape(1, d))


def _matmul_kernel(x_ref, w_ref, o_ref):
    o_ref[...] = _dot(x_ref[...], w_ref[...])


def _matmul(x, w, tm, tn):
    t, k = x.shape
    n = w.shape[1]
    return pl.pallas_call(
        _matmul_kernel,
        grid=(n // tn, t // tm),
        in_specs=[pl.BlockSpec((tm, k), lambda j, i: (i, 0)), pl.BlockSpec((k, tn), lambda j, i: (0, j))],
        out_specs=pl.BlockSpec((tm, tn), lambda j, i: (i, j)),
        out_shape=jax.ShapeDtypeStruct((t, n), F32),
        compiler_params=_params(("parallel", "arbitrary")),
    )(x, w)


def _chunk_masks(nb, sl):
    r = nb * sl
    shift = int(math.log2(sl))
    ri = lax.broadcasted_iota(jnp.int32, (r, r), 0)
    ci = lax.broadcasted_iota(jnp.int32, (r, r), 1)
    same = lax.shift_right_logical(ri, shift) == lax.shift_right_logical(ci, shift)
    return same & (ci <= ri), same & (ci < ri)


def _row_valid(nb, sl, n_valid):
    rowid = lax.broadcasted_iota(jnp.int32, (nb * sl, 1), 0)
    return (rowid & (sl - 1)) < n_valid


def _last_row_bcast(x, nb, sl):
    c = x.shape[-1]
    x3 = x.reshape(nb, sl, c)
    return jnp.broadcast_to(x3[:, sl - 1:sl, :], (nb, sl, c)).reshape(nb * sl, c)


def _gated_rmsnorm(o, w, gate):
    return o * lax.rsqrt(jnp.mean(o * o, axis=-1, keepdims=True) + EPS) * w * _silu(gate)


def _gdn_kernel(*refs, nb, sl, n_valid, has_init, mode, inv_mode):
    if has_init:
        (qkv_ref, z_ref, ba_ref, cw_ref, pv_ref, nw_ref, s0_ref, c0_ref,
         o_ref, sout_ref, s_scr, carry_scr) = refs
    else:
        (qkv_ref, z_ref, ba_ref, cw_ref, pv_ref, nw_ref,
         o_ref, sout_ref, s_scr, carry_scr) = refs
    c = pl.program_id(1)
    r = nb * sl
    dh = D_HEAD

    @pl.when(c == 0)
    def _():
        if has_init:
            s_scr[...] = s0_ref[...]
            carry_scr[...] = c0_ref[...]
        else:
            s_scr[...] = jnp.zeros_like(s_scr)
            carry_scr[...] = jnp.zeros_like(carry_scr)

    incl, strict = _chunk_masks(nb, sl)
    lmask = jnp.where(incl, 1.0, 0.0)
    valid = _row_valid(nb, sl, n_valid)
    masked = n_valid < sl
    rowid = lax.broadcasted_iota(jnp.int32, (r, 1), 0)
    row_seq = lax.shift_right_logical(rowid, int(math.log2(sl)))
    n_sq = int(math.log2(sl)) - 1

    ba = ba_ref[...].reshape(r, LANES)
    pv = pv_ref[...]
    beta_all = _sigmoid(ba)
    g_all = -jnp.exp(pv[0:1]) * _softplus(ba + pv[1:2])
    if masked:
        g_all = jnp.where(valid, g_all, 0.0)
    gcum = _masked_cumsum(lmask, g_all)
    gcum_t = gcum.T
    glast = _last_row_bcast(gcum, nb, sl)
    cw = cw_ref[...]
    nw = nw_ref[...]

    def conv_slice(c0):
        u = qkv_ref[:, :, c0:c0 + dh]
        prev = carry_scr[:, :, c0:c0 + dh]
        full = jnp.concatenate([prev, u], axis=1)
        acc = None
        for j in range(CONV_W):
            off = SUBLANES - (CONV_W - 1) + j
            term = full[:, off:off + sl, :] * cw[j:j + 1, c0:c0 + dh]
            acc = term if acc is None else acc + term
        return _silu(acc).reshape(r, dh)

    for h in range(N_HEADS):
        q = conv_slice(h * dh)
        k = conv_slice(N_HEADS * dh + h * dh)
        v = conv_slice(2 * N_HEADS * dh + h * dh)
        q = q * lax.rsqrt(jnp.sum(q * q, axis=-1, keepdims=True) + EPS) * (dh ** -0.5)
        k = k * lax.rsqrt(jnp.sum(k * k, axis=-1, keepdims=True) + EPS)
        if masked:
            q = jnp.where(valid, q, 0.0)
            k = jnp.where(valid, k, 0.0)
            v = jnp.where(valid, v, 0.0)
        bc = beta_all[:, h:h + 1]
        gc = gcum[:, N_HEADS + h:N_HEADS + h + 1]
        gr = gcum_t[N_HEADS + h:N_HEADS + h + 1, :]
        gl = glast[:, N_HEADS + h:N_HEADS + h + 1]
        eg = jnp.exp(gc)
        decay = jnp.exp(jnp.where(incl, gc - gr, NEG_BIG))
        qk_kk = _dot_nt(jnp.concatenate([q, k], axis=0), k, mode)
        qk = qk_kk[:r]
        kk = qk_kk[r:]
        a = jnp.where(strict, -(bc * kk * decay), 0.0)
        tm1 = a
        pw = a
        for _ in range(n_sq):
            pw = _dot(pw, pw, inv_mode)
            tm1 = tm1 + pw + _dot(tm1, pw, inv_mode)
        kb = k * bc
        rhs = jnp.concatenate([v * bc, kb * eg], axis=1)
        uw = rhs + _dot(tm1, rhs, mode)
        u = uw[:, :dh]
        w = uw[:, dh:]
        qe = q * eg
        vnew_parts, ointer_parts = [], []
        for b in range(nb):
            rows = slice(b * sl, (b + 1) * sl)
            ws = _dot(jnp.concatenate([w[rows], qe[rows]], axis=0), s_scr[b, h], mode)
            vnew_parts.append(u[rows] - ws[:sl])
            ointer_parts.append(ws[sl:])
        vnew = vnew_parts[0] if nb == 1 else jnp.concatenate(vnew_parts, axis=0)
        ointer = ointer_parts[0] if nb == 1 else jnp.concatenate(ointer_parts, axis=0)
        attn = jnp.where(incl, qk * decay, 0.0)
        o = ointer + _dot(attn, vnew, mode)
        ktil = k * jnp.exp(gl - gc)
        for b in range(nb):
            kt_b = ktil if nb == 1 else jnp.where(row_seq == b, ktil, 0.0)
            gl_b = gl[b * sl:b * sl + 1, :]
            s_scr[b, h] = s_scr[b, h] * jnp.exp(gl_b) + _dot_tn(kt_b, vnew, mode)
        z = z_ref[:, :, h * dh:(h + 1) * dh].reshape(r, dh)
        o_ref[:, :, h * dh:(h + 1) * dh] = _gated_rmsnorm(o, nw, z).reshape(nb, sl, dh)

    carry_scr[...] = qkv_ref[:, sl - SUBLANES:sl, :]

    @pl.when(c == pl.num_programs(1) - 1)
    def _():
        sout_ref[...] = s_scr[...]


def _hgrn_kernel(*refs, nb, sl, n_valid, has_init, mode):
    if has_init:
        (q_ref, f_ref, i_ref, g_ref, lb_ref, nw_ref, s0_ref, o_ref, sout_ref, s_scr) = refs
    else:
        (q_ref, f_ref, i_ref, g_ref, lb_ref, nw_ref, o_ref, sout_ref, s_scr) = refs
    c = pl.program_id(1)
    r = nb * sl
    dh = D_HEAD
    width = N_HEADS * dh

    @pl.when(c == 0)
    def _():
        if has_init:
            s_scr[...] = s0_ref[...]
        else:
            s_scr[...] = jnp.zeros_like(s_scr)

    incl, _ = _chunk_masks(nb, sl)
    lmask = jnp.where(incl, 1.0, 0.0)
    valid = _row_valid(nb, sl, n_valid)
    masked = n_valid < sl
    rowid = lax.broadcasted_iota(jnp.int32, (r, 1), 0)
    row_seq = lax.shift_right_logical(rowid, int(math.log2(sl)))
    sub = min(SUB, sl)
    nblk = r // sub
    sub_shift = int(math.log2(sub))
    ri = lax.broadcasted_iota(jnp.int32, (r, r), 0)
    ci = lax.broadcasted_iota(jnp.int32, (r, r), 1)
    same_blk = lax.shift_right_logical(ri, sub_shift) == lax.shift_right_logical(ci, sub_shift)
    diag_mask = incl & same_blk
    cross_mask = incl & jnp.logical_not(same_blk)

    lb = lb_ref[...]
    f = lb + (1.0 - lb) * _sigmoid(f_ref[...].reshape(r, width))
    lf = jnp.log(f)
    k_all = 1.0 - f
    if masked:
        lf = jnp.where(valid, lf, 0.0)
        k_all = jnp.where(valid, k_all, 0.0)
    bcum = _masked_cumsum(lmask, lf)
    nw = nw_ref[...]

    for h in range(N_HEADS):
        cols = slice(h * dh, (h + 1) * dh)
        bh = bcum[:, cols]
        q = _silu(q_ref[:, :, cols].reshape(r, dh)) * (dh ** -0.5)
        k = k_all[:, cols]
        v = i_ref[:, :, cols].reshape(r, dh)
        if masked:
            v = jnp.where(valid, v, 0.0)
        bmid = jnp.broadcast_to(bh.reshape(nblk, sub, dh)[:, sub // 2:sub // 2 + 1, :],
                                (nblk, sub, dh)).reshape(r, dh)
        attn = jnp.where(diag_mask, _dot_nt(q * jnp.exp(bh - bmid), k * jnp.exp(bmid - bh), mode), 0.0)
        if sl > sub:
            parts = [jnp.zeros((sub, r), F32)]
            for blk in range(1, nblk):
                start = blk * sub
                bref = bh[start - 1:start, :]
                qc = q[start:start + sub] * jnp.exp(bh[start:start + sub] - bref)
                kc = k * jnp.exp(jnp.minimum(bref - bh, 0.0))
                parts.append(_dot_nt(qc, kc, mode))
            attn = attn + jnp.where(cross_mask, jnp.concatenate(parts, axis=0), 0.0)
        qe = q * jnp.exp(bh)
        ointer_parts = []
        for b in range(nb):
            rows = slice(b * sl, (b + 1) * sl)
            ointer_parts.append(_dot(qe[rows], s_scr[b, h], mode))
        ointer = ointer_parts[0] if nb == 1 else jnp.concatenate(ointer_parts, axis=0)
        o = ointer + _dot(attn, v, mode)
        blast = _last_row_bcast(bh, nb, sl)
        ktil = k * jnp.exp(blast - bh)
        tr = jnp.concatenate([ktil, blast] + ([jnp.zeros((LANES - 2 * r, dh), F32)] if 2 * r < LANES else []),
                             axis=0).T
        for b in range(nb):
            kt_b = ktil if nb == 1 else jnp.where(row_seq == b, ktil, 0.0)
            dec_col = jnp.exp(tr[:, r + b * sl:r + b * sl + 1])
            s_scr[b, h] = s_scr[b, h] * dec_col + _dot_tn(kt_b, v, mode)
        gate = g_ref[:, :, cols].reshape(r, dh)
        o_ref[:, :, cols] = _gated_rmsnorm(o, nw, gate).reshape(nb, sl, dh)

    @pl.when(c == pl.num_programs(1) - 1)
    def _():
        sout_ref[...] = s_scr[...]


def _mixers(proj3, nb, sl, n_valid, conv_w, pvec, gdn_nw, lb, hgrn_nw, s_gdn0, conv0, s_hgrn0, mode, inv_mode):
    nseq, length, _ = proj3.shape
    has_init = s_gdn0 is not None
    width = N_HEADS * D_HEAD
    conv_ch = 3 * width
    grid = (nseq // nb, length // sl)
    state_spec = pl.BlockSpec((nb, N_HEADS, D_HEAD, D_HEAD), lambda g, c: (g, 0, 0, 0))
    state_shape = jax.ShapeDtypeStruct((nseq, N_HEADS, D_HEAD, D_HEAD), F32)

    def col_spec(w, idx):
        return pl.BlockSpec((nb, sl, w), lambda g, c: (g, c, idx))

    def const_spec(shape):
        return pl.BlockSpec(shape, lambda g, c: (0,) * len(shape))

    gdn_in = [proj3, proj3, proj3, conv_w, pvec, gdn_nw]
    gdn_specs = [col_spec(conv_ch, 0), col_spec(width, 3), col_spec(LANES, 8 * width // LANES),
                 const_spec(conv_w.shape), const_spec(pvec.shape), const_spec(gdn_nw.shape)]
    if has_init:
        gdn_in += [s_gdn0, conv0]
        gdn_specs += [state_spec, pl.BlockSpec((nb, SUBLANES, conv_ch), lambda g, c: (g, 0, 0))]
    o_gdn, s_gdn = pl.pallas_call(
        functools.partial(_gdn_kernel, nb=nb, sl=sl, n_valid=n_valid, has_init=has_init, mode=mode,
                          inv_mode=inv_mode),
        grid=grid,
        in_specs=gdn_specs,
        out_specs=[col_spec(width, 0), state_spec],
        out_shape=[jax.ShapeDtypeStruct((nseq, length, width), F32), state_shape],
        scratch_shapes=[pltpu.VMEM((nb, N_HEADS, D_HEAD, D_HEAD), F32), pltpu.VMEM((nb, SUBLANES, conv_ch), F32)],
        compiler_params=_params(("parallel", "arbitrary")),
    )(*gdn_in)

    hgrn_in = [proj3, proj3, proj3, proj3, lb, hgrn_nw]
    hgrn_specs = [col_spec(width, 4), col_spec(width, 5), col_spec(width, 6), col_spec(width, 7),
                  const_spec(lb.shape), const_spec(hgrn_nw.shape)]
    if has_init:
        hgrn_in += [s_hgrn0]
        hgrn_specs += [state_spec]
    o_hgrn, s_hgrn = pl.pallas_call(
        functools.partial(_hgrn_kernel, nb=nb, sl=sl, n_valid=n_valid, has_init=has_init, mode=mode),
        grid=grid,
        in_specs=hgrn_specs,
        out_specs=[col_spec(width, 0), state_spec],
        out_shape=[jax.ShapeDtypeStruct((nseq, length, width), F32), state_shape],
        scratch_shapes=[pltpu.VMEM((nb, N_HEADS, D_HEAD, D_HEAD), F32)],
        compiler_params=_params(("parallel", "arbitrary")),
    )(*hgrn_in)
    return o_gdn, o_hgrn, s_gdn, s_hgrn


def _outproj_router_kernel(oa_ref, ob_ref, hp_ref, wo_ref, n2_ref, wr_ref, br_ref,
                           hp2_ref, xn2_ref, ri_ref, rw_ref, cnt_ref, carry_scr, *, tm):
    i = pl.program_id(0)

    @pl.when(i == 0)
    def _():
        carry_scr[...] = jnp.zeros_like(carry_scr)

    half = oa_ref.shape[-1]
    mix = _dot(oa_ref[...].astype(BF16), wo_ref[:half, :]) + _dot(ob_ref[...].astype(BF16), wo_ref[half:, :])
    hp2 = hp_ref[...] + mix
    hp2_ref[...] = hp2
    xn2 = hp2 * lax.rsqrt(jnp.mean(hp2 * hp2, axis=-1, keepdims=True) + EPS) * n2_ref[...]
    xn2_ref[...] = xn2
    logits = _dot(xn2, wr_ref[...], "f32") + br_ref[...]

    lane = lax.broadcasted_iota(jnp.int32, (tm, LANES), 1)
    lane_f = lane.astype(F32)
    far = float(4 * LANES)
    is_g = (lane >= N_EXPERTS) & (lane < N_EXPERTS + N_GROUPS)
    lg = jnp.where(is_g, logits, -jnp.inf)
    gmax = jnp.max(lg, axis=-1, keepdims=True)
    gsel = jnp.min(jnp.where(lg == gmax, lane_f, far), axis=-1, keepdims=True).astype(jnp.int32) - N_EXPERTS
    p_top = 1.0 / jnp.sum(jnp.where(is_g, jnp.exp(logits - gmax), 0.0), axis=-1, keepdims=True)
    in_grp = (lane < N_EXPERTS) & (lax.shift_right_logical(lane, 3) == gsel)
    le = jnp.where(in_grp, logits, -jnp.inf)
    m1 = jnp.max(le, axis=-1, keepdims=True)
    i1 = jnp.min(jnp.where(le == m1, lane_f, far), axis=-1, keepdims=True).astype(jnp.int32)
    le2 = jnp.where(lane == i1, -jnp.inf, le)
    m2 = jnp.max(le2, axis=-1, keepdims=True)
    i2 = jnp.min(jnp.where(le2 == m2, lane_f, far), axis=-1, keepdims=True).astype(jnp.int32)
    e2 = jnp.exp(m2 - m1)
    w1 = p_top / (1.0 + e2)
    w2 = p_top * e2 / (1.0 + e2)

    onehot = (lane == i1) | (lane == i2)
    onehot_f = jnp.where(onehot, 1.0, 0.0)
    tri = (lax.broadcasted_iota(jnp.int32, (tm, tm), 1) < lax.broadcasted_iota(jnp.int32, (tm, tm), 0))
    before = _dot(jnp.where(tri, 1.0, 0.0).astype(BF16), onehot_f.astype(BF16)) + carry_scr[...]
    r1 = jnp.sum(jnp.where(lane == i1, before, 0.0), axis=-1, keepdims=True).astype(jnp.int32)
    r2 = jnp.sum(jnp.where(lane == i2, before, 0.0), axis=-1, keepdims=True).astype(jnp.int32)
    carry_scr[...] = carry_scr[...] + jnp.sum(onehot_f, axis=0, keepdims=True)

    ri_ref[...] = jnp.where(lane == 0, i1, jnp.where(lane == 1, i2, jnp.where(lane == 2, r1, jnp.where(lane == 3, r2, 0))))
    rw_ref[...] = jnp.where(lane == 0, w1, jnp.where(lane == 1, w2, 0.0))

    @pl.when(i == pl.num_programs(0) - 1)
    def _():
        cnt_ref[...] = carry_scr[...]


def _outproj_router(o_a, o_b, hp, w_out, norm2_w, w_r, b_r, tm):
    t, d = hp.shape
    half = o_a.shape[1]
    row = lambda w: pl.BlockSpec((tm, w), lambda i: (i, 0))
    const = lambda shape: pl.BlockSpec(shape, lambda i: (0,) * len(shape))
    return pl.pallas_call(
        functools.partial(_outproj_router_kernel, tm=tm),
        grid=(t // tm,),
        in_specs=[row(half), row(half), row(d), const(w_out.shape), const((1, d)), const(w_r.shape), const((1, LANES))],
        out_specs=[row(d), row(d), row(LANES), row(LANES), const((1, LANES))],
        out_shape=[jax.ShapeDtypeStruct((t, d), F32), jax.ShapeDtypeStruct((t, d), F32),
                   jax.ShapeDtypeStruct((t, LANES), jnp.int32), jax.ShapeDtypeStruct((t, LANES), F32),
                   jax.ShapeDtypeStruct((1, LANES), F32)],
        scratch_shapes=[pltpu.VMEM((1, LANES), F32)],
        compiler_params=_params(("arbitrary",)),
    )(o_a, o_b, hp, w_out, norm2_w.reshape(1, d), w_r, b_r)


def _gather_rows(idx_ref, base, n, src_hbm, dst, sem):
    def issue(r, carry):
        pltpu.make_async_copy(src_hbm.at[pl.ds(idx_ref[base + r], 1), :], dst.at[pl.ds(r, 1), :], sem).start()
        return carry
    lax.fori_loop(0, n, issue, 0)
    pltpu.make_async_copy(src_hbm.at[pl.ds(0, n), :], dst, sem).wait()


def _moe_kernel(te_ref, st_ref, nu_ref, x_hbm, wg_ref, wu_ref, wd_ref, o_ref, xbuf, sem, *, tile):
    i = pl.program_id(0)

    @pl.when(i < nu_ref[0])
    def _():
        _gather_rows(st_ref, i * tile, tile, x_hbm, xbuf, sem)
        x = xbuf[...].astype(BF16)
        g = _dot(x, wg_ref[0])
        u = _dot(x, wu_ref[0])
        o_ref[...] = _dot((_silu(g) * u).astype(BF16), wd_ref[0])

    @pl.when(i >= nu_ref[0])
    def _():
        o_ref[...] = jnp.zeros_like(o_ref)


def _moe(xn2, w_gate, w_up, w_down, tile_expert, slot_token, n_used, tile):
    d = xn2.shape[1]
    n_tiles = tile_expert.shape[0]
    de = w_gate.shape[2]
    grid_spec = pltpu.PrefetchScalarGridSpec(
        num_scalar_prefetch=3,
        grid=(n_tiles,),
        in_specs=[pl.BlockSpec(memory_space=pl.ANY),
                  pl.BlockSpec((1, d, de), lambda i, te, st, nu: (te[i], 0, 0)),
                  pl.BlockSpec((1, d, de), lambda i, te, st, nu: (te[i], 0, 0)),
                  pl.BlockSpec((1, de, d), lambda i, te, st, nu: (te[i], 0, 0))],
        out_specs=pl.BlockSpec((tile, d), lambda i, te, st, nu: (i, 0)),
        scratch_shapes=[pltpu.VMEM((tile, d), F32), pltpu.SemaphoreType.DMA(())],
    )
    return pl.pallas_call(
        functools.partial(_moe_kernel, tile=tile),
        grid_spec=grid_spec,
        out_shape=jax.ShapeDtypeStruct((n_tiles * tile, d), F32),
        compiler_params=_params(("arbitrary",)),
    )(tile_expert, slot_token, n_used, xn2, w_gate, w_up, w_down)


def _combine_kernel(p0_ref, p1_ref, ys_hbm, hp2_ref, rw_ref, fw_ref, o_ref, buf0, buf1, sem0, sem1, *, tm):
    i = pl.program_id(0)
    _gather_rows(p0_ref, i * tm, tm, ys_hbm, buf0, sem0)
    _gather_rows(p1_ref, i * tm, tm, ys_hbm, buf1, sem1)
    rw = rw_ref[...]
    y = hp2_ref[...] + rw[:, 0:1] * buf0[...] + rw[:, 1:2] * buf1[...]
    o_ref[...] = y * lax.rsqrt(jnp.mean(y * y, axis=-1, keepdims=True) + EPS) * fw_ref[...]


def _combine(ys, hp2, route_w, final_w, pos0, pos1, tm):
    t, d = hp2.shape
    grid_spec = pltpu.PrefetchScalarGridSpec(
        num_scalar_prefetch=2,
        grid=(t // tm,),
        in_specs=[pl.BlockSpec(memory_space=pl.ANY),
                  pl.BlockSpec((tm, d), lambda i, p0, p1: (i, 0)),
                  pl.BlockSpec((tm, LANES), lambda i, p0, p1: (i, 0)),
                  pl.BlockSpec((1, d), lambda i, p0, p1: (0, 0))],
        out_specs=pl.BlockSpec((tm, d), lambda i, p0, p1: (i, 0)),
        scratch_shapes=[pltpu.VMEM((tm, d), F32), pltpu.VMEM((tm, d), F32),
                        pltpu.SemaphoreType.DMA(()), pltpu.SemaphoreType.DMA(())],
    )
    return pl.pallas_call(
        functools.partial(_combine_kernel, tm=tm),
        grid_spec=grid_spec,
        out_shape=jax.ShapeDtypeStruct((t, d), F32),
        compiler_params=_params(("arbitrary",)),
    )(pos0, pos1, ys, hp2, route_w, final_w.reshape(1, d))


def kernel(x_prompt, x_sample, state_gdn, state_conv, state_hgrn, meta_tokens, norm1_w, w_in, conv_w, a_log,
           dt_bias, gdn_norm_w, lb_logits, hgrn_norm_w, w_out, norm2_w, w_router_group, b_router_group,
           w_router_expert, b_router_expert, w_gate, w_up, w_down, final_norm_w):
    bp, seq, d = x_prompt.shape
    bs, dec_seq, _ = x_sample.shape
    depth = w_in.shape[0]
    assert depth == 1, "single-layer trunk"
    width = N_HEADS * D_HEAD
    conv_ch = 3 * width
    tm = 256
    tile = 256

    lp = N_META + seq
    front = (-lp) % CHUNK
    lp_pad = lp + front
    sl_s = SUBLANES
    assert dec_seq <= sl_s
    hp = jnp.concatenate([jnp.zeros((bp, front, d), F32),
                          jnp.broadcast_to(meta_tokens[None].astype(F32), (bp, N_META, d)), x_prompt], axis=1)
    hs = jnp.concatenate([x_sample, jnp.zeros((bs, sl_s - dec_seq, d), F32)], axis=1)
    tp, ts = bp * lp_pad, bs * sl_s
    x_all = jnp.concatenate([hp.reshape(tp, d), hs.reshape(ts, d)], axis=0)
    t = tp + ts
    assert t % tm == 0 and bs % (CHUNK // sl_s) == 0

    wi = w_in[0]
    small = wi[:, 4 * width:4 * width + 2 * N_HEADS]
    w_cat = jnp.concatenate([wi[:, :4 * width], wi[:, 4 * width + 2 * N_HEADS:], small,
                             jnp.zeros((d, LANES - 2 * N_HEADS), F32)], axis=1).astype(BF16)
    n_cols = w_cat.shape[1]

    xn = _rmsnorm(x_all, norm1_w[0], BF16, tm)
    proj = _matmul(xn, w_cat, tm, 5 * LANES)

    pvec = jnp.zeros((2, LANES), F32)
    pvec = pvec.at[0, N_HEADS:2 * N_HEADS].set(a_log[0]).at[1, N_HEADS:2 * N_HEADS].set(dt_bias[0])
    lb = jnp.cumsum(jax.nn.softmax(lb_logits.astype(F32), axis=0), axis=0)[0].reshape(1, width)
    gdn_nw = gdn_norm_w[0].reshape(1, D_HEAD)
    hgrn_nw = hgrn_norm_w[0].reshape(1, D_HEAD)
    cw = conv_w[0]

    proj_p = proj[:tp].reshape(bp, lp_pad, n_cols)
    proj_s = proj[tp:].reshape(bs, sl_s, n_cols)
    oa_p, ob_p, sg_p, sh_p = _mixers(proj_p, 1, CHUNK, CHUNK, cw, pvec, gdn_nw, lb, hgrn_nw,
                                     None, None, None, "bf16", "bf16x3")
    conv0 = jnp.concatenate([jnp.zeros((bs, SUBLANES - (CONV_W - 1), conv_ch), F32), state_conv[0]], axis=1)
    oa_s, ob_s, sg_s, sh_s = _mixers(proj_s, CHUNK // sl_s, sl_s, dec_seq, cw, pvec, gdn_nw, lb, hgrn_nw,
                                     state_gdn[0], conv0, state_hgrn[0], "bf16", "bf16x3")
    o_a = jnp.concatenate([oa_p.reshape(tp, width), oa_s.reshape(ts, width)], axis=0)
    o_b = jnp.concatenate([ob_p.reshape(tp, width), ob_s.reshape(ts, width)], axis=0)

    w_r = jnp.concatenate([w_router_expert[0], w_router_group[0],
                           jnp.zeros((d, LANES - N_EXPERTS - N_GROUPS), F32)], axis=1)
    b_r = jnp.concatenate([b_router_expert[0], b_router_group[0],
                           jnp.zeros((LANES - N_EXPERTS - N_GROUPS,), F32)]).reshape(1, LANES)
    hp2, xn2, route_i, route_w, counts = _outproj_router(o_a, o_b, x_all, w_out[0].astype(BF16), norm2_w[0],
                                                         w_r, b_r, tm)

    cnt = counts[0, :N_EXPERTS].astype(jnp.int32)
    padded = (cnt + tile - 1) // tile * tile
    ends = jnp.cumsum(padded)
    offs = ends - padded
    eid = route_i[:, 0:2]
    pos = offs[eid] + route_i[:, 2:4]
    n_tiles = (2 * t) // tile + N_EXPERTS
    tile_start = jnp.arange(n_tiles, dtype=jnp.int32) * tile
    tile_expert = jnp.minimum(jnp.sum((ends[None, :] <= tile_start[:, None]).astype(jnp.int32), axis=1),
                              N_EXPERTS - 1)
    n_used = (ends[-1] // tile).astype(jnp.int32).reshape(1)
    tok = jnp.broadcast_to(jnp.arange(t, dtype=jnp.int32)[:, None], (t, 2))
    slot_token = jnp.zeros((n_tiles * tile,), jnp.int32).at[pos.reshape(-1)].set(tok.reshape(-1))

    ys = _moe(xn2, w_gate[0].astype(BF16), w_up[0].astype(BF16), w_down[0].astype(BF16),
              tile_expert, slot_token, n_used, tile)
    y = _combine(ys, hp2, route_w, final_norm_w, pos[:, 0], pos[:, 1], tm)

    y_prompt = y[:tp].reshape(bp, lp_pad, d)[:, front + N_META:]
    y_sample = y[tp:].reshape(bs, sl_s, d)[:, :dec_seq]
    conv_p = proj_p[:, lp_pad - (CONV_W - 1):, :conv_ch]
    conv_s = jnp.concatenate([state_conv[0], proj_s[:, :dec_seq, :conv_ch]], axis=1)[:, dec_seq:]
    return (y_prompt, y_sample, sg_p[None], conv_p[None], sh_p[None], sg_s[None], conv_s[None], sh_s[None])
```

```python
import functools
import math

import jax
import jax.numpy as jnp
from jax import lax
from jax.experimental import pallas as pl
from jax.experimental.pallas import tpu as pltpu

F32 = jnp.float32
BF16 = jnp.bfloat16
HIGHEST = lax.Precision.HIGHEST

EPS = 1e-6
N_META = 16
CONV_W = 4
N_HEADS = 8
D_HEAD = 128
N_GROUPS = 4
EXPERTS_PER_GROUP = 8
N_EXPERTS = N_GROUPS * EXPERTS_PER_GROUP

LANES = 128
SUBLANES = 8
CHUNK = 64
SUB = 16
VMEM_LIMIT = 56 * 1024 * 1024
NEG_BIG = -1e30


def _sigmoid(x):
    return 1.0 / (1.0 + jnp.exp(-x))


def _silu(x):
    return x * _sigmoid(x)


def _softplus(x):
    return jnp.maximum(x, 0.0) + jnp.log1p(jnp.exp(-jnp.abs(x)))


def _split_bf16(a, pieces):
    out = []
    for _ in range(pieces - 1):
        hi = a.astype(BF16)
        out.append(hi)
        a = a - hi.astype(F32)
    out.append(a.astype(BF16))
    return out


def _mm(a, b, dims, mode):
    dg = functools.partial(lax.dot_general, dimension_numbers=(dims, ((), ())), preferred_element_type=F32)
    if mode == "f32":
        return dg(a, b, precision=HIGHEST)
    if mode == "bf16":
        return dg(a.astype(BF16), b.astype(BF16))
    assert mode == "bf16x3"
    ah, al = _split_bf16(a, 2)
    bh, bl = _split_bf16(b, 2)
    return dg(ah, bh) + dg(ah, bl) + dg(al, bh)


def _dot(a, b, mode="bf16"):
    return _mm(a, b, ((1,), (0,)), mode)


def _dot_nt(a, b, mode="bf16"):
    return _mm(a, b, ((1,), (1,)), mode)


def _dot_tn(a, b, mode="bf16"):
    return _mm(a, b, ((0,), (0,)), mode)


def _masked_cumsum(lmask, x):
    lm = lmask.astype(BF16)
    return sum(lax.dot_general(lm, p, (((1,), (0,)), ((), ())), preferred_element_type=F32)
               for p in _split_bf16(x, 3))


def _params(sem):
    return pltpu.CompilerParams(dimension_semantics=sem, vmem_limit_bytes=VMEM_LIMIT)


def _rmsnorm_kernel(x_ref, w_ref, o_ref):
    x = x_ref[...]
    ms = jnp.mean(x * x, axis=-1, keepdims=True)
    o_ref[...] = (x * lax.rsqrt(ms + EPS) * w_ref[...]).astype(o_ref.dtype)


def _rmsnorm(x, w, out_dtype, tm):
    t, d = x.shape
    return pl.pallas_call(
        _rmsnorm_kernel,
        grid=(t // tm,),
        in_specs=[pl.BlockSpec((tm, d), lambda i: (i, 0)), pl.BlockSpec((1, d), lambda i: (0, 0))],
        out_specs=pl.BlockSpec((tm, d), lambda i: (i, 0)),
        out_shape=jax.ShapeDtypeStruct((t, d), out_dtype),
        compiler_params=_params(("parallel",)),
    )(x, w.reshape(1, d))


def _matmul_kernel(x_ref, w_ref, o_ref):
    o_ref[...] = _dot(x_ref[...], w_ref[...])


def _matmul(x, w, tm, tn):
    t, k = x.shape
    n = w.shape[1]
    return pl.pallas_call(
        _matmul_kernel,
        grid=(n // tn, t // tm),
        in_specs=[pl.BlockSpec((tm, k), lambda j, i: (i, 0)), pl.BlockSpec((k, tn), lambda j, i: (0, j))],
        out_specs=pl.BlockSpec((tm, tn), lambda j, i: (i, j)),
        out_shape=jax.ShapeDtypeStruct((t, n), F32),
        compiler_params=_params(("parallel", "arbitrary")),
    )(x, w)


def _chunk_masks(nb, sl):
    r = nb * sl
    shift = int(math.log2(sl))
    ri = lax.broadcasted_iota(jnp.int32, (r, r), 0)
    ci = lax.broadcasted_iota(jnp.int32, (r, r), 1)
    same = lax.shift_right_logical(ri, shift) == lax.shift_right_logical(ci, shift)
    return same & (ci <= ri), same & (ci < ri)


def _row_valid(nb, sl, n_valid):
    rowid = lax.broadcasted_iota(jnp.int32, (nb * sl, 1), 0)
    return (rowid & (sl - 1)) < n_valid


def _last_row_bcast(x, nb, sl):
    c = x.shape[-1]
    x3 = x.reshape(nb, sl, c)
    return jnp.broadcast_to(x3[:, sl - 1:sl, :], (nb, sl, c)).reshape(nb * sl, c)


def _gated_rmsnorm(o, w, gate):
    return o * lax.rsqrt(jnp.mean(o * o, axis=-1, keepdims=True) + EPS) * w * _silu(gate)


def _gdn_kernel(*refs, nb, sl, n_valid, has_init, mode, inv_mode, head_group):
    if has_init:
        (qkv_ref, z_ref, ba_ref, cw_ref, pv_ref, nw_ref, s0_ref, c0_ref,
         o_ref, sout_ref, s_scr, carry_scr) = refs
    else:
        (qkv_ref, z_ref, ba_ref, cw_ref, pv_ref, nw_ref,
         o_ref, sout_ref, s_scr, carry_scr) = refs
    c = pl.program_id(1)
    r = nb * sl
    dh = D_HEAD

    @pl.when(c == 0)
    def _():
        if has_init:
            s_scr[...] = s0_ref[...]
            carry_scr[...] = c0_ref[...]
        else:
            s_scr[...] = jnp.zeros_like(s_scr)
            carry_scr[...] = jnp.zeros_like(carry_scr)

    incl, strict = _chunk_masks(nb, sl)
    lmask = jnp.where(incl, 1.0, 0.0)
    valid = _row_valid(nb, sl, n_valid)
    masked = n_valid < sl
    rowid = lax.broadcasted_iota(jnp.int32, (r, 1), 0)
    row_seq = lax.shift_right_logical(rowid, int(math.log2(sl)))
    n_sq = int(math.log2(sl)) - 1

    ba = ba_ref[...].reshape(r, LANES)
    pv = pv_ref[...]
    beta_all = _sigmoid(ba)
    g_all = -jnp.exp(pv[0:1]) * _softplus(ba + pv[1:2])
    if masked:
        g_all = jnp.where(valid, g_all, 0.0)
    gcum = _masked_cumsum(lmask, g_all)
    gcum_t = gcum.T
    glast = _last_row_bcast(gcum, nb, sl)
    cw = cw_ref[...]
    nw = nw_ref[...]

    def conv_slice(c0):
        u = qkv_ref[:, :, c0:c0 + dh]
        prev = carry_scr[:, :, c0:c0 + dh]
        full = jnp.concatenate([prev, u], axis=1)
        acc = None
        for j in range(CONV_W):
            off = SUBLANES - (CONV_W - 1) + j
            term = full[:, off:off + sl, :] * cw[j:j + 1, c0:c0 + dh]
            acc = term if acc is None else acc + term
        return _silu(acc).reshape(r, dh)

    for h0 in range(0, N_HEADS, head_group):
        heads = range(h0, h0 + head_group)
        qs, ks, vs, bcs, gcs, gls, egs, decays = [], [], [], [], [], [], [], []
        for h in heads:
            q = conv_slice(h * dh)
            k = conv_slice(N_HEADS * dh + h * dh)
            v = conv_slice(2 * N_HEADS * dh + h * dh)
            q = q * lax.rsqrt(jnp.sum(q * q, axis=-1, keepdims=True) + EPS) * (dh ** -0.5)
            k = k * lax.rsqrt(jnp.sum(k * k, axis=-1, keepdims=True) + EPS)
            if masked:
                q = jnp.where(valid, q, 0.0)
                k = jnp.where(valid, k, 0.0)
                v = jnp.where(valid, v, 0.0)
            gc = gcum[:, N_HEADS + h:N_HEADS + h + 1]
            gr = gcum_t[N_HEADS + h:N_HEADS + h + 1, :]
            qs.append(q)
            ks.append(k)
            vs.append(v)
            bcs.append(beta_all[:, h:h + 1])
            gcs.append(gc)
            gls.append(glast[:, N_HEADS + h:N_HEADS + h + 1])
            egs.append(jnp.exp(gc))
            decays.append(jnp.exp(jnp.where(incl, gc - gr, NEG_BIG)))
        n = len(qs)
        qk_kk = [_dot_nt(jnp.concatenate([qs[i], ks[i]], axis=0), ks[i], mode) for i in range(n)]
        tm1 = [jnp.where(strict, -(bcs[i] * qk_kk[i][r:] * decays[i]), 0.0) for i in range(n)]
        pw = list(tm1)
        for _ in range(n_sq):
            pw = [_dot(pw[i], pw[i], inv_mode) for i in range(n)]
            tm1 = [tm1[i] + pw[i] + _dot(tm1[i], pw[i], inv_mode) for i in range(n)]
        rhs = [jnp.concatenate([vs[i] * bcs[i], ks[i] * (bcs[i] * egs[i])], axis=1) for i in range(n)]
        uw = [rhs[i] + _dot(tm1[i], rhs[i], mode) for i in range(n)]
        vnew, ointer = [], []
        for i, h in enumerate(heads):
            u = uw[i][:, :dh]
            w = uw[i][:, dh:]
            qe = qs[i] * egs[i]
            vnew_parts, ointer_parts = [], []
            for b in range(nb):
                rows = slice(b * sl, (b + 1) * sl)
                ws = _dot(jnp.concatenate([w[rows], qe[rows]], axis=0), s_scr[b, h], mode)
                vnew_parts.append(u[rows] - ws[:sl])
                ointer_parts.append(ws[sl:])
            vnew.append(vnew_parts[0] if nb == 1 else jnp.concatenate(vnew_parts, axis=0))
            ointer.append(ointer_parts[0] if nb == 1 else jnp.concatenate(ointer_parts, axis=0))
        for i, h in enumerate(heads):
            attn = jnp.where(incl, qk_kk[i][:r] * decays[i], 0.0)
            o = ointer[i] + _dot(attn, vnew[i], mode)
            z = z_ref[:, :, h * dh:(h + 1) * dh].reshape(r, dh)
            o_ref[:, :, h * dh:(h + 1) * dh] = _gated_rmsnorm(o, nw, z).reshape(nb, sl, dh)
        for i, h in enumerate(heads):
            ktil = ks[i] * jnp.exp(gls[i] - gcs[i])
            for b in range(nb):
                kt_b = ktil if nb == 1 else jnp.where(row_seq == b, ktil, 0.0)
                gl_b = gls[i][b * sl:b * sl + 1, :]
                s_scr[b, h] = s_scr[b, h] * jnp.exp(gl_b) + _dot_tn(kt_b, vnew[i], mode)

    carry_scr[...] = qkv_ref[:, sl - SUBLANES:sl, :]

    @pl.when(c == pl.num_programs(1) - 1)
    def _():
        sout_ref[...] = s_scr[...]


def _hgrn_kernel(*refs, nb, sl, n_valid, has_init, mode, head_group):
    if has_init:
        (q_ref, f_ref, i_ref, g_ref, lb_ref, nw_ref, s0_ref, o_ref, sout_ref, s_scr) = refs
    else:
        (q_ref, f_ref, i_ref, g_ref, lb_ref, nw_ref, o_ref, sout_ref, s_scr) = refs
    c = pl.program_id(1)
    r = nb * sl
    dh = D_HEAD
    width = N_HEADS * dh

    @pl.when(c == 0)
    def _():
        if has_init:
            s_scr[...] = s0_ref[...]
        else:
            s_scr[...] = jnp.zeros_like(s_scr)

    incl, _ = _chunk_masks(nb, sl)
    lmask = jnp.where(incl, 1.0, 0.0)
    valid = _row_valid(nb, sl, n_valid)
    masked = n_valid < sl
    rowid = lax.broadcasted_iota(jnp.int32, (r, 1), 0)
    row_seq = lax.shift_right_logical(rowid, int(math.log2(sl)))
    sub = min(SUB, sl)
    nblk = r // sub
    sub_shift = int(math.log2(sub))
    ri = lax.broadcasted_iota(jnp.int32, (r, r), 0)
    ci = lax.broadcasted_iota(jnp.int32, (r, r), 1)
    same_blk = lax.shift_right_logical(ri, sub_shift) == lax.shift_right_logical(ci, sub_shift)
    diag_mask = incl & same_blk
    cross_mask = incl & jnp.logical_not(same_blk)

    lb = lb_ref[...]
    f = lb + (1.0 - lb) * _sigmoid(f_ref[...].reshape(r, width))
    lf = jnp.log(f)
    k_all = 1.0 - f
    if masked:
        lf = jnp.where(valid, lf, 0.0)
        k_all = jnp.where(valid, k_all, 0.0)
    bcum = _masked_cumsum(lmask, lf)
    nw = nw_ref[...]

    def head_inputs(h):
        cols = slice(h * dh, (h + 1) * dh)
        q = _silu(q_ref[:, :, cols].reshape(r, dh)) * (dh ** -0.5)
        v = i_ref[:, :, cols].reshape(r, dh)
        if masked:
            v = jnp.where(valid, v, 0.0)
        return q, k_all[:, cols], v, bcum[:, cols]

    def intra_attn(q, k, bh):
        bmid = jnp.broadcast_to(bh.reshape(nblk, sub, dh)[:, sub // 2:sub // 2 + 1, :],
                                (nblk, sub, dh)).reshape(r, dh)
        attn = jnp.where(diag_mask, _dot_nt(q * jnp.exp(bh - bmid), k * jnp.exp(bmid - bh), mode), 0.0)
        if sl > sub:
            parts = [jnp.zeros((sub, r), F32)]
            for blk in range(1, nblk):
                start = blk * sub
                bref = bh[start - 1:start, :]
                qc = q[start:start + sub] * jnp.exp(bh[start:start + sub] - bref)
                kc = k * jnp.exp(jnp.minimum(bref - bh, 0.0))
                parts.append(_dot_nt(qc, kc, mode))
            attn = attn + jnp.where(cross_mask, jnp.concatenate(parts, axis=0), 0.0)
        return attn

    for h0 in range(0, N_HEADS, head_group):
        heads = range(h0, h0 + head_group)
        ins = [head_inputs(h) for h in heads]
        attns = [intra_attn(q, k, bh) for (q, k, v, bh) in ins]
        for i, h in enumerate(heads):
            q, k, v, bh = ins[i]
            qe = q * jnp.exp(bh)
            ointer_parts = []
            for b in range(nb):
                rows = slice(b * sl, (b + 1) * sl)
                ointer_parts.append(_dot(qe[rows], s_scr[b, h], mode))
            ointer = ointer_parts[0] if nb == 1 else jnp.concatenate(ointer_parts, axis=0)
            o = ointer + _dot(attns[i], v, mode)
            cols = slice(h * dh, (h + 1) * dh)
            gate = g_ref[:, :, cols].reshape(r, dh)
            o_ref[:, :, cols] = _gated_rmsnorm(o, nw, gate).reshape(nb, sl, dh)
        for i, h in enumerate(heads):
            q, k, v, bh = ins[i]
            blast = _last_row_bcast(bh, nb, sl)
            ktil = k * jnp.exp(blast - bh)
            pad = [jnp.zeros((LANES - r, dh), F32)] if r < LANES else []
            tr = jnp.concatenate([blast] + pad, axis=0).T
            for b in range(nb):
                kt_b = ktil if nb == 1 else jnp.where(row_seq == b, ktil, 0.0)
                dec_col = jnp.exp(tr[:, b * sl:b * sl + 1])
                s_scr[b, h] = s_scr[b, h] * dec_col + _dot_tn(kt_b, v, mode)

    @pl.when(c == pl.num_programs(1) - 1)
    def _():
        sout_ref[...] = s_scr[...]


def _mixers(proj3, nb, sl, n_valid, conv_w, pvec, gdn_nw, lb, hgrn_nw, s_gdn0, conv0, s_hgrn0, mode, inv_mode):
    nseq, length, _ = proj3.shape
    has_init = s_gdn0 is not None
    width = N_HEADS * D_HEAD
    conv_ch = 3 * width
    grid = (nseq // nb, length // sl)
    state_spec = pl.BlockSpec((nb, N_HEADS, D_HEAD, D_HEAD), lambda g, c: (g, 0, 0, 0))
    state_shape = jax.ShapeDtypeStruct((nseq, N_HEADS, D_HEAD, D_HEAD), F32)

    def col_spec(w, idx):
        return pl.BlockSpec((nb, sl, w), lambda g, c: (g, c, idx))

    def const_spec(shape):
        return pl.BlockSpec(shape, lambda g, c: (0,) * len(shape))

    gdn_in = [proj3, proj3, proj3, conv_w, pvec, gdn_nw]
    gdn_specs = [col_spec(conv_ch, 0), col_spec(width, 3), col_spec(LANES, 8 * width // LANES),
                 const_spec(conv_w.shape), const_spec(pvec.shape), const_spec(gdn_nw.shape)]
    if has_init:
        gdn_in += [s_gdn0, conv0]
        gdn_specs += [state_spec, pl.BlockSpec((nb, SUBLANES, conv_ch), lambda g, c: (g, 0, 0))]
    o_gdn, s_gdn = pl.pallas_call(
        functools.partial(_gdn_kernel, nb=nb, sl=sl, n_valid=n_valid, has_init=has_init, mode=mode,
                          inv_mode=inv_mode, head_group=4 if nb == 1 else 2),
        grid=grid,
        in_specs=gdn_specs,
        out_specs=[col_spec(width, 0), state_spec],
        out_shape=[jax.ShapeDtypeStruct((nseq, length, width), F32), state_shape],
        scratch_shapes=[pltpu.VMEM((nb, N_HEADS, D_HEAD, D_HEAD), F32), pltpu.VMEM((nb, SUBLANES, conv_ch), F32)],
        compiler_params=_params(("parallel", "arbitrary")),
    )(*gdn_in)

    hgrn_in = [proj3, proj3, proj3, proj3, lb, hgrn_nw]
    hgrn_specs = [col_spec(width, 4), col_spec(width, 5), col_spec(width, 6), col_spec(width, 7),
                  const_spec(lb.shape), const_spec(hgrn_nw.shape)]
    if has_init:
        hgrn_in += [s_hgrn0]
        hgrn_specs += [state_spec]
    o_hgrn, s_hgrn = pl.pallas_call(
        functools.partial(_hgrn_kernel, nb=nb, sl=sl, n_valid=n_valid, has_init=has_init, mode=mode,
                          head_group=4 if nb == 1 else 2),
        grid=grid,
        in_specs=hgrn_specs,
        out_specs=[col_spec(width, 0), state_spec],
        out_shape=[jax.ShapeDtypeStruct((nseq, length, width), F32), state_shape],
        scratch_shapes=[pltpu.VMEM((nb, N_HEADS, D_HEAD, D_HEAD), F32)],
        compiler_params=_params(("parallel", "arbitrary")),
    )(*hgrn_in)
    return o_gdn, o_hgrn, s_gdn, s_hgrn


def _outproj_router_kernel(oa_ref, ob_ref, hp_ref, wo_ref, n2_ref, wr_ref, br_ref,
                           hp2_ref, xn2_ref, ri_ref, rw_ref, cnt_ref, carry_scr, *, tm):
    i = pl.program_id(0)

    @pl.when(i == 0)
    def _():
        carry_scr[...] = jnp.zeros_like(carry_scr)

    half = oa_ref.shape[-1]
    mix = _dot(oa_ref[...].astype(BF16), wo_ref[:half, :]) + _dot(ob_ref[...].astype(BF16), wo_ref[half:, :])
    hp2 = hp_ref[...] + mix
    hp2_ref[...] = hp2
    xn2 = hp2 * lax.rsqrt(jnp.mean(hp2 * hp2, axis=-1, keepdims=True) + EPS) * n2_ref[...]
    xn2_ref[...] = xn2
    logits = _dot(xn2, wr_ref[...], "bf16x3") + br_ref[...]

    lane = lax.broadcasted_iota(jnp.int32, (tm, LANES), 1)
    lane_f = lane.astype(F32)
    far = float(4 * LANES)
    is_g = (lane >= N_EXPERTS) & (lane < N_EXPERTS + N_GROUPS)
    lg = jnp.where(is_g, logits, -jnp.inf)
    gmax = jnp.max(lg, axis=-1, keepdims=True)
    gsel = jnp.min(jnp.where(lg == gmax, lane_f, far), axis=-1, keepdims=True).astype(jnp.int32) - N_EXPERTS
    p_top = 1.0 / jnp.sum(jnp.where(is_g, jnp.exp(logits - gmax), 0.0), axis=-1, keepdims=True)
    in_grp = (lane < N_EXPERTS) & (lax.shift_right_logical(lane, 3) == gsel)
    le = jnp.where(in_grp, logits, -jnp.inf)
    m1 = jnp.max(le, axis=-1, keepdims=True)
    i1 = jnp.min(jnp.where(le == m1, lane_f, far), axis=-1, keepdims=True).astype(jnp.int32)
    le2 = jnp.where(lane == i1, -jnp.inf, le)
    m2 = jnp.max(le2, axis=-1, keepdims=True)
    i2 = jnp.min(jnp.where(le2 == m2, lane_f, far), axis=-1, keepdims=True).astype(jnp.int32)
    e2 = jnp.exp(m2 - m1)
    w1 = p_top / (1.0 + e2)
    w2 = p_top * e2 / (1.0 + e2)

    onehot = (lane == i1) | (lane == i2)
    onehot_f = jnp.where(onehot, 1.0, 0.0)
    tri = (lax.broadcasted_iota(jnp.int32, (tm, tm), 1) < lax.broadcasted_iota(jnp.int32, (tm, tm), 0))
    before = _dot(jnp.where(tri, 1.0, 0.0).astype(BF16), onehot_f.astype(BF16)) + carry_scr[...]
    r1 = jnp.sum(jnp.where(lane == i1, before, 0.0), axis=-1, keepdims=True).astype(jnp.int32)
    r2 = jnp.sum(jnp.where(lane == i2, before, 0.0), axis=-1, keepdims=True).astype(jnp.int32)
    carry_scr[...] = carry_scr[...] + jnp.sum(onehot_f, axis=0, keepdims=True)

    ri_ref[...] = jnp.where(lane == 0, i1, jnp.where(lane == 1, i2, jnp.where(lane == 2, r1, jnp.where(lane == 3, r2, 0))))
    rw_ref[...] = jnp.where(lane == 0, w1, jnp.where(lane == 1, w2, 0.0))

    @pl.when(i == pl.num_programs(0) - 1)
    def _():
        cnt_ref[...] = carry_scr[...]


def _outproj_router(o_a, o_b, hp, w_out, norm2_w, w_r, b_r, tm):
    t, d = hp.shape
    half = o_a.shape[1]
    row = lambda w: pl.BlockSpec((tm, w), lambda i: (i, 0))
    const = lambda shape: pl.BlockSpec(shape, lambda i: (0,) * len(shape))
    return pl.pallas_call(
        functools.partial(_outproj_router_kernel, tm=tm),
        grid=(t // tm,),
        in_specs=[row(half), row(half), row(d), const(w_out.shape), const((1, d)), const(w_r.shape), const((1, LANES))],
        out_specs=[row(d), row(d), row(LANES), row(LANES), const((1, LANES))],
        out_shape=[jax.ShapeDtypeStruct((t, d), F32), jax.ShapeDtypeStruct((t, d), F32),
                   jax.ShapeDtypeStruct((t, LANES), jnp.int32), jax.ShapeDtypeStruct((t, LANES), F32),
                   jax.ShapeDtypeStruct((1, LANES), F32)],
        scratch_shapes=[pltpu.VMEM((1, LANES), F32)],
        compiler_params=_params(("arbitrary",)),
    )(o_a, o_b, hp, w_out, norm2_w.reshape(1, d), w_r, b_r)


def _gather_rows(idx_ref, base, n, src_hbm, dst, sem):
    def issue(r, carry):
        pltpu.make_async_copy(src_hbm.at[pl.ds(idx_ref[base + r], 1), :], dst.at[pl.ds(r, 1), :], sem).start()
        return carry
    lax.fori_loop(0, n, issue, 0)
    pltpu.make_async_copy(src_hbm.at[pl.ds(0, n), :], dst, sem).wait()


def _moe_kernel(te_ref, st_ref, nu_ref, x_hbm, wg_ref, wu_ref, wd_ref, o_ref, xbuf, sem, *, tile):
    i = pl.program_id(0)

    @pl.when(i < nu_ref[0])
    def _():
        _gather_rows(st_ref, i * tile, tile, x_hbm, xbuf, sem)
        x = xbuf[...].astype(BF16)
        g = _dot(x, wg_ref[0])
        u = _dot(x, wu_ref[0])
        o_ref[...] = _dot((_silu(g) * u).astype(BF16), wd_ref[0])

    @pl.when(i >= nu_ref[0])
    def _():
        o_ref[...] = jnp.zeros_like(o_ref)


def _moe(xn2, w_gate, w_up, w_down, tile_expert, slot_token, n_used, tile):
    d = xn2.shape[1]
    n_tiles = tile_expert.shape[0]
    de = w_gate.shape[2]
    grid_spec = pltpu.PrefetchScalarGridSpec(
        num_scalar_prefetch=3,
        grid=(n_tiles,),
        in_specs=[pl.BlockSpec(memory_space=pl.ANY),
                  pl.BlockSpec((1, d, de), lambda i, te, st, nu: (te[i], 0, 0)),
                  pl.BlockSpec((1, d, de), lambda i, te, st, nu: (te[i], 0, 0)),
                  pl.BlockSpec((1, de, d), lambda i, te, st, nu: (te[i], 0, 0))],
        out_specs=pl.BlockSpec((tile, d), lambda i, te, st, nu: (i, 0)),
        scratch_shapes=[pltpu.VMEM((tile, d), F32), pltpu.SemaphoreType.DMA(())],
    )
    return pl.pallas_call(
        functools.partial(_moe_kernel, tile=tile),
        grid_spec=grid_spec,
        out_shape=jax.ShapeDtypeStruct((n_tiles * tile, d), F32),
        compiler_params=_params(("arbitrary",)),
    )(tile_expert, slot_token, n_used, xn2, w_gate, w_up, w_down)


def _combine_kernel(p0_ref, p1_ref, ys_hbm, hp2_ref, rw_ref, fw_ref, o_ref, buf0, buf1, sem0, sem1, *, tm):
    i = pl.program_id(0)
    _gather_rows(p0_ref, i * tm, tm, ys_hbm, buf0, sem0)
    _gather_rows(p1_ref, i * tm, tm, ys_hbm, buf1, sem1)
    rw = rw_ref[...]
    y = hp2_ref[...] + rw[:, 0:1] * buf0[...] + rw[:, 1:2] * buf1[...]
    o_ref[...] = y * lax.rsqrt(jnp.mean(y * y, axis=-1, keepdims=True) + EPS) * fw_ref[...]


def _combine(ys, hp2, route_w, final_w, pos0, pos1, tm):
    t, d = hp2.shape
    grid_spec = pltpu.PrefetchScalarGridSpec(
        num_scalar_prefetch=2,
        grid=(t // tm,),
        in_specs=[pl.BlockSpec(memory_space=pl.ANY),
                  pl.BlockSpec((tm, d), lambda i, p0, p1: (i, 0)),
                  pl.BlockSpec((tm, LANES), lambda i, p0, p1: (i, 0)),
                  pl.BlockSpec((1, d), lambda i, p0, p1: (0, 0))],
        out_specs=pl.BlockSpec((tm, d), lambda i, p0, p1: (i, 0)),
        scratch_shapes=[pltpu.VMEM((tm, d), F32), pltpu.VMEM((tm, d), F32),
                        pltpu.SemaphoreType.DMA(()), pltpu.SemaphoreType.DMA(())],
    )
    return pl.pallas_call(
        functools.partial(_combine_kernel, tm=tm),
        grid_spec=grid_spec,
        out_shape=jax.ShapeDtypeStruct((t, d), F32),
        compiler_params=_params(("arbitrary",)),
    )(pos0, pos1, ys, hp2, route_w, final_w.reshape(1, d))


def kernel(x_prompt, x_sample, state_gdn, state_conv, state_hgrn, meta_tokens, norm1_w, w_in, conv_w, a_log,
           dt_bias, gdn_norm_w, lb_logits, hgrn_norm_w, w_out, norm2_w, w_router_group, b_router_group,
           w_router_expert, b_router_expert, w_gate, w_up, w_down, final_norm_w):
    bp, seq, d = x_prompt.shape
    bs, dec_seq, _ = x_sample.shape
    depth = w_in.shape[0]
    assert depth == 1, "single-layer trunk"
    width = N_HEADS * D_HEAD
    conv_ch = 3 * width
    tm = 256
    tile = 256

    lp = N_META + seq
    front = (-lp) % CHUNK
    lp_pad = lp + front
    sl_s = SUBLANES
    assert dec_seq <= sl_s
    hp = jnp.concatenate([jnp.zeros((bp, front, d), F32),
                          jnp.broadcast_to(meta_tokens[None].astype(F32), (bp, N_META, d)), x_prompt], axis=1)
    hs = jnp.concatenate([x_sample, jnp.zeros((bs, sl_s - dec_seq, d), F32)], axis=1)
    tp, ts = bp * lp_pad, bs * sl_s
    x_all = jnp.concatenate([hp.reshape(tp, d), hs.reshape(ts, d)], axis=0)
    t = tp + ts
    assert t % tm == 0 and bs % (CHUNK // sl_s) == 0

    wi = w_in[0]
    small = wi[:, 4 * width:4 * width + 2 * N_HEADS]
    w_cat = jnp.concatenate([wi[:, :4 * width], wi[:, 4 * width + 2 * N_HEADS:], small,
                             jnp.zeros((d, LANES - 2 * N_HEADS), F32)], axis=1).astype(BF16)
    n_cols = w_cat.shape[1]

    xn = _rmsnorm(x_all, norm1_w[0], BF16, tm)
    proj = _matmul(xn, w_cat, tm, 5 * LANES)

    pvec = jnp.zeros((2, LANES), F32)
    pvec = pvec.at[0, N_HEADS:2 * N_HEADS].set(a_log[0]).at[1, N_HEADS:2 * N_HEADS].set(dt_bias[0])
    lb = jnp.cumsum(jax.nn.softmax(lb_logits.astype(F32), axis=0), axis=0)[0].reshape(1, width)
    gdn_nw = gdn_norm_w[0].reshape(1, D_HEAD)
    hgrn_nw = hgrn_norm_w[0].reshape(1, D_HEAD)
    cw = conv_w[0]

    proj_p = proj[:tp].reshape(bp, lp_pad, n_cols)
    proj_s = proj[tp:].reshape(bs, sl_s, n_cols)
    oa_p, ob_p, sg_p, sh_p = _mixers(proj_p, 1, CHUNK, CHUNK, cw, pvec, gdn_nw, lb, hgrn_nw,
                                     None, None, None, "bf16", "bf16x3")
    conv0 = jnp.concatenate([jnp.zeros((bs, SUBLANES - (CONV_W - 1), conv_ch), F32), state_conv[0]], axis=1)
    oa_s, ob_s, sg_s, sh_s = _mixers(proj_s, CHUNK // sl_s, sl_s, dec_seq, cw, pvec, gdn_nw, lb, hgrn_nw,
                                     state_gdn[0], conv0, state_hgrn[0], "bf16", "bf16x3")
    o_a = jnp.concatenate([oa_p.reshape(tp, width), oa_s.reshape(ts, width)], axis=0)
    o_b = jnp.concatenate([ob_p.reshape(tp, width), ob_s.reshape(ts, width)], axis=0)

    w_r = jnp.concatenate([w_router_expert[0], w_router_group[0],
                           jnp.zeros((d, LANES - N_EXPERTS - N_GROUPS), F32)], axis=1)
    b_r = jnp.concatenate([b_router_expert[0], b_router_group[0],
                           jnp.zeros((LANES - N_EXPERTS - N_GROUPS,), F32)]).reshape(1, LANES)
    hp2, xn2, route_i, route_w, counts = _outproj_router(o_a, o_b, x_all, w_out[0].astype(BF16), norm2_w[0],
                                                         w_r, b_r, tm)

    cnt = counts[0, :N_EXPERTS].astype(jnp.int32)
    padded = (cnt + tile - 1) // tile * tile
    ends = jnp.cumsum(padded)
    offs = ends - padded
    eid = route_i[:, 0:2]
    pos = offs[eid] + route_i[:, 2:4]
    n_tiles = (2 * t) // tile + N_EXPERTS
    tile_start = jnp.arange(n_tiles, dtype=jnp.int32) * tile
    tile_expert = jnp.minimum(jnp.sum((ends[None, :] <= tile_start[:, None]).astype(jnp.int32), axis=1),
                              N_EXPERTS - 1)
    n_used = (ends[-1] // tile).astype(jnp.int32).reshape(1)
    tok = jnp.broadcast_to(jnp.arange(t, dtype=jnp.int32)[:, None], (t, 2))
    slot_token = jnp.zeros((n_tiles * tile,), jnp.int32).at[pos.reshape(-1)].set(tok.reshape(-1))

    ys = _moe(xn2, w_gate[0].astype(BF16), w_up[0].astype(BF16), w_down[0].astype(BF16),
              tile_expert, slot_token, n_used, tile)
    y = _combine(ys, hp2, route_w, final_norm_w, pos[:, 0], pos[:, 1], tm)

    y_prompt = y[:tp].reshape(bp, lp_pad, d)[:, front + N_META:]
    y_sample = y[tp:].reshape(bs, sl_s, d)[:, :dec_seq]
    conv_p = proj_p[:, lp_pad - (CONV_W - 1):, :conv_ch]
    conv_s = jnp.concatenate([state_conv[0], proj_s[:, :dec_seq, :conv_ch]], axis=1)[:, dec_seq:]
    return (y_prompt, y_sample, sg_p[None], conv_p[None], sh_p[None], sg_s[None], conv_s[None], sh_s[None])
```

```python
import functools
import math

import jax
import jax.numpy as jnp
from jax import lax
from jax.experimental import pallas as pl
from jax.experimental.pallas import tpu as pltpu

F32 = jnp.float32
BF16 = jnp.bfloat16
HIGHEST = lax.Precision.HIGHEST

EPS = 1e-6
N_META = 16
CONV_W = 4
N_HEADS = 8
D_HEAD = 128
N_GROUPS = 4
EXPERTS_PER_GROUP = 8
N_EXPERTS = N_GROUPS * EXPERTS_PER_GROUP

LANES = 128
SUBLANES = 8
CHUNK = 64
SUB = 16
VMEM_LIMIT = 56 * 1024 * 1024
NEG_BIG = -1e30


def _sigmoid(x):
    return 1.0 / (1.0 + jnp.exp(-x))


def _silu(x):
    return x * _sigmoid(x)


def _softplus(x):
    return jnp.maximum(x, 0.0) + jnp.log1p(jnp.exp(-jnp.abs(x)))


def _split_bf16(a, pieces):
    out = []
    for _ in range(pieces - 1):
        hi = a.astype(BF16)
        out.append(hi)
        a = a - hi.astype(F32)
    out.append(a.astype(BF16))
    return out


def _mm(a, b, dims, mode):
    dg = functools.partial(lax.dot_general, dimension_numbers=(dims, ((), ())), preferred_element_type=F32)
    if mode == "f32":
        return dg(a, b, precision=HIGHEST)
    if mode == "bf16":
        return dg(a.astype(BF16), b.astype(BF16))
    assert mode == "bf16x3"
    ah, al = _split_bf16(a, 2)
    bh, bl = _split_bf16(b, 2)
    return dg(ah, bh) + dg(ah, bl) + dg(al, bh)


def _dot(a, b, mode="bf16"):
    return _mm(a, b, ((1,), (0,)), mode)


def _dot_nt(a, b, mode="bf16"):
    return _mm(a, b, ((1,), (1,)), mode)


def _dot_tn(a, b, mode="bf16"):
    return _mm(a, b, ((0,), (0,)), mode)


def _masked_cumsum(lmask, x):
    lm = lmask.astype(BF16)
    return sum(lax.dot_general(lm, p, (((1,), (0,)), ((), ())), preferred_element_type=F32)
               for p in _split_bf16(x, 3))


def _params(sem):
    return pltpu.CompilerParams(dimension_semantics=sem, vmem_limit_bytes=VMEM_LIMIT)


def _row_tile(n, target):
    best = max(c for c in range(16, min(n, target) + 1, 16) if n % c == 0)
    return best


def _rmsnorm_kernel(x_ref, w_ref, o_ref):
    x = x_ref[...]
    ms = jnp.mean(x * x, axis=-1, keepdims=True)
    o_ref[...] = (x * lax.rsqrt(ms + EPS) * w_ref[...]).astype(o_ref.dtype)


def _rmsnorm(x, w, out_dtype, tm):
    t, d = x.shape
    return pl.pallas_call(
        _rmsnorm_kernel,
        grid=(t // tm,),
        in_specs=[pl.BlockSpec((tm, d), lambda i: (i, 0)), pl.BlockSpec((1, d), lambda i: (0, 0))],
        out_specs=pl.BlockSpec((tm, d), lambda i: (i, 0)),
        out_shape=jax.ShapeDtypeStruct((t, d), out_dtype),
        compiler_params=_params(("parallel",)),
    )(x, w.reshape(1, d))


def _inproj_kernel(x_ref, wa_ref, wb_ref, o_ref, w_scr, *, n_a):
    j = pl.program_id(0)
    i = pl.program_id(1)

    @pl.when((i == 0) & (j < n_a))
    def _():
        w_scr[...] = wa_ref[...].astype(BF16)

    @pl.when((i == 0) & (j >= n_a))
    def _():
        w_scr[...] = wb_ref[...].astype(BF16)

    o_ref[...] = _dot(x_ref[...], w_scr[...])


def _inproj(x, w_a, n_a, w_b, tm, tn):
    t, k = x.shape
    n_b = w_b.shape[1] // tn
    return pl.pallas_call(
        functools.partial(_inproj_kernel, n_a=n_a),
        grid=(n_a + n_b, t // tm),
        in_specs=[pl.BlockSpec((tm, k), lambda j, i: (i, 0)),
                  pl.BlockSpec((k, tn), lambda j, i: (0, jnp.minimum(j, n_a - 1))),
                  pl.BlockSpec((k, tn), lambda j, i: (0, jnp.maximum(j - n_a, 0)))],
        out_specs=pl.BlockSpec((tm, tn), lambda j, i: (i, j)),
        out_shape=jax.ShapeDtypeStruct((t, (n_a + n_b) * tn), F32),
        scratch_shapes=[pltpu.VMEM((k, tn), BF16)],
        compiler_params=_params(("arbitrary", "arbitrary")),
    )(x, w_a, w_b)


def _matmul_kernel(x_ref, w_ref, o_ref):
    o_ref[...] = _dot(x_ref[...], w_ref[...])


def _matmul(x, w, tm, tn):
    t, k = x.shape
    n = w.shape[1]
    return pl.pallas_call(
        _matmul_kernel,
        grid=(n // tn, t // tm),
        in_specs=[pl.BlockSpec((tm, k), lambda j, i: (i, 0)), pl.BlockSpec((k, tn), lambda j, i: (0, j))],
        out_specs=pl.BlockSpec((tm, tn), lambda j, i: (i, j)),
        out_shape=jax.ShapeDtypeStruct((t, n), F32),
        compiler_params=_params(("parallel", "arbitrary")),
    )(x, w)


def _chunk_masks(nb, sl):
    r = nb * sl
    shift = int(math.log2(sl))
    ri = lax.broadcasted_iota(jnp.int32, (r, r), 0)
    ci = lax.broadcasted_iota(jnp.int32, (r, r), 1)
    same = lax.shift_right_logical(ri, shift) == lax.shift_right_logical(ci, shift)
    return same & (ci <= ri), same & (ci < ri)


def _row_valid(nb, sl, n_valid):
    rowid = lax.broadcasted_iota(jnp.int32, (nb * sl, 1), 0)
    return (rowid & (sl - 1)) < n_valid


def _last_row_bcast(x, nb, sl):
    c = x.shape[-1]
    x3 = x.reshape(nb, sl, c)
    return jnp.broadcast_to(x3[:, sl - 1:sl, :], (nb, sl, c)).reshape(nb * sl, c)


def _gated_rmsnorm(o, w, gate):
    return o * lax.rsqrt(jnp.mean(o * o, axis=-1, keepdims=True) + EPS) * w * _silu(gate)


def _gdn_kernel(*refs, nb, sl, n_valid, has_init, mode, inv_mode, head_group):
    if has_init:
        (qkv_ref, z_ref, ba_ref, cw_ref, pv_ref, nw_ref, s0_ref, c0_ref,
         o_ref, sout_ref, s_scr, carry_scr) = refs
    else:
        (qkv_ref, z_ref, ba_ref, cw_ref, pv_ref, nw_ref,
         o_ref, sout_ref, s_scr, carry_scr) = refs
    c = pl.program_id(1)
    r = nb * sl
    dh = D_HEAD

    @pl.when(c == 0)
    def _():
        if has_init:
            s_scr[...] = s0_ref[...]
            carry_scr[...] = c0_ref[...]
        else:
            s_scr[...] = jnp.zeros_like(s_scr)
            carry_scr[...] = jnp.zeros_like(carry_scr)

    incl, strict = _chunk_masks(nb, sl)
    lmask = jnp.where(incl, 1.0, 0.0)
    valid = _row_valid(nb, sl, n_valid)
    masked = n_valid < sl
    rowid = lax.broadcasted_iota(jnp.int32, (r, 1), 0)
    row_seq = lax.shift_right_logical(rowid, int(math.log2(sl)))
    n_sq = int(math.log2(sl)) - 1

    ba = ba_ref[...].reshape(r, LANES)
    pv = pv_ref[...]
    beta_all = _sigmoid(ba)
    g_all = -jnp.exp(pv[0:1]) * _softplus(ba + pv[1:2])
    if masked:
        g_all = jnp.where(valid, g_all, 0.0)
    gcum = _masked_cumsum(lmask, g_all)
    gcum_t = gcum.T
    glast = _last_row_bcast(gcum, nb, sl)
    cw = cw_ref[...]
    nw = nw_ref[...]

    def conv_slice(c0):
        u = qkv_ref[:, :, c0:c0 + dh]
        prev = carry_scr[:, :, c0:c0 + dh]
        full = jnp.concatenate([prev, u], axis=1)
        acc = None
        for j in range(CONV_W):
            off = SUBLANES - (CONV_W - 1) + j
            term = full[:, off:off + sl, :] * cw[j:j + 1, c0:c0 + dh]
            acc = term if acc is None else acc + term
        return _silu(acc).reshape(r, dh)

    for h0 in range(0, N_HEADS, head_group):
        heads = range(h0, h0 + head_group)
        qs, ks, vs, bcs, gcs, gls, egs, decays = [], [], [], [], [], [], [], []
        for h in heads:
            q = conv_slice(h * dh)
            k = conv_slice(N_HEADS * dh + h * dh)
            v = conv_slice(2 * N_HEADS * dh + h * dh)
            q = q * lax.rsqrt(jnp.sum(q * q, axis=-1, keepdims=True) + EPS) * (dh ** -0.5)
            k = k * lax.rsqrt(jnp.sum(k * k, axis=-1, keepdims=True) + EPS)
            if masked:
                q = jnp.where(valid, q, 0.0)
                k = jnp.where(valid, k, 0.0)
                v = jnp.where(valid, v, 0.0)
            gc = gcum[:, N_HEADS + h:N_HEADS + h + 1]
            gr = gcum_t[N_HEADS + h:N_HEADS + h + 1, :]
            qs.append(q)
            ks.append(k)
            vs.append(v)
            bcs.append(beta_all[:, h:h + 1])
            gcs.append(gc)
            gls.append(glast[:, N_HEADS + h:N_HEADS + h + 1])
            egs.append(jnp.exp(gc))
            decays.append(jnp.exp(jnp.where(incl, gc - gr, NEG_BIG)))
        n = len(qs)
        qk_kk = [_dot_nt(jnp.concatenate([qs[i], ks[i]], axis=0), ks[i], mode) for i in range(n)]
        tm1 = [jnp.where(strict, -(bcs[i] * qk_kk[i][r:] * decays[i]), 0.0) for i in range(n)]
        pw = list(tm1)
        for _ in range(n_sq):
            pw = [_dot(pw[i], pw[i], inv_mode) for i in range(n)]
            tm1 = [tm1[i] + pw[i] + _dot(tm1[i], pw[i], inv_mode) for i in range(n)]
        rhs = [jnp.concatenate([vs[i] * bcs[i], ks[i] * (bcs[i] * egs[i])], axis=1) for i in range(n)]
        uw = [rhs[i] + _dot(tm1[i], rhs[i], mode) for i in range(n)]
        vnew, ointer = [], []
        for i, h in enumerate(heads):
            u = uw[i][:, :dh]
            w = uw[i][:, dh:]
            qe = qs[i] * egs[i]
            vnew_parts, ointer_parts = [], []
            for b in range(nb):
                rows = slice(b * sl, (b + 1) * sl)
                ws = _dot(jnp.concatenate([w[rows], qe[rows]], axis=0), s_scr[b, h], mode)
                vnew_parts.append(u[rows] - ws[:sl])
                ointer_parts.append(ws[sl:])
            vnew.append(vnew_parts[0] if nb == 1 else jnp.concatenate(vnew_parts, axis=0))
            ointer.append(ointer_parts[0] if nb == 1 else jnp.concatenate(ointer_parts, axis=0))
        for i, h in enumerate(heads):
            attn = jnp.where(incl, qk_kk[i][:r] * decays[i], 0.0)
            o = ointer[i] + _dot(attn, vnew[i], mode)
            z = z_ref[:, :, h * dh:(h + 1) * dh].reshape(r, dh)
            o_ref[:, :, h * dh:(h + 1) * dh] = _gated_rmsnorm(o, nw, z).reshape(nb, sl, dh).astype(o_ref.dtype)
        for i, h in enumerate(heads):
            ktil = ks[i] * jnp.exp(gls[i] - gcs[i])
            for b in range(nb):
                kt_b = ktil if nb == 1 else jnp.where(row_seq == b, ktil, 0.0)
                gl_b = gls[i][b * sl:b * sl + 1, :]
                s_scr[b, h] = s_scr[b, h] * jnp.exp(gl_b) + _dot_tn(kt_b, vnew[i], mode)

    carry_scr[...] = qkv_ref[:, sl - SUBLANES:sl, :]

    @pl.when(c == pl.num_programs(1) - 1)
    def _():
        sout_ref[...] = s_scr[...]


def _hgrn_kernel(*refs, nb, sl, n_valid, has_init, mode, head_group):
    if has_init:
        (q_ref, f_ref, i_ref, g_ref, lb_ref, nw_ref, s0_ref, o_ref, sout_ref, s_scr) = refs
    else:
        (q_ref, f_ref, i_ref, g_ref, lb_ref, nw_ref, o_ref, sout_ref, s_scr) = refs
    c = pl.program_id(1)
    r = nb * sl
    dh = D_HEAD
    width = N_HEADS * dh

    @pl.when(c == 0)
    def _():
        if has_init:
            s_scr[...] = s0_ref[...]
        else:
            s_scr[...] = jnp.zeros_like(s_scr)

    incl, _ = _chunk_masks(nb, sl)
    lmask = jnp.where(incl, 1.0, 0.0)
    valid = _row_valid(nb, sl, n_valid)
    masked = n_valid < sl
    rowid = lax.broadcasted_iota(jnp.int32, (r, 1), 0)
    row_seq = lax.shift_right_logical(rowid, int(math.log2(sl)))
    sub = min(SUB, sl)
    nblk = r // sub
    sub_shift = int(math.log2(sub))
    ri = lax.broadcasted_iota(jnp.int32, (r, r), 0)
    ci = lax.broadcasted_iota(jnp.int32, (r, r), 1)
    same_blk = lax.shift_right_logical(ri, sub_shift) == lax.shift_right_logical(ci, sub_shift)
    diag_mask = incl & same_blk
    cross_mask = incl & jnp.logical_not(same_blk)

    lb = lb_ref[...]
    f = lb + (1.0 - lb) * _sigmoid(f_ref[...].reshape(r, width))
    lf = jnp.log(f)
    k_all = 1.0 - f
    if masked:
        lf = jnp.where(valid, lf, 0.0)
        k_all = jnp.where(valid, k_all, 0.0)
    bcum = _masked_cumsum(lmask, lf)
    nw = nw_ref[...]

    def head_inputs(h):
        cols = slice(h * dh, (h + 1) * dh)
        q = _silu(q_ref[:, :, cols].reshape(r, dh)) * (dh ** -0.5)
        v = i_ref[:, :, cols].reshape(r, dh)
        if masked:
            v = jnp.where(valid, v, 0.0)
        return q, k_all[:, cols], v, bcum[:, cols]

    def intra_attn(q, k, bh):
        bmid = jnp.broadcast_to(bh.reshape(nblk, sub, dh)[:, sub // 2:sub // 2 + 1, :],
                                (nblk, sub, dh)).reshape(r, dh)
        attn = jnp.where(diag_mask, _dot_nt(q * jnp.exp(bh - bmid), k * jnp.exp(bmid - bh), mode), 0.0)
        if sl > sub:
            parts = [jnp.zeros((sub, r), F32)]
            for blk in range(1, nblk):
                start = blk * sub
                bref = bh[start - 1:start, :]
                qc = q[start:start + sub] * jnp.exp(bh[start:start + sub] - bref)
                kc = k * jnp.exp(jnp.minimum(bref - bh, 0.0))
                parts.append(_dot_nt(qc, kc, mode))
            attn = attn + jnp.where(cross_mask, jnp.concatenate(parts, axis=0), 0.0)
        return attn

    for h0 in range(0, N_HEADS, head_group):
        heads = range(h0, h0 + head_group)
        ins = [head_inputs(h) for h in heads]
        attns = [intra_attn(q, k, bh) for (q, k, v, bh) in ins]
        for i, h in enumerate(heads):
            q, k, v, bh = ins[i]
            qe = q * jnp.exp(bh)
            ointer_parts = []
            for b in range(nb):
                rows = slice(b * sl, (b + 1) * sl)
                ointer_parts.append(_dot(qe[rows], s_scr[b, h], mode))
            ointer = ointer_parts[0] if nb == 1 else jnp.concatenate(ointer_parts, axis=0)
            o = ointer + _dot(attns[i], v, mode)
            cols = slice(h * dh, (h + 1) * dh)
            gate = g_ref[:, :, cols].reshape(r, dh)
            o_ref[:, :, cols] = _gated_rmsnorm(o, nw, gate).reshape(nb, sl, dh).astype(o_ref.dtype)
        for i, h in enumerate(heads):
            q, k, v, bh = ins[i]
            blast = _last_row_bcast(bh, nb, sl)
            ktil = k * jnp.exp(blast - bh)
            pad = [jnp.zeros((LANES - r, dh), F32)] if r < LANES else []
            tr = jnp.concatenate([blast] + pad, axis=0).T
            for b in range(nb):
                kt_b = ktil if nb == 1 else jnp.where(row_seq == b, ktil, 0.0)
                dec_col = jnp.exp(tr[:, b * sl:b * sl + 1])
                s_scr[b, h] = s_scr[b, h] * dec_col + _dot_tn(kt_b, v, mode)

    @pl.when(c == pl.num_programs(1) - 1)
    def _():
        sout_ref[...] = s_scr[...]


def _mixers(proj3, ba3, blk_off, nseq, nb, sl, n_valid, conv_w, pvec, gdn_nw, lb, hgrn_nw,
            s_gdn0, conv0, s_hgrn0, shared_init, out_dtype, mode, inv_mode):
    length = proj3.shape[1]
    has_init = s_gdn0 is not None
    width = N_HEADS * D_HEAD
    conv_ch = 3 * width
    grid = (nseq // nb, length // sl)
    state_spec = pl.BlockSpec((nb, N_HEADS, D_HEAD, D_HEAD), lambda g, c: (g, 0, 0, 0))
    state_shape = jax.ShapeDtypeStruct((nseq, N_HEADS, D_HEAD, D_HEAD), F32)
    init_idx = (lambda g: 0) if shared_init else (lambda g: g)
    init_state_spec = pl.BlockSpec((nb, N_HEADS, D_HEAD, D_HEAD), lambda g, c: (init_idx(g), 0, 0, 0))
    head_group = 4 if nb == 1 else 2

    def col_spec(w, idx):
        return pl.BlockSpec((nb, sl, w), lambda g, c: (g + blk_off, c, idx))

    def out_spec(w):
        return pl.BlockSpec((nb, sl, w), lambda g, c: (g, c, 0))

    def const_spec(shape):
        return pl.BlockSpec(shape, lambda g, c: (0,) * len(shape))

    gdn_in = [proj3, proj3, ba3, conv_w, pvec, gdn_nw]
    gdn_specs = [col_spec(conv_ch, 0), col_spec(width, 3), col_spec(LANES, 0),
                 const_spec(conv_w.shape), const_spec(pvec.shape), const_spec(gdn_nw.shape)]
    if has_init:
        gdn_in += [s_gdn0, conv0]
        gdn_specs += [init_state_spec, pl.BlockSpec((nb, SUBLANES, conv_ch), lambda g, c: (init_idx(g), 0, 0))]
    o_gdn, s_gdn = pl.pallas_call(
        functools.partial(_gdn_kernel, nb=nb, sl=sl, n_valid=n_valid, has_init=has_init, mode=mode,
                          inv_mode=inv_mode, head_group=head_group),
        grid=grid,
        in_specs=gdn_specs,
        out_specs=[out_spec(width), state_spec],
        out_shape=[jax.ShapeDtypeStruct((nseq, length, width), out_dtype), state_shape],
        scratch_shapes=[pltpu.VMEM((nb, N_HEADS, D_HEAD, D_HEAD), F32), pltpu.VMEM((nb, SUBLANES, conv_ch), F32)],
        compiler_params=_params(("parallel", "arbitrary")),
    )(*gdn_in)

    hgrn_in = [proj3, proj3, proj3, proj3, lb, hgrn_nw]
    hgrn_specs = [col_spec(width, 4), col_spec(width, 5), col_spec(width, 6), col_spec(width, 7),
                  const_spec(lb.shape), const_spec(hgrn_nw.shape)]
    if has_init:
        hgrn_in += [s_hgrn0]
        hgrn_specs += [init_state_spec]
    o_hgrn, s_hgrn = pl.pallas_call(
        functools.partial(_hgrn_kernel, nb=nb, sl=sl, n_valid=n_valid, has_init=has_init, mode=mode,
                          head_group=head_group),
        grid=grid,
        in_specs=hgrn_specs,
        out_specs=[out_spec(width), state_spec],
        out_shape=[jax.ShapeDtypeStruct((nseq, length, width), out_dtype), state_shape],
        scratch_shapes=[pltpu.VMEM((nb, N_HEADS, D_HEAD, D_HEAD), F32)],
        compiler_params=_params(("parallel", "arbitrary")),
    )(*hgrn_in)
    return o_gdn, o_hgrn, s_gdn, s_hgrn


def _outproj_router_kernel(oap_ref, obp_ref, hpp_ref, oas_ref, obs_ref, hps_ref, wo_ref, n2_ref, wr_ref, br_ref,
                           hp2_ref, xn2_ref, ri_ref, rw_ref, cnt_ref, carry_scr, *, tm, n_p):
    i = pl.program_id(0)

    @pl.when(i == 0)
    def _():
        carry_scr[...] = jnp.zeros_like(carry_scr)

    body = functools.partial(_outproj_router_tile, wo_ref=wo_ref, n2_ref=n2_ref, wr_ref=wr_ref, br_ref=br_ref,
                             hp2_ref=hp2_ref, xn2_ref=xn2_ref, ri_ref=ri_ref, rw_ref=rw_ref,
                             carry_scr=carry_scr, tm=tm)

    @pl.when(i < n_p)
    def _():
        body(oap_ref[...], obp_ref[...], hpp_ref[...])

    @pl.when(i >= n_p)
    def _():
        body(oas_ref[...], obs_ref[...], hps_ref[...])

    @pl.when(i == pl.num_programs(0) - 1)
    def _():
        cnt_ref[...] = carry_scr[...]


def _outproj_router_tile(oa, ob, hp, *, wo_ref, n2_ref, wr_ref, br_ref, hp2_ref, xn2_ref, ri_ref, rw_ref,
                         carry_scr, tm):
    half = oa.shape[-1]
    mix = _dot(oa.astype(BF16), wo_ref[:half, :]) + _dot(ob.astype(BF16), wo_ref[half:, :])
    hp2 = hp + mix
    hp2_ref[...] = hp2
    xn2 = hp2 * lax.rsqrt(jnp.mean(hp2 * hp2, axis=-1, keepdims=True) + EPS) * n2_ref[...]
    for cidx in range(xn2_ref.shape[1]):
        xn2_ref[:, cidx, :] = xn2[:, cidx * LANES:(cidx + 1) * LANES]
    logits = _dot(xn2, wr_ref[...], "bf16x3") + br_ref[...]

    lane = lax.broadcasted_iota(jnp.int32, (tm, LANES), 1)
    lane_f = lane.astype(F32)
    far = float(4 * LANES)
    is_g = (lane >= N_EXPERTS) & (lane < N_EXPERTS + N_GROUPS)
    lg = jnp.where(is_g, logits, -jnp.inf)
    gmax = jnp.max(lg, axis=-1, keepdims=True)
    gsel = jnp.min(jnp.where(lg == gmax, lane_f, far), axis=-1, keepdims=True).astype(jnp.int32) - N_EXPERTS
    p_top = 1.0 / jnp.sum(jnp.where(is_g, jnp.exp(logits - gmax), 0.0), axis=-1, keepdims=True)
    in_grp = (lane < N_EXPERTS) & (lax.shift_right_logical(lane, 3) == gsel)
    le = jnp.where(in_grp, logits, -jnp.inf)
    m1 = jnp.max(le, axis=-1, keepdims=True)
    i1 = jnp.min(jnp.where(le == m1, lane_f, far), axis=-1, keepdims=True).astype(jnp.int32)
    le2 = jnp.where(lane == i1, -jnp.inf, le)
    m2 = jnp.max(le2, axis=-1, keepdims=True)
    i2 = jnp.min(jnp.where(le2 == m2, lane_f, far), axis=-1, keepdims=True).astype(jnp.int32)
    e2 = jnp.exp(m2 - m1)
    w1 = p_top / (1.0 + e2)
    w2 = p_top * e2 / (1.0 + e2)

    onehot = (lane == i1) | (lane == i2)
    onehot_f = jnp.where(onehot, 1.0, 0.0)
    tri = (lax.broadcasted_iota(jnp.int32, (tm, tm), 1) < lax.broadcasted_iota(jnp.int32, (tm, tm), 0))
    before = _dot(jnp.where(tri, 1.0, 0.0).astype(BF16), onehot_f.astype(BF16)) + carry_scr[...]
    r1 = jnp.sum(jnp.where(lane == i1, before, 0.0), axis=-1, keepdims=True).astype(jnp.int32)
    r2 = jnp.sum(jnp.where(lane == i2, before, 0.0), axis=-1, keepdims=True).astype(jnp.int32)
    carry_scr[...] = carry_scr[...] + jnp.sum(onehot_f, axis=0, keepdims=True)

    ri_ref[...] = jnp.where(lane == 0, i1, jnp.where(lane == 1, i2, jnp.where(lane == 2, r1, jnp.where(lane == 3, r2, 0))))
    rw_ref[...] = jnp.where(lane == 0, w1, jnp.where(lane == 1, w2, 0.0))


def _outproj_router(oa_p, ob_p, hp_p, oa_s, ob_s, hp_s, w_out, norm2_w, w_r, b_r, tm):
    (tp, d), ts = hp_p.shape, hp_s.shape[0]
    half = oa_p.shape[1]
    n_p, n_s = tp // tm, ts // tm
    t = tp + ts
    prow = lambda w: pl.BlockSpec((tm, w), lambda i: (jnp.minimum(i, n_p - 1), 0))
    srow = lambda w: pl.BlockSpec((tm, w), lambda i: (jnp.maximum(i - n_p, 0), 0))
    row = lambda w: pl.BlockSpec((tm, w), lambda i: (i, 0))
    const = lambda shape: pl.BlockSpec(shape, lambda i: (0,) * len(shape))
    return pl.pallas_call(
        functools.partial(_outproj_router_kernel, tm=tm, n_p=n_p),
        grid=(n_p + n_s,),
        in_specs=[prow(half), prow(half), prow(d), srow(half), srow(half), srow(d),
                  const(w_out.shape), const((1, d)), const(w_r.shape), const((1, LANES))],
        out_specs=[row(d), pl.BlockSpec((tm, d // LANES, LANES), lambda i: (i, 0, 0)), row(LANES), row(LANES),
                   const((1, LANES))],
        out_shape=[jax.ShapeDtypeStruct((t, d), F32), jax.ShapeDtypeStruct((t, d // LANES, LANES), F32),
                   jax.ShapeDtypeStruct((t, LANES), jnp.int32), jax.ShapeDtypeStruct((t, LANES), F32),
                   jax.ShapeDtypeStruct((1, LANES), F32)],
        scratch_shapes=[pltpu.VMEM((1, LANES), F32)],
        compiler_params=_params(("arbitrary",)),
    )(oa_p, ob_p, hp_p, oa_s, ob_s, hp_s, w_out, norm2_w.reshape(1, d), w_r, b_r)


def _start_row_gather(idx_ref, base, n, src_hbm, dst, sem):
    def issue(r, carry):
        pltpu.make_async_copy(src_hbm.at[idx_ref[base + r]], dst.at[r], sem).start()
        return carry
    lax.fori_loop(0, n, issue, 0, unroll=8)


def _wait_row_gather(n, src_hbm, dst, sem):
    pltpu.make_async_copy(src_hbm.at[pl.ds(0, n)], dst, sem).wait()


def _moe_kernel(te_ref, st_ref, nu_ref, x_hbm, wg_ref, wu_ref, wd_ref, o_ref,
                xbuf, wg_scr, wu_scr, wd_scr, sems, *, tile):
    i = pl.program_id(0)
    n_used = nu_ref[0]
    slot = lax.rem(i, 2)

    @pl.when((i == 0) & (n_used > 0))
    def _():
        _start_row_gather(st_ref, 0, tile, x_hbm, xbuf.at[0], sems.at[0])

    @pl.when(i < n_used)
    def _():
        @pl.when(i + 1 < n_used)
        def _():
            _start_row_gather(st_ref, (i + 1) * tile, tile, x_hbm, xbuf.at[1 - slot], sems.at[1 - slot])

        @pl.when((i == 0) | (te_ref[i] != te_ref[jnp.maximum(i - 1, 0)]))
        def _():
            wg_scr[...] = wg_ref[0].astype(BF16)
            wu_scr[...] = wu_ref[0].astype(BF16)
            wd_scr[...] = wd_ref[0].astype(BF16)

        _wait_row_gather(tile, x_hbm, xbuf.at[slot], sems.at[slot])
        xb = xbuf.at[slot]
        n_c = xb.shape[1]
        g = u = None
        for c in range(0, n_c, 2):
            lhs = jnp.concatenate([xb[:, c, :], xb[:, c + 1, :]], axis=1).astype(BF16)
            rows = slice(c * LANES, (c + 2) * LANES)
            pg = _dot(lhs, wg_scr[rows, :])
            pu = _dot(lhs, wu_scr[rows, :])
            g = pg if g is None else g + pg
            u = pu if u is None else u + pu
        y = _dot((_silu(g) * u).astype(BF16), wd_scr[...])
        for c in range(n_c):
            o_ref[:, c, :] = y[:, c * LANES:(c + 1) * LANES]

    @pl.when(i >= n_used)
    def _():
        o_ref[...] = jnp.zeros_like(o_ref)


def _moe(xn2, w_gate, w_up, w_down, tile_expert, slot_token, n_used, tile):
    _, n_c, _ = xn2.shape
    d = n_c * LANES
    n_tiles = tile_expert.shape[0]
    de = w_gate.shape[2]
    grid_spec = pltpu.PrefetchScalarGridSpec(
        num_scalar_prefetch=3,
        grid=(n_tiles,),
        in_specs=[pl.BlockSpec(memory_space=pl.ANY),
                  pl.BlockSpec((1, d, de), lambda i, te, st, nu: (te[i], 0, 0)),
                  pl.BlockSpec((1, d, de), lambda i, te, st, nu: (te[i], 0, 0)),
                  pl.BlockSpec((1, de, d), lambda i, te, st, nu: (te[i], 0, 0))],
        out_specs=pl.BlockSpec((tile, n_c, LANES), lambda i, te, st, nu: (i, 0, 0)),
        scratch_shapes=[pltpu.VMEM((2, tile, n_c, LANES), F32), pltpu.VMEM((d, de), BF16),
                        pltpu.VMEM((d, de), BF16), pltpu.VMEM((de, d), BF16), pltpu.SemaphoreType.DMA((2,))],
    )
    return pl.pallas_call(
        functools.partial(_moe_kernel, tile=tile),
        grid_spec=grid_spec,
        out_shape=jax.ShapeDtypeStruct((n_tiles * tile, n_c, LANES), F32),
        compiler_params=_params(("arbitrary",)),
    )(tile_expert, slot_token, n_used, xn2, w_gate, w_up, w_down)


def _combine_kernel(p0_ref, p1_ref, ys_hbm, hp2_ref, rw_ref, fw_ref, o_ref, buf0, buf1, sems, *, tm, tile_off):
    i = pl.program_id(0)
    slot = lax.rem(i, 2)

    def start(step, s):
        base = (step + tile_off) * tm
        _start_row_gather(p0_ref, base, tm, ys_hbm, buf0.at[s], sems.at[0, s])
        _start_row_gather(p1_ref, base, tm, ys_hbm, buf1.at[s], sems.at[1, s])

    @pl.when(i == 0)
    def _():
        start(0, 0)

    @pl.when(i + 1 < pl.num_programs(0))
    def _():
        start(i + 1, 1 - slot)

    _wait_row_gather(tm, ys_hbm, buf0.at[slot], sems.at[0, slot])
    _wait_row_gather(tm, ys_hbm, buf1.at[slot], sems.at[1, slot])
    b0 = buf0.at[slot]
    b1 = buf1.at[slot]
    rw = rw_ref[...]
    w0 = rw[:, 0:1]
    w1 = rw[:, 1:2]
    n_c = b0.shape[1]
    ssq = jnp.zeros((tm, 1), F32)
    for c in range(n_c):
        cols = slice(c * LANES, (c + 1) * LANES)
        y = hp2_ref[:, cols] + w0 * b0[:, c, :] + w1 * b1[:, c, :]
        ssq = ssq + jnp.sum(y * y, axis=-1, keepdims=True)
        o_ref[:, cols] = y
    scale = lax.rsqrt(ssq * (1.0 / (n_c * LANES)) + EPS)
    for c in range(n_c):
        cols = slice(c * LANES, (c + 1) * LANES)
        o_ref[:, cols] = o_ref[:, cols] * scale * fw_ref[:, cols]


def _combine(ys, hp2, route_w, final_w, pos0, pos1, tm, tile_off, n_tiles):
    _, d = hp2.shape
    n_c = ys.shape[1]
    grid_spec = pltpu.PrefetchScalarGridSpec(
        num_scalar_prefetch=2,
        grid=(n_tiles,),
        in_specs=[pl.BlockSpec(memory_space=pl.ANY),
                  pl.BlockSpec((tm, d), lambda i, p0, p1: (i + tile_off, 0)),
                  pl.BlockSpec((tm, LANES), lambda i, p0, p1: (i + tile_off, 0)),
                  pl.BlockSpec((1, d), lambda i, p0, p1: (0, 0))],
        out_specs=pl.BlockSpec((tm, d), lambda i, p0, p1: (i, 0)),
        scratch_shapes=[pltpu.VMEM((2, tm, n_c, LANES), F32), pltpu.VMEM((2, tm, n_c, LANES), F32),
                        pltpu.SemaphoreType.DMA((2, 2))],
    )
    return pl.pallas_call(
        functools.partial(_combine_kernel, tm=tm, tile_off=tile_off),
        grid_spec=grid_spec,
        out_shape=jax.ShapeDtypeStruct((n_tiles * tm, d), F32),
        compiler_params=_params(("arbitrary",)),
    )(pos0, pos1, ys, hp2, route_w, final_w.reshape(1, d))


def kernel(x_prompt, x_sample, state_gdn, state_conv, state_hgrn, meta_tokens, norm1_w, w_in, conv_w, a_log,
           dt_bias, gdn_norm_w, lb_logits, hgrn_norm_w, w_out, norm2_w, w_router_group, b_router_group,
           w_router_expert, b_router_expert, w_gate, w_up, w_down, final_norm_w):
    bp, seq, d = x_prompt.shape
    bs, dec_seq, _ = x_sample.shape
    assert w_in.shape[0] == 1, "single-layer trunk"
    width = N_HEADS * D_HEAD
    conv_ch = 3 * width
    tile = 256
    tn = 512
    sl_s = SUBLANES
    nb_s = CHUNK // sl_s
    assert seq % CHUNK == 0 and N_META <= CHUNK and dec_seq <= sl_s and bs % nb_s == 0
    tp, ts, ts_pad = bp * seq, bs * dec_seq, bs * sl_s
    t_small = ts_pad + CHUNK
    tm = _row_tile(math.gcd(tp, ts), 256)
    tm_p = _row_tile(tp, 1024)

    xp = x_prompt.reshape(tp, d)
    x_small = jnp.concatenate([jnp.pad(x_sample, ((0, 0), (0, sl_s - dec_seq), (0, 0))).reshape(ts_pad, d),
                               jnp.zeros((CHUNK - N_META, d), F32), meta_tokens.astype(F32)], axis=0)

    wi = w_in[0]
    n_a = 4 * width // tn
    w_b = wi[:, 4 * width + 2 * N_HEADS:]
    w_ba = jnp.concatenate([wi[:, 4 * width:4 * width + 2 * N_HEADS],
                            jnp.zeros((d, LANES - 2 * N_HEADS), F32)], axis=1).astype(BF16)
    n_cols = 8 * width

    xn_p = _rmsnorm(xp, norm1_w[0], BF16, _row_tile(tp, 512))
    xn_s = _rmsnorm(x_small, norm1_w[0], BF16, _row_tile(t_small, 1024))
    proj_p = _inproj(xn_p, wi, n_a, w_b, tm_p, tn)
    proj_s = _inproj(xn_s, wi, n_a, w_b, t_small, tn)
    ba_p = _matmul(xn_p, w_ba, tm_p, LANES)
    ba_s = _matmul(xn_s, w_ba, t_small, LANES)

    pvec = jnp.zeros((2, LANES), F32)
    pvec = pvec.at[0, N_HEADS:2 * N_HEADS].set(a_log[0]).at[1, N_HEADS:2 * N_HEADS].set(dt_bias[0])
    lb = jnp.cumsum(jax.nn.softmax(lb_logits.astype(F32), axis=0), axis=0)[0].reshape(1, width)
    gdn_nw = gdn_norm_w[0].reshape(1, D_HEAD)
    hgrn_nw = hgrn_norm_w[0].reshape(1, D_HEAD)
    cw = conv_w[0]
    mix_args = (cw, pvec, gdn_nw, lb, hgrn_nw)

    _, _, sg_m, sh_m = _mixers(proj_s.reshape(t_small // CHUNK, CHUNK, n_cols),
                               ba_s.reshape(t_small // CHUNK, CHUNK, LANES), ts_pad // CHUNK, 1, 1, CHUNK, CHUNK,
                               *mix_args, None, None, None, False, F32, "bf16", "bf16x3")
    conv_m = proj_s[t_small - SUBLANES:, :conv_ch].reshape(1, SUBLANES, conv_ch)
    oa_p, ob_p, sg_p, sh_p = _mixers(proj_p.reshape(bp, seq, n_cols), ba_p.reshape(bp, seq, LANES), 0, bp, 1,
                                     CHUNK, CHUNK, *mix_args, sg_m, conv_m, sh_m, True, BF16, "bf16", "bf16x3")
    conv0 = jnp.pad(state_conv[0], ((0, 0), (SUBLANES - (CONV_W - 1), 0), (0, 0)))
    oa_s, ob_s, sg_s, sh_s = _mixers(proj_s.reshape(t_small // sl_s, sl_s, n_cols),
                                     ba_s.reshape(t_small // sl_s, sl_s, LANES), 0, bs, nb_s, sl_s, dec_seq,
                                     *mix_args, state_gdn[0], conv0, state_hgrn[0], False, F32, "bf16", "bf16x3")

    w_r = jnp.concatenate([w_router_expert[0], w_router_group[0],
                           jnp.zeros((d, LANES - N_EXPERTS - N_GROUPS), F32)], axis=1)
    b_r = jnp.concatenate([b_router_expert[0], b_router_group[0],
                           jnp.zeros((LANES - N_EXPERTS - N_GROUPS,), F32)]).reshape(1, LANES)
    t = tp + ts
    hp2, xn2, route_i, route_w, counts = _outproj_router(
        oa_p.reshape(tp, width), ob_p.reshape(tp, width), xp,
        oa_s[:, :dec_seq].reshape(ts, width), ob_s[:, :dec_seq].reshape(ts, width), x_sample.reshape(ts, d),
        w_out[0].astype(BF16), norm2_w[0], w_r, b_r, tm)

    cnt = counts[0, :N_EXPERTS].astype(jnp.int32)
    padded = (cnt + tile - 1) // tile * tile
    ends = jnp.cumsum(padded)
    offs = ends - padded
    eid = route_i[:, 0:2]
    pos = offs[eid] + route_i[:, 2:4]
    n_tiles = (2 * t) // tile + N_EXPERTS
    tile_start = jnp.arange(n_tiles, dtype=jnp.int32) * tile
    tile_expert = jnp.minimum(jnp.sum((ends[None, :] <= tile_start[:, None]).astype(jnp.int32), axis=1),
                              N_EXPERTS - 1)
    n_used = (ends[-1] // tile).astype(jnp.int32).reshape(1)
    tok = jnp.broadcast_to(jnp.arange(t, dtype=jnp.int32)[:, None], (t, 2))
    slot_token = jnp.zeros((n_tiles * tile,), jnp.int32).at[pos.reshape(-1)].set(tok.reshape(-1))

    ys = _moe(xn2, w_gate[0], w_up[0], w_down[0], tile_expert, slot_token, n_used, tile)
    pos0, pos1 = pos[:, 0], pos[:, 1]
    y_prompt = _combine(ys, hp2, route_w, final_norm_w, pos0, pos1, tm, 0, tp // tm).reshape(bp, seq, d)
    y_sample = _combine(ys, hp2, route_w, final_norm_w, pos0, pos1, tm, tp // tm, ts // tm).reshape(bs, dec_seq, d)

    conv_p = proj_p.reshape(bp, seq, n_cols)[:, seq - (CONV_W - 1):, :conv_ch]
    u_s = proj_s[:ts_pad].reshape(bs, sl_s, n_cols)[:, :dec_seq, :conv_ch]
    conv_s = jnp.concatenate([state_conv[0], u_s], axis=1)[:, dec_seq:]
    return (y_prompt, y_sample, sg_p[None], conv_p[None], sh_p[None], sg_s[None], conv_s[None], sh_s[None])
```

```python
import functools
import math

import jax
import jax.numpy as jnp
from jax import lax
from jax.experimental import pallas as pl
from jax.experimental.pallas import tpu as pltpu

F32 = jnp.float32
BF16 = jnp.bfloat16
HIGHEST = lax.Precision.HIGHEST

EPS = 1e-6
N_META = 16
CONV_W = 4
N_HEADS = 8
D_HEAD = 128
N_GROUPS = 4
EXPERTS_PER_GROUP = 8
N_EXPERTS = N_GROUPS * EXPERTS_PER_GROUP

LANES = 128
SUBLANES = 8
CHUNK = 64
SUB = 16
VMEM_LIMIT = 56 * 1024 * 1024
NEG_BIG = -1e30


def _sigmoid(x):
    return 1.0 / (1.0 + jnp.exp(-x))


def _silu(x):
    return x * _sigmoid(x)


def _softplus(x):
    return jnp.maximum(x, 0.0) + jnp.log1p(jnp.exp(-jnp.abs(x)))


def _split_bf16(a, pieces):
    out = []
    for _ in range(pieces - 1):
        hi = a.astype(BF16)
        out.append(hi)
        a = a - hi.astype(F32)
    out.append(a.astype(BF16))
    return out


def _mm(a, b, dims, mode):
    dg = functools.partial(lax.dot_general, dimension_numbers=(dims, ((), ())), preferred_element_type=F32)
    if mode == "f32":
        return dg(a, b, precision=HIGHEST)
    if mode == "bf16":
        return dg(a.astype(BF16), b.astype(BF16))
    assert mode == "bf16x3"
    ah, al = _split_bf16(a, 2)
    bh, bl = _split_bf16(b, 2)
    return dg(ah, bh) + dg(ah, bl) + dg(al, bh)


def _dot(a, b, mode="bf16"):
    return _mm(a, b, ((1,), (0,)), mode)


def _dot_nt(a, b, mode="bf16"):
    return _mm(a, b, ((1,), (1,)), mode)


def _dot_tn(a, b, mode="bf16"):
    return _mm(a, b, ((0,), (0,)), mode)


def _masked_cumsum(lmask, x):
    lm = lmask.astype(BF16)
    return sum(lax.dot_general(lm, p, (((1,), (0,)), ((), ())), preferred_element_type=F32)
               for p in _split_bf16(x, 3))


def _params(sem):
    return pltpu.CompilerParams(dimension_semantics=sem, vmem_limit_bytes=VMEM_LIMIT)


def _row_tile(n, target):
    best = max(c for c in range(16, min(n, target) + 1, 16) if n % c == 0)
    return best


def _rmsnorm_kernel(x_ref, w_ref, o_ref):
    x = x_ref[...]
    ms = jnp.mean(x * x, axis=-1, keepdims=True)
    o_ref[...] = (x * lax.rsqrt(ms + EPS) * w_ref[...]).astype(o_ref.dtype)


def _rmsnorm(x, w, out_dtype, tm):
    t, d = x.shape
    return pl.pallas_call(
        _rmsnorm_kernel,
        grid=(t // tm,),
        in_specs=[pl.BlockSpec((tm, d), lambda i: (i, 0)), pl.BlockSpec((1, d), lambda i: (0, 0))],
        out_specs=pl.BlockSpec((tm, d), lambda i: (i, 0)),
        out_shape=jax.ShapeDtypeStruct((t, d), out_dtype),
        compiler_params=_params(("parallel",)),
    )(x, w.reshape(1, d))


def _inproj_kernel(x_ref, wa_ref, wb_ref, o_ref, w_scr, *, n_a):
    j = pl.program_id(0)
    i = pl.program_id(1)

    @pl.when((i == 0) & (j < n_a))
    def _():
        w_scr[...] = wa_ref[...].astype(BF16)

    @pl.when((i == 0) & (j >= n_a))
    def _():
        w_scr[...] = wb_ref[...].astype(BF16)

    o_ref[...] = _dot(x_ref[...], w_scr[...])


def _inproj(x, w_a, n_a, w_b, tm, tn):
    t, k = x.shape
    n_b = w_b.shape[1] // tn
    return pl.pallas_call(
        functools.partial(_inproj_kernel, n_a=n_a),
        grid=(n_a + n_b, t // tm),
        in_specs=[pl.BlockSpec((tm, k), lambda j, i: (i, 0)),
                  pl.BlockSpec((k, tn), lambda j, i: (0, jnp.minimum(j, n_a - 1))),
                  pl.BlockSpec((k, tn), lambda j, i: (0, jnp.maximum(j - n_a, 0)))],
        out_specs=pl.BlockSpec((tm, tn), lambda j, i: (i, j)),
        out_shape=jax.ShapeDtypeStruct((t, (n_a + n_b) * tn), F32),
        scratch_shapes=[pltpu.VMEM((k, tn), BF16)],
        compiler_params=_params(("arbitrary", "arbitrary")),
    )(x, w_a, w_b)


def _matmul_kernel(x_ref, w_ref, o_ref):
    o_ref[...] = _dot(x_ref[...], w_ref[...])


def _matmul(x, w, tm, tn):
    t, k = x.shape
    n = w.shape[1]
    return pl.pallas_call(
        _matmul_kernel,
        grid=(n // tn, t // tm),
        in_specs=[pl.BlockSpec((tm, k), lambda j, i: (i, 0)), pl.BlockSpec((k, tn), lambda j, i: (0, j))],
        out_specs=pl.BlockSpec((tm, tn), lambda j, i: (i, j)),
        out_shape=jax.ShapeDtypeStruct((t, n), F32),
        compiler_params=_params(("parallel", "arbitrary")),
    )(x, w)


def _chunk_masks(nb, sl):
    r = nb * sl
    shift = int(math.log2(sl))
    ri = lax.broadcasted_iota(jnp.int32, (r, r), 0)
    ci = lax.broadcasted_iota(jnp.int32, (r, r), 1)
    same = lax.shift_right_logical(ri, shift) == lax.shift_right_logical(ci, shift)
    return same & (ci <= ri), same & (ci < ri)


def _row_valid(nb, sl, n_valid):
    rowid = lax.broadcasted_iota(jnp.int32, (nb * sl, 1), 0)
    return (rowid & (sl - 1)) < n_valid


def _last_row_bcast(x, nb, sl):
    c = x.shape[-1]
    x3 = x.reshape(nb, sl, c)
    return jnp.broadcast_to(x3[:, sl - 1:sl, :], (nb, sl, c)).reshape(nb * sl, c)


def _gated_rmsnorm(o, w, gate):
    return o * lax.rsqrt(jnp.mean(o * o, axis=-1, keepdims=True) + EPS) * w * _silu(gate)


def _gdn_kernel(*refs, nb, sl, n_valid, has_init, mode, inv_mode, head_group):
    if has_init:
        (qkv_ref, z_ref, ba_ref, cw_ref, pv_ref, nw_ref, s0_ref, c0_ref,
         o_ref, sout_ref, s_scr, carry_scr) = refs
    else:
        (qkv_ref, z_ref, ba_ref, cw_ref, pv_ref, nw_ref,
         o_ref, sout_ref, s_scr, carry_scr) = refs
    c = pl.program_id(1)
    r = nb * sl
    dh = D_HEAD

    @pl.when(c == 0)
    def _():
        if has_init:
            s_scr[...] = s0_ref[...]
            carry_scr[...] = c0_ref[...]
        else:
            s_scr[...] = jnp.zeros_like(s_scr)
            carry_scr[...] = jnp.zeros_like(carry_scr)

    incl, strict = _chunk_masks(nb, sl)
    lmask = jnp.where(incl, 1.0, 0.0)
    valid = _row_valid(nb, sl, n_valid)
    masked = n_valid < sl
    rowid = lax.broadcasted_iota(jnp.int32, (r, 1), 0)
    row_seq = lax.shift_right_logical(rowid, int(math.log2(sl)))
    n_sq = int(math.log2(sl)) - 1

    ba = ba_ref[...].reshape(r, LANES)
    pv = pv_ref[...]
    beta_all = _sigmoid(ba)
    g_all = -jnp.exp(pv[0:1]) * _softplus(ba + pv[1:2])
    if masked:
        g_all = jnp.where(valid, g_all, 0.0)
    gcum = _masked_cumsum(lmask, g_all)
    gcum_t = gcum.T
    glast = _last_row_bcast(gcum, nb, sl)
    cw = cw_ref[...]
    nw = nw_ref[...]

    def conv_slice(c0):
        u = qkv_ref[:, :, c0:c0 + dh]
        prev = carry_scr[:, :, c0:c0 + dh]
        full = jnp.concatenate([prev, u], axis=1)
        acc = None
        for j in range(CONV_W):
            off = SUBLANES - (CONV_W - 1) + j
            term = full[:, off:off + sl, :] * cw[j:j + 1, c0:c0 + dh]
            acc = term if acc is None else acc + term
        return _silu(acc).reshape(r, dh)

    for h0 in range(0, N_HEADS, head_group):
        heads = range(h0, h0 + head_group)
        qs, ks, vs, bcs, gcs, gls, egs, decays = [], [], [], [], [], [], [], []
        for h in heads:
            q = conv_slice(h * dh)
            k = conv_slice(N_HEADS * dh + h * dh)
            v = conv_slice(2 * N_HEADS * dh + h * dh)
            q = q * lax.rsqrt(jnp.sum(q * q, axis=-1, keepdims=True) + EPS) * (dh ** -0.5)
            k = k * lax.rsqrt(jnp.sum(k * k, axis=-1, keepdims=True) + EPS)
            if masked:
                q = jnp.where(valid, q, 0.0)
                k = jnp.where(valid, k, 0.0)
                v = jnp.where(valid, v, 0.0)
            gc = gcum[:, N_HEADS + h:N_HEADS + h + 1]
            gr = gcum_t[N_HEADS + h:N_HEADS + h + 1, :]
            qs.append(q)
            ks.append(k)
            vs.append(v)
            bcs.append(beta_all[:, h:h + 1])
            gcs.append(gc)
            gls.append(glast[:, N_HEADS + h:N_HEADS + h + 1])
            egs.append(jnp.exp(gc))
            decays.append(jnp.exp(jnp.where(incl, gc - gr, NEG_BIG)))
        n = len(qs)
        qk_kk = [_dot_nt(jnp.concatenate([qs[i], ks[i]], axis=0), ks[i], mode) for i in range(n)]
        tm1 = [jnp.where(strict, -(bcs[i] * qk_kk[i][r:] * decays[i]), 0.0) for i in range(n)]
        pw = list(tm1)
        for _ in range(n_sq):
            pw = [_dot(pw[i], pw[i], inv_mode) for i in range(n)]
            tm1 = [tm1[i] + pw[i] + _dot(tm1[i], pw[i], inv_mode) for i in range(n)]
        rhs = [jnp.concatenate([vs[i] * bcs[i], ks[i] * (bcs[i] * egs[i])], axis=1) for i in range(n)]
        uw = [rhs[i] + _dot(tm1[i], rhs[i], mode) for i in range(n)]
        vnew, ointer = [], []
        for i, h in enumerate(heads):
            u = uw[i][:, :dh]
            w = uw[i][:, dh:]
            qe = qs[i] * egs[i]
            vnew_parts, ointer_parts = [], []
            for b in range(nb):
                rows = slice(b * sl, (b + 1) * sl)
                ws = _dot(jnp.concatenate([w[rows], qe[rows]], axis=0), s_scr[b, h], mode)
                vnew_parts.append(u[rows] - ws[:sl])
                ointer_parts.append(ws[sl:])
            vnew.append(vnew_parts[0] if nb == 1 else jnp.concatenate(vnew_parts, axis=0))
            ointer.append(ointer_parts[0] if nb == 1 else jnp.concatenate(ointer_parts, axis=0))
        for i, h in enumerate(heads):
            attn = jnp.where(incl, qk_kk[i][:r] * decays[i], 0.0)
            o = ointer[i] + _dot(attn, vnew[i], mode)
            z = z_ref[:, :, h * dh:(h + 1) * dh].reshape(r, dh)
            o_ref[:, :, h * dh:(h + 1) * dh] = _gated_rmsnorm(o, nw, z).reshape(nb, sl, dh).astype(o_ref.dtype)
        for i, h in enumerate(heads):
            ktil = ks[i] * jnp.exp(gls[i] - gcs[i])
            for b in range(nb):
                kt_b = ktil if nb == 1 else jnp.where(row_seq == b, ktil, 0.0)
                gl_b = gls[i][b * sl:b * sl + 1, :]
                s_scr[b, h] = s_scr[b, h] * jnp.exp(gl_b) + _dot_tn(kt_b, vnew[i], mode)

    carry_scr[...] = qkv_ref[:, sl - SUBLANES:sl, :]

    @pl.when(c == pl.num_programs(1) - 1)
    def _():
        sout_ref[...] = s_scr[...]


def _hgrn_kernel(*refs, nb, sl, n_valid, has_init, mode, head_group):
    if has_init:
        (q_ref, f_ref, i_ref, g_ref, lb_ref, nw_ref, s0_ref, o_ref, sout_ref, s_scr) = refs
    else:
        (q_ref, f_ref, i_ref, g_ref, lb_ref, nw_ref, o_ref, sout_ref, s_scr) = refs
    c = pl.program_id(1)
    r = nb * sl
    dh = D_HEAD
    width = N_HEADS * dh

    @pl.when(c == 0)
    def _():
        if has_init:
            s_scr[...] = s0_ref[...]
        else:
            s_scr[...] = jnp.zeros_like(s_scr)

    incl, _ = _chunk_masks(nb, sl)
    lmask = jnp.where(incl, 1.0, 0.0)
    valid = _row_valid(nb, sl, n_valid)
    masked = n_valid < sl
    rowid = lax.broadcasted_iota(jnp.int32, (r, 1), 0)
    row_seq = lax.shift_right_logical(rowid, int(math.log2(sl)))
    sub = min(SUB, sl)
    nblk = r // sub
    sub_shift = int(math.log2(sub))
    ri = lax.broadcasted_iota(jnp.int32, (r, r), 0)
    ci = lax.broadcasted_iota(jnp.int32, (r, r), 1)
    same_blk = lax.shift_right_logical(ri, sub_shift) == lax.shift_right_logical(ci, sub_shift)
    diag_mask = incl & same_blk
    cross_mask = incl & jnp.logical_not(same_blk)

    lb = lb_ref[...]
    f = lb + (1.0 - lb) * _sigmoid(f_ref[...].reshape(r, width))
    lf = jnp.log(f)
    k_all = 1.0 - f
    if masked:
        lf = jnp.where(valid, lf, 0.0)
        k_all = jnp.where(valid, k_all, 0.0)
    bcum = _masked_cumsum(lmask, lf)
    nw = nw_ref[...]

    def head_inputs(h):
        cols = slice(h * dh, (h + 1) * dh)
        q = _silu(q_ref[:, :, cols].reshape(r, dh)) * (dh ** -0.5)
        v = i_ref[:, :, cols].reshape(r, dh)
        if masked:
            v = jnp.where(valid, v, 0.0)
        return q, k_all[:, cols], v, bcum[:, cols]

    def intra_attn(q, k, bh):
        bmid = jnp.broadcast_to(bh.reshape(nblk, sub, dh)[:, sub // 2:sub // 2 + 1, :],
                                (nblk, sub, dh)).reshape(r, dh)
        attn = jnp.where(diag_mask, _dot_nt(q * jnp.exp(bh - bmid), k * jnp.exp(bmid - bh), mode), 0.0)
        if sl > sub:
            parts = [jnp.zeros((sub, r), F32)]
            for blk in range(1, nblk):
                start = blk * sub
                bref = bh[start - 1:start, :]
                qc = q[start:start + sub] * jnp.exp(bh[start:start + sub] - bref)
                kc = k * jnp.exp(jnp.minimum(bref - bh, 0.0))
                parts.append(_dot_nt(qc, kc, mode))
            attn = attn + jnp.where(cross_mask, jnp.concatenate(parts, axis=0), 0.0)
        return attn

    for h0 in range(0, N_HEADS, head_group):
        heads = range(h0, h0 + head_group)
        ins = [head_inputs(h) for h in heads]
        attns = [intra_attn(q, k, bh) for (q, k, v, bh) in ins]
        for i, h in enumerate(heads):
            q, k, v, bh = ins[i]
            qe = q * jnp.exp(bh)
            ointer_parts = []
            for b in range(nb):
                rows = slice(b * sl, (b + 1) * sl)
                ointer_parts.append(_dot(qe[rows], s_scr[b, h], mode))
            ointer = ointer_parts[0] if nb == 1 else jnp.concatenate(ointer_parts, axis=0)
            o = ointer + _dot(attns[i], v, mode)
            cols = slice(h * dh, (h + 1) * dh)
            gate = g_ref[:, :, cols].reshape(r, dh)
            o_ref[:, :, cols] = _gated_rmsnorm(o, nw, gate).reshape(nb, sl, dh).astype(o_ref.dtype)
        for i, h in enumerate(heads):
            q, k, v, bh = ins[i]
            blast = _last_row_bcast(bh, nb, sl)
            ktil = k * jnp.exp(blast - bh)
            pad = [jnp.zeros((LANES - r, dh), F32)] if r < LANES else []
            tr = jnp.concatenate([blast] + pad, axis=0).T
            for b in range(nb):
                kt_b = ktil if nb == 1 else jnp.where(row_seq == b, ktil, 0.0)
                dec_col = jnp.exp(tr[:, b * sl:b * sl + 1])
                s_scr[b, h] = s_scr[b, h] * dec_col + _dot_tn(kt_b, v, mode)

    @pl.when(c == pl.num_programs(1) - 1)
    def _():
        sout_ref[...] = s_scr[...]


def _mixers(proj3, ba3, blk_off, nseq, nb, sl, n_valid, conv_w, pvec, gdn_nw, lb, hgrn_nw,
            s_gdn0, conv0, s_hgrn0, shared_init, out_dtype, mode, inv_mode):
    length = proj3.shape[1]
    has_init = s_gdn0 is not None
    width = N_HEADS * D_HEAD
    conv_ch = 3 * width
    grid = (nseq // nb, length // sl)
    state_spec = pl.BlockSpec((nb, N_HEADS, D_HEAD, D_HEAD), lambda g, c: (g, 0, 0, 0))
    state_shape = jax.ShapeDtypeStruct((nseq, N_HEADS, D_HEAD, D_HEAD), F32)
    init_idx = (lambda g: 0) if shared_init else (lambda g: g)
    init_state_spec = pl.BlockSpec((nb, N_HEADS, D_HEAD, D_HEAD), lambda g, c: (init_idx(g), 0, 0, 0))
    head_group = 4 if nb == 1 else 2

    def col_spec(w, idx):
        return pl.BlockSpec((nb, sl, w), lambda g, c: (g + blk_off, c, idx))

    def out_spec(w):
        return pl.BlockSpec((nb, sl, w), lambda g, c: (g, c, 0))

    def const_spec(shape):
        return pl.BlockSpec(shape, lambda g, c: (0,) * len(shape))

    gdn_in = [proj3, proj3, ba3, conv_w, pvec, gdn_nw]
    gdn_specs = [col_spec(conv_ch, 0), col_spec(width, 3), col_spec(LANES, 0),
                 const_spec(conv_w.shape), const_spec(pvec.shape), const_spec(gdn_nw.shape)]
    if has_init:
        gdn_in += [s_gdn0, conv0]
        gdn_specs += [init_state_spec, pl.BlockSpec((nb, SUBLANES, conv_ch), lambda g, c: (init_idx(g), 0, 0))]
    o_gdn, s_gdn = pl.pallas_call(
        functools.partial(_gdn_kernel, nb=nb, sl=sl, n_valid=n_valid, has_init=has_init, mode=mode,
                          inv_mode=inv_mode, head_group=head_group),
        grid=grid,
        in_specs=gdn_specs,
        out_specs=[out_spec(width), state_spec],
        out_shape=[jax.ShapeDtypeStruct((nseq, length, width), out_dtype), state_shape],
        scratch_shapes=[pltpu.VMEM((nb, N_HEADS, D_HEAD, D_HEAD), F32), pltpu.VMEM((nb, SUBLANES, conv_ch), F32)],
        compiler_params=_params(("parallel", "arbitrary")),
    )(*gdn_in)

    hgrn_in = [proj3, proj3, proj3, proj3, lb, hgrn_nw]
    hgrn_specs = [col_spec(width, 4), col_spec(width, 5), col_spec(width, 6), col_spec(width, 7),
                  const_spec(lb.shape), const_spec(hgrn_nw.shape)]
    if has_init:
        hgrn_in += [s_hgrn0]
        hgrn_specs += [init_state_spec]
    o_hgrn, s_hgrn = pl.pallas_call(
        functools.partial(_hgrn_kernel, nb=nb, sl=sl, n_valid=n_valid, has_init=has_init, mode=mode,
                          head_group=head_group),
        grid=grid,
        in_specs=hgrn_specs,
        out_specs=[out_spec(width), state_spec],
        out_shape=[jax.ShapeDtypeStruct((nseq, length, width), out_dtype), state_shape],
        scratch_shapes=[pltpu.VMEM((nb, N_HEADS, D_HEAD, D_HEAD), F32)],
        compiler_params=_params(("parallel", "arbitrary")),
    )(*hgrn_in)
    return o_gdn, o_hgrn, s_gdn, s_hgrn


def _outproj_router_kernel(oap_ref, obp_ref, hpp_ref, oas_ref, obs_ref, hps_ref, wo_ref, n2_ref, wr_ref, br_ref,
                           hp2_ref, xn2_ref, ri_ref, rw_ref, cnt_ref, carry_scr, *, tm, n_p):
    i = pl.program_id(0)

    @pl.when(i == 0)
    def _():
        carry_scr[...] = jnp.zeros_like(carry_scr)

    body = functools.partial(_outproj_router_tile, wo_ref=wo_ref, n2_ref=n2_ref, wr_ref=wr_ref, br_ref=br_ref,
                             hp2_ref=hp2_ref, xn2_ref=xn2_ref, ri_ref=ri_ref, rw_ref=rw_ref,
                             carry_scr=carry_scr, tm=tm)

    @pl.when(i < n_p)
    def _():
        body(oap_ref[...], obp_ref[...], hpp_ref[...])

    @pl.when(i >= n_p)
    def _():
        body(oas_ref[...], obs_ref[...], hps_ref[...])

    @pl.when(i == pl.num_programs(0) - 1)
    def _():
        cnt_ref[...] = carry_scr[...]


def _outproj_router_tile(oa, ob, hp, *, wo_ref, n2_ref, wr_ref, br_ref, hp2_ref, xn2_ref, ri_ref, rw_ref,
                         carry_scr, tm):
    half = oa.shape[-1]
    mix = _dot(oa.astype(BF16), wo_ref[:half, :]) + _dot(ob.astype(BF16), wo_ref[half:, :])
    hp2 = hp + mix
    hp2_ref[...] = hp2
    xn2 = hp2 * lax.rsqrt(jnp.mean(hp2 * hp2, axis=-1, keepdims=True) + EPS) * n2_ref[...]
    xn2_ref[...] = xn2
    logits = _dot(xn2, wr_ref[...], "bf16x3") + br_ref[...]

    lane = lax.broadcasted_iota(jnp.int32, (tm, LANES), 1)
    lane_f = lane.astype(F32)
    far = float(4 * LANES)
    is_g = (lane >= N_EXPERTS) & (lane < N_EXPERTS + N_GROUPS)
    lg = jnp.where(is_g, logits, -jnp.inf)
    gmax = jnp.max(lg, axis=-1, keepdims=True)
    gsel = jnp.min(jnp.where(lg == gmax, lane_f, far), axis=-1, keepdims=True).astype(jnp.int32) - N_EXPERTS
    p_top = 1.0 / jnp.sum(jnp.where(is_g, jnp.exp(logits - gmax), 0.0), axis=-1, keepdims=True)
    in_grp = (lane < N_EXPERTS) & (lax.shift_right_logical(lane, 3) == gsel)
    le = jnp.where(in_grp, logits, -jnp.inf)
    m1 = jnp.max(le, axis=-1, keepdims=True)
    i1 = jnp.min(jnp.where(le == m1, lane_f, far), axis=-1, keepdims=True).astype(jnp.int32)
    le2 = jnp.where(lane == i1, -jnp.inf, le)
    m2 = jnp.max(le2, axis=-1, keepdims=True)
    i2 = jnp.min(jnp.where(le2 == m2, lane_f, far), axis=-1, keepdims=True).astype(jnp.int32)
    e2 = jnp.exp(m2 - m1)
    w1 = p_top / (1.0 + e2)
    w2 = p_top * e2 / (1.0 + e2)

    onehot = (lane == i1) | (lane == i2)
    onehot_f = jnp.where(onehot, 1.0, 0.0)
    tri = (lax.broadcasted_iota(jnp.int32, (tm, tm), 1) < lax.broadcasted_iota(jnp.int32, (tm, tm), 0))
    before = _dot(jnp.where(tri, 1.0, 0.0).astype(BF16), onehot_f.astype(BF16)) + carry_scr[...]
    r1 = jnp.sum(jnp.where(lane == i1, before, 0.0), axis=-1, keepdims=True).astype(jnp.int32)
    r2 = jnp.sum(jnp.where(lane == i2, before, 0.0), axis=-1, keepdims=True).astype(jnp.int32)
    carry_scr[...] = carry_scr[...] + jnp.sum(onehot_f, axis=0, keepdims=True)

    ri_ref[...] = jnp.where(lane == 0, i1, jnp.where(lane == 1, i2, jnp.where(lane == 2, r1, jnp.where(lane == 3, r2, 0))))
    rw_ref[...] = jnp.where(lane == 0, w1, jnp.where(lane == 1, w2, 0.0))


def _outproj_router(oa_p, ob_p, hp_p, oa_s, ob_s, hp_s, w_out, norm2_w, w_r, b_r, tm):
    (tp, d), ts = hp_p.shape, hp_s.shape[0]
    half = oa_p.shape[1]
    n_p, n_s = tp // tm, ts // tm
    t = tp + ts
    prow = lambda w: pl.BlockSpec((tm, w), lambda i: (jnp.minimum(i, n_p - 1), 0))
    srow = lambda w: pl.BlockSpec((tm, w), lambda i: (jnp.maximum(i - n_p, 0), 0))
    row = lambda w: pl.BlockSpec((tm, w), lambda i: (i, 0))
    const = lambda shape: pl.BlockSpec(shape, lambda i: (0,) * len(shape))
    return pl.pallas_call(
        functools.partial(_outproj_router_kernel, tm=tm, n_p=n_p),
        grid=(n_p + n_s,),
        in_specs=[prow(half), prow(half), prow(d), srow(half), srow(half), srow(d),
                  const(w_out.shape), const((1, d)), const(w_r.shape), const((1, LANES))],
        out_specs=[row(d), row(d), row(LANES), row(LANES), const((1, LANES))],
        out_shape=[jax.ShapeDtypeStruct((t, d), F32), jax.ShapeDtypeStruct((t, d), F32),
                   jax.ShapeDtypeStruct((t, LANES), jnp.int32), jax.ShapeDtypeStruct((t, LANES), F32),
                   jax.ShapeDtypeStruct((1, LANES), F32)],
        scratch_shapes=[pltpu.VMEM((1, LANES), F32)],
        compiler_params=_params(("arbitrary",)),
    )(oa_p, ob_p, hp_p, oa_s, ob_s, hp_s, w_out, norm2_w.reshape(1, d), w_r, b_r)


def _start_row_gather(idx_ref, base, n, src_hbm, dst, sem):
    def issue(r, carry):
        pltpu.make_async_copy(src_hbm.at[pl.ds(idx_ref[base + r], 1), :], dst.at[pl.ds(r, 1), :], sem).start()
        return carry
    lax.fori_loop(0, n, issue, 0, unroll=8)


def _wait_row_gather(n, src_hbm, dst, sem):
    pltpu.make_async_copy(src_hbm.at[pl.ds(0, n), :], dst, sem).wait()


def _moe_kernel(te_ref, st_ref, nu_ref, x_hbm, wg_ref, wu_ref, wd_ref, o_ref,
                xbuf, wg_scr, wu_scr, wd_scr, sems, *, tile):
    i = pl.program_id(0)
    n_used = nu_ref[0]
    slot = lax.rem(i, 2)

    @pl.when((i == 0) & (n_used > 0))
    def _():
        _start_row_gather(st_ref, 0, tile, x_hbm, xbuf.at[0], sems.at[0])

    @pl.when(i < n_used)
    def _():
        @pl.when(i + 1 < n_used)
        def _():
            _start_row_gather(st_ref, (i + 1) * tile, tile, x_hbm, xbuf.at[1 - slot], sems.at[1 - slot])

        @pl.when((i == 0) | (te_ref[i] != te_ref[jnp.maximum(i - 1, 0)]))
        def _():
            wg_scr[...] = wg_ref[0].astype(BF16)
            wu_scr[...] = wu_ref[0].astype(BF16)
            wd_scr[...] = wd_ref[0].astype(BF16)

        _wait_row_gather(tile, x_hbm, xbuf.at[slot], sems.at[slot])
        x = xbuf[slot].astype(BF16)
        g = _dot(x, wg_scr[...])
        u = _dot(x, wu_scr[...])
        o_ref[...] = _dot((_silu(g) * u).astype(BF16), wd_scr[...])

    @pl.when(i >= n_used)
    def _():
        o_ref[...] = jnp.zeros_like(o_ref)


def _moe(xn2, w_gate, w_up, w_down, tile_expert, slot_token, n_used, tile):
    d = xn2.shape[1]
    n_tiles = tile_expert.shape[0]
    de = w_gate.shape[2]
    grid_spec = pltpu.PrefetchScalarGridSpec(
        num_scalar_prefetch=3,
        grid=(n_tiles,),
        in_specs=[pl.BlockSpec(memory_space=pl.ANY),
                  pl.BlockSpec((1, d, de), lambda i, te, st, nu: (te[i], 0, 0)),
                  pl.BlockSpec((1, d, de), lambda i, te, st, nu: (te[i], 0, 0)),
                  pl.BlockSpec((1, de, d), lambda i, te, st, nu: (te[i], 0, 0))],
        out_specs=pl.BlockSpec((tile, d), lambda i, te, st, nu: (i, 0)),
        scratch_shapes=[pltpu.VMEM((2, tile, d), F32), pltpu.VMEM((d, de), BF16),
                        pltpu.VMEM((d, de), BF16), pltpu.VMEM((de, d), BF16), pltpu.SemaphoreType.DMA((2,))],
    )
    return pl.pallas_call(
        functools.partial(_moe_kernel, tile=tile),
        grid_spec=grid_spec,
        out_shape=jax.ShapeDtypeStruct((n_tiles * tile, d), F32),
        compiler_params=_params(("arbitrary",)),
    )(tile_expert, slot_token, n_used, xn2, w_gate, w_up, w_down)


def _combine_kernel(p0_ref, p1_ref, ys_hbm, hp2_ref, rw_ref, fw_ref, o_ref, buf0, buf1, sems, *, tm, tile_off):
    i = pl.program_id(0)
    slot = lax.rem(i, 2)

    def start(step, s):
        base = (step + tile_off) * tm
        _start_row_gather(p0_ref, base, tm, ys_hbm, buf0.at[s], sems.at[0, s])
        _start_row_gather(p1_ref, base, tm, ys_hbm, buf1.at[s], sems.at[1, s])

    @pl.when(i == 0)
    def _():
        start(0, 0)

    @pl.when(i + 1 < pl.num_programs(0))
    def _():
        start(i + 1, 1 - slot)

    _wait_row_gather(tm, ys_hbm, buf0.at[slot], sems.at[0, slot])
    _wait_row_gather(tm, ys_hbm, buf1.at[slot], sems.at[1, slot])
    rw = rw_ref[...]
    y = hp2_ref[...] + rw[:, 0:1] * buf0[slot] + rw[:, 1:2] * buf1[slot]
    o_ref[...] = y * lax.rsqrt(jnp.mean(y * y, axis=-1, keepdims=True) + EPS) * fw_ref[...]


def _combine(ys, hp2, route_w, final_w, pos0, pos1, tm, tile_off, n_tiles):
    _, d = hp2.shape
    grid_spec = pltpu.PrefetchScalarGridSpec(
        num_scalar_prefetch=2,
        grid=(n_tiles,),
        in_specs=[pl.BlockSpec(memory_space=pl.ANY),
                  pl.BlockSpec((tm, d), lambda i, p0, p1: (i + tile_off, 0)),
                  pl.BlockSpec((tm, LANES), lambda i, p0, p1: (i + tile_off, 0)),
                  pl.BlockSpec((1, d), lambda i, p0, p1: (0, 0))],
        out_specs=pl.BlockSpec((tm, d), lambda i, p0, p1: (i, 0)),
        scratch_shapes=[pltpu.VMEM((2, tm, d), F32), pltpu.VMEM((2, tm, d), F32),
                        pltpu.SemaphoreType.DMA((2, 2))],
    )
    return pl.pallas_call(
        functools.partial(_combine_kernel, tm=tm, tile_off=tile_off),
        grid_spec=grid_spec,
        out_shape=jax.ShapeDtypeStruct((n_tiles * tm, d), F32),
        compiler_params=_params(("arbitrary",)),
    )(pos0, pos1, ys, hp2, route_w, final_w.reshape(1, d))


def kernel(x_prompt, x_sample, state_gdn, state_conv, state_hgrn, meta_tokens, norm1_w, w_in, conv_w, a_log,
           dt_bias, gdn_norm_w, lb_logits, hgrn_norm_w, w_out, norm2_w, w_router_group, b_router_group,
           w_router_expert, b_router_expert, w_gate, w_up, w_down, final_norm_w):
    bp, seq, d = x_prompt.shape
    bs, dec_seq, _ = x_sample.shape
    assert w_in.shape[0] == 1, "single-layer trunk"
    width = N_HEADS * D_HEAD
    conv_ch = 3 * width
    tile = 256
    tn = 512
    sl_s = SUBLANES
    nb_s = CHUNK // sl_s
    assert seq % CHUNK == 0 and N_META <= CHUNK and dec_seq <= sl_s and bs % nb_s == 0
    tp, ts, ts_pad = bp * seq, bs * dec_seq, bs * sl_s
    t_small = ts_pad + CHUNK
    tm = _row_tile(math.gcd(tp, ts), 256)
    tm_p = _row_tile(tp, 1024)

    xp = x_prompt.reshape(tp, d)
    x_small = jnp.concatenate([jnp.pad(x_sample, ((0, 0), (0, sl_s - dec_seq), (0, 0))).reshape(ts_pad, d),
                               jnp.zeros((CHUNK - N_META, d), F32), meta_tokens.astype(F32)], axis=0)

    wi = w_in[0]
    n_a = 4 * width // tn
    w_b = wi[:, 4 * width + 2 * N_HEADS:]
    w_ba = jnp.concatenate([wi[:, 4 * width:4 * width + 2 * N_HEADS],
                            jnp.zeros((d, LANES - 2 * N_HEADS), F32)], axis=1).astype(BF16)
    n_cols = 8 * width

    xn_p = _rmsnorm(xp, norm1_w[0], BF16, _row_tile(tp, 512))
    xn_s = _rmsnorm(x_small, norm1_w[0], BF16, _row_tile(t_small, 1024))
    proj_p = _inproj(xn_p, wi, n_a, w_b, tm_p, tn)
    proj_s = _inproj(xn_s, wi, n_a, w_b, t_small, tn)
    ba_p = _matmul(xn_p, w_ba, tm_p, LANES)
    ba_s = _matmul(xn_s, w_ba, t_small, LANES)

    pvec = jnp.zeros((2, LANES), F32)
    pvec = pvec.at[0, N_HEADS:2 * N_HEADS].set(a_log[0]).at[1, N_HEADS:2 * N_HEADS].set(dt_bias[0])
    lb = jnp.cumsum(jax.nn.softmax(lb_logits.astype(F32), axis=0), axis=0)[0].reshape(1, width)
    gdn_nw = gdn_norm_w[0].reshape(1, D_HEAD)
    hgrn_nw = hgrn_norm_w[0].reshape(1, D_HEAD)
    cw = conv_w[0]
    mix_args = (cw, pvec, gdn_nw, lb, hgrn_nw)

    _, _, sg_m, sh_m = _mixers(proj_s.reshape(t_small // CHUNK, CHUNK, n_cols),
                               ba_s.reshape(t_small // CHUNK, CHUNK, LANES), ts_pad // CHUNK, 1, 1, CHUNK, CHUNK,
                               *mix_args, None, None, None, False, F32, "bf16", "bf16x3")
    conv_m = proj_s[t_small - SUBLANES:, :conv_ch].reshape(1, SUBLANES, conv_ch)
    oa_p, ob_p, sg_p, sh_p = _mixers(proj_p.reshape(bp, seq, n_cols), ba_p.reshape(bp, seq, LANES), 0, bp, 1,
                                     CHUNK, CHUNK, *mix_args, sg_m, conv_m, sh_m, True, BF16, "bf16", "bf16x3")
    conv0 = jnp.pad(state_conv[0], ((0, 0), (SUBLANES - (CONV_W - 1), 0), (0, 0)))
    oa_s, ob_s, sg_s, sh_s = _mixers(proj_s.reshape(t_small // sl_s, sl_s, n_cols),
                                     ba_s.reshape(t_small // sl_s, sl_s, LANES), 0, bs, nb_s, sl_s, dec_seq,
                                     *mix_args, state_gdn[0], conv0, state_hgrn[0], False, F32, "bf16", "bf16x3")

    w_r = jnp.concatenate([w_router_expert[0], w_router_group[0],
                           jnp.zeros((d, LANES - N_EXPERTS - N_GROUPS), F32)], axis=1)
    b_r = jnp.concatenate([b_router_expert[0], b_router_group[0],
                           jnp.zeros((LANES - N_EXPERTS - N_GROUPS,), F32)]).reshape(1, LANES)
    t = tp + ts
    hp2, xn2, route_i, route_w, counts = _outproj_router(
        oa_p.reshape(tp, width), ob_p.reshape(tp, width), xp,
        oa_s[:, :dec_seq].reshape(ts, width), ob_s[:, :dec_seq].reshape(ts, width), x_sample.reshape(ts, d),
        w_out[0].astype(BF16), norm2_w[0], w_r, b_r, tm)

    cnt = counts[0, :N_EXPERTS].astype(jnp.int32)
    padded = (cnt + tile - 1) // tile * tile
    ends = jnp.cumsum(padded)
    offs = ends - padded
    eid = route_i[:, 0:2]
    expert_ids = jnp.arange(N_EXPERTS, dtype=jnp.int32)
    pos = jnp.sum(jnp.where(eid[:, :, None] == expert_ids, offs, 0), axis=-1) + route_i[:, 2:4]
    n_tiles = (2 * t) // tile + N_EXPERTS
    tile_start = jnp.arange(n_tiles, dtype=jnp.int32) * tile
    tile_expert = jnp.minimum(jnp.sum((ends[None, :] <= tile_start[:, None]).astype(jnp.int32), axis=1),
                              N_EXPERTS - 1)
    n_used = (ends[-1] // tile).astype(jnp.int32).reshape(1)
    tok = jnp.broadcast_to(jnp.arange(t, dtype=jnp.int32)[:, None], (t, 2))
    slot_token = jnp.zeros((n_tiles * tile,), jnp.int32).at[pos.reshape(-1)].set(tok.reshape(-1))

    ys = _moe(xn2, w_gate[0], w_up[0], w_down[0], tile_expert, slot_token, n_used, tile)
    pos0, pos1 = pos[:, 0], pos[:, 1]
    y_prompt = _combine(ys, hp2, route_w, final_norm_w, pos0, pos1, tm, 0, tp // tm).reshape(bp, seq, d)
    y_sample = _combine(ys, hp2, route_w, final_norm_w, pos0, pos1, tm, tp // tm, ts // tm).reshape(bs, dec_seq, d)

    conv_p = proj_p.reshape(bp, seq, n_cols)[:, seq - (CONV_W - 1):, :conv_ch]
    u_s = proj_s[:ts_pad].reshape(bs, sl_s, n_cols)[:, :dec_seq, :conv_ch]
    conv_s = jnp.concatenate([state_conv[0], u_s], axis=1)[:, dec_seq:]
    return (y_prompt, y_sample, sg_p[None], conv_p[None], sh_p[None], sg_s[None], conv_s[None], sh_s[None])
```

```python
import functools
import math

import jax
import jax.numpy as jnp
from jax import lax
from jax.experimental import pallas as pl
from jax.experimental.pallas import tpu as pltpu

F32 = jnp.float32
BF16 = jnp.bfloat16
HIGHEST = lax.Precision.HIGHEST

EPS = 1e-6
N_META = 16
CONV_W = 4
N_HEADS = 8
D_HEAD = 128
N_GROUPS = 4
EXPERTS_PER_GROUP = 8
N_EXPERTS = N_GROUPS * EXPERTS_PER_GROUP

LANES = 128
SUBLANES = 8
CHUNK = 64
SUB = 16
VMEM_LIMIT = 56 * 1024 * 1024
NEG_BIG = -1e30


def _sigmoid(x):
    return 1.0 / (1.0 + jnp.exp(-x))


def _silu(x):
    return x * _sigmoid(x)


def _softplus(x):
    return jnp.maximum(x, 0.0) + jnp.log1p(jnp.exp(-jnp.abs(x)))


def _split_bf16(a, pieces):
    out = []
    for _ in range(pieces - 1):
        hi = a.astype(BF16)
        out.append(hi)
        a = a - hi.astype(F32)
    out.append(a.astype(BF16))
    return out


def _mm(a, b, dims, mode):
    dg = functools.partial(lax.dot_general, dimension_numbers=(dims, ((), ())), preferred_element_type=F32)
    if mode == "f32":
        return dg(a, b, precision=HIGHEST)
    if mode == "bf16":
        return dg(a.astype(BF16), b.astype(BF16))
    assert mode == "bf16x3"
    ah, al = _split_bf16(a, 2)
    bh, bl = _split_bf16(b, 2)
    return dg(ah, bh) + dg(ah, bl) + dg(al, bh)


def _dot(a, b, mode="bf16"):
    return _mm(a, b, ((1,), (0,)), mode)


def _dot_nt(a, b, mode="bf16"):
    return _mm(a, b, ((1,), (1,)), mode)


def _dot_tn(a, b, mode="bf16"):
    return _mm(a, b, ((0,), (0,)), mode)


def _masked_cumsum(lmask, x):
    lm = lmask.astype(BF16)
    return sum(lax.dot_general(lm, p, (((1,), (0,)), ((), ())), preferred_element_type=F32)
               for p in _split_bf16(x, 3))


def _params(sem):
    return pltpu.CompilerParams(dimension_semantics=sem, vmem_limit_bytes=VMEM_LIMIT)


def _row_tile(n, target):
    best = max(c for c in range(16, min(n, target) + 1, 16) if n % c == 0)
    return best


def _rmsnorm_kernel(x_ref, w_ref, o_ref):
    x = x_ref[...]
    ms = jnp.mean(x * x, axis=-1, keepdims=True)
    o_ref[...] = (x * lax.rsqrt(ms + EPS) * w_ref[...]).astype(o_ref.dtype)


def _rmsnorm(x, w, out_dtype, tm):
    t, d = x.shape
    return pl.pallas_call(
        _rmsnorm_kernel,
        grid=(t // tm,),
        in_specs=[pl.BlockSpec((tm, d), lambda i: (i, 0)), pl.BlockSpec((1, d), lambda i: (0, 0))],
        out_specs=pl.BlockSpec((tm, d), lambda i: (i, 0)),
        out_shape=jax.ShapeDtypeStruct((t, d), out_dtype),
        compiler_params=_params(("parallel",)),
    )(x, w.reshape(1, d))


def _inproj_kernel(x_ref, wa_ref, wb_ref, o_ref, w_scr, *, n_a):
    j = pl.program_id(0)
    i = pl.program_id(1)

    @pl.when((i == 0) & (j < n_a))
    def _():
        w_scr[...] = wa_ref[...].astype(BF16)

    @pl.when((i == 0) & (j >= n_a))
    def _():
        w_scr[...] = wb_ref[...].astype(BF16)

    o_ref[...] = _dot(x_ref[...], w_scr[...])


def _inproj(x, w_a, n_a, w_b, tm, tn):
    t, k = x.shape
    n_b = w_b.shape[1] // tn
    return pl.pallas_call(
        functools.partial(_inproj_kernel, n_a=n_a),
        grid=(n_a + n_b, t // tm),
        in_specs=[pl.BlockSpec((tm, k), lambda j, i: (i, 0)),
                  pl.BlockSpec((k, tn), lambda j, i: (0, jnp.minimum(j, n_a - 1))),
                  pl.BlockSpec((k, tn), lambda j, i: (0, jnp.maximum(j - n_a, 0)))],
        out_specs=pl.BlockSpec((tm, tn), lambda j, i: (i, j)),
        out_shape=jax.ShapeDtypeStruct((t, (n_a + n_b) * tn), F32),
        scratch_shapes=[pltpu.VMEM((k, tn), BF16)],
        compiler_params=_params(("arbitrary", "arbitrary")),
    )(x, w_a, w_b)


def _matmul_kernel(x_ref, w_ref, o_ref):
    o_ref[...] = _dot(x_ref[...], w_ref[...])


def _matmul(x, w, tm, tn):
    t, k = x.shape
    n = w.shape[1]
    return pl.pallas_call(
        _matmul_kernel,
        grid=(n // tn, t // tm),
        in_specs=[pl.BlockSpec((tm, k), lambda j, i: (i, 0)), pl.BlockSpec((k, tn), lambda j, i: (0, j))],
        out_specs=pl.BlockSpec((tm, tn), lambda j, i: (i, j)),
        out_shape=jax.ShapeDtypeStruct((t, n), F32),
        compiler_params=_params(("parallel", "arbitrary")),
    )(x, w)


def _chunk_masks(nb, sl):
    r = nb * sl
    shift = int(math.log2(sl))
    ri = lax.broadcasted_iota(jnp.int32, (r, r), 0)
    ci = lax.broadcasted_iota(jnp.int32, (r, r), 1)
    same = lax.shift_right_logical(ri, shift) == lax.shift_right_logical(ci, shift)
    return same & (ci <= ri), same & (ci < ri)


def _row_valid(nb, sl, n_valid):
    rowid = lax.broadcasted_iota(jnp.int32, (nb * sl, 1), 0)
    return (rowid & (sl - 1)) < n_valid


def _last_row_bcast(x, nb, sl):
    c = x.shape[-1]
    x3 = x.reshape(nb, sl, c)
    return jnp.broadcast_to(x3[:, sl - 1:sl, :], (nb, sl, c)).reshape(nb * sl, c)


def _gated_rmsnorm(o, w, gate):
    return o * lax.rsqrt(jnp.mean(o * o, axis=-1, keepdims=True) + EPS) * w * _silu(gate)


def _gdn_kernel(*refs, nb, sl, n_valid, has_init, mode, inv_mode, head_group):
    if has_init:
        (qkv_ref, z_ref, ba_ref, cw_ref, pv_ref, nw_ref, s0_ref, c0_ref,
         o_ref, sout_ref, s_scr, carry_scr) = refs
    else:
        (qkv_ref, z_ref, ba_ref, cw_ref, pv_ref, nw_ref,
         o_ref, sout_ref, s_scr, carry_scr) = refs
    c = pl.program_id(1)
    r = nb * sl
    dh = D_HEAD

    @pl.when(c == 0)
    def _():
        if has_init:
            s_scr[...] = s0_ref[...]
            carry_scr[...] = c0_ref[...]
        else:
            s_scr[...] = jnp.zeros_like(s_scr)
            carry_scr[...] = jnp.zeros_like(carry_scr)

    incl, strict = _chunk_masks(nb, sl)
    lmask = jnp.where(incl, 1.0, 0.0)
    valid = _row_valid(nb, sl, n_valid)
    masked = n_valid < sl
    rowid = lax.broadcasted_iota(jnp.int32, (r, 1), 0)
    row_seq = lax.shift_right_logical(rowid, int(math.log2(sl)))
    n_sq = int(math.log2(sl)) - 1

    ba = ba_ref[...].reshape(r, LANES)
    pv = pv_ref[...]
    beta_all = _sigmoid(ba)
    g_all = -jnp.exp(pv[0:1]) * _softplus(ba + pv[1:2])
    if masked:
        g_all = jnp.where(valid, g_all, 0.0)
    gcum = _masked_cumsum(lmask, g_all)
    gcum_t = gcum.T
    glast = _last_row_bcast(gcum, nb, sl)
    cw = cw_ref[...]
    nw = nw_ref[...]

    def conv_slice(c0):
        u = qkv_ref[:, :, c0:c0 + dh]
        prev = carry_scr[:, :, c0:c0 + dh]
        full = jnp.concatenate([prev, u], axis=1)
        acc = None
        for j in range(CONV_W):
            off = SUBLANES - (CONV_W - 1) + j
            term = full[:, off:off + sl, :] * cw[j:j + 1, c0:c0 + dh]
            acc = term if acc is None else acc + term
        return _silu(acc).reshape(r, dh)

    for h0 in range(0, N_HEADS, head_group):
        heads = range(h0, h0 + head_group)
        qs, ks, vs, bcs, gcs, gls, egs, decays = [], [], [], [], [], [], [], []
        for h in heads:
            q = conv_slice(h * dh)
            k = conv_slice(N_HEADS * dh + h * dh)
            v = conv_slice(2 * N_HEADS * dh + h * dh)
            q = q * lax.rsqrt(jnp.sum(q * q, axis=-1, keepdims=True) + EPS) * (dh ** -0.5)
            k = k * lax.rsqrt(jnp.sum(k * k, axis=-1, keepdims=True) + EPS)
            if masked:
                q = jnp.where(valid, q, 0.0)
                k = jnp.where(valid, k, 0.0)
                v = jnp.where(valid, v, 0.0)
            gc = gcum[:, N_HEADS + h:N_HEADS + h + 1]
            gr = gcum_t[N_HEADS + h:N_HEADS + h + 1, :]
            qs.append(q)
            ks.append(k)
            vs.append(v)
            bcs.append(beta_all[:, h:h + 1])
            gcs.append(gc)
            gls.append(glast[:, N_HEADS + h:N_HEADS + h + 1])
            egs.append(jnp.exp(gc))
            decays.append(jnp.exp(jnp.where(incl, gc - gr, NEG_BIG)))
        n = len(qs)
        qk_kk = [_dot_nt(jnp.concatenate([qs[i], ks[i]], axis=0), ks[i], mode) for i in range(n)]
        tm1 = [jnp.where(strict, -(bcs[i] * qk_kk[i][r:] * decays[i]), 0.0) for i in range(n)]
        pw = list(tm1)
        for _ in range(n_sq):
            pw = [_dot(pw[i], pw[i], inv_mode) for i in range(n)]
            tm1 = [tm1[i] + pw[i] + _dot(tm1[i], pw[i], inv_mode) for i in range(n)]
        rhs = [jnp.concatenate([vs[i] * bcs[i], ks[i] * (bcs[i] * egs[i])], axis=1) for i in range(n)]
        uw = [rhs[i] + _dot(tm1[i], rhs[i], mode) for i in range(n)]
        vnew, ointer = [], []
        for i, h in enumerate(heads):
            u = uw[i][:, :dh]
            w = uw[i][:, dh:]
            qe = qs[i] * egs[i]
            vnew_parts, ointer_parts = [], []
            for b in range(nb):
                rows = slice(b * sl, (b + 1) * sl)
                ws = _dot(jnp.concatenate([w[rows], qe[rows]], axis=0), s_scr[b, h], mode)
                vnew_parts.append(u[rows] - ws[:sl])
                ointer_parts.append(ws[sl:])
            vnew.append(vnew_parts[0] if nb == 1 else jnp.concatenate(vnew_parts, axis=0))
            ointer.append(ointer_parts[0] if nb == 1 else jnp.concatenate(ointer_parts, axis=0))
        for i, h in enumerate(heads):
            attn = jnp.where(incl, qk_kk[i][:r] * decays[i], 0.0)
            o = ointer[i] + _dot(attn, vnew[i], mode)
            z = z_ref[:, :, h * dh:(h + 1) * dh].reshape(r, dh)
            o_ref[:, :, h * dh:(h + 1) * dh] = _gated_rmsnorm(o, nw, z).reshape(nb, sl, dh).astype(o_ref.dtype)
        for i, h in enumerate(heads):
            ktil = ks[i] * jnp.exp(gls[i] - gcs[i])
            for b in range(nb):
                kt_b = ktil if nb == 1 else jnp.where(row_seq == b, ktil, 0.0)
                gl_b = gls[i][b * sl:b * sl + 1, :]
                s_scr[b, h] = s_scr[b, h] * jnp.exp(gl_b) + _dot_tn(kt_b, vnew[i], mode)

    carry_scr[...] = qkv_ref[:, sl - SUBLANES:sl, :]

    @pl.when(c == pl.num_programs(1) - 1)
    def _():
        sout_ref[...] = s_scr[...]


def _hgrn_kernel(*refs, nb, sl, n_valid, has_init, mode, head_group):
    if has_init:
        (q_ref, f_ref, i_ref, g_ref, lb_ref, nw_ref, s0_ref, o_ref, sout_ref, s_scr) = refs
    else:
        (q_ref, f_ref, i_ref, g_ref, lb_ref, nw_ref, o_ref, sout_ref, s_scr) = refs
    c = pl.program_id(1)
    r = nb * sl
    dh = D_HEAD
    width = N_HEADS * dh

    @pl.when(c == 0)
    def _():
        if has_init:
            s_scr[...] = s0_ref[...]
        else:
            s_scr[...] = jnp.zeros_like(s_scr)

    incl, _ = _chunk_masks(nb, sl)
    lmask = jnp.where(incl, 1.0, 0.0)
    valid = _row_valid(nb, sl, n_valid)
    masked = n_valid < sl
    rowid = lax.broadcasted_iota(jnp.int32, (r, 1), 0)
    row_seq = lax.shift_right_logical(rowid, int(math.log2(sl)))
    sub = min(SUB, sl)
    nblk = r // sub
    sub_shift = int(math.log2(sub))
    ri = lax.broadcasted_iota(jnp.int32, (r, r), 0)
    ci = lax.broadcasted_iota(jnp.int32, (r, r), 1)
    same_blk = lax.shift_right_logical(ri, sub_shift) == lax.shift_right_logical(ci, sub_shift)
    diag_mask = incl & same_blk
    cross_mask = incl & jnp.logical_not(same_blk)

    lb = lb_ref[...]
    f = lb + (1.0 - lb) * _sigmoid(f_ref[...].reshape(r, width))
    lf = jnp.log(f)
    k_all = 1.0 - f
    if masked:
        lf = jnp.where(valid, lf, 0.0)
        k_all = jnp.where(valid, k_all, 0.0)
    bcum = _masked_cumsum(lmask, lf)
    nw = nw_ref[...]

    def head_inputs(h):
        cols = slice(h * dh, (h + 1) * dh)
        q = _silu(q_ref[:, :, cols].reshape(r, dh)) * (dh ** -0.5)
        v = i_ref[:, :, cols].reshape(r, dh)
        if masked:
            v = jnp.where(valid, v, 0.0)
        return q, k_all[:, cols], v, bcum[:, cols]

    def intra_attn(q, k, bh):
        bmid = jnp.broadcast_to(bh.reshape(nblk, sub, dh)[:, sub // 2:sub // 2 + 1, :],
                                (nblk, sub, dh)).reshape(r, dh)
        attn = jnp.where(diag_mask, _dot_nt(q * jnp.exp(bh - bmid), k * jnp.exp(bmid - bh), mode), 0.0)
        if sl > sub:
            parts = [jnp.zeros((sub, r), F32)]
            for blk in range(1, nblk):
                start = blk * sub
                bref = bh[start - 1:start, :]
                qc = q[start:start + sub] * jnp.exp(bh[start:start + sub] - bref)
                kc = k * jnp.exp(jnp.minimum(bref - bh, 0.0))
                parts.append(_dot_nt(qc, kc, mode))
            attn = attn + jnp.where(cross_mask, jnp.concatenate(parts, axis=0), 0.0)
        return attn

    for h0 in range(0, N_HEADS, head_group):
        heads = range(h0, h0 + head_group)
        ins = [head_inputs(h) for h in heads]
        attns = [intra_attn(q, k, bh) for (q, k, v, bh) in ins]
        for i, h in enumerate(heads):
            q, k, v, bh = ins[i]
            qe = q * jnp.exp(bh)
            ointer_parts = []
            for b in range(nb):
                rows = slice(b * sl, (b + 1) * sl)
                ointer_parts.append(_dot(qe[rows], s_scr[b, h], mode))
            ointer = ointer_parts[0] if nb == 1 else jnp.concatenate(ointer_parts, axis=0)
            o = ointer + _dot(attns[i], v, mode)
            cols = slice(h * dh, (h + 1) * dh)
            gate = g_ref[:, :, cols].reshape(r, dh)
            o_ref[:, :, cols] = _gated_rmsnorm(o, nw, gate).reshape(nb, sl, dh).astype(o_ref.dtype)
        for i, h in enumerate(heads):
            q, k, v, bh = ins[i]
            blast = _last_row_bcast(bh, nb, sl)
            ktil = k * jnp.exp(blast - bh)
            pad = [jnp.zeros((LANES - r, dh), F32)] if r < LANES else []
            tr = jnp.concatenate([blast] + pad, axis=0).T
            for b in range(nb):
                kt_b = ktil if nb == 1 else jnp.where(row_seq == b, ktil, 0.0)
                dec_col = jnp.exp(tr[:, b * sl:b * sl + 1])
                s_scr[b, h] = s_scr[b, h] * dec_col + _dot_tn(kt_b, v, mode)

    @pl.when(c == pl.num_programs(1) - 1)
    def _():
        sout_ref[...] = s_scr[...]


def _mixers(proj3, ba3, blk_off, nseq, nb, sl, n_valid, conv_w, pvec, gdn_nw, lb, hgrn_nw,
            s_gdn0, conv0, s_hgrn0, shared_init, out_dtype, mode, inv_mode):
    length = proj3.shape[1]
    has_init = s_gdn0 is not None
    width = N_HEADS * D_HEAD
    conv_ch = 3 * width
    grid = (nseq // nb, length // sl)
    state_spec = pl.BlockSpec((nb, N_HEADS, D_HEAD, D_HEAD), lambda g, c: (g, 0, 0, 0))
    state_shape = jax.ShapeDtypeStruct((nseq, N_HEADS, D_HEAD, D_HEAD), F32)
    init_idx = (lambda g: 0) if shared_init else (lambda g: g)
    init_state_spec = pl.BlockSpec((nb, N_HEADS, D_HEAD, D_HEAD), lambda g, c: (init_idx(g), 0, 0, 0))
    head_group = 8 if nb == 1 else 4

    def col_spec(w, idx):
        return pl.BlockSpec((nb, sl, w), lambda g, c: (g + blk_off, c, idx))

    def out_spec(w):
        return pl.BlockSpec((nb, sl, w), lambda g, c: (g, c, 0))

    def const_spec(shape):
        return pl.BlockSpec(shape, lambda g, c: (0,) * len(shape))

    gdn_in = [proj3, proj3, ba3, conv_w, pvec, gdn_nw]
    gdn_specs = [col_spec(conv_ch, 0), col_spec(width, 3), col_spec(LANES, 0),
                 const_spec(conv_w.shape), const_spec(pvec.shape), const_spec(gdn_nw.shape)]
    if has_init:
        gdn_in += [s_gdn0, conv0]
        gdn_specs += [init_state_spec, pl.BlockSpec((nb, SUBLANES, conv_ch), lambda g, c: (init_idx(g), 0, 0))]
    o_gdn, s_gdn = pl.pallas_call(
        functools.partial(_gdn_kernel, nb=nb, sl=sl, n_valid=n_valid, has_init=has_init, mode=mode,
                          inv_mode=inv_mode, head_group=head_group),
        grid=grid,
        in_specs=gdn_specs,
        out_specs=[out_spec(width), state_spec],
        out_shape=[jax.ShapeDtypeStruct((nseq, length, width), out_dtype), state_shape],
        scratch_shapes=[pltpu.VMEM((nb, N_HEADS, D_HEAD, D_HEAD), F32), pltpu.VMEM((nb, SUBLANES, conv_ch), F32)],
        compiler_params=_params(("parallel", "arbitrary")),
    )(*gdn_in)

    hgrn_in = [proj3, proj3, proj3, proj3, lb, hgrn_nw]
    hgrn_specs = [col_spec(width, 4), col_spec(width, 5), col_spec(width, 6), col_spec(width, 7),
                  const_spec(lb.shape), const_spec(hgrn_nw.shape)]
    if has_init:
        hgrn_in += [s_hgrn0]
        hgrn_specs += [init_state_spec]
    o_hgrn, s_hgrn = pl.pallas_call(
        functools.partial(_hgrn_kernel, nb=nb, sl=sl, n_valid=n_valid, has_init=has_init, mode=mode,
                          head_group=head_group),
        grid=grid,
        in_specs=hgrn_specs,
        out_specs=[out_spec(width), state_spec],
        out_shape=[jax.ShapeDtypeStruct((nseq, length, width), out_dtype), state_shape],
        scratch_shapes=[pltpu.VMEM((nb, N_HEADS, D_HEAD, D_HEAD), F32)],
        compiler_params=_params(("parallel", "arbitrary")),
    )(*hgrn_in)
    return o_gdn, o_hgrn, s_gdn, s_hgrn


def _outproj_router_kernel(oap_ref, obp_ref, hpp_ref, oas_ref, obs_ref, hps_ref, wo_ref, n2_ref, wr_ref, br_ref,
                           hp2_ref, xn2_ref, ri_ref, rw_ref, cnt_ref, carry_scr, *, tm, n_p):
    i = pl.program_id(0)

    @pl.when(i == 0)
    def _():
        carry_scr[...] = jnp.zeros_like(carry_scr)

    body = functools.partial(_outproj_router_tile, wo_ref=wo_ref, n2_ref=n2_ref, wr_ref=wr_ref, br_ref=br_ref,
                             hp2_ref=hp2_ref, xn2_ref=xn2_ref, ri_ref=ri_ref, rw_ref=rw_ref,
                             carry_scr=carry_scr, tm=tm)

    @pl.when(i < n_p)
    def _():
        body(oap_ref[...], obp_ref[...], hpp_ref[...])

    @pl.when(i >= n_p)
    def _():
        body(oas_ref[...], obs_ref[...], hps_ref[...])

    @pl.when(i == pl.num_programs(0) - 1)
    def _():
        cnt_ref[...] = carry_scr[...]


def _outproj_router_tile(oa, ob, hp, *, wo_ref, n2_ref, wr_ref, br_ref, hp2_ref, xn2_ref, ri_ref, rw_ref,
                         carry_scr, tm):
    half = oa.shape[-1]
    mix = _dot(oa.astype(BF16), wo_ref[:half, :]) + _dot(ob.astype(BF16), wo_ref[half:, :])
    hp2 = hp + mix
    hp2_ref[...] = hp2
    xn2 = hp2 * lax.rsqrt(jnp.mean(hp2 * hp2, axis=-1, keepdims=True) + EPS) * n2_ref[...]
    xn2_ref[...] = xn2
    logits = _dot(xn2, wr_ref[...], "bf16x3") + br_ref[...]

    lane = lax.broadcasted_iota(jnp.int32, (tm, LANES), 1)
    lane_f = lane.astype(F32)
    far = float(4 * LANES)
    is_g = (lane >= N_EXPERTS) & (lane < N_EXPERTS + N_GROUPS)
    lg = jnp.where(is_g, logits, -jnp.inf)
    gmax = jnp.max(lg, axis=-1, keepdims=True)
    gsel = jnp.min(jnp.where(lg == gmax, lane_f, far), axis=-1, keepdims=True).astype(jnp.int32) - N_EXPERTS
    p_top = 1.0 / jnp.sum(jnp.where(is_g, jnp.exp(logits - gmax), 0.0), axis=-1, keepdims=True)
    in_grp = (lane < N_EXPERTS) & (lax.shift_right_logical(lane, 3) == gsel)
    le = jnp.where(in_grp, logits, -jnp.inf)
    m1 = jnp.max(le, axis=-1, keepdims=True)
    i1 = jnp.min(jnp.where(le == m1, lane_f, far), axis=-1, keepdims=True).astype(jnp.int32)
    le2 = jnp.where(lane == i1, -jnp.inf, le)
    m2 = jnp.max(le2, axis=-1, keepdims=True)
    i2 = jnp.min(jnp.where(le2 == m2, lane_f, far), axis=-1, keepdims=True).astype(jnp.int32)
    e2 = jnp.exp(m2 - m1)
    w1 = p_top / (1.0 + e2)
    w2 = p_top * e2 / (1.0 + e2)

    onehot = (lane == i1) | (lane == i2)
    onehot_f = jnp.where(onehot, 1.0, 0.0)
    tri = (lax.broadcasted_iota(jnp.int32, (tm, tm), 1) < lax.broadcasted_iota(jnp.int32, (tm, tm), 0))
    before = _dot(jnp.where(tri, 1.0, 0.0).astype(BF16), onehot_f.astype(BF16)) + carry_scr[...]
    r1 = jnp.sum(jnp.where(lane == i1, before, 0.0), axis=-1, keepdims=True).astype(jnp.int32)
    r2 = jnp.sum(jnp.where(lane == i2, before, 0.0), axis=-1, keepdims=True).astype(jnp.int32)
    carry_scr[...] = carry_scr[...] + jnp.sum(onehot_f, axis=0, keepdims=True)

    ri_ref[...] = jnp.where(lane == 0, i1, jnp.where(lane == 1, i2, jnp.where(lane == 2, r1, jnp.where(lane == 3, r2, 0))))
    rw_ref[...] = jnp.where(lane == 0, w1, jnp.where(lane == 1, w2, 0.0))


def _outproj_router(oa_p, ob_p, hp_p, oa_s, ob_s, hp_s, w_out, norm2_w, w_r, b_r, tm):
    (tp, d), ts = hp_p.shape, hp_s.shape[0]
    half = oa_p.shape[1]
    n_p, n_s = tp // tm, ts // tm
    t = tp + ts
    prow = lambda w: pl.BlockSpec((tm, w), lambda i: (jnp.minimum(i, n_p - 1), 0))
    srow = lambda w: pl.BlockSpec((tm, w), lambda i: (jnp.maximum(i - n_p, 0), 0))
    row = lambda w: pl.BlockSpec((tm, w), lambda i: (i, 0))
    const = lambda shape: pl.BlockSpec(shape, lambda i: (0,) * len(shape))
    return pl.pallas_call(
        functools.partial(_outproj_router_kernel, tm=tm, n_p=n_p),
        grid=(n_p + n_s,),
        in_specs=[prow(half), prow(half), prow(d), srow(half), srow(half), srow(d),
                  const(w_out.shape), const((1, d)), const(w_r.shape), const((1, LANES))],
        out_specs=[row(d), row(d), row(LANES), row(LANES), const((1, LANES))],
        out_shape=[jax.ShapeDtypeStruct((t, d), F32), jax.ShapeDtypeStruct((t, d), F32),
                   jax.ShapeDtypeStruct((t, LANES), jnp.int32), jax.ShapeDtypeStruct((t, LANES), F32),
                   jax.ShapeDtypeStruct((1, LANES), F32)],
        scratch_shapes=[pltpu.VMEM((1, LANES), F32)],
        compiler_params=_params(("arbitrary",)),
    )(oa_p, ob_p, hp_p, oa_s, ob_s, hp_s, w_out, norm2_w.reshape(1, d), w_r, b_r)


def _start_row_gather(idx_ref, base, n, src_hbm, dst, sem):
    def issue(r, carry):
        pltpu.make_async_copy(src_hbm.at[pl.ds(idx_ref[base + r], 1), :], dst.at[pl.ds(r, 1), :], sem).start()
        return carry
    lax.fori_loop(0, n, issue, 0, unroll=8)


def _wait_row_gather(n, src_hbm, dst, sem):
    pltpu.make_async_copy(src_hbm.at[pl.ds(0, n), :], dst, sem).wait()


def _moe_kernel(te_ref, st_ref, nx_ref, nu_ref, x_hbm, wg_hbm, wu_hbm, wd_hbm, o_ref,
                xbuf, wg_f32, wu_f32, wd_f32, wg_scr, wu_scr, wd_scr, sems, wsems, *, tile):
    i = pl.program_id(0)
    n_used = nu_ref[0]
    slot = lax.rem(i, 2)
    weights = ((wg_hbm, wg_f32, wg_scr), (wu_hbm, wu_f32, wu_scr), (wd_hbm, wd_f32, wd_scr))

    def start_weights(e):
        for k, (w_hbm, w_f32, _) in enumerate(weights):
            pltpu.make_async_copy(w_hbm.at[e], w_f32, wsems.at[k]).start()

    @pl.when((i == 0) & (n_used > 0))
    def _():
        start_weights(te_ref[0])
        _start_row_gather(st_ref, 0, tile, x_hbm, xbuf.at[0], sems.at[0])

    @pl.when(i < n_used)
    def _():
        @pl.when(i + 1 < n_used)
        def _():
            _start_row_gather(st_ref, (i + 1) * tile, tile, x_hbm, xbuf.at[1 - slot], sems.at[1 - slot])

        e = te_ref[i]

        @pl.when((i == 0) | (e != te_ref[jnp.maximum(i - 1, 0)]))
        def _():
            for k, (w_hbm, w_f32, w_scr) in enumerate(weights):
                pltpu.make_async_copy(w_hbm.at[0], w_f32, wsems.at[k]).wait()
                w_scr[...] = w_f32[...].astype(BF16)
            nxt = nx_ref[e]

            @pl.when(nxt < N_EXPERTS)
            def _():
                start_weights(nxt)

        _wait_row_gather(tile, x_hbm, xbuf.at[slot], sems.at[slot])
        x = xbuf[slot].astype(BF16)
        g = _dot(x, wg_scr[...])
        u = _dot(x, wu_scr[...])
        o_ref[...] = _dot((_silu(g) * u).astype(BF16), wd_scr[...])

    @pl.when(i >= n_used)
    def _():
        o_ref[...] = jnp.zeros_like(o_ref)


def _moe(xn2, w_gate, w_up, w_down, tile_expert, slot_token, next_expert, n_used, tile):
    d = xn2.shape[1]
    n_tiles = tile_expert.shape[0]
    de = w_gate.shape[2]
    hbm = pl.BlockSpec(memory_space=pl.ANY)
    grid_spec = pltpu.PrefetchScalarGridSpec(
        num_scalar_prefetch=4,
        grid=(n_tiles,),
        in_specs=[hbm, hbm, hbm, hbm],
        out_specs=pl.BlockSpec((tile, d), lambda i, te, st, nx, nu: (i, 0)),
        scratch_shapes=[pltpu.VMEM((2, tile, d), F32),
                        pltpu.VMEM((d, de), F32), pltpu.VMEM((d, de), F32), pltpu.VMEM((de, d), F32),
                        pltpu.VMEM((d, de), BF16), pltpu.VMEM((d, de), BF16), pltpu.VMEM((de, d), BF16),
                        pltpu.SemaphoreType.DMA((2,)), pltpu.SemaphoreType.DMA((3,))],
    )
    return pl.pallas_call(
        functools.partial(_moe_kernel, tile=tile),
        grid_spec=grid_spec,
        out_shape=jax.ShapeDtypeStruct((n_tiles * tile, d), F32),
        compiler_params=_params(("arbitrary",)),
    )(tile_expert, slot_token, next_expert, n_used, xn2, w_gate, w_up, w_down)


def _combine_kernel(p0_ref, p1_ref, ys_hbm, hp2_ref, rw_ref, fw_ref, o_ref, buf0, buf1, sems, *, tm, tile_off):
    i = pl.program_id(0)
    slot = lax.rem(i, 2)

    def start(step, s):
        base = (step + tile_off) * tm
        _start_row_gather(p0_ref, base, tm, ys_hbm, buf0.at[s], sems.at[0, s])
        _start_row_gather(p1_ref, base, tm, ys_hbm, buf1.at[s], sems.at[1, s])

    @pl.when(i == 0)
    def _():
        start(0, 0)

    @pl.when(i + 1 < pl.num_programs(0))
    def _():
        start(i + 1, 1 - slot)

    _wait_row_gather(tm, ys_hbm, buf0.at[slot], sems.at[0, slot])
    _wait_row_gather(tm, ys_hbm, buf1.at[slot], sems.at[1, slot])
    rw = rw_ref[...]
    y = hp2_ref[...] + rw[:, 0:1] * buf0[slot] + rw[:, 1:2] * buf1[slot]
    o_ref[...] = y * lax.rsqrt(jnp.mean(y * y, axis=-1, keepdims=True) + EPS) * fw_ref[...]


def _combine(ys, hp2, route_w, final_w, pos0, pos1, tm, tile_off, n_tiles):
    _, d = hp2.shape
    grid_spec = pltpu.PrefetchScalarGridSpec(
        num_scalar_prefetch=2,
        grid=(n_tiles,),
        in_specs=[pl.BlockSpec(memory_space=pl.ANY),
                  pl.BlockSpec((tm, d), lambda i, p0, p1: (i + tile_off, 0)),
                  pl.BlockSpec((tm, LANES), lambda i, p0, p1: (i + tile_off, 0)),
                  pl.BlockSpec((1, d), lambda i, p0, p1: (0, 0))],
        out_specs=pl.BlockSpec((tm, d), lambda i, p0, p1: (i, 0)),
        scratch_shapes=[pltpu.VMEM((2, tm, d), F32), pltpu.VMEM((2, tm, d), F32),
                        pltpu.SemaphoreType.DMA((2, 2))],
    )
    return pl.pallas_call(
        functools.partial(_combine_kernel, tm=tm, tile_off=tile_off),
        grid_spec=grid_spec,
        out_shape=jax.ShapeDtypeStruct((n_tiles * tm, d), F32),
        compiler_params=_params(("arbitrary",)),
    )(pos0, pos1, ys, hp2, route_w, final_w.reshape(1, d))


def kernel(x_prompt, x_sample, state_gdn, state_conv, state_hgrn, meta_tokens, norm1_w, w_in, conv_w, a_log,
           dt_bias, gdn_norm_w, lb_logits, hgrn_norm_w, w_out, norm2_w, w_router_group, b_router_group,
           w_router_expert, b_router_expert, w_gate, w_up, w_down, final_norm_w):
    bp, seq, d = x_prompt.shape
    bs, dec_seq, _ = x_sample.shape
    assert w_in.shape[0] == 1, "single-layer trunk"
    width = N_HEADS * D_HEAD
    conv_ch = 3 * width
    tile = 256
    tn = 512
    sl_s = SUBLANES
    nb_s = CHUNK // sl_s
    assert seq % CHUNK == 0 and N_META <= CHUNK and dec_seq <= sl_s and bs % nb_s == 0
    tp, ts, ts_pad = bp * seq, bs * dec_seq, bs * sl_s
    t_small = ts_pad + CHUNK
    tm = _row_tile(math.gcd(tp, ts), 256)
    tm_p = _row_tile(tp, 1024)

    xp = x_prompt.reshape(tp, d)
    x_small = jnp.concatenate([jnp.pad(x_sample, ((0, 0), (0, sl_s - dec_seq), (0, 0))).reshape(ts_pad, d),
                               jnp.zeros((CHUNK - N_META, d), F32), meta_tokens.astype(F32)], axis=0)

    wi = w_in[0]
    n_a = 4 * width // tn
    w_b = wi[:, 4 * width + 2 * N_HEADS:]
    w_ba = jnp.concatenate([wi[:, 4 * width:4 * width + 2 * N_HEADS],
                            jnp.zeros((d, LANES - 2 * N_HEADS), F32)], axis=1).astype(BF16)
    n_cols = 8 * width

    xn_p = _rmsnorm(xp, norm1_w[0], BF16, _row_tile(tp, 512))
    xn_s = _rmsnorm(x_small, norm1_w[0], BF16, _row_tile(t_small, 1024))
    proj_p = _inproj(xn_p, wi, n_a, w_b, tm_p, tn)
    proj_s = _inproj(xn_s, wi, n_a, w_b, t_small, tn)
    ba_p = _matmul(xn_p, w_ba, tm_p, LANES)
    ba_s = _matmul(xn_s, w_ba, t_small, LANES)

    pvec = jnp.zeros((2, LANES), F32)
    pvec = pvec.at[0, N_HEADS:2 * N_HEADS].set(a_log[0]).at[1, N_HEADS:2 * N_HEADS].set(dt_bias[0])
    lb = jnp.cumsum(jax.nn.softmax(lb_logits.astype(F32), axis=0), axis=0)[0].reshape(1, width)
    gdn_nw = gdn_norm_w[0].reshape(1, D_HEAD)
    hgrn_nw = hgrn_norm_w[0].reshape(1, D_HEAD)
    cw = conv_w[0]
    mix_args = (cw, pvec, gdn_nw, lb, hgrn_nw)

    _, _, sg_m, sh_m = _mixers(proj_s.reshape(t_small // CHUNK, CHUNK, n_cols),
                               ba_s.reshape(t_small // CHUNK, CHUNK, LANES), ts_pad // CHUNK, 1, 1, CHUNK, CHUNK,
                               *mix_args, None, None, None, False, F32, "bf16", "bf16x3")
    conv_m = proj_s[t_small - SUBLANES:, :conv_ch].reshape(1, SUBLANES, conv_ch)
    oa_p, ob_p, sg_p, sh_p = _mixers(proj_p.reshape(bp, seq, n_cols), ba_p.reshape(bp, seq, LANES), 0, bp, 1,
                                     CHUNK, CHUNK, *mix_args, sg_m, conv_m, sh_m, True, BF16, "bf16", "bf16x3")
    conv0 = jnp.pad(state_conv[0], ((0, 0), (SUBLANES - (CONV_W - 1), 0), (0, 0)))
    oa_s, ob_s, sg_s, sh_s = _mixers(proj_s.reshape(t_small // sl_s, sl_s, n_cols),
                                     ba_s.reshape(t_small // sl_s, sl_s, LANES), 0, bs, nb_s, sl_s, dec_seq,
                                     *mix_args, state_gdn[0], conv0, state_hgrn[0], False, F32, "bf16", "bf16x3")

    w_r = jnp.concatenate([w_router_expert[0], w_router_group[0],
                           jnp.zeros((d, LANES - N_EXPERTS - N_GROUPS), F32)], axis=1)
    b_r = jnp.concatenate([b_router_expert[0], b_router_group[0],
                           jnp.zeros((LANES - N_EXPERTS - N_GROUPS,), F32)]).reshape(1, LANES)
    t = tp + ts
    hp2, xn2, route_i, route_w, counts = _outproj_router(
        oa_p.reshape(tp, width), ob_p.reshape(tp, width), xp,
        oa_s[:, :dec_seq].reshape(ts, width), ob_s[:, :dec_seq].reshape(ts, width), x_sample.reshape(ts, d),
        w_out[0].astype(BF16), norm2_w[0], w_r, b_r, tm)

    cnt = counts[0, :N_EXPERTS].astype(jnp.int32)
    padded = (cnt + tile - 1) // tile * tile
    ends = jnp.cumsum(padded)
    offs = ends - padded
    eid = route_i[:, 0:2]
    expert_ids = jnp.arange(N_EXPERTS, dtype=jnp.int32)
    pos = jnp.sum(jnp.where(eid[:, :, None] == expert_ids, offs, 0), axis=-1) + route_i[:, 2:4]
    n_tiles = (2 * t) // tile + N_EXPERTS
    tile_start = jnp.arange(n_tiles, dtype=jnp.int32) * tile
    tile_expert = jnp.minimum(jnp.sum((ends[None, :] <= tile_start[:, None]).astype(jnp.int32), axis=1),
                              N_EXPERTS - 1)
    n_used = (ends[-1] // tile).astype(jnp.int32).reshape(1)
    tok = jnp.broadcast_to(jnp.arange(t, dtype=jnp.int32)[:, None], (t, 2))
    slot_token = jnp.zeros((n_tiles * tile,), jnp.int32).at[pos.reshape(-1)].set(tok.reshape(-1))

    later_active = (expert_ids[None, :] > expert_ids[:, None]) & (cnt[None, :] > 0)
    next_expert = jnp.min(jnp.where(later_active, expert_ids[None, :], N_EXPERTS), axis=1).astype(jnp.int32)
    ys = _moe(xn2, w_gate[0], w_up[0], w_down[0], tile_expert, slot_token, next_expert, n_used, tile)
    pos0, pos1 = pos[:, 0], pos[:, 1]
    y_prompt = _combine(ys, hp2, route_w, final_norm_w, pos0, pos1, tm, 0, tp // tm).reshape(bp, seq, d)
    y_sample = _combine(ys, hp2, route_w, final_norm_w, pos0, pos1, tm, tp // tm, ts // tm).reshape(bs, dec_seq, d)

    conv_p = proj_p.reshape(bp, seq, n_cols)[:, seq - (CONV_W - 1):, :conv_ch]
    u_s = proj_s[:ts_pad].reshape(bs, sl_s, n_cols)[:, :dec_seq, :conv_ch]
    conv_s = jnp.concatenate([state_conv[0], u_s], axis=1)[:, dec_seq:]
    return (y_prompt, y_sample, sg_p[None], conv_p[None], sh_p[None], sg_s[None], conv_s[None], sh_s[None])
```

```python
import functools
import math

import jax
import jax.numpy as jnp
from jax import lax
from jax.experimental import pallas as pl
from jax.experimental.pallas import tpu as pltpu

F32 = jnp.float32
BF16 = jnp.bfloat16
HIGHEST = lax.Precision.HIGHEST

EPS = 1e-6
N_META = 16
CONV_W = 4
N_HEADS = 8
D_HEAD = 128
N_GROUPS = 4
EXPERTS_PER_GROUP = 8
N_EXPERTS = N_GROUPS * EXPERTS_PER_GROUP

LANES = 128
SUBLANES = 8
CHUNK = 64
SUB = 16
VMEM_LIMIT = 56 * 1024 * 1024
NEG_BIG = -1e30


def _sigmoid(x):
    return 1.0 / (1.0 + jnp.exp(-x))


def _silu(x):
    return x * _sigmoid(x)


def _softplus(x):
    return jnp.maximum(x, 0.0) + jnp.log1p(jnp.exp(-jnp.abs(x)))


def _split_bf16(a, pieces):
    out = []
    for _ in range(pieces - 1):
        hi = a.astype(BF16)
        out.append(hi)
        a = a - hi.astype(F32)
    out.append(a.astype(BF16))
    return out


def _mm(a, b, dims, mode):
    dg = functools.partial(lax.dot_general, dimension_numbers=(dims, ((), ())), preferred_element_type=F32)
    if mode == "f32":
        return dg(a, b, precision=HIGHEST)
    if mode == "bf16":
        return dg(a.astype(BF16), b.astype(BF16))
    assert mode == "bf16x3"
    ah, al = _split_bf16(a, 2)
    bh, bl = _split_bf16(b, 2)
    return dg(ah, bh) + dg(ah, bl) + dg(al, bh)


def _dot(a, b, mode="bf16"):
    return _mm(a, b, ((1,), (0,)), mode)


def _dot_nt(a, b, mode="bf16"):
    return _mm(a, b, ((1,), (1,)), mode)


def _dot_tn(a, b, mode="bf16"):
    return _mm(a, b, ((0,), (0,)), mode)


def _masked_cumsum(lmask, x):
    lm = lmask.astype(BF16)
    return sum(lax.dot_general(lm, p, (((1,), (0,)), ((), ())), preferred_element_type=F32)
               for p in _split_bf16(x, 3))


def _params(sem):
    return pltpu.CompilerParams(dimension_semantics=sem, vmem_limit_bytes=VMEM_LIMIT)


def _row_tile(n, target):
    best = max(c for c in range(16, min(n, target) + 1, 16) if n % c == 0)
    return best


def _rmsnorm_kernel(x_ref, w_ref, o_ref):
    x = x_ref[...]
    ms = jnp.mean(x * x, axis=-1, keepdims=True)
    o_ref[...] = (x * lax.rsqrt(ms + EPS) * w_ref[...]).astype(o_ref.dtype)


def _rmsnorm(x, w, out_dtype, tm):
    t, d = x.shape
    return pl.pallas_call(
        _rmsnorm_kernel,
        grid=(t // tm,),
        in_specs=[pl.BlockSpec((tm, d), lambda i: (i, 0)), pl.BlockSpec((1, d), lambda i: (0, 0))],
        out_specs=pl.BlockSpec((tm, d), lambda i: (i, 0)),
        out_shape=jax.ShapeDtypeStruct((t, d), out_dtype),
        compiler_params=_params(("parallel",)),
    )(x, w.reshape(1, d))


def _inproj_kernel(x_ref, wa_ref, wb_ref, o_ref, w_scr, *, n_a):
    j = pl.program_id(0)
    i = pl.program_id(1)

    @pl.when((i == 0) & (j < n_a))
    def _():
        w_scr[...] = wa_ref[...].astype(BF16)

    @pl.when((i == 0) & (j >= n_a))
    def _():
        w_scr[...] = wb_ref[...].astype(BF16)

    o_ref[...] = _dot(x_ref[...], w_scr[...])


def _inproj(x, w_a, n_a, w_b, tm, tn):
    t, k = x.shape
    n_b = w_b.shape[1] // tn
    return pl.pallas_call(
        functools.partial(_inproj_kernel, n_a=n_a),
        grid=(n_a + n_b, t // tm),
        in_specs=[pl.BlockSpec((tm, k), lambda j, i: (i, 0)),
                  pl.BlockSpec((None, k, tn), lambda j, i: (0, 0, jnp.minimum(j, n_a - 1))),
                  pl.BlockSpec((k, tn), lambda j, i: (0, jnp.maximum(j - n_a, 0)))],
        out_specs=pl.BlockSpec((tm, tn), lambda j, i: (i, j)),
        out_shape=jax.ShapeDtypeStruct((t, (n_a + n_b) * tn), F32),
        scratch_shapes=[pltpu.VMEM((k, tn), BF16)],
        compiler_params=_params(("arbitrary", "arbitrary")),
    )(x, w_a, w_b)


def _matmul_kernel(x_ref, w_ref, o_ref):
    o_ref[...] = _dot(x_ref[...], w_ref[...])


def _matmul(x, w, tm, tn):
    t, k = x.shape
    n = w.shape[1]
    return pl.pallas_call(
        _matmul_kernel,
        grid=(n // tn, t // tm),
        in_specs=[pl.BlockSpec((tm, k), lambda j, i: (i, 0)), pl.BlockSpec((k, tn), lambda j, i: (0, j))],
        out_specs=pl.BlockSpec((tm, tn), lambda j, i: (i, j)),
        out_shape=jax.ShapeDtypeStruct((t, n), F32),
        compiler_params=_params(("parallel", "arbitrary")),
    )(x, w)


def _chunk_masks(nb, sl):
    r = nb * sl
    shift = int(math.log2(sl))
    ri = lax.broadcasted_iota(jnp.int32, (r, r), 0)
    ci = lax.broadcasted_iota(jnp.int32, (r, r), 1)
    same = lax.shift_right_logical(ri, shift) == lax.shift_right_logical(ci, shift)
    return same & (ci <= ri), same & (ci < ri)


def _row_valid(nb, sl, n_valid):
    rowid = lax.broadcasted_iota(jnp.int32, (nb * sl, 1), 0)
    return (rowid & (sl - 1)) < n_valid


def _last_row_bcast(x, nb, sl):
    c = x.shape[-1]
    x3 = x.reshape(nb, sl, c)
    return jnp.broadcast_to(x3[:, sl - 1:sl, :], (nb, sl, c)).reshape(nb * sl, c)


def _gated_rmsnorm(o, w, gate):
    return o * lax.rsqrt(jnp.mean(o * o, axis=-1, keepdims=True) + EPS) * w * _silu(gate)


def _gdn_kernel(*refs, nb, sl, n_valid, has_init, mode, inv_mode, head_group):
    if has_init:
        (qkv_ref, z_ref, ba_ref, cw_ref, pv_ref, nw_ref, s0_ref, c0_ref,
         o_ref, sout_ref, s_scr, carry_scr) = refs
    else:
        (qkv_ref, z_ref, ba_ref, cw_ref, pv_ref, nw_ref,
         o_ref, sout_ref, s_scr, carry_scr) = refs
    c = pl.program_id(1)
    r = nb * sl
    dh = D_HEAD

    @pl.when(c == 0)
    def _():
        if has_init:
            s_scr[...] = s0_ref[...]
            carry_scr[...] = c0_ref[...]
        else:
            s_scr[...] = jnp.zeros_like(s_scr)
            carry_scr[...] = jnp.zeros_like(carry_scr)

    incl, strict = _chunk_masks(nb, sl)
    lmask = jnp.where(incl, 1.0, 0.0)
    valid = _row_valid(nb, sl, n_valid)
    masked = n_valid < sl
    rowid = lax.broadcasted_iota(jnp.int32, (r, 1), 0)
    row_seq = lax.shift_right_logical(rowid, int(math.log2(sl)))
    n_sq = int(math.log2(sl)) - 1

    ba = ba_ref[...].reshape(r, LANES)
    pv = pv_ref[...]
    beta_all = _sigmoid(ba)
    g_all = -jnp.exp(pv[0:1]) * _softplus(ba + pv[1:2])
    if masked:
        g_all = jnp.where(valid, g_all, 0.0)
    gcum = _masked_cumsum(lmask, g_all)
    gcum_t = gcum.T
    glast = _last_row_bcast(gcum, nb, sl)
    cw = cw_ref[...]
    nw = nw_ref[...]

    def conv_slice(c0):
        u = qkv_ref[:, :, c0:c0 + dh]
        prev = carry_scr[:, :, c0:c0 + dh]
        full = jnp.concatenate([prev, u], axis=1)
        acc = None
        for j in range(CONV_W):
            off = SUBLANES - (CONV_W - 1) + j
            term = full[:, off:off + sl, :] * cw[j:j + 1, c0:c0 + dh]
            acc = term if acc is None else acc + term
        return _silu(acc).reshape(r, dh)

    for h0 in range(0, N_HEADS, head_group):
        heads = range(h0, h0 + head_group)
        qs, ks, vs, bcs, gcs, gls, egs, decays = [], [], [], [], [], [], [], []
        for h in heads:
            q = conv_slice(h * dh)
            k = conv_slice(N_HEADS * dh + h * dh)
            v = conv_slice(2 * N_HEADS * dh + h * dh)
            q = q * lax.rsqrt(jnp.sum(q * q, axis=-1, keepdims=True) + EPS) * (dh ** -0.5)
            k = k * lax.rsqrt(jnp.sum(k * k, axis=-1, keepdims=True) + EPS)
            if masked:
                q = jnp.where(valid, q, 0.0)
                k = jnp.where(valid, k, 0.0)
                v = jnp.where(valid, v, 0.0)
            gc = gcum[:, N_HEADS + h:N_HEADS + h + 1]
            gr = gcum_t[N_HEADS + h:N_HEADS + h + 1, :]
            qs.append(q)
            ks.append(k)
            vs.append(v)
            bcs.append(beta_all[:, h:h + 1])
            gcs.append(gc)
            gls.append(glast[:, N_HEADS + h:N_HEADS + h + 1])
            egs.append(jnp.exp(gc))
            decays.append(jnp.exp(jnp.where(incl, gc - gr, NEG_BIG)))
        n = len(qs)
        qk_kk = [_dot_nt(jnp.concatenate([qs[i], ks[i]], axis=0), ks[i], mode) for i in range(n)]
        tm1 = [jnp.where(strict, -(bcs[i] * qk_kk[i][r:] * decays[i]), 0.0) for i in range(n)]
        pw = list(tm1)
        for _ in range(n_sq):
            pw = [_dot(pw[i], pw[i], inv_mode) for i in range(n)]
            tm1 = [tm1[i] + pw[i] + _dot(tm1[i], pw[i], inv_mode) for i in range(n)]
        rhs = [jnp.concatenate([vs[i] * bcs[i], ks[i] * (bcs[i] * egs[i])], axis=1) for i in range(n)]
        uw = [rhs[i] + _dot(tm1[i], rhs[i], mode) for i in range(n)]
        vnew, ointer = [], []
        for i, h in enumerate(heads):
            u = uw[i][:, :dh]
            w = uw[i][:, dh:]
            qe = qs[i] * egs[i]
            vnew_parts, ointer_parts = [], []
            for b in range(nb):
                rows = slice(b * sl, (b + 1) * sl)
                ws = _dot(jnp.concatenate([w[rows], qe[rows]], axis=0), s_scr[b, h], mode)
                vnew_parts.append(u[rows] - ws[:sl])
                ointer_parts.append(ws[sl:])
            vnew.append(vnew_parts[0] if nb == 1 else jnp.concatenate(vnew_parts, axis=0))
            ointer.append(ointer_parts[0] if nb == 1 else jnp.concatenate(ointer_parts, axis=0))
        for i, h in enumerate(heads):
            attn = jnp.where(incl, qk_kk[i][:r] * decays[i], 0.0)
            o = ointer[i] + _dot(attn, vnew[i], mode)
            z = z_ref[:, :, h * dh:(h + 1) * dh].reshape(r, dh)
            o_ref[:, :, h * dh:(h + 1) * dh] = _gated_rmsnorm(o, nw, z).reshape(nb, sl, dh).astype(o_ref.dtype)
        for i, h in enumerate(heads):
            ktil = ks[i] * jnp.exp(gls[i] - gcs[i])
            for b in range(nb):
                kt_b = ktil if nb == 1 else jnp.where(row_seq == b, ktil, 0.0)
                gl_b = gls[i][b * sl:b * sl + 1, :]
                s_scr[b, h] = s_scr[b, h] * jnp.exp(gl_b) + _dot_tn(kt_b, vnew[i], mode)

    carry_scr[...] = qkv_ref[:, sl - SUBLANES:sl, :]

    @pl.when(c == pl.num_programs(1) - 1)
    def _():
        sout_ref[...] = s_scr[...]


def _hgrn_kernel(*refs, nb, sl, n_valid, has_init, mode, head_group):
    if has_init:
        (q_ref, f_ref, i_ref, g_ref, lb_ref, nw_ref, s0_ref, o_ref, sout_ref, s_scr) = refs
    else:
        (q_ref, f_ref, i_ref, g_ref, lb_ref, nw_ref, o_ref, sout_ref, s_scr) = refs
    c = pl.program_id(1)
    r = nb * sl
    dh = D_HEAD
    width = N_HEADS * dh

    @pl.when(c == 0)
    def _():
        if has_init:
            s_scr[...] = s0_ref[...]
        else:
            s_scr[...] = jnp.zeros_like(s_scr)

    incl, _ = _chunk_masks(nb, sl)
    lmask = jnp.where(incl, 1.0, 0.0)
    valid = _row_valid(nb, sl, n_valid)
    masked = n_valid < sl
    rowid = lax.broadcasted_iota(jnp.int32, (r, 1), 0)
    row_seq = lax.shift_right_logical(rowid, int(math.log2(sl)))
    sub = min(SUB, sl)
    nblk = r // sub
    sub_shift = int(math.log2(sub))
    ri = lax.broadcasted_iota(jnp.int32, (r, r), 0)
    ci = lax.broadcasted_iota(jnp.int32, (r, r), 1)
    same_blk = lax.shift_right_logical(ri, sub_shift) == lax.shift_right_logical(ci, sub_shift)
    diag_mask = incl & same_blk
    cross_mask = incl & jnp.logical_not(same_blk)

    lb = lb_ref[...]
    f = lb + (1.0 - lb) * _sigmoid(f_ref[...].reshape(r, width))
    lf = jnp.log(f)
    k_all = 1.0 - f
    if masked:
        lf = jnp.where(valid, lf, 0.0)
        k_all = jnp.where(valid, k_all, 0.0)
    bcum = _masked_cumsum(lmask, lf)
    nw = nw_ref[...]

    def head_inputs(h):
        cols = slice(h * dh, (h + 1) * dh)
        q = _silu(q_ref[:, :, cols].reshape(r, dh)) * (dh ** -0.5)
        v = i_ref[:, :, cols].reshape(r, dh)
        if masked:
            v = jnp.where(valid, v, 0.0)
        return q, k_all[:, cols], v, bcum[:, cols]

    def intra_attn(q, k, bh):
        bmid = jnp.broadcast_to(bh.reshape(nblk, sub, dh)[:, sub // 2:sub // 2 + 1, :],
                                (nblk, sub, dh)).reshape(r, dh)
        attn = jnp.where(diag_mask, _dot_nt(q * jnp.exp(bh - bmid), k * jnp.exp(bmid - bh), mode), 0.0)
        if sl > sub:
            parts = [jnp.zeros((sub, r), F32)]
            for blk in range(1, nblk):
                start = blk * sub
                bref = bh[start - 1:start, :]
                qc = q[start:start + sub] * jnp.exp(bh[start:start + sub] - bref)
                kc = k * jnp.exp(jnp.minimum(bref - bh, 0.0))
                parts.append(_dot_nt(qc, kc, mode))
            attn = attn + jnp.where(cross_mask, jnp.concatenate(parts, axis=0), 0.0)
        return attn

    for h0 in range(0, N_HEADS, head_group):
        heads = range(h0, h0 + head_group)
        ins = [head_inputs(h) for h in heads]
        attns = [intra_attn(q, k, bh) for (q, k, v, bh) in ins]
        for i, h in enumerate(heads):
            q, k, v, bh = ins[i]
            qe = q * jnp.exp(bh)
            ointer_parts = []
            for b in range(nb):
                rows = slice(b * sl, (b + 1) * sl)
                ointer_parts.append(_dot(qe[rows], s_scr[b, h], mode))
            ointer = ointer_parts[0] if nb == 1 else jnp.concatenate(ointer_parts, axis=0)
            o = ointer + _dot(attns[i], v, mode)
            cols = slice(h * dh, (h + 1) * dh)
            gate = g_ref[:, :, cols].reshape(r, dh)
            o_ref[:, :, cols] = _gated_rmsnorm(o, nw, gate).reshape(nb, sl, dh).astype(o_ref.dtype)
        for i, h in enumerate(heads):
            q, k, v, bh = ins[i]
            blast = _last_row_bcast(bh, nb, sl)
            ktil = k * jnp.exp(blast - bh)
            pad = [jnp.zeros((LANES - r, dh), F32)] if r < LANES else []
            tr = jnp.concatenate([blast] + pad, axis=0).T
            for b in range(nb):
                kt_b = ktil if nb == 1 else jnp.where(row_seq == b, ktil, 0.0)
                dec_col = jnp.exp(tr[:, b * sl:b * sl + 1])
                s_scr[b, h] = s_scr[b, h] * dec_col + _dot_tn(kt_b, v, mode)

    @pl.when(c == pl.num_programs(1) - 1)
    def _():
        sout_ref[...] = s_scr[...]


def _mixers(proj3, ba3, blk_off, nseq, nb, sl, n_valid, conv_w, pvec, gdn_nw, lb, hgrn_nw,
            s_gdn0, conv0, s_hgrn0, shared_init, out_dtype, mode, inv_mode):
    length = proj3.shape[1]
    has_init = s_gdn0 is not None
    width = N_HEADS * D_HEAD
    conv_ch = 3 * width
    grid = (nseq // nb, length // sl)
    state_spec = pl.BlockSpec((nb, N_HEADS, D_HEAD, D_HEAD), lambda g, c: (g, 0, 0, 0))
    state_shape = jax.ShapeDtypeStruct((nseq, N_HEADS, D_HEAD, D_HEAD), F32)
    init_idx = (lambda g: 0) if shared_init else (lambda g: g)
    init_state_spec = pl.BlockSpec((nb, N_HEADS, D_HEAD, D_HEAD), lambda g, c: (init_idx(g), 0, 0, 0))
    head_group = 8 if nb == 1 else 4

    def col_spec(w, idx):
        return pl.BlockSpec((nb, sl, w), lambda g, c: (g + blk_off, c, idx))

    def out_spec(w):
        return pl.BlockSpec((nb, sl, w), lambda g, c: (g, c, 0))

    def const_spec(shape):
        return pl.BlockSpec(shape, lambda g, c: (0,) * len(shape))

    gdn_in = [proj3, proj3, ba3, conv_w, pvec, gdn_nw]
    gdn_specs = [col_spec(conv_ch, 0), col_spec(width, 3), col_spec(LANES, 0),
                 const_spec(conv_w.shape), const_spec(pvec.shape), const_spec(gdn_nw.shape)]
    if has_init:
        gdn_in += [s_gdn0, conv0]
        gdn_specs += [init_state_spec, pl.BlockSpec((nb, SUBLANES, conv_ch), lambda g, c: (init_idx(g), 0, 0))]
    o_gdn, s_gdn = pl.pallas_call(
        functools.partial(_gdn_kernel, nb=nb, sl=sl, n_valid=n_valid, has_init=has_init, mode=mode,
                          inv_mode=inv_mode, head_group=head_group),
        grid=grid,
        in_specs=gdn_specs,
        out_specs=[out_spec(width), state_spec],
        out_shape=[jax.ShapeDtypeStruct((nseq, length, width), out_dtype), state_shape],
        scratch_shapes=[pltpu.VMEM((nb, N_HEADS, D_HEAD, D_HEAD), F32), pltpu.VMEM((nb, SUBLANES, conv_ch), F32)],
        compiler_params=_params(("parallel", "arbitrary")),
    )(*gdn_in)

    hgrn_in = [proj3, proj3, proj3, proj3, lb, hgrn_nw]
    hgrn_specs = [col_spec(width, 4), col_spec(width, 5), col_spec(width, 6), col_spec(width, 7),
                  const_spec(lb.shape), const_spec(hgrn_nw.shape)]
    if has_init:
        hgrn_in += [s_hgrn0]
        hgrn_specs += [init_state_spec]
    o_hgrn, s_hgrn = pl.pallas_call(
        functools.partial(_hgrn_kernel, nb=nb, sl=sl, n_valid=n_valid, has_init=has_init, mode=mode,
                          head_group=head_group),
        grid=grid,
        in_specs=hgrn_specs,
        out_specs=[out_spec(width), state_spec],
        out_shape=[jax.ShapeDtypeStruct((nseq, length, width), out_dtype), state_shape],
        scratch_shapes=[pltpu.VMEM((nb, N_HEADS, D_HEAD, D_HEAD), F32)],
        compiler_params=_params(("parallel", "arbitrary")),
    )(*hgrn_in)
    return o_gdn, o_hgrn, s_gdn, s_hgrn


def _outproj_router_kernel(oap_ref, obp_ref, hpp_ref, oas_ref, obs_ref, hps_ref, wo_ref, n2_ref, wr_ref, br_ref,
                           hp2_ref, xn2_ref, ri_ref, rw_ref, cnt_ref, carry_scr, *, tm, n_p):
    i = pl.program_id(0)

    @pl.when(i == 0)
    def _():
        carry_scr[...] = jnp.zeros_like(carry_scr)

    body = functools.partial(_outproj_router_tile, wo_ref=wo_ref, n2_ref=n2_ref, wr_ref=wr_ref, br_ref=br_ref,
                             hp2_ref=hp2_ref, xn2_ref=xn2_ref, ri_ref=ri_ref, rw_ref=rw_ref,
                             carry_scr=carry_scr, tm=tm)

    @pl.when(i < n_p)
    def _():
        body(oap_ref[...], obp_ref[...], hpp_ref[...])

    @pl.when(i >= n_p)
    def _():
        body(oas_ref[...], obs_ref[...], hps_ref[...])

    @pl.when(i == pl.num_programs(0) - 1)
    def _():
        cnt_ref[...] = carry_scr[...]


def _outproj_router_tile(oa, ob, hp, *, wo_ref, n2_ref, wr_ref, br_ref, hp2_ref, xn2_ref, ri_ref, rw_ref,
                         carry_scr, tm):
    half = oa.shape[-1]
    mix = _dot(oa.astype(BF16), wo_ref[:half, :]) + _dot(ob.astype(BF16), wo_ref[half:, :])
    hp2 = hp + mix
    hp2_ref[...] = hp2
    xn2 = hp2 * lax.rsqrt(jnp.mean(hp2 * hp2, axis=-1, keepdims=True) + EPS) * n2_ref[...]
    xn2_ref[...] = xn2
    logits = _dot(xn2, wr_ref[...], "bf16x3") + br_ref[...]

    lane = lax.broadcasted_iota(jnp.int32, (tm, LANES), 1)
    lane_f = lane.astype(F32)
    far = float(4 * LANES)
    is_g = (lane >= N_EXPERTS) & (lane < N_EXPERTS + N_GROUPS)
    lg = jnp.where(is_g, logits, -jnp.inf)
    gmax = jnp.max(lg, axis=-1, keepdims=True)
    gsel = jnp.min(jnp.where(lg == gmax, lane_f, far), axis=-1, keepdims=True).astype(jnp.int32) - N_EXPERTS
    p_top = 1.0 / jnp.sum(jnp.where(is_g, jnp.exp(logits - gmax), 0.0), axis=-1, keepdims=True)
    in_grp = (lane < N_EXPERTS) & (lax.shift_right_logical(lane, 3) == gsel)
    le = jnp.where(in_grp, logits, -jnp.inf)
    m1 = jnp.max(le, axis=-1, keepdims=True)
    i1 = jnp.min(jnp.where(le == m1, lane_f, far), axis=-1, keepdims=True).astype(jnp.int32)
    le2 = jnp.where(lane == i1, -jnp.inf, le)
    m2 = jnp.max(le2, axis=-1, keepdims=True)
    i2 = jnp.min(jnp.where(le2 == m2, lane_f, far), axis=-1, keepdims=True).astype(jnp.int32)
    e2 = jnp.exp(m2 - m1)
    w1 = p_top / (1.0 + e2)
    w2 = p_top * e2 / (1.0 + e2)

    onehot = (lane == i1) | (lane == i2)
    onehot_f = jnp.where(onehot, 1.0, 0.0)
    tri = (lax.broadcasted_iota(jnp.int32, (tm, tm), 1) < lax.broadcasted_iota(jnp.int32, (tm, tm), 0))
    before = _dot(jnp.where(tri, 1.0, 0.0).astype(BF16), onehot_f.astype(BF16)) + carry_scr[...]
    r1 = jnp.sum(jnp.where(lane == i1, before, 0.0), axis=-1, keepdims=True).astype(jnp.int32)
    r2 = jnp.sum(jnp.where(lane == i2, before, 0.0), axis=-1, keepdims=True).astype(jnp.int32)
    carry_scr[...] = carry_scr[...] + jnp.sum(onehot_f, axis=0, keepdims=True)

    ri_ref[...] = jnp.where(lane == 0, i1, jnp.where(lane == 1, i2, jnp.where(lane == 2, r1, jnp.where(lane == 3, r2, 0))))
    rw_ref[...] = jnp.where(lane == 0, w1, jnp.where(lane == 1, w2, 0.0))


def _outproj_router(oa_p, ob_p, hp_p, oa_s, ob_s, hp_s, w_out, norm2_w, w_r, b_r, tm):
    (tp, d), ts = hp_p.shape, hp_s.shape[0]
    half = oa_p.shape[1]
    n_p, n_s = tp // tm, ts // tm
    t = tp + ts
    prow = lambda w: pl.BlockSpec((tm, w), lambda i: (jnp.minimum(i, n_p - 1), 0))
    srow = lambda w: pl.BlockSpec((tm, w), lambda i: (jnp.maximum(i - n_p, 0), 0))
    row = lambda w: pl.BlockSpec((tm, w), lambda i: (i, 0))
    const = lambda shape: pl.BlockSpec(shape, lambda i: (0,) * len(shape))
    return pl.pallas_call(
        functools.partial(_outproj_router_kernel, tm=tm, n_p=n_p),
        grid=(n_p + n_s,),
        in_specs=[prow(half), prow(half), prow(d), srow(half), srow(half), srow(d),
                  const(w_out.shape), const((1, d)), const(w_r.shape), const((1, LANES))],
        out_specs=[row(d), row(d), row(LANES), row(LANES), const((1, LANES))],
        out_shape=[jax.ShapeDtypeStruct((t, d), F32), jax.ShapeDtypeStruct((t, d), F32),
                   jax.ShapeDtypeStruct((t, LANES), jnp.int32), jax.ShapeDtypeStruct((t, LANES), F32),
                   jax.ShapeDtypeStruct((1, LANES), F32)],
        scratch_shapes=[pltpu.VMEM((1, LANES), F32)],
        compiler_params=_params(("arbitrary",)),
    )(oa_p, ob_p, hp_p, oa_s, ob_s, hp_s, w_out, norm2_w.reshape(1, d), w_r, b_r)


def _start_row_gather(idx_ref, base, n, src_hbm, dst, sem):
    def issue(r, carry):
        pltpu.make_async_copy(src_hbm.at[pl.ds(idx_ref[base + r], 1), :], dst.at[pl.ds(r, 1), :], sem).start()
        return carry
    lax.fori_loop(0, n, issue, 0, unroll=8)


def _wait_row_gather(n, src_hbm, dst, sem):
    pltpu.make_async_copy(src_hbm.at[pl.ds(0, n), :], dst, sem).wait()


def _moe_kernel(te_ref, st_ref, nx_ref, nu_ref, x_hbm, wg_hbm, wu_hbm, wd_hbm, o_ref,
                xbuf, xbf, wg_f32, wu_f32, wd_f32, wg_scr, wu_scr, wd_scr, sems, wsems, *, tile):
    i = pl.program_id(0)
    n_used = nu_ref[0]
    slot = lax.rem(i, 2)
    weights = ((wg_hbm, wg_f32, wg_scr), (wu_hbm, wu_f32, wu_scr), (wd_hbm, wd_f32, wd_scr))

    def start_weights(e):
        for k, (w_hbm, w_f32, _) in enumerate(weights):
            pltpu.make_async_copy(w_hbm.at[e], w_f32, wsems.at[k]).start()

    @pl.when((i == 0) & (n_used > 0))
    def _():
        start_weights(te_ref[0])
        _start_row_gather(st_ref, 0, tile, x_hbm, xbuf.at[0], sems.at[0])

    @pl.when(i < n_used)
    def _():
        e = te_ref[i]

        @pl.when((i == 0) | (e != te_ref[jnp.maximum(i - 1, 0)]))
        def _():
            for k, (w_hbm, w_f32, w_scr) in enumerate(weights):
                pltpu.make_async_copy(w_hbm.at[0], w_f32, wsems.at[k]).wait()
                w_scr[...] = w_f32[...].astype(BF16)
            nxt = nx_ref[e]

            @pl.when(nxt < N_EXPERTS)
            def _():
                start_weights(nxt)

        _wait_row_gather(tile, x_hbm, xbuf.at[slot], sems.at[slot])
        xbf[...] = xbuf[slot].astype(BF16)

        nbase = jnp.minimum(i + 1, n_used - 1) * tile
        dst = xbuf.at[1 - slot]
        sem = sems.at[1 - slot]
        de = wg_scr.shape[1]
        d = wd_scr.shape[1]
        n_gu, n_dn = 2, 4
        bounds = [tile * k // (n_gu + n_dn) for k in range(n_gu + n_dn + 1)]

        def start_rows(phase):
            for r in range(bounds[phase], bounds[phase + 1]):
                pltpu.make_async_copy(x_hbm.at[pl.ds(st_ref[nbase + r], 1), :], dst.at[pl.ds(r, 1), :], sem).start()

        x = xbf[...]
        hs = []
        for k in range(n_gu):
            start_rows(k)
            cols = slice(k * de // n_gu, (k + 1) * de // n_gu)
            hs.append((_silu(_dot(x, wg_scr[:, cols])) * _dot(x, wu_scr[:, cols])).astype(BF16))
        for k in range(n_dn):
            start_rows(n_gu + k)
            cols = slice(k * d // n_dn, (k + 1) * d // n_dn)
            acc = None
            for j in range(n_gu):
                part = _dot(hs[j], wd_scr[j * de // n_gu:(j + 1) * de // n_gu, cols])
                acc = part if acc is None else acc + part
            o_ref[:, cols] = acc

        @pl.when(i == n_used - 1)
        def _():
            _wait_row_gather(tile, x_hbm, dst, sem)

    @pl.when(i >= n_used)
    def _():
        o_ref[...] = jnp.zeros_like(o_ref)


def _moe(xn2, w_gate, w_up, w_down, tile_expert, slot_token, next_expert, n_used, tile):
    d = xn2.shape[1]
    n_tiles = tile_expert.shape[0]
    de = w_gate.shape[2]
    hbm = pl.BlockSpec(memory_space=pl.ANY)
    grid_spec = pltpu.PrefetchScalarGridSpec(
        num_scalar_prefetch=4,
        grid=(n_tiles,),
        in_specs=[hbm, hbm, hbm, hbm],
        out_specs=pl.BlockSpec((tile, d), lambda i, te, st, nx, nu: (i, 0)),
        scratch_shapes=[pltpu.VMEM((2, tile, d), F32), pltpu.VMEM((tile, d), BF16),
                        pltpu.VMEM((d, de), F32), pltpu.VMEM((d, de), F32), pltpu.VMEM((de, d), F32),
                        pltpu.VMEM((d, de), BF16), pltpu.VMEM((d, de), BF16), pltpu.VMEM((de, d), BF16),
                        pltpu.SemaphoreType.DMA((2,)), pltpu.SemaphoreType.DMA((3,))],
    )
    return pl.pallas_call(
        functools.partial(_moe_kernel, tile=tile),
        grid_spec=grid_spec,
        out_shape=jax.ShapeDtypeStruct((n_tiles * tile, d), F32),
        compiler_params=_params(("arbitrary",)),
    )(tile_expert, slot_token, next_expert, n_used, xn2, w_gate, w_up, w_down)


def _combine_kernel(p0_ref, p1_ref, ys_hbm, hp2_ref, rw_ref, fw_ref, o_ref, buf0, buf1, sems, *, tm, tile_off):
    i = pl.program_id(0)
    slot = lax.rem(i, 2)

    def start(step, s):
        base = (step + tile_off) * tm
        _start_row_gather(p0_ref, base, tm, ys_hbm, buf0.at[s], sems.at[0, s])
        _start_row_gather(p1_ref, base, tm, ys_hbm, buf1.at[s], sems.at[1, s])

    @pl.when(i == 0)
    def _():
        start(0, 0)

    @pl.when(i + 1 < pl.num_programs(0))
    def _():
        start(i + 1, 1 - slot)

    _wait_row_gather(tm, ys_hbm, buf0.at[slot], sems.at[0, slot])
    _wait_row_gather(tm, ys_hbm, buf1.at[slot], sems.at[1, slot])
    rw = rw_ref[...]
    y = hp2_ref[...] + rw[:, 0:1] * buf0[slot] + rw[:, 1:2] * buf1[slot]
    o_ref[...] = y * lax.rsqrt(jnp.mean(y * y, axis=-1, keepdims=True) + EPS) * fw_ref[...]


def _combine(ys, hp2, route_w, final_w, pos0, pos1, tm, tile_off, n_tiles):
    _, d = hp2.shape
    grid_spec = pltpu.PrefetchScalarGridSpec(
        num_scalar_prefetch=2,
        grid=(n_tiles,),
        in_specs=[pl.BlockSpec(memory_space=pl.ANY),
                  pl.BlockSpec((tm, d), lambda i, p0, p1: (i + tile_off, 0)),
                  pl.BlockSpec((tm, LANES), lambda i, p0, p1: (i + tile_off, 0)),
                  pl.BlockSpec((1, d), lambda i, p0, p1: (0, 0))],
        out_specs=pl.BlockSpec((tm, d), lambda i, p0, p1: (i, 0)),
        scratch_shapes=[pltpu.VMEM((2, tm, d), F32), pltpu.VMEM((2, tm, d), F32),
                        pltpu.SemaphoreType.DMA((2, 2))],
    )
    return pl.pallas_call(
        functools.partial(_combine_kernel, tm=tm, tile_off=tile_off),
        grid_spec=grid_spec,
        out_shape=jax.ShapeDtypeStruct((n_tiles * tm, d), F32),
        compiler_params=_params(("arbitrary",)),
    )(pos0, pos1, ys, hp2, route_w, final_w.reshape(1, d))


def kernel(x_prompt, x_sample, state_gdn, state_conv, state_hgrn, meta_tokens, norm1_w, w_in, conv_w, a_log,
           dt_bias, gdn_norm_w, lb_logits, hgrn_norm_w, w_out, norm2_w, w_router_group, b_router_group,
           w_router_expert, b_router_expert, w_gate, w_up, w_down, final_norm_w):
    bp, seq, d = x_prompt.shape
    bs, dec_seq, _ = x_sample.shape
    assert w_in.shape[0] == 1, "single-layer trunk"
    width = N_HEADS * D_HEAD
    conv_ch = 3 * width
    tile = 256
    tn = 512
    sl_s = SUBLANES
    nb_s = CHUNK // sl_s
    assert seq % CHUNK == 0 and N_META <= CHUNK and dec_seq <= sl_s and bs % nb_s == 0
    tp, ts, ts_pad = bp * seq, bs * dec_seq, bs * sl_s
    t_small = ts_pad + CHUNK
    tm = _row_tile(math.gcd(tp, ts), 256)
    tm_p = _row_tile(tp, 1024)

    xp = x_prompt.reshape(tp, d)
    x_small = jnp.concatenate([jnp.pad(x_sample, ((0, 0), (0, sl_s - dec_seq), (0, 0))).reshape(ts_pad, d),
                               jnp.zeros((CHUNK - N_META, d), F32), meta_tokens.astype(F32)], axis=0)

    wi = w_in[0]
    n_a = 4 * width // tn
    w_b = wi[:, 4 * width + 2 * N_HEADS:]
    w_ba = jnp.concatenate([wi[:, 4 * width:4 * width + 2 * N_HEADS],
                            jnp.zeros((d, LANES - 2 * N_HEADS), F32)], axis=1).astype(BF16)
    n_cols = 8 * width

    xn_p = _rmsnorm(xp, norm1_w[0], BF16, _row_tile(tp, 512))
    xn_s = _rmsnorm(x_small, norm1_w[0], BF16, _row_tile(t_small, 1024))
    proj_p = _inproj(xn_p, w_in, n_a, w_b, tm_p, tn)
    proj_s = _inproj(xn_s, w_in, n_a, w_b, t_small, tn)
    ba_p = _matmul(xn_p, w_ba, tm_p, LANES)
    ba_s = _matmul(xn_s, w_ba, t_small, LANES)

    pvec = jnp.zeros((2, LANES), F32)
    pvec = pvec.at[0, N_HEADS:2 * N_HEADS].set(a_log[0]).at[1, N_HEADS:2 * N_HEADS].set(dt_bias[0])
    lb = jnp.cumsum(jax.nn.softmax(lb_logits.astype(F32), axis=0), axis=0)[0].reshape(1, width)
    gdn_nw = gdn_norm_w[0].reshape(1, D_HEAD)
    hgrn_nw = hgrn_norm_w[0].reshape(1, D_HEAD)
    cw = conv_w[0]
    mix_args = (cw, pvec, gdn_nw, lb, hgrn_nw)

    _, _, sg_m, sh_m = _mixers(proj_s.reshape(t_small // CHUNK, CHUNK, n_cols),
                               ba_s.reshape(t_small // CHUNK, CHUNK, LANES), ts_pad // CHUNK, 1, 1, CHUNK, CHUNK,
                               *mix_args, None, None, None, False, F32, "bf16", "bf16x3")
    conv_m = proj_s[t_small - SUBLANES:, :conv_ch].reshape(1, SUBLANES, conv_ch)
    oa_p, ob_p, sg_p, sh_p = _mixers(proj_p.reshape(bp, seq, n_cols), ba_p.reshape(bp, seq, LANES), 0, bp, 1,
                                     CHUNK, CHUNK, *mix_args, sg_m, conv_m, sh_m, True, BF16, "bf16", "bf16x3")
    conv0 = jnp.pad(state_conv[0], ((0, 0), (SUBLANES - (CONV_W - 1), 0), (0, 0)))
    oa_s, ob_s, sg_s, sh_s = _mixers(proj_s.reshape(t_small // sl_s, sl_s, n_cols),
                                     ba_s.reshape(t_small // sl_s, sl_s, LANES), 0, bs, nb_s, sl_s, dec_seq,
                                     *mix_args, state_gdn[0], conv0, state_hgrn[0], False, F32, "bf16", "bf16x3")

    w_r = jnp.concatenate([w_router_expert[0], w_router_group[0],
                           jnp.zeros((d, LANES - N_EXPERTS - N_GROUPS), F32)], axis=1)
    b_r = jnp.concatenate([b_router_expert[0], b_router_group[0],
                           jnp.zeros((LANES - N_EXPERTS - N_GROUPS,), F32)]).reshape(1, LANES)
    t = tp + ts
    hp2, xn2, route_i, route_w, counts = _outproj_router(
        oa_p.reshape(tp, width), ob_p.reshape(tp, width), xp,
        oa_s[:, :dec_seq].reshape(ts, width), ob_s[:, :dec_seq].reshape(ts, width), x_sample.reshape(ts, d),
        w_out[0].astype(BF16), norm2_w[0], w_r, b_r, tm)

    cnt = counts[0, :N_EXPERTS].astype(jnp.int32)
    padded = (cnt + tile - 1) // tile * tile
    ends = jnp.cumsum(padded)
    offs = ends - padded
    eid = route_i[:, 0:2]
    expert_ids = jnp.arange(N_EXPERTS, dtype=jnp.int32)
    pos = jnp.sum(jnp.where(eid[:, :, None] == expert_ids, offs, 0), axis=-1) + route_i[:, 2:4]
    n_tiles = (2 * t) // tile + N_EXPERTS
    tile_start = jnp.arange(n_tiles, dtype=jnp.int32) * tile
    tile_expert = jnp.minimum(jnp.sum((ends[None, :] <= tile_start[:, None]).astype(jnp.int32), axis=1),
                              N_EXPERTS - 1)
    n_used = (ends[-1] // tile).astype(jnp.int32).reshape(1)
    tok = jnp.broadcast_to(jnp.arange(t, dtype=jnp.int32)[:, None], (t, 2))
    slot_token = jnp.zeros((n_tiles * tile,), jnp.int32).at[pos.reshape(-1)].set(tok.reshape(-1))

    later_active = (expert_ids[None, :] > expert_ids[:, None]) & (cnt[None, :] > 0)
    next_expert = jnp.min(jnp.where(later_active, expert_ids[None, :], N_EXPERTS), axis=1).astype(jnp.int32)
    ys = _moe(xn2, w_gate[0], w_up[0], w_down[0], tile_expert, slot_token, next_expert, n_used, tile)
    pos0, pos1 = pos[:, 0], pos[:, 1]
    y_prompt = _combine(ys, hp2, route_w, final_norm_w, pos0, pos1, tm, 0, tp // tm).reshape(bp, seq, d)
    y_sample = _combine(ys, hp2, route_w, final_norm_w, pos0, pos1, tm, tp // tm, ts // tm).reshape(bs, dec_seq, d)

    conv_p = proj_p.reshape(bp, seq, n_cols)[:, seq - (CONV_W - 1):, :conv_ch]
    u_s = proj_s[:ts_pad].reshape(bs, sl_s, n_cols)[:, :dec_seq, :conv_ch]
    conv_s = jnp.concatenate([state_conv[0], u_s], axis=1)[:, dec_seq:]
    return (y_prompt, y_sample, sg_p[None], conv_p[None], sh_p[None], sg_s[None], conv_s[None], sh_s[None])
```

```python
import functools
import math

import jax
import jax.numpy as jnp
from jax import lax
from jax.experimental import pallas as pl
from jax.experimental.pallas import tpu as pltpu

F32 = jnp.float32
BF16 = jnp.bfloat16
HIGHEST = lax.Precision.HIGHEST

EPS = 1e-6
N_META = 16
CONV_W = 4
N_HEADS = 8
D_HEAD = 128
N_GROUPS = 4
EXPERTS_PER_GROUP = 8
N_EXPERTS = N_GROUPS * EXPERTS_PER_GROUP

LANES = 128
SUBLANES = 8
CHUNK = 64
SUB = 16
VMEM_LIMIT = 56 * 1024 * 1024
NEG_BIG = -1e30
MIX_MODE = "bf16"
INV_MODE = "bf16"


def _sigmoid(x):
    return 1.0 / (1.0 + jnp.exp(-x))


def _silu(x):
    return x * _sigmoid(x)


def _softplus(x):
    return jnp.maximum(x, 0.0) + jnp.log1p(jnp.exp(-jnp.abs(x)))


def _split_bf16(a, pieces):
    out = []
    for _ in range(pieces - 1):
        hi = a.astype(BF16)
        out.append(hi)
        a = a - hi.astype(F32)
    out.append(a.astype(BF16))
    return out


def _mm(a, b, dims, mode):
    dg = functools.partial(lax.dot_general, dimension_numbers=(dims, ((), ())), preferred_element_type=F32)
    if mode == "f32":
        return dg(a, b, precision=HIGHEST)
    if mode == "bf16":
        return dg(a.astype(BF16), b.astype(BF16))
    assert mode == "bf16x3"
    ah, al = _split_bf16(a, 2)
    bh, bl = _split_bf16(b, 2)
    return dg(ah, bh) + dg(ah, bl) + dg(al, bh)


def _dot(a, b, mode="bf16"):
    return _mm(a, b, ((1,), (0,)), mode)


def _dot_nt(a, b, mode="bf16"):
    return _mm(a, b, ((1,), (1,)), mode)


def _dot_tn(a, b, mode="bf16"):
    return _mm(a, b, ((0,), (0,)), mode)


def _masked_cumsum(lmask, x):
    lm = lmask.astype(BF16)
    return sum(lax.dot_general(lm, p, (((1,), (0,)), ((), ())), preferred_element_type=F32)
               for p in _split_bf16(x, 3))


def _params(sem):
    return pltpu.CompilerParams(dimension_semantics=sem, vmem_limit_bytes=VMEM_LIMIT)


def _row_tile(n, target):
    best = max(c for c in range(16, min(n, target) + 1, 16) if n % c == 0)
    return best


def _rmsnorm_kernel(x_ref, w_ref, o_ref):
    x = x_ref[...]
    ms = jnp.mean(x * x, axis=-1, keepdims=True)
    o_ref[...] = (x * lax.rsqrt(ms + EPS) * w_ref[...]).astype(o_ref.dtype)


def _rmsnorm(x, w, out_dtype, tm):
    t, d = x.shape
    return pl.pallas_call(
        _rmsnorm_kernel,
        grid=(t // tm,),
        in_specs=[pl.BlockSpec((tm, d), lambda i: (i, 0)), pl.BlockSpec((1, d), lambda i: (0, 0))],
        out_specs=pl.BlockSpec((tm, d), lambda i: (i, 0)),
        out_shape=jax.ShapeDtypeStruct((t, d), out_dtype),
        compiler_params=_params(("parallel",)),
    )(x, w.reshape(1, d))


def _inproj_kernel(x_ref, wa_ref, wb_ref, o_ref, w_scr, *, n_a):
    j = pl.program_id(0)
    i = pl.program_id(1)

    @pl.when((i == 0) & (j < n_a))
    def _():
        w_scr[...] = wa_ref[...].astype(BF16)

    @pl.when((i == 0) & (j >= n_a))
    def _():
        w_scr[...] = wb_ref[...].astype(BF16)

    o_ref[...] = _dot(x_ref[...], w_scr[...])


def _inproj(x, w_a, n_a, w_b, tm, tn):
    t, k = x.shape
    n_b = w_b.shape[1] // tn
    return pl.pallas_call(
        functools.partial(_inproj_kernel, n_a=n_a),
        grid=(n_a + n_b, t // tm),
        in_specs=[pl.BlockSpec((tm, k), lambda j, i: (i, 0)),
                  pl.BlockSpec((k, tn), lambda j, i: (0, jnp.minimum(j, n_a - 1))),
                  pl.BlockSpec((k, tn), lambda j, i: (0, jnp.maximum(j - n_a, 0)))],
        out_specs=pl.BlockSpec((tm, tn), lambda j, i: (i, j)),
        out_shape=jax.ShapeDtypeStruct((t, (n_a + n_b) * tn), F32),
        scratch_shapes=[pltpu.VMEM((k, tn), BF16)],
        compiler_params=_params(("arbitrary", "arbitrary")),
    )(x, w_a, w_b)


def _matmul_kernel(x_ref, w_ref, o_ref):
    o_ref[...] = _dot(x_ref[...], w_ref[...])


def _matmul(x, w, tm, tn):
    t, k = x.shape
    n = w.shape[1]
    return pl.pallas_call(
        _matmul_kernel,
        grid=(n // tn, t // tm),
        in_specs=[pl.BlockSpec((tm, k), lambda j, i: (i, 0)), pl.BlockSpec((k, tn), lambda j, i: (0, j))],
        out_specs=pl.BlockSpec((tm, tn), lambda j, i: (i, j)),
        out_shape=jax.ShapeDtypeStruct((t, n), F32),
        compiler_params=_params(("parallel", "arbitrary")),
    )(x, w)


def _chunk_masks(nb, sl):
    r = nb * sl
    shift = int(math.log2(sl))
    ri = lax.broadcasted_iota(jnp.int32, (r, r), 0)
    ci = lax.broadcasted_iota(jnp.int32, (r, r), 1)
    same = lax.shift_right_logical(ri, shift) == lax.shift_right_logical(ci, shift)
    return same & (ci <= ri), same & (ci < ri)


def _row_valid(nb, sl, n_valid):
    rowid = lax.broadcasted_iota(jnp.int32, (nb * sl, 1), 0)
    return (rowid & (sl - 1)) < n_valid


def _last_row_bcast(x, nb, sl):
    c = x.shape[-1]
    x3 = x.reshape(nb, sl, c)
    return jnp.broadcast_to(x3[:, sl - 1:sl, :], (nb, sl, c)).reshape(nb * sl, c)


def _gated_rmsnorm(o, w, gate):
    return o * lax.rsqrt(jnp.mean(o * o, axis=-1, keepdims=True) + EPS) * w * _silu(gate)


def _gdn_kernel(*refs, nb, sl, n_valid, has_init, mode, inv_mode, head_group):
    if has_init:
        (qkv_ref, z_ref, ba_ref, cw_ref, pv_ref, nw_ref, s0_ref, c0_ref,
         o_ref, sout_ref, s_scr, carry_scr) = refs
    else:
        (qkv_ref, z_ref, ba_ref, cw_ref, pv_ref, nw_ref,
         o_ref, sout_ref, s_scr, carry_scr) = refs
    c = pl.program_id(1)
    r = nb * sl
    dh = D_HEAD

    @pl.when(c == 0)
    def _():
        if has_init:
            s_scr[...] = s0_ref[...]
            carry_scr[...] = c0_ref[...]
        else:
            s_scr[...] = jnp.zeros_like(s_scr)
            carry_scr[...] = jnp.zeros_like(carry_scr)

    incl, strict = _chunk_masks(nb, sl)
    lmask = jnp.where(incl, 1.0, 0.0)
    offdiag = jnp.where(strict, 1.0, 0.0)
    valid = _row_valid(nb, sl, n_valid)
    masked = n_valid < sl
    rowid = lax.broadcasted_iota(jnp.int32, (r, 1), 0)
    row_seq = lax.shift_right_logical(rowid, int(math.log2(sl)))
    n_sq = int(math.log2(sl)) - 1

    ba = ba_ref[...].reshape(r, LANES)
    pv = pv_ref[...]
    beta_all = _sigmoid(ba)
    g_all = -jnp.exp(pv[0:1]) * _softplus(ba + pv[1:2])
    if masked:
        g_all = jnp.where(valid, g_all, 0.0)
    gcum = _masked_cumsum(lmask, g_all)
    gcum_t = gcum.T
    glast = _last_row_bcast(gcum, nb, sl)
    cw = cw_ref[...]
    nw = nw_ref[...]

    def conv_slice(c0):
        u = qkv_ref[:, :, c0:c0 + dh]
        prev = carry_scr[:, :, c0:c0 + dh]
        full = jnp.concatenate([prev, u], axis=1)
        acc = None
        for j in range(CONV_W):
            off = SUBLANES - (CONV_W - 1) + j
            term = full[:, off:off + sl, :] * cw[j:j + 1, c0:c0 + dh]
            acc = term if acc is None else acc + term
        return _silu(acc).reshape(r, dh)

    for h0 in range(0, N_HEADS, head_group):
        heads = range(h0, h0 + head_group)
        qs, ks, vs, bcs, gcs, gls, egs, decays = [], [], [], [], [], [], [], []
        for h in heads:
            q = conv_slice(h * dh)
            k = conv_slice(N_HEADS * dh + h * dh)
            v = conv_slice(2 * N_HEADS * dh + h * dh)
            q = q * lax.rsqrt(jnp.sum(q * q, axis=-1, keepdims=True) + EPS) * (dh ** -0.5)
            k = k * lax.rsqrt(jnp.sum(k * k, axis=-1, keepdims=True) + EPS)
            if masked:
                q = jnp.where(valid, q, 0.0)
                k = jnp.where(valid, k, 0.0)
                v = jnp.where(valid, v, 0.0)
            gc = gcum[:, N_HEADS + h:N_HEADS + h + 1]
            gr = gcum_t[N_HEADS + h:N_HEADS + h + 1, :]
            qs.append(q)
            ks.append(k)
            vs.append(v)
            bcs.append(beta_all[:, h:h + 1])
            gcs.append(gc)
            gls.append(glast[:, N_HEADS + h:N_HEADS + h + 1])
            egs.append(jnp.exp(gc))
            decays.append(jnp.exp(jnp.where(incl, gc - gr, NEG_BIG)))
        n = len(qs)
        qk_kk = [_dot_nt(jnp.concatenate([qs[i], ks[i]], axis=0), ks[i], mode) for i in range(n)]
        tm1 = [qk_kk[i][r:] * (decays[i] * offdiag) * (-bcs[i]) for i in range(n)]
        pw = list(tm1)
        for _ in range(n_sq):
            pw = [_dot(pw[i], pw[i], inv_mode) for i in range(n)]
            tm1 = [tm1[i] + pw[i] + _dot(tm1[i], pw[i], inv_mode) for i in range(n)]
        rhs = [jnp.concatenate([vs[i] * bcs[i], ks[i] * (bcs[i] * egs[i])], axis=1) for i in range(n)]
        uw = [rhs[i] + _dot(tm1[i], rhs[i], mode) for i in range(n)]
        vnew, ointer = [], []
        for i, h in enumerate(heads):
            u = uw[i][:, :dh]
            w = uw[i][:, dh:]
            qe = qs[i] * egs[i]
            vnew_parts, ointer_parts = [], []
            for b in range(nb):
                rows = slice(b * sl, (b + 1) * sl)
                ws = _dot(jnp.concatenate([w[rows], qe[rows]], axis=0), s_scr[b, h], mode)
                vnew_parts.append(u[rows] - ws[:sl])
                ointer_parts.append(ws[sl:])
            vnew.append(vnew_parts[0] if nb == 1 else jnp.concatenate(vnew_parts, axis=0))
            ointer.append(ointer_parts[0] if nb == 1 else jnp.concatenate(ointer_parts, axis=0))
        for i, h in enumerate(heads):
            attn = qk_kk[i][:r] * decays[i]
            o = ointer[i] + _dot(attn, vnew[i], mode)
            z = z_ref[:, :, h * dh:(h + 1) * dh].reshape(r, dh)
            o_ref[:, :, h * dh:(h + 1) * dh] = _gated_rmsnorm(o, nw, z).reshape(nb, sl, dh).astype(o_ref.dtype)
        for i, h in enumerate(heads):
            ktil = ks[i] * jnp.exp(gls[i] - gcs[i])
            for b in range(nb):
                kt_b = ktil if nb == 1 else jnp.where(row_seq == b, ktil, 0.0)
                gl_b = gls[i][b * sl:b * sl + 1, :]
                s_scr[b, h] = s_scr[b, h] * jnp.exp(gl_b) + _dot_tn(kt_b, vnew[i], mode)

    carry_scr[...] = qkv_ref[:, sl - SUBLANES:sl, :]

    @pl.when(c == pl.num_programs(1) - 1)
    def _():
        sout_ref[...] = s_scr[...]


def _hgrn_kernel(*refs, nb, sl, n_valid, has_init, mode, head_group):
    if has_init:
        (q_ref, f_ref, i_ref, g_ref, lb_ref, nw_ref, s0_ref, o_ref, sout_ref, s_scr) = refs
    else:
        (q_ref, f_ref, i_ref, g_ref, lb_ref, nw_ref, o_ref, sout_ref, s_scr) = refs
    c = pl.program_id(1)
    r = nb * sl
    dh = D_HEAD
    width = N_HEADS * dh

    @pl.when(c == 0)
    def _():
        if has_init:
            s_scr[...] = s0_ref[...]
        else:
            s_scr[...] = jnp.zeros_like(s_scr)

    incl, _ = _chunk_masks(nb, sl)
    lmask = jnp.where(incl, 1.0, 0.0)
    valid = _row_valid(nb, sl, n_valid)
    masked = n_valid < sl
    rowid = lax.broadcasted_iota(jnp.int32, (r, 1), 0)
    row_seq = lax.shift_right_logical(rowid, int(math.log2(sl)))
    sub = min(SUB, sl)
    nblk = r // sub
    sub_shift = int(math.log2(sub))
    ri = lax.broadcasted_iota(jnp.int32, (r, r), 0)
    ci = lax.broadcasted_iota(jnp.int32, (r, r), 1)
    same_blk = lax.shift_right_logical(ri, sub_shift) == lax.shift_right_logical(ci, sub_shift)
    diag_mask = incl & same_blk
    cross_mask = incl & jnp.logical_not(same_blk)

    lb = lb_ref[...]
    f = lb + (1.0 - lb) * _sigmoid(f_ref[...].reshape(r, width))
    lf = jnp.log(f)
    k_all = 1.0 - f
    if masked:
        lf = jnp.where(valid, lf, 0.0)
        k_all = jnp.where(valid, k_all, 0.0)
    bcum = _masked_cumsum(lmask, lf)
    nw = nw_ref[...]

    def head_inputs(h):
        cols = slice(h * dh, (h + 1) * dh)
        q = _silu(q_ref[:, :, cols].reshape(r, dh)) * (dh ** -0.5)
        v = i_ref[:, :, cols].reshape(r, dh)
        if masked:
            v = jnp.where(valid, v, 0.0)
        return q, k_all[:, cols], v, bcum[:, cols]

    def intra_attn(q, k, bh):
        bmid = jnp.broadcast_to(bh.reshape(nblk, sub, dh)[:, sub // 2:sub // 2 + 1, :],
                                (nblk, sub, dh)).reshape(r, dh)
        attn = jnp.where(diag_mask, _dot_nt(q * jnp.exp(bh - bmid), k * jnp.exp(bmid - bh), mode), 0.0)
        if sl > sub:
            parts = [jnp.zeros((sub, r), F32)]
            for blk in range(1, nblk):
                start = blk * sub
                bref = bh[start - 1:start, :]
                qc = q[start:start + sub] * jnp.exp(bh[start:start + sub] - bref)
                kc = k * jnp.exp(jnp.minimum(bref - bh, 0.0))
                parts.append(_dot_nt(qc, kc, mode))
            attn = attn + jnp.where(cross_mask, jnp.concatenate(parts, axis=0), 0.0)
        return attn

    for h0 in range(0, N_HEADS, head_group):
        heads = range(h0, h0 + head_group)
        ins = [head_inputs(h) for h in heads]
        attns = [intra_attn(q, k, bh) for (q, k, v, bh) in ins]
        for i, h in enumerate(heads):
            q, k, v, bh = ins[i]
            qe = q * jnp.exp(bh)
            ointer_parts = []
            for b in range(nb):
                rows = slice(b * sl, (b + 1) * sl)
                ointer_parts.append(_dot(qe[rows], s_scr[b, h], mode))
            ointer = ointer_parts[0] if nb == 1 else jnp.concatenate(ointer_parts, axis=0)
            o = ointer + _dot(attns[i], v, mode)
            cols = slice(h * dh, (h + 1) * dh)
            gate = g_ref[:, :, cols].reshape(r, dh)
            o_ref[:, :, cols] = _gated_rmsnorm(o, nw, gate).reshape(nb, sl, dh).astype(o_ref.dtype)
        for i, h in enumerate(heads):
            q, k, v, bh = ins[i]
            blast = _last_row_bcast(bh, nb, sl)
            ktil = k * jnp.exp(blast - bh)
            pad = [jnp.zeros((LANES - r, dh), F32)] if r < LANES else []
            tr = jnp.concatenate([blast] + pad, axis=0).T
            for b in range(nb):
                kt_b = ktil if nb == 1 else jnp.where(row_seq == b, ktil, 0.0)
                dec_col = jnp.exp(tr[:, b * sl:b * sl + 1])
                s_scr[b, h] = s_scr[b, h] * dec_col + _dot_tn(kt_b, v, mode)

    @pl.when(c == pl.num_programs(1) - 1)
    def _():
        sout_ref[...] = s_scr[...]


def _mixers(proj3, ba3, blk_off, nseq, nb, sl, n_valid, conv_w, pvec, gdn_nw, lb, hgrn_nw,
            s_gdn0, conv0, s_hgrn0, shared_init, out_dtype, mode, inv_mode):
    length = proj3.shape[1]
    has_init = s_gdn0 is not None
    width = N_HEADS * D_HEAD
    conv_ch = 3 * width
    grid = (nseq // nb, length // sl)
    state_spec = pl.BlockSpec((nb, N_HEADS, D_HEAD, D_HEAD), lambda g, c: (g, 0, 0, 0))
    state_shape = jax.ShapeDtypeStruct((nseq, N_HEADS, D_HEAD, D_HEAD), F32)
    init_idx = (lambda g: 0) if shared_init else (lambda g: g)
    init_state_spec = pl.BlockSpec((nb, N_HEADS, D_HEAD, D_HEAD), lambda g, c: (init_idx(g), 0, 0, 0))
    head_group = 8 if nb == 1 else 4

    def col_spec(w, idx):
        return pl.BlockSpec((nb, sl, w), lambda g, c: (g + blk_off, c, idx))

    def out_spec(w):
        return pl.BlockSpec((nb, sl, w), lambda g, c: (g, c, 0))

    def const_spec(shape):
        return pl.BlockSpec(shape, lambda g, c: (0,) * len(shape))

    gdn_in = [proj3, proj3, ba3, conv_w, pvec, gdn_nw]
    gdn_specs = [col_spec(conv_ch, 0), col_spec(width, 3), col_spec(LANES, 0),
                 const_spec(conv_w.shape), const_spec(pvec.shape), const_spec(gdn_nw.shape)]
    if has_init:
        gdn_in += [s_gdn0, conv0]
        gdn_specs += [init_state_spec, pl.BlockSpec((nb, SUBLANES, conv_ch), lambda g, c: (init_idx(g), 0, 0))]
    o_gdn, s_gdn = pl.pallas_call(
        functools.partial(_gdn_kernel, nb=nb, sl=sl, n_valid=n_valid, has_init=has_init, mode=mode,
                          inv_mode=inv_mode, head_group=head_group),
        grid=grid,
        in_specs=gdn_specs,
        out_specs=[out_spec(width), state_spec],
        out_shape=[jax.ShapeDtypeStruct((nseq, length, width), out_dtype), state_shape],
        scratch_shapes=[pltpu.VMEM((nb, N_HEADS, D_HEAD, D_HEAD), F32), pltpu.VMEM((nb, SUBLANES, conv_ch), F32)],
        compiler_params=_params(("parallel", "arbitrary")),
    )(*gdn_in)

    hgrn_in = [proj3, proj3, proj3, proj3, lb, hgrn_nw]
    hgrn_specs = [col_spec(width, 4), col_spec(width, 5), col_spec(width, 6), col_spec(width, 7),
                  const_spec(lb.shape), const_spec(hgrn_nw.shape)]
    if has_init:
        hgrn_in += [s_hgrn0]
        hgrn_specs += [init_state_spec]
    o_hgrn, s_hgrn = pl.pallas_call(
        functools.partial(_hgrn_kernel, nb=nb, sl=sl, n_valid=n_valid, has_init=has_init, mode=mode,
                          head_group=head_group),
        grid=grid,
        in_specs=hgrn_specs,
        out_specs=[out_spec(width), state_spec],
        out_shape=[jax.ShapeDtypeStruct((nseq, length, width), out_dtype), state_shape],
        scratch_shapes=[pltpu.VMEM((nb, N_HEADS, D_HEAD, D_HEAD), F32)],
        compiler_params=_params(("parallel", "arbitrary")),
    )(*hgrn_in)
    return o_gdn, o_hgrn, s_gdn, s_hgrn


def _outproj_router_kernel(oap_ref, obp_ref, hpp_ref, oas_ref, obs_ref, hps_ref, wo_ref, n2_ref, wr_ref, br_ref,
                           hp2_ref, xn2_ref, ri_ref, rw_ref, cnt_ref, carry_scr, *, tm, n_p):
    i = pl.program_id(0)

    @pl.when(i == 0)
    def _():
        carry_scr[...] = jnp.zeros_like(carry_scr)

    body = functools.partial(_outproj_router_tile, wo_ref=wo_ref, n2_ref=n2_ref, wr_ref=wr_ref, br_ref=br_ref,
                             hp2_ref=hp2_ref, xn2_ref=xn2_ref, ri_ref=ri_ref, rw_ref=rw_ref,
                             carry_scr=carry_scr, tm=tm)

    @pl.when(i < n_p)
    def _():
        body(oap_ref[...], obp_ref[...], hpp_ref[...])

    @pl.when(i >= n_p)
    def _():
        body(oas_ref[...], obs_ref[...], hps_ref[...])

    @pl.when(i == pl.num_programs(0) - 1)
    def _():
        cnt_ref[...] = carry_scr[...]


def _outproj_router_tile(oa, ob, hp, *, wo_ref, n2_ref, wr_ref, br_ref, hp2_ref, xn2_ref, ri_ref, rw_ref,
                         carry_scr, tm):
    half = oa.shape[-1]
    mix = _dot(oa.astype(BF16), wo_ref[:half, :]) + _dot(ob.astype(BF16), wo_ref[half:, :])
    hp2 = hp + mix
    hp2_ref[...] = hp2
    xn2 = hp2 * lax.rsqrt(jnp.mean(hp2 * hp2, axis=-1, keepdims=True) + EPS) * n2_ref[...]
    xn2_ref[...] = xn2
    logits = _dot(xn2, wr_ref[...], "bf16x3") + br_ref[...]

    lane = lax.broadcasted_iota(jnp.int32, (tm, LANES), 1)
    lane_f = lane.astype(F32)
    far = float(4 * LANES)
    is_g = (lane >= N_EXPERTS) & (lane < N_EXPERTS + N_GROUPS)
    lg = jnp.where(is_g, logits, -jnp.inf)
    gmax = jnp.max(lg, axis=-1, keepdims=True)
    gsel = jnp.min(jnp.where(lg == gmax, lane_f, far), axis=-1, keepdims=True).astype(jnp.int32) - N_EXPERTS
    p_top = 1.0 / jnp.sum(jnp.where(is_g, jnp.exp(logits - gmax), 0.0), axis=-1, keepdims=True)
    in_grp = (lane < N_EXPERTS) & (lax.shift_right_logical(lane, 3) == gsel)
    le = jnp.where(in_grp, logits, -jnp.inf)
    m1 = jnp.max(le, axis=-1, keepdims=True)
    i1 = jnp.min(jnp.where(le == m1, lane_f, far), axis=-1, keepdims=True).astype(jnp.int32)
    le2 = jnp.where(lane == i1, -jnp.inf, le)
    m2 = jnp.max(le2, axis=-1, keepdims=True)
    i2 = jnp.min(jnp.where(le2 == m2, lane_f, far), axis=-1, keepdims=True).astype(jnp.int32)
    e2 = jnp.exp(m2 - m1)
    w1 = p_top / (1.0 + e2)
    w2 = p_top * e2 / (1.0 + e2)

    onehot = (lane == i1) | (lane == i2)
    onehot_f = jnp.where(onehot, 1.0, 0.0)
    tri = (lax.broadcasted_iota(jnp.int32, (tm, tm), 1) < lax.broadcasted_iota(jnp.int32, (tm, tm), 0))
    before = _dot(jnp.where(tri, 1.0, 0.0).astype(BF16), onehot_f.astype(BF16)) + carry_scr[...]
    r1 = jnp.sum(jnp.where(lane == i1, before, 0.0), axis=-1, keepdims=True).astype(jnp.int32)
    r2 = jnp.sum(jnp.where(lane == i2, before, 0.0), axis=-1, keepdims=True).astype(jnp.int32)
    carry_scr[...] = carry_scr[...] + jnp.sum(onehot_f, axis=0, keepdims=True)

    ri_ref[...] = jnp.where(lane == 0, i1, jnp.where(lane == 1, i2, jnp.where(lane == 2, r1, jnp.where(lane == 3, r2, 0))))
    rw_ref[...] = jnp.where(lane == 0, w1, jnp.where(lane == 1, w2, 0.0))


def _outproj_router(oa_p, ob_p, hp_p, oa_s, ob_s, hp_s, w_out, norm2_w, w_r, b_r, tm):
    (tp, d), ts = hp_p.shape, hp_s.shape[0]
    half = oa_p.shape[1]
    n_p, n_s = tp // tm, ts // tm
    t = tp + ts
    prow = lambda w: pl.BlockSpec((tm, w), lambda i: (jnp.minimum(i, n_p - 1), 0))
    srow = lambda w: pl.BlockSpec((tm, w), lambda i: (jnp.maximum(i - n_p, 0), 0))
    row = lambda w: pl.BlockSpec((tm, w), lambda i: (i, 0))
    const = lambda shape: pl.BlockSpec(shape, lambda i: (0,) * len(shape))
    return pl.pallas_call(
        functools.partial(_outproj_router_kernel, tm=tm, n_p=n_p),
        grid=(n_p + n_s,),
        in_specs=[prow(half), prow(half), prow(d), srow(half), srow(half), srow(d),
                  const(w_out.shape), const((1, d)), const(w_r.shape), const((1, LANES))],
        out_specs=[row(d), row(d), row(LANES), row(LANES), const((1, LANES))],
        out_shape=[jax.ShapeDtypeStruct((t, d), F32), jax.ShapeDtypeStruct((t, d), F32),
                   jax.ShapeDtypeStruct((t, LANES), jnp.int32), jax.ShapeDtypeStruct((t, LANES), F32),
                   jax.ShapeDtypeStruct((1, LANES), F32)],
        scratch_shapes=[pltpu.VMEM((1, LANES), F32)],
        compiler_params=_params(("arbitrary",)),
    )(oa_p, ob_p, hp_p, oa_s, ob_s, hp_s, w_out, norm2_w.reshape(1, d), w_r, b_r)


def _start_row_gather(idx_ref, base, n, src_hbm, dst, sem):
    def issue(r, carry):
        pltpu.make_async_copy(src_hbm.at[pl.ds(idx_ref[base + r], 1), :], dst.at[pl.ds(r, 1), :], sem).start()
        return carry
    lax.fori_loop(0, n, issue, 0, unroll=8)


def _wait_row_gather(n, src_hbm, dst, sem):
    pltpu.make_async_copy(src_hbm.at[pl.ds(0, n), :], dst, sem).wait()


def _moe_kernel(te_ref, st_ref, nx_ref, nu_ref, x_hbm, wg_hbm, wu_hbm, wd_hbm, o_ref,
                xbuf, wg_f32, wu_f32, wd_f32, wg_scr, wu_scr, wd_scr, sems, wsems, *, tile):
    i = pl.program_id(0)
    n_used = nu_ref[0]
    slot = lax.rem(i, 2)
    weights = ((wg_hbm, wg_f32, wg_scr), (wu_hbm, wu_f32, wu_scr), (wd_hbm, wd_f32, wd_scr))

    def start_weights(e):
        for k, (w_hbm, w_f32, _) in enumerate(weights):
            pltpu.make_async_copy(w_hbm.at[e], w_f32, wsems.at[k]).start()

    @pl.when((i == 0) & (n_used > 0))
    def _():
        start_weights(te_ref[0])
        _start_row_gather(st_ref, 0, tile, x_hbm, xbuf.at[0], sems.at[0])

    @pl.when(i < n_used)
    def _():
        @pl.when(i + 1 < n_used)
        def _():
            _start_row_gather(st_ref, (i + 1) * tile, tile, x_hbm, xbuf.at[1 - slot], sems.at[1 - slot])

        e = te_ref[i]

        @pl.when((i == 0) | (e != te_ref[jnp.maximum(i - 1, 0)]))
        def _():
            for k, (w_hbm, w_f32, w_scr) in enumerate(weights):
                pltpu.make_async_copy(w_hbm.at[0], w_f32, wsems.at[k]).wait()
                w_scr[...] = w_f32[...].astype(BF16)
            nxt = nx_ref[e]

            @pl.when(nxt < N_EXPERTS)
            def _():
                start_weights(nxt)

        _wait_row_gather(tile, x_hbm, xbuf.at[slot], sems.at[slot])
        x = xbuf[slot].astype(BF16)
        g = _dot(x, wg_scr[...])
        u = _dot(x, wu_scr[...])
        o_ref[...] = _dot((_silu(g) * u).astype(BF16), wd_scr[...])

    @pl.when(i >= n_used)
    def _():
        o_ref[...] = jnp.zeros_like(o_ref)


def _moe(xn2, w_gate, w_up, w_down, tile_expert, slot_token, next_expert, n_used, tile):
    d = xn2.shape[1]
    n_tiles = tile_expert.shape[0]
    de = w_gate.shape[2]
    hbm = pl.BlockSpec(memory_space=pl.ANY)
    grid_spec = pltpu.PrefetchScalarGridSpec(
        num_scalar_prefetch=4,
        grid=(n_tiles,),
        in_specs=[hbm, hbm, hbm, hbm],
        out_specs=pl.BlockSpec((tile, d), lambda i, te, st, nx, nu: (i, 0)),
        scratch_shapes=[pltpu.VMEM((2, tile, d), F32),
                        pltpu.VMEM((d, de), F32), pltpu.VMEM((d, de), F32), pltpu.VMEM((de, d), F32),
                        pltpu.VMEM((d, de), BF16), pltpu.VMEM((d, de), BF16), pltpu.VMEM((de, d), BF16),
                        pltpu.SemaphoreType.DMA((2,)), pltpu.SemaphoreType.DMA((3,))],
    )
    return pl.pallas_call(
        functools.partial(_moe_kernel, tile=tile),
        grid_spec=grid_spec,
        out_shape=jax.ShapeDtypeStruct((n_tiles * tile, d), F32),
        compiler_params=_params(("arbitrary",)),
    )(tile_expert, slot_token, next_expert, n_used, xn2, w_gate, w_up, w_down)


def _combine_kernel(p0_ref, p1_ref, ys_hbm, hp2_ref, rw_ref, fw_ref, o_ref, buf0, buf1, sems, *, tm, tile_off):
    i = pl.program_id(0)
    slot = lax.rem(i, 2)

    def start(step, s):
        base = (step + tile_off) * tm
        _start_row_gather(p0_ref, base, tm, ys_hbm, buf0.at[s], sems.at[0, s])
        _start_row_gather(p1_ref, base, tm, ys_hbm, buf1.at[s], sems.at[1, s])

    @pl.when(i == 0)
    def _():
        start(0, 0)

    @pl.when(i + 1 < pl.num_programs(0))
    def _():
        start(i + 1, 1 - slot)

    _wait_row_gather(tm, ys_hbm, buf0.at[slot], sems.at[0, slot])
    _wait_row_gather(tm, ys_hbm, buf1.at[slot], sems.at[1, slot])
    rw = rw_ref[...]
    y = hp2_ref[...] + rw[:, 0:1] * buf0[slot] + rw[:, 1:2] * buf1[slot]
    o_ref[...] = y * lax.rsqrt(jnp.mean(y * y, axis=-1, keepdims=True) + EPS) * fw_ref[...]


def _combine(ys, hp2, route_w, final_w, pos0, pos1, tm, tile_off, n_tiles):
    _, d = hp2.shape
    grid_spec = pltpu.PrefetchScalarGridSpec(
        num_scalar_prefetch=2,
        grid=(n_tiles,),
        in_specs=[pl.BlockSpec(memory_space=pl.ANY),
                  pl.BlockSpec((tm, d), lambda i, p0, p1: (i + tile_off, 0)),
                  pl.BlockSpec((tm, LANES), lambda i, p0, p1: (i + tile_off, 0)),
                  pl.BlockSpec((1, d), lambda i, p0, p1: (0, 0))],
        out_specs=pl.BlockSpec((tm, d), lambda i, p0, p1: (i, 0)),
        scratch_shapes=[pltpu.VMEM((2, tm, d), F32), pltpu.VMEM((2, tm, d), F32),
                        pltpu.SemaphoreType.DMA((2, 2))],
    )
    return pl.pallas_call(
        functools.partial(_combine_kernel, tm=tm, tile_off=tile_off),
        grid_spec=grid_spec,
        out_shape=jax.ShapeDtypeStruct((n_tiles * tm, d), F32),
        compiler_params=_params(("arbitrary",)),
    )(pos0, pos1, ys, hp2, route_w, final_w.reshape(1, d))


def kernel(x_prompt, x_sample, state_gdn, state_conv, state_hgrn, meta_tokens, norm1_w, w_in, conv_w, a_log,
           dt_bias, gdn_norm_w, lb_logits, hgrn_norm_w, w_out, norm2_w, w_router_group, b_router_group,
           w_router_expert, b_router_expert, w_gate, w_up, w_down, final_norm_w):
    bp, seq, d = x_prompt.shape
    bs, dec_seq, _ = x_sample.shape
    assert w_in.shape[0] == 1, "single-layer trunk"
    width = N_HEADS * D_HEAD
    conv_ch = 3 * width
    tile = 256
    tn = 512
    sl_s = SUBLANES
    nb_s = CHUNK // sl_s
    assert seq % CHUNK == 0 and N_META <= CHUNK and dec_seq <= sl_s and bs % nb_s == 0
    tp, ts, ts_pad = bp * seq, bs * dec_seq, bs * sl_s
    t_small = ts_pad + CHUNK
    tm = _row_tile(math.gcd(tp, ts), 256)
    tm_p = _row_tile(tp, 1024)

    xp = x_prompt.reshape(tp, d)
    x_small = jnp.concatenate([jnp.pad(x_sample, ((0, 0), (0, sl_s - dec_seq), (0, 0))).reshape(ts_pad, d),
                               jnp.zeros((CHUNK - N_META, d), F32), meta_tokens.astype(F32)], axis=0)

    wi = w_in[0]
    n_a = 4 * width // tn
    w_b = wi[:, 4 * width + 2 * N_HEADS:]
    w_ba = jnp.concatenate([wi[:, 4 * width:4 * width + 2 * N_HEADS],
                            jnp.zeros((d, LANES - 2 * N_HEADS), F32)], axis=1).astype(BF16)
    n_cols = 8 * width

    xn_p = _rmsnorm(xp, norm1_w[0], BF16, _row_tile(tp, 512))
    xn_s = _rmsnorm(x_small, norm1_w[0], BF16, _row_tile(t_small, 1024))
    w_a = wi[:, :4 * width]
    proj_p = _inproj(xn_p, w_a, n_a, w_b, tm_p, tn)
    proj_s = _inproj(xn_s, w_a, n_a, w_b, t_small, tn)
    ba_p = _matmul(xn_p, w_ba, tm_p, LANES)
    ba_s = _matmul(xn_s, w_ba, t_small, LANES)

    pvec = jnp.zeros((2, LANES), F32)
    pvec = pvec.at[0, N_HEADS:2 * N_HEADS].set(a_log[0]).at[1, N_HEADS:2 * N_HEADS].set(dt_bias[0])
    lb = jnp.cumsum(jax.nn.softmax(lb_logits.astype(F32), axis=0), axis=0)[0].reshape(1, width)
    gdn_nw = gdn_norm_w[0].reshape(1, D_HEAD)
    hgrn_nw = hgrn_norm_w[0].reshape(1, D_HEAD)
    cw = conv_w[0]
    mix_args = (cw, pvec, gdn_nw, lb, hgrn_nw)

    _, _, sg_m, sh_m = _mixers(proj_s.reshape(t_small // CHUNK, CHUNK, n_cols),
                               ba_s.reshape(t_small // CHUNK, CHUNK, LANES), ts_pad // CHUNK, 1, 1, CHUNK, CHUNK,
                               *mix_args, None, None, None, False, F32, MIX_MODE, INV_MODE)
    conv_m = proj_s[t_small - SUBLANES:, :conv_ch].reshape(1, SUBLANES, conv_ch)
    oa_p, ob_p, sg_p, sh_p = _mixers(proj_p.reshape(bp, seq, n_cols), ba_p.reshape(bp, seq, LANES), 0, bp, 1,
                                     CHUNK, CHUNK, *mix_args, sg_m, conv_m, sh_m, True, BF16, MIX_MODE, INV_MODE)
    conv0 = jnp.pad(state_conv[0], ((0, 0), (SUBLANES - (CONV_W - 1), 0), (0, 0)))
    oa_s, ob_s, sg_s, sh_s = _mixers(proj_s.reshape(t_small // sl_s, sl_s, n_cols),
                                     ba_s.reshape(t_small // sl_s, sl_s, LANES), 0, bs, nb_s, sl_s, dec_seq,
                                     *mix_args, state_gdn[0], conv0, state_hgrn[0], False, F32, MIX_MODE, INV_MODE)

    w_r = jnp.concatenate([w_router_expert[0], w_router_group[0],
                           jnp.zeros((d, LANES - N_EXPERTS - N_GROUPS), F32)], axis=1)
    b_r = jnp.concatenate([b_router_expert[0], b_router_group[0],
                           jnp.zeros((LANES - N_EXPERTS - N_GROUPS,), F32)]).reshape(1, LANES)
    t = tp + ts
    hp2, xn2, route_i, route_w, counts = _outproj_router(
        oa_p.reshape(tp, width), ob_p.reshape(tp, width), xp,
        oa_s[:, :dec_seq].reshape(ts, width), ob_s[:, :dec_seq].reshape(ts, width), x_sample.reshape(ts, d),
        w_out[0].astype(BF16), norm2_w[0], w_r, b_r, tm)

    cnt = counts[0, :N_EXPERTS].astype(jnp.int32)
    padded = (cnt + tile - 1) // tile * tile
    ends = jnp.cumsum(padded)
    offs = ends - padded
    eid = route_i[:, 0:2]
    expert_ids = jnp.arange(N_EXPERTS, dtype=jnp.int32)
    pos = jnp.sum(jnp.where(eid[:, :, None] == expert_ids, offs, 0), axis=-1) + route_i[:, 2:4]
    n_tiles = (2 * t) // tile + N_EXPERTS
    tile_start = jnp.arange(n_tiles, dtype=jnp.int32) * tile
    tile_expert = jnp.minimum(jnp.sum((ends[None, :] <= tile_start[:, None]).astype(jnp.int32), axis=1),
                              N_EXPERTS - 1)
    n_used = (ends[-1] // tile).astype(jnp.int32).reshape(1)
    tok = jnp.broadcast_to(jnp.arange(t, dtype=jnp.int32)[:, None], (t, 2))
    slot_token = jnp.zeros((n_tiles * tile,), jnp.int32).at[pos.reshape(-1)].set(
        tok.reshape(-1), unique_indices=True, mode="promise_in_bounds")

    later_active = (expert_ids[None, :] > expert_ids[:, None]) & (cnt[None, :] > 0)
    next_expert = jnp.min(jnp.where(later_active, expert_ids[None, :], N_EXPERTS), axis=1).astype(jnp.int32)
    ys = _moe(xn2, w_gate[0], w_up[0], w_down[0], tile_expert, slot_token, next_expert, n_used, tile)
    pos0, pos1 = pos[:, 0], pos[:, 1]
    y_prompt = _combine(ys, hp2, route_w, final_norm_w, pos0, pos1, tm, 0, tp // tm).reshape(bp, seq, d)
    y_sample = _combine(ys, hp2, route_w, final_norm_w, pos0, pos1, tm, tp // tm, ts // tm).reshape(bs, dec_seq, d)

    conv_p = proj_p.reshape(bp, seq, n_cols)[:, seq - (CONV_W - 1):, :conv_ch]
    u_s = proj_s.reshape(t_small // sl_s, sl_s, n_cols)[:bs, :dec_seq, :conv_ch]
    conv_s = jnp.concatenate([state_conv[0], u_s], axis=1)[:, dec_seq:]
    return (y_prompt, y_sample, sg_p[None], conv_p[None], sh_p[None], sg_s[None], conv_s[None], sh_s[None])
```

```python
import functools
import math

import jax
import jax.numpy as jnp
from jax import lax
from jax.experimental import pallas as pl
from jax.experimental.pallas import tpu as pltpu

F32 = jnp.float32
BF16 = jnp.bfloat16
HIGHEST = lax.Precision.HIGHEST

EPS = 1e-6
N_META = 16
CONV_W = 4
N_HEADS = 8
D_HEAD = 128
N_GROUPS = 4
EXPERTS_PER_GROUP = 8
N_EXPERTS = N_GROUPS * EXPERTS_PER_GROUP

LANES = 128
SUBLANES = 8
CHUNK = 64
SUB = 16
VMEM_LIMIT = 56 * 1024 * 1024
NEG_BIG = -1e30
MIX_MODE = "bf16"
INV_MODE = "bf16"


def _sigmoid(x):
    return 0.5 * jnp.tanh(0.5 * x) + 0.5


def _silu(x):
    return x * _sigmoid(x)


def _softplus(x):
    return jnp.maximum(x, 0.0) + jnp.log1p(jnp.exp(-jnp.abs(x)))


def _split_bf16(a, pieces):
    out = []
    for _ in range(pieces - 1):
        hi = a.astype(BF16)
        out.append(hi)
        a = a - hi.astype(F32)
    out.append(a.astype(BF16))
    return out


def _mm(a, b, dims, mode):
    dg = functools.partial(lax.dot_general, dimension_numbers=(dims, ((), ())), preferred_element_type=F32)
    if mode == "f32":
        return dg(a, b, precision=HIGHEST)
    if mode == "bf16":
        return dg(a.astype(BF16), b.astype(BF16))
    assert mode == "bf16x3"
    ah, al = _split_bf16(a, 2)
    bh, bl = _split_bf16(b, 2)
    return dg(ah, bh) + dg(ah, bl) + dg(al, bh)


def _dot(a, b, mode="bf16"):
    return _mm(a, b, ((1,), (0,)), mode)


def _dot_nt(a, b, mode="bf16"):
    return _mm(a, b, ((1,), (1,)), mode)


def _dot_tn(a, b, mode="bf16"):
    return _mm(a, b, ((0,), (0,)), mode)


def _masked_cumsum(lmask, x):
    lm = lmask.astype(BF16)
    return sum(lax.dot_general(lm, p, (((1,), (0,)), ((), ())), preferred_element_type=F32)
               for p in _split_bf16(x, 3))


def _params(sem):
    return pltpu.CompilerParams(dimension_semantics=sem, vmem_limit_bytes=VMEM_LIMIT)


def _row_tile(n, target):
    best = max(c for c in range(16, min(n, target) + 1, 16) if n % c == 0)
    return best


def _rmsnorm_kernel(x_ref, w_ref, o_ref):
    x = x_ref[...]
    ms = jnp.mean(x * x, axis=-1, keepdims=True)
    o_ref[...] = (x * lax.rsqrt(ms + EPS) * w_ref[...]).astype(o_ref.dtype)


def _rmsnorm(x, w, out_dtype, tm):
    t, d = x.shape
    return pl.pallas_call(
        _rmsnorm_kernel,
        grid=(t // tm,),
        in_specs=[pl.BlockSpec((tm, d), lambda i: (i, 0)), pl.BlockSpec((1, d), lambda i: (0, 0))],
        out_specs=pl.BlockSpec((tm, d), lambda i: (i, 0)),
        out_shape=jax.ShapeDtypeStruct((t, d), out_dtype),
        compiler_params=_params(("parallel",)),
    )(x, w.reshape(1, d))


def _inproj_kernel(x_ref, wa_ref, wt_hbm, o_ref, w_scr, wt_buf, sem, *, n_a, tn, shift):
    j = pl.program_id(0)
    i = pl.program_id(1)

    @pl.when((i == 0) & (j < n_a))
    def _():
        w_scr[...] = wa_ref[...].astype(BF16)

    @pl.when((i == 0) & (j >= n_a))
    def _():
        start = pl.multiple_of((j - n_a) * tn, LANES)
        copy = pltpu.make_async_copy(wt_hbm.at[:, pl.ds(start, tn + LANES)], wt_buf, sem)
        copy.start()
        copy.wait()
        w_scr[...] = wt_buf[:, shift:shift + tn].astype(BF16)

    o_ref[...] = _dot(x_ref[...], w_scr[...])


def _inproj(x, w_a, n_a, w_tail, n_b, shift, tm, tn):
    t, k = x.shape
    assert w_tail.shape[1] >= n_b * tn + LANES and 0 <= shift < LANES
    return pl.pallas_call(
        functools.partial(_inproj_kernel, n_a=n_a, tn=tn, shift=shift),
        grid=(n_a + n_b, t // tm),
        in_specs=[pl.BlockSpec((tm, k), lambda j, i: (i, 0)),
                  pl.BlockSpec((k, tn), lambda j, i: (0, jnp.minimum(j, n_a - 1))),
                  pl.BlockSpec(memory_space=pl.ANY)],
        out_specs=pl.BlockSpec((tm, tn), lambda j, i: (i, j)),
        out_shape=jax.ShapeDtypeStruct((t, (n_a + n_b) * tn), F32),
        scratch_shapes=[pltpu.VMEM((k, tn), BF16), pltpu.VMEM((k, tn + LANES), F32), pltpu.SemaphoreType.DMA(())],
        compiler_params=_params(("arbitrary", "arbitrary")),
    )(x, w_a, w_tail)


def _matmul_kernel(x_ref, w_ref, o_ref):
    o_ref[...] = _dot(x_ref[...], w_ref[...])


def _matmul(x, w, tm, tn):
    t, k = x.shape
    n = w.shape[1]
    return pl.pallas_call(
        _matmul_kernel,
        grid=(n // tn, t // tm),
        in_specs=[pl.BlockSpec((tm, k), lambda j, i: (i, 0)), pl.BlockSpec((k, tn), lambda j, i: (0, j))],
        out_specs=pl.BlockSpec((tm, tn), lambda j, i: (i, j)),
        out_shape=jax.ShapeDtypeStruct((t, n), F32),
        compiler_params=_params(("parallel", "arbitrary")),
    )(x, w)


def _chunk_masks(nb, sl):
    r = nb * sl
    shift = int(math.log2(sl))
    ri = lax.broadcasted_iota(jnp.int32, (r, r), 0)
    ci = lax.broadcasted_iota(jnp.int32, (r, r), 1)
    same = lax.shift_right_logical(ri, shift) == lax.shift_right_logical(ci, shift)
    return same & (ci <= ri), same & (ci < ri)


def _row_valid(nb, sl, n_valid):
    rowid = lax.broadcasted_iota(jnp.int32, (nb * sl, 1), 0)
    return (rowid & (sl - 1)) < n_valid


def _last_row_bcast(x, nb, sl):
    c = x.shape[-1]
    x3 = x.reshape(nb, sl, c)
    return jnp.broadcast_to(x3[:, sl - 1:sl, :], (nb, sl, c)).reshape(nb * sl, c)


def _gated_rmsnorm(o, w, gate):
    return o * lax.rsqrt(jnp.mean(o * o, axis=-1, keepdims=True) + EPS) * w * _silu(gate)


def _gdn_kernel(*refs, nb, sl, n_valid, has_init, mode, inv_mode, head_group):
    if has_init:
        (qkv_ref, z_ref, ba_ref, cw_ref, pv_ref, nw_ref, s0_ref, c0_ref,
         o_ref, sout_ref, s_scr, carry_scr) = refs
    else:
        (qkv_ref, z_ref, ba_ref, cw_ref, pv_ref, nw_ref,
         o_ref, sout_ref, s_scr, carry_scr) = refs
    c = pl.program_id(1)
    r = nb * sl
    dh = D_HEAD

    @pl.when(c == 0)
    def _():
        if has_init:
            s_scr[...] = s0_ref[...]
            carry_scr[...] = c0_ref[...]
        else:
            s_scr[...] = jnp.zeros_like(s_scr)
            carry_scr[...] = jnp.zeros_like(carry_scr)

    incl, strict = _chunk_masks(nb, sl)
    lmask = jnp.where(incl, 1.0, 0.0)
    offdiag = jnp.where(strict, 1.0, 0.0)
    valid = _row_valid(nb, sl, n_valid)
    masked = n_valid < sl
    rowid = lax.broadcasted_iota(jnp.int32, (r, 1), 0)
    row_seq = lax.shift_right_logical(rowid, int(math.log2(sl)))
    n_sq = int(math.log2(sl)) - 1

    ba = ba_ref[...].reshape(r, LANES)
    pv = pv_ref[...]
    beta_all = _sigmoid(ba)
    g_all = -jnp.exp(pv[0:1]) * _softplus(ba + pv[1:2])
    if masked:
        g_all = jnp.where(valid, g_all, 0.0)
    gcum = _masked_cumsum(lmask, g_all)
    gcum_t = gcum.T
    glast = _last_row_bcast(gcum, nb, sl)
    cw = cw_ref[...]
    nw = nw_ref[...]

    def conv_slice(c0):
        u = qkv_ref[:, :, c0:c0 + dh]
        prev = carry_scr[:, :, c0:c0 + dh]
        full = jnp.concatenate([prev, u], axis=1)
        acc = None
        for j in range(CONV_W):
            off = SUBLANES - (CONV_W - 1) + j
            term = full[:, off:off + sl, :] * cw[j:j + 1, c0:c0 + dh]
            acc = term if acc is None else acc + term
        return _silu(acc).reshape(r, dh)

    for h0 in range(0, N_HEADS, head_group):
        heads = range(h0, h0 + head_group)
        qs, ks, vs, bcs, gcs, gls, egs, decays = [], [], [], [], [], [], [], []
        for h in heads:
            q = conv_slice(h * dh)
            k = conv_slice(N_HEADS * dh + h * dh)
            v = conv_slice(2 * N_HEADS * dh + h * dh)
            q = q * lax.rsqrt(jnp.sum(q * q, axis=-1, keepdims=True) + EPS) * (dh ** -0.5)
            k = k * lax.rsqrt(jnp.sum(k * k, axis=-1, keepdims=True) + EPS)
            if masked:
                q = jnp.where(valid, q, 0.0)
                k = jnp.where(valid, k, 0.0)
                v = jnp.where(valid, v, 0.0)
            gc = gcum[:, N_HEADS + h:N_HEADS + h + 1]
            gr = gcum_t[N_HEADS + h:N_HEADS + h + 1, :]
            qs.append(q)
            ks.append(k)
            vs.append(v)
            bcs.append(beta_all[:, h:h + 1])
            gcs.append(gc)
            gls.append(glast[:, N_HEADS + h:N_HEADS + h + 1])
            egs.append(jnp.exp(gc))
            decays.append(jnp.exp(jnp.where(incl, gc - gr, NEG_BIG)))
        n = len(qs)
        qk_kk = [_dot_nt(jnp.concatenate([qs[i], ks[i]], axis=0), ks[i], mode) for i in range(n)]
        tm1 = [qk_kk[i][r:] * (decays[i] * offdiag) * (-bcs[i]) for i in range(n)]
        pw = list(tm1)
        for _ in range(n_sq):
            pw = [_dot(pw[i], pw[i], inv_mode) for i in range(n)]
            tm1 = [tm1[i] + pw[i] + _dot(tm1[i], pw[i], inv_mode) for i in range(n)]
        rhs = [jnp.concatenate([vs[i] * bcs[i], ks[i] * (bcs[i] * egs[i])], axis=1) for i in range(n)]
        uw = [rhs[i] + _dot(tm1[i], rhs[i], mode) for i in range(n)]
        vnew, ointer = [], []
        for i, h in enumerate(heads):
            u = uw[i][:, :dh]
            w = uw[i][:, dh:]
            qe = qs[i] * egs[i]
            vnew_parts, ointer_parts = [], []
            for b in range(nb):
                rows = slice(b * sl, (b + 1) * sl)
                ws = _dot(jnp.concatenate([w[rows], qe[rows]], axis=0), s_scr[b, h], mode)
                vnew_parts.append(u[rows] - ws[:sl])
                ointer_parts.append(ws[sl:])
            vnew.append(vnew_parts[0] if nb == 1 else jnp.concatenate(vnew_parts, axis=0))
            ointer.append(ointer_parts[0] if nb == 1 else jnp.concatenate(ointer_parts, axis=0))
        for i, h in enumerate(heads):
            attn = qk_kk[i][:r] * decays[i]
            o = ointer[i] + _dot(attn, vnew[i], mode)
            z = z_ref[:, :, h * dh:(h + 1) * dh].reshape(r, dh)
            o_ref[:, :, h * dh:(h + 1) * dh] = _gated_rmsnorm(o, nw, z).reshape(nb, sl, dh).astype(o_ref.dtype)
        for i, h in enumerate(heads):
            ktil = ks[i] * jnp.exp(gls[i] - gcs[i])
            for b in range(nb):
                kt_b = ktil if nb == 1 else jnp.where(row_seq == b, ktil, 0.0)
                gl_b = gls[i][b * sl:b * sl + 1, :]
                s_scr[b, h] = s_scr[b, h] * jnp.exp(gl_b) + _dot_tn(kt_b, vnew[i], mode)

    carry_scr[...] = qkv_ref[:, sl - SUBLANES:sl, :]

    @pl.when(c == pl.num_programs(1) - 1)
    def _():
        sout_ref[...] = s_scr[...]


def _hgrn_kernel(*refs, nb, sl, n_valid, has_init, mode, head_group):
    if has_init:
        (q_ref, f_ref, i_ref, g_ref, lb_ref, nw_ref, s0_ref, o_ref, sout_ref, s_scr) = refs
    else:
        (q_ref, f_ref, i_ref, g_ref, lb_ref, nw_ref, o_ref, sout_ref, s_scr) = refs
    c = pl.program_id(1)
    r = nb * sl
    dh = D_HEAD
    width = N_HEADS * dh

    @pl.when(c == 0)
    def _():
        if has_init:
            s_scr[...] = s0_ref[...]
        else:
            s_scr[...] = jnp.zeros_like(s_scr)

    incl, _ = _chunk_masks(nb, sl)
    lmask = jnp.where(incl, 1.0, 0.0)
    valid = _row_valid(nb, sl, n_valid)
    masked = n_valid < sl
    rowid = lax.broadcasted_iota(jnp.int32, (r, 1), 0)
    row_seq = lax.shift_right_logical(rowid, int(math.log2(sl)))
    sub = min(SUB, sl)
    nblk = r // sub
    sub_shift = int(math.log2(sub))
    ri = lax.broadcasted_iota(jnp.int32, (r, r), 0)
    ci = lax.broadcasted_iota(jnp.int32, (r, r), 1)
    same_blk = lax.shift_right_logical(ri, sub_shift) == lax.shift_right_logical(ci, sub_shift)
    diag_mask = incl & same_blk
    cross_mask = incl & jnp.logical_not(same_blk)

    lb = lb_ref[...]
    f = lb + (1.0 - lb) * _sigmoid(f_ref[...].reshape(r, width))
    lf = jnp.log(f)
    k_all = 1.0 - f
    if masked:
        lf = jnp.where(valid, lf, 0.0)
        k_all = jnp.where(valid, k_all, 0.0)
    bcum = _masked_cumsum(lmask, lf)
    nw = nw_ref[...]

    def head_inputs(h):
        cols = slice(h * dh, (h + 1) * dh)
        q = _silu(q_ref[:, :, cols].reshape(r, dh)) * (dh ** -0.5)
        v = i_ref[:, :, cols].reshape(r, dh)
        if masked:
            v = jnp.where(valid, v, 0.0)
        return q, k_all[:, cols], v, bcum[:, cols]

    def intra_attn(q, k, bh):
        bmid = jnp.broadcast_to(bh.reshape(nblk, sub, dh)[:, sub // 2:sub // 2 + 1, :],
                                (nblk, sub, dh)).reshape(r, dh)
        attn = jnp.where(diag_mask, _dot_nt(q * jnp.exp(bh - bmid), k * jnp.exp(bmid - bh), mode), 0.0)
        if sl > sub:
            parts = [jnp.zeros((sub, r), F32)]
            for blk in range(1, nblk):
                start = blk * sub
                bref = bh[start - 1:start, :]
                qc = q[start:start + sub] * jnp.exp(bh[start:start + sub] - bref)
                kc = k * jnp.exp(jnp.minimum(bref - bh, 0.0))
                parts.append(_dot_nt(qc, kc, mode))
            attn = attn + jnp.where(cross_mask, jnp.concatenate(parts, axis=0), 0.0)
        return attn

    for h0 in range(0, N_HEADS, head_group):
        heads = range(h0, h0 + head_group)
        ins = [head_inputs(h) for h in heads]
        attns = [intra_attn(q, k, bh) for (q, k, v, bh) in ins]
        for i, h in enumerate(heads):
            q, k, v, bh = ins[i]
            qe = q * jnp.exp(bh)
            ointer_parts = []
            for b in range(nb):
                rows = slice(b * sl, (b + 1) * sl)
                ointer_parts.append(_dot(qe[rows], s_scr[b, h], mode))
            ointer = ointer_parts[0] if nb == 1 else jnp.concatenate(ointer_parts, axis=0)
            o = ointer + _dot(attns[i], v, mode)
            cols = slice(h * dh, (h + 1) * dh)
            gate = g_ref[:, :, cols].reshape(r, dh)
            o_ref[:, :, cols] = _gated_rmsnorm(o, nw, gate).reshape(nb, sl, dh).astype(o_ref.dtype)
        for i, h in enumerate(heads):
            q, k, v, bh = ins[i]
            blast = _last_row_bcast(bh, nb, sl)
            ktil = k * jnp.exp(blast - bh)
            pad = [jnp.zeros((LANES - r, dh), F32)] if r < LANES else []
            tr = jnp.concatenate([blast] + pad, axis=0).T
            for b in range(nb):
                kt_b = ktil if nb == 1 else jnp.where(row_seq == b, ktil, 0.0)
                dec_col = jnp.exp(tr[:, b * sl:b * sl + 1])
                s_scr[b, h] = s_scr[b, h] * dec_col + _dot_tn(kt_b, v, mode)

    @pl.when(c == pl.num_programs(1) - 1)
    def _():
        sout_ref[...] = s_scr[...]


def _mixers(proj3, ba3, blk_off, nseq, nb, sl, n_valid, conv_w, pvec, gdn_nw, lb, hgrn_nw,
            s_gdn0, conv0, s_hgrn0, shared_init, out_dtype, mode, inv_mode):
    length = proj3.shape[1]
    has_init = s_gdn0 is not None
    width = N_HEADS * D_HEAD
    conv_ch = 3 * width
    grid = (nseq // nb, length // sl)
    state_spec = pl.BlockSpec((nb, N_HEADS, D_HEAD, D_HEAD), lambda g, c: (g, 0, 0, 0))
    state_shape = jax.ShapeDtypeStruct((nseq, N_HEADS, D_HEAD, D_HEAD), F32)
    init_idx = (lambda g: 0) if shared_init else (lambda g: g)
    init_state_spec = pl.BlockSpec((nb, N_HEADS, D_HEAD, D_HEAD), lambda g, c: (init_idx(g), 0, 0, 0))
    head_group = 8 if nb == 1 else 4

    def col_spec(w, idx):
        return pl.BlockSpec((nb, sl, w), lambda g, c: (g + blk_off, c, idx))

    def out_spec(w):
        return pl.BlockSpec((nb, sl, w), lambda g, c: (g, c, 0))

    def const_spec(shape):
        return pl.BlockSpec(shape, lambda g, c: (0,) * len(shape))

    gdn_in = [proj3, proj3, ba3, conv_w, pvec, gdn_nw]
    gdn_specs = [col_spec(conv_ch, 0), col_spec(width, 3), col_spec(LANES, 0),
                 const_spec(conv_w.shape), const_spec(pvec.shape), const_spec(gdn_nw.shape)]
    if has_init:
        gdn_in += [s_gdn0, conv0]
        gdn_specs += [init_state_spec, pl.BlockSpec((nb, SUBLANES, conv_ch), lambda g, c: (init_idx(g), 0, 0))]
    o_gdn, s_gdn = pl.pallas_call(
        functools.partial(_gdn_kernel, nb=nb, sl=sl, n_valid=n_valid, has_init=has_init, mode=mode,
                          inv_mode=inv_mode, head_group=head_group),
        grid=grid,
        in_specs=gdn_specs,
        out_specs=[out_spec(width), state_spec],
        out_shape=[jax.ShapeDtypeStruct((nseq, length, width), out_dtype), state_shape],
        scratch_shapes=[pltpu.VMEM((nb, N_HEADS, D_HEAD, D_HEAD), F32), pltpu.VMEM((nb, SUBLANES, conv_ch), F32)],
        compiler_params=_params(("parallel", "arbitrary")),
    )(*gdn_in)

    hgrn_in = [proj3, proj3, proj3, proj3, lb, hgrn_nw]
    hgrn_specs = [col_spec(width, 4), col_spec(width, 5), col_spec(width, 6), col_spec(width, 7),
                  const_spec(lb.shape), const_spec(hgrn_nw.shape)]
    if has_init:
        hgrn_in += [s_hgrn0]
        hgrn_specs += [init_state_spec]
    o_hgrn, s_hgrn = pl.pallas_call(
        functools.partial(_hgrn_kernel, nb=nb, sl=sl, n_valid=n_valid, has_init=has_init, mode=mode,
                          head_group=head_group),
        grid=grid,
        in_specs=hgrn_specs,
        out_specs=[out_spec(width), state_spec],
        out_shape=[jax.ShapeDtypeStruct((nseq, length, width), out_dtype), state_shape],
        scratch_shapes=[pltpu.VMEM((nb, N_HEADS, D_HEAD, D_HEAD), F32)],
        compiler_params=_params(("parallel", "arbitrary")),
    )(*hgrn_in)
    return o_gdn, o_hgrn, s_gdn, s_hgrn


def _outproj_router_kernel(oap_ref, obp_ref, hpp_ref, oas_ref, obs_ref, hps_ref, wo_ref, n2_ref, wr_ref, br_ref,
                           hp2_ref, xn2_ref, ri_ref, rw_ref, cnt_ref, carry_scr, *, tm, n_p):
    i = pl.program_id(0)

    @pl.when(i == 0)
    def _():
        carry_scr[...] = jnp.zeros_like(carry_scr)

    body = functools.partial(_outproj_router_tile, wo_ref=wo_ref, n2_ref=n2_ref, wr_ref=wr_ref, br_ref=br_ref,
                             hp2_ref=hp2_ref, xn2_ref=xn2_ref, ri_ref=ri_ref, rw_ref=rw_ref,
                             carry_scr=carry_scr, tm=tm)

    @pl.when(i < n_p)
    def _():
        body(oap_ref[...], obp_ref[...], hpp_ref[...])

    @pl.when(i >= n_p)
    def _():
        body(oas_ref[...], obs_ref[...], hps_ref[...])

    @pl.when(i == pl.num_programs(0) - 1)
    def _():
        cnt_ref[...] = carry_scr[...]


def _outproj_router_tile(oa, ob, hp, *, wo_ref, n2_ref, wr_ref, br_ref, hp2_ref, xn2_ref, ri_ref, rw_ref,
                         carry_scr, tm):
    half = oa.shape[-1]
    mix = _dot(oa.astype(BF16), wo_ref[:half, :]) + _dot(ob.astype(BF16), wo_ref[half:, :])
    hp2 = hp + mix
    hp2_ref[...] = hp2
    xn2 = hp2 * lax.rsqrt(jnp.mean(hp2 * hp2, axis=-1, keepdims=True) + EPS) * n2_ref[...]
    xn2_ref[...] = xn2
    logits = _dot(xn2, wr_ref[...], "bf16x3") + br_ref[...]

    lane = lax.broadcasted_iota(jnp.int32, (tm, LANES), 1)
    lane_f = lane.astype(F32)
    far = float(4 * LANES)
    is_g = (lane >= N_EXPERTS) & (lane < N_EXPERTS + N_GROUPS)
    lg = jnp.where(is_g, logits, -jnp.inf)
    gmax = jnp.max(lg, axis=-1, keepdims=True)
    gsel = jnp.min(jnp.where(lg == gmax, lane_f, far), axis=-1, keepdims=True).astype(jnp.int32) - N_EXPERTS
    p_top = 1.0 / jnp.sum(jnp.where(is_g, jnp.exp(logits - gmax), 0.0), axis=-1, keepdims=True)
    in_grp = (lane < N_EXPERTS) & (lax.shift_right_logical(lane, 3) == gsel)
    le = jnp.where(in_grp, logits, -jnp.inf)
    m1 = jnp.max(le, axis=-1, keepdims=True)
    i1 = jnp.min(jnp.where(le == m1, lane_f, far), axis=-1, keepdims=True).astype(jnp.int32)
    le2 = jnp.where(lane == i1, -jnp.inf, le)
    m2 = jnp.max(le2, axis=-1, keepdims=True)
    i2 = jnp.min(jnp.where(le2 == m2, lane_f, far), axis=-1, keepdims=True).astype(jnp.int32)
    e2 = jnp.exp(m2 - m1)
    w1 = p_top / (1.0 + e2)
    w2 = p_top * e2 / (1.0 + e2)

    onehot = (lane == i1) | (lane == i2)
    onehot_f = jnp.where(onehot, 1.0, 0.0)
    tri = (lax.broadcasted_iota(jnp.int32, (tm, tm), 1) < lax.broadcasted_iota(jnp.int32, (tm, tm), 0))
    before = _dot(jnp.where(tri, 1.0, 0.0).astype(BF16), onehot_f.astype(BF16)) + carry_scr[...]
    r1 = jnp.sum(jnp.where(lane == i1, before, 0.0), axis=-1, keepdims=True).astype(jnp.int32)
    r2 = jnp.sum(jnp.where(lane == i2, before, 0.0), axis=-1, keepdims=True).astype(jnp.int32)
    carry_scr[...] = carry_scr[...] + jnp.sum(onehot_f, axis=0, keepdims=True)

    ri_ref[...] = jnp.where(lane == 0, i1, jnp.where(lane == 1, i2, jnp.where(lane == 2, r1, jnp.where(lane == 3, r2, 0))))
    rw_ref[...] = jnp.where(lane == 0, w1, jnp.where(lane == 1, w2, 0.0))


def _outproj_router(oa_p, ob_p, hp_p, oa_s, ob_s, hp_s, w_out, norm2_w, w_r, b_r, tm):
    (tp, d), ts = hp_p.shape, hp_s.shape[0]
    half = oa_p.shape[1]
    n_p, n_s = tp // tm, ts // tm
    t = tp + ts
    prow = lambda w: pl.BlockSpec((tm, w), lambda i: (jnp.minimum(i, n_p - 1), 0))
    srow = lambda w: pl.BlockSpec((tm, w), lambda i: (jnp.maximum(i - n_p, 0), 0))
    row = lambda w: pl.BlockSpec((tm, w), lambda i: (i, 0))
    const = lambda shape: pl.BlockSpec(shape, lambda i: (0,) * len(shape))
    return pl.pallas_call(
        functools.partial(_outproj_router_kernel, tm=tm, n_p=n_p),
        grid=(n_p + n_s,),
        in_specs=[prow(half), prow(half), prow(d), srow(half), srow(half), srow(d),
                  const(w_out.shape), const((1, d)), const(w_r.shape), const((1, LANES))],
        out_specs=[row(d), row(d), row(LANES), row(LANES), const((1, LANES))],
        out_shape=[jax.ShapeDtypeStruct((t, d), F32), jax.ShapeDtypeStruct((t, d), F32),
                   jax.ShapeDtypeStruct((t, LANES), jnp.int32), jax.ShapeDtypeStruct((t, LANES), F32),
                   jax.ShapeDtypeStruct((1, LANES), F32)],
        scratch_shapes=[pltpu.VMEM((1, LANES), F32)],
        compiler_params=_params(("arbitrary",)),
    )(oa_p, ob_p, hp_p, oa_s, ob_s, hp_s, w_out, norm2_w.reshape(1, d), w_r, b_r)


def _start_row_gather(idx_ref, base, n, src_hbm, dst, sem):
    def issue(r, carry):
        pltpu.make_async_copy(src_hbm.at[pl.ds(idx_ref[base + r], 1), :], dst.at[pl.ds(r, 1), :], sem).start()
        return carry
    lax.fori_loop(0, n, issue, 0, unroll=8)


def _wait_row_gather(n, src_hbm, dst, sem):
    pltpu.make_async_copy(src_hbm.at[pl.ds(0, n), :], dst, sem).wait()


def _moe_kernel(te_ref, st_ref, nx_ref, nu_ref, x_hbm, wg_hbm, wu_hbm, wd_hbm, o_ref,
                xbuf, wg_f32, wu_f32, wd_f32, wg_scr, wu_scr, wd_scr, sems, wsems, *, tile):
    i = pl.program_id(0)
    n_used = nu_ref[0]
    slot = lax.rem(i, 2)
    weights = ((wg_hbm, wg_f32, wg_scr), (wu_hbm, wu_f32, wu_scr), (wd_hbm, wd_f32, wd_scr))

    def start_weights(e):
        for k, (w_hbm, w_f32, _) in enumerate(weights):
            pltpu.make_async_copy(w_hbm.at[e], w_f32, wsems.at[k]).start()

    @pl.when((i == 0) & (n_used > 0))
    def _():
        start_weights(te_ref[0])
        _start_row_gather(st_ref, 0, tile, x_hbm, xbuf.at[0], sems.at[0])

    @pl.when(i < n_used)
    def _():
        @pl.when(i + 1 < n_used)
        def _():
            _start_row_gather(st_ref, (i + 1) * tile, tile, x_hbm, xbuf.at[1 - slot], sems.at[1 - slot])

        e = te_ref[i]

        @pl.when((i == 0) | (e != te_ref[jnp.maximum(i - 1, 0)]))
        def _():
            for k, (w_hbm, w_f32, w_scr) in enumerate(weights):
                pltpu.make_async_copy(w_hbm.at[0], w_f32, wsems.at[k]).wait()
                w_scr[...] = w_f32[...].astype(BF16)
            nxt = nx_ref[e]

            @pl.when(nxt < N_EXPERTS)
            def _():
                start_weights(nxt)

        _wait_row_gather(tile, x_hbm, xbuf.at[slot], sems.at[slot])
        x = xbuf[slot].astype(BF16)
        g = _dot(x, wg_scr[...])
        u = _dot(x, wu_scr[...])
        o_ref[...] = _dot((_silu(g) * u).astype(BF16), wd_scr[...])

    @pl.when(i >= n_used)
    def _():
        o_ref[...] = jnp.zeros_like(o_ref)


def _moe(xn2, w_gate, w_up, w_down, tile_expert, slot_token, next_expert, n_used, tile):
    d = xn2.shape[1]
    n_tiles = tile_expert.shape[0]
    de = w_gate.shape[2]
    hbm = pl.BlockSpec(memory_space=pl.ANY)
    grid_spec = pltpu.PrefetchScalarGridSpec(
        num_scalar_prefetch=4,
        grid=(n_tiles,),
        in_specs=[hbm, hbm, hbm, hbm],
        out_specs=pl.BlockSpec((tile, d), lambda i, te, st, nx, nu: (i, 0)),
        scratch_shapes=[pltpu.VMEM((2, tile, d), F32),
                        pltpu.VMEM((d, de), F32), pltpu.VMEM((d, de), F32), pltpu.VMEM((de, d), F32),
                        pltpu.VMEM((d, de), BF16), pltpu.VMEM((d, de), BF16), pltpu.VMEM((de, d), BF16),
                        pltpu.SemaphoreType.DMA((2,)), pltpu.SemaphoreType.DMA((3,))],
    )
    return pl.pallas_call(
        functools.partial(_moe_kernel, tile=tile),
        grid_spec=grid_spec,
        out_shape=jax.ShapeDtypeStruct((n_tiles * tile, d), F32),
        compiler_params=_params(("arbitrary",)),
    )(tile_expert, slot_token, next_expert, n_used, xn2, w_gate, w_up, w_down)


def _combine_kernel(p0_ref, p1_ref, ys_hbm, hp2_ref, rw_ref, fw_ref, o_ref, buf0, buf1, sems, *, tm, tile_off):
    i = pl.program_id(0)
    slot = lax.rem(i, 2)

    def start(step, s):
        base = (step + tile_off) * tm
        _start_row_gather(p0_ref, base, tm, ys_hbm, buf0.at[s], sems.at[0, s])
        _start_row_gather(p1_ref, base, tm, ys_hbm, buf1.at[s], sems.at[1, s])

    @pl.when(i == 0)
    def _():
        start(0, 0)

    @pl.when(i + 1 < pl.num_programs(0))
    def _():
        start(i + 1, 1 - slot)

    _wait_row_gather(tm, ys_hbm, buf0.at[slot], sems.at[0, slot])
    _wait_row_gather(tm, ys_hbm, buf1.at[slot], sems.at[1, slot])
    rw = rw_ref[...]
    y = hp2_ref[...] + rw[:, 0:1] * buf0[slot] + rw[:, 1:2] * buf1[slot]
    o_ref[...] = y * lax.rsqrt(jnp.mean(y * y, axis=-1, keepdims=True) + EPS) * fw_ref[...]


def _combine(ys, hp2, route_w, final_w, pos0, pos1, tm, tile_off, n_tiles):
    _, d = hp2.shape
    grid_spec = pltpu.PrefetchScalarGridSpec(
        num_scalar_prefetch=2,
        grid=(n_tiles,),
        in_specs=[pl.BlockSpec(memory_space=pl.ANY),
                  pl.BlockSpec((tm, d), lambda i, p0, p1: (i + tile_off, 0)),
                  pl.BlockSpec((tm, LANES), lambda i, p0, p1: (i + tile_off, 0)),
                  pl.BlockSpec((1, d), lambda i, p0, p1: (0, 0))],
        out_specs=pl.BlockSpec((tm, d), lambda i, p0, p1: (i, 0)),
        scratch_shapes=[pltpu.VMEM((2, tm, d), F32), pltpu.VMEM((2, tm, d), F32),
                        pltpu.SemaphoreType.DMA((2, 2))],
    )
    return pl.pallas_call(
        functools.partial(_combine_kernel, tm=tm, tile_off=tile_off),
        grid_spec=grid_spec,
        out_shape=jax.ShapeDtypeStruct((n_tiles * tm, d), F32),
        compiler_params=_params(("arbitrary",)),
    )(pos0, pos1, ys, hp2, route_w, final_w.reshape(1, d))


def kernel(x_prompt, x_sample, state_gdn, state_conv, state_hgrn, meta_tokens, norm1_w, w_in, conv_w, a_log,
           dt_bias, gdn_norm_w, lb_logits, hgrn_norm_w, w_out, norm2_w, w_router_group, b_router_group,
           w_router_expert, b_router_expert, w_gate, w_up, w_down, final_norm_w):
    bp, seq, d = x_prompt.shape
    bs, dec_seq, _ = x_sample.shape
    assert w_in.shape[0] == 1, "single-layer trunk"
    width = N_HEADS * D_HEAD
    conv_ch = 3 * width
    tile = 256
    tn = 512
    sl_s = SUBLANES
    nb_s = CHUNK // sl_s
    assert seq % CHUNK == 0 and N_META <= CHUNK and dec_seq <= sl_s and bs % nb_s == 0
    tp, ts, ts_pad = bp * seq, bs * dec_seq, bs * sl_s
    t_small = ts_pad + CHUNK
    tm = _row_tile(math.gcd(tp, ts), 256)
    tm_p = _row_tile(tp, 1024)

    xp = x_prompt.reshape(tp, d)
    x_small = jnp.concatenate([jnp.pad(x_sample, ((0, 0), (0, sl_s - dec_seq), (0, 0))).reshape(ts_pad, d),
                               jnp.zeros((CHUNK - N_META, d), F32), meta_tokens.astype(F32)], axis=0)

    wi = w_in[0]
    n_a = n_b = 4 * width // tn
    shift = 2 * N_HEADS
    w_tail = jnp.pad(wi[:, 4 * width:], ((0, 0), (0, LANES - shift)))
    w_ba = jnp.pad(wi[:, 4 * width:4 * width + shift], ((0, 0), (0, LANES - shift))).astype(BF16)
    n_cols = 8 * width

    xn_p = _rmsnorm(xp, norm1_w[0], BF16, _row_tile(tp, 512))
    xn_s = _rmsnorm(x_small, norm1_w[0], BF16, _row_tile(t_small, 1024))
    w_a = wi[:, :4 * width]
    proj_p = _inproj(xn_p, w_a, n_a, w_tail, n_b, shift, tm_p, tn)
    proj_s = _inproj(xn_s, w_a, n_a, w_tail, n_b, shift, t_small, tn)
    ba_p = _matmul(xn_p, w_ba, tm_p, LANES)
    ba_s = _matmul(xn_s, w_ba, t_small, LANES)

    pvec = jnp.zeros((2, LANES), F32)
    pvec = pvec.at[0, N_HEADS:2 * N_HEADS].set(a_log[0]).at[1, N_HEADS:2 * N_HEADS].set(dt_bias[0])
    lb = jnp.cumsum(jax.nn.softmax(lb_logits.astype(F32), axis=0), axis=0)[0].reshape(1, width)
    gdn_nw = gdn_norm_w[0].reshape(1, D_HEAD)
    hgrn_nw = hgrn_norm_w[0].reshape(1, D_HEAD)
    cw = conv_w[0]
    mix_args = (cw, pvec, gdn_nw, lb, hgrn_nw)

    _, _, sg_m, sh_m = _mixers(proj_s.reshape(t_small // CHUNK, CHUNK, n_cols),
                               ba_s.reshape(t_small // CHUNK, CHUNK, LANES), ts_pad // CHUNK, 1, 1, CHUNK, CHUNK,
                               *mix_args, None, None, None, False, F32, MIX_MODE, INV_MODE)
    conv_m = proj_s[t_small - SUBLANES:, :conv_ch].reshape(1, SUBLANES, conv_ch)
    oa_p, ob_p, sg_p, sh_p = _mixers(proj_p.reshape(bp, seq, n_cols), ba_p.reshape(bp, seq, LANES), 0, bp, 1,
                                     CHUNK, CHUNK, *mix_args, sg_m, conv_m, sh_m, True, BF16, MIX_MODE, INV_MODE)
    conv0 = jnp.pad(state_conv[0], ((0, 0), (SUBLANES - (CONV_W - 1), 0), (0, 0)))
    oa_s, ob_s, sg_s, sh_s = _mixers(proj_s.reshape(t_small // sl_s, sl_s, n_cols),
                                     ba_s.reshape(t_small // sl_s, sl_s, LANES), 0, bs, nb_s, sl_s, dec_seq,
                                     *mix_args, state_gdn[0], conv0, state_hgrn[0], False, F32, MIX_MODE, INV_MODE)

    w_r = jnp.concatenate([w_router_expert[0], w_router_group[0],
                           jnp.zeros((d, LANES - N_EXPERTS - N_GROUPS), F32)], axis=1)
    b_r = jnp.concatenate([b_router_expert[0], b_router_group[0],
                           jnp.zeros((LANES - N_EXPERTS - N_GROUPS,), F32)]).reshape(1, LANES)
    t = tp + ts
    hp2, xn2, route_i, route_w, counts = _outproj_router(
        oa_p.reshape(tp, width), ob_p.reshape(tp, width), xp,
        oa_s[:, :dec_seq].reshape(ts, width), ob_s[:, :dec_seq].reshape(ts, width), x_sample.reshape(ts, d),
        w_out[0].astype(BF16), norm2_w[0], w_r, b_r, tm)

    cnt = counts[0, :N_EXPERTS].astype(jnp.int32)
    padded = (cnt + tile - 1) // tile * tile
    ends = jnp.cumsum(padded)
    offs = ends - padded
    eid = route_i[:, 0:2]
    expert_ids = jnp.arange(N_EXPERTS, dtype=jnp.int32)
    pos = jnp.sum(jnp.where(eid[:, :, None] == expert_ids, offs, 0), axis=-1) + route_i[:, 2:4]
    n_tiles = (2 * t) // tile + N_EXPERTS
    tile_start = jnp.arange(n_tiles, dtype=jnp.int32) * tile
    tile_expert = jnp.minimum(jnp.sum((ends[None, :] <= tile_start[:, None]).astype(jnp.int32), axis=1),
                              N_EXPERTS - 1)
    n_used = (ends[-1] // tile).astype(jnp.int32).reshape(1)
    tok = jnp.broadcast_to(jnp.arange(t, dtype=jnp.int32)[:, None], (t, 2))
    slot_token = jnp.zeros((n_tiles * tile,), jnp.int32).at[pos.reshape(-1)].set(
        tok.reshape(-1), unique_indices=True, mode="promise_in_bounds")

    later_active = (expert_ids[None, :] > expert_ids[:, None]) & (cnt[None, :] > 0)
    next_expert = jnp.min(jnp.where(later_active, expert_ids[None, :], N_EXPERTS), axis=1).astype(jnp.int32)
    ys = _moe(xn2, w_gate[0], w_up[0], w_down[0], tile_expert, slot_token, next_expert, n_used, tile)
    pos0, pos1 = pos[:, 0], pos[:, 1]
    y_prompt = _combine(ys, hp2, route_w, final_norm_w, pos0, pos1, tm, 0, tp // tm).reshape(bp, seq, d)
    y_sample = _combine(ys, hp2, route_w, final_norm_w, pos0, pos1, tm, tp // tm, ts // tm).reshape(bs, dec_seq, d)

    conv_p = proj_p.reshape(bp, seq, n_cols)[:, seq - (CONV_W - 1):, :conv_ch]
    u_s = proj_s.reshape(t_small // sl_s, sl_s, n_cols)[:bs, :dec_seq, :conv_ch]
    conv_s = jnp.concatenate([state_conv[0], u_s], axis=1)[:, dec_seq:]
    return (y_prompt, y_sample, sg_p[None], conv_p[None], sh_p[None], sg_s[None], conv_s[None], sh_s[None])
```

```python
import functools
import math

import jax
import jax.numpy as jnp
from jax import lax
from jax.experimental import pallas as pl
from jax.experimental.pallas import tpu as pltpu

F32 = jnp.float32
BF16 = jnp.bfloat16
HIGHEST = lax.Precision.HIGHEST

EPS = 1e-6
N_META = 16
CONV_W = 4
N_HEADS = 8
D_HEAD = 128
N_GROUPS = 4
EXPERTS_PER_GROUP = 8
N_EXPERTS = N_GROUPS * EXPERTS_PER_GROUP

LANES = 128
SUBLANES = 8
CHUNK = 64
SUB = 16
VMEM_LIMIT = 56 * 1024 * 1024
NEG_BIG = -1e30
MIX_MODE = "bf16"
INV_MODE = "bf16"


def _sigmoid(x):
    return 0.5 * jnp.tanh(0.5 * x) + 0.5


def _silu(x):
    return x * _sigmoid(x)


def _softplus(x):
    return jnp.maximum(x, 0.0) + jnp.log1p(jnp.exp(-jnp.abs(x)))


def _split_bf16(a, pieces):
    out = []
    for _ in range(pieces - 1):
        hi = a.astype(BF16)
        out.append(hi)
        a = a - hi.astype(F32)
    out.append(a.astype(BF16))
    return out


def _mm(a, b, dims, mode):
    dg = functools.partial(lax.dot_general, dimension_numbers=(dims, ((), ())), preferred_element_type=F32)
    if mode == "f32":
        return dg(a, b, precision=HIGHEST)
    if mode == "bf16":
        return dg(a.astype(BF16), b.astype(BF16))
    assert mode == "bf16x3"
    ah, al = _split_bf16(a, 2)
    bh, bl = _split_bf16(b, 2)
    return dg(ah, bh) + dg(ah, bl) + dg(al, bh)


def _dot(a, b, mode="bf16"):
    return _mm(a, b, ((1,), (0,)), mode)


def _dot_nt(a, b, mode="bf16"):
    return _mm(a, b, ((1,), (1,)), mode)


def _dot_tn(a, b, mode="bf16"):
    return _mm(a, b, ((0,), (0,)), mode)


def _masked_cumsum(lmask, x):
    lm = lmask.astype(BF16)
    return sum(lax.dot_general(lm, p, (((1,), (0,)), ((), ())), preferred_element_type=F32)
               for p in _split_bf16(x, 3))


def _params(sem):
    return pltpu.CompilerParams(dimension_semantics=sem, vmem_limit_bytes=VMEM_LIMIT)


def _row_tile(n, target):
    best = max(c for c in range(16, min(n, target) + 1, 16) if n % c == 0)
    return best


def _rmsnorm_kernel(x_ref, w_ref, o_ref):
    x = x_ref[...]
    ms = jnp.mean(x * x, axis=-1, keepdims=True)
    o_ref[...] = (x * lax.rsqrt(ms + EPS) * w_ref[...]).astype(o_ref.dtype)


def _rmsnorm(x, w, out_dtype, tm):
    t, d = x.shape
    return pl.pallas_call(
        _rmsnorm_kernel,
        grid=(t // tm,),
        in_specs=[pl.BlockSpec((tm, d), lambda i: (i, 0)), pl.BlockSpec((1, d), lambda i: (0, 0))],
        out_specs=pl.BlockSpec((tm, d), lambda i: (i, 0)),
        out_shape=jax.ShapeDtypeStruct((t, d), out_dtype),
        compiler_params=_params(("parallel",)),
    )(x, w.reshape(1, d))


def _inproj_kernel(x_ref, wa_ref, wb_ref, o_ref, w_scr, *, n_a):
    j = pl.program_id(0)
    i = pl.program_id(1)

    @pl.when((i == 0) & (j < n_a))
    def _():
        w_scr[...] = wa_ref[...].astype(BF16)

    @pl.when((i == 0) & (j >= n_a))
    def _():
        w_scr[...] = wb_ref[...].astype(BF16)

    o_ref[...] = _dot(x_ref[...], w_scr[...])


def _inproj(x, w_a, n_a, w_b, tm, tn):
    t, k = x.shape
    n_b = w_b.shape[1] // tn
    return pl.pallas_call(
        functools.partial(_inproj_kernel, n_a=n_a),
        grid=(n_a + n_b, t // tm),
        in_specs=[pl.BlockSpec((tm, k), lambda j, i: (i, 0)),
                  pl.BlockSpec((k, tn), lambda j, i: (0, jnp.minimum(j, n_a - 1))),
                  pl.BlockSpec((k, tn), lambda j, i: (0, jnp.maximum(j - n_a, 0)))],
        out_specs=pl.BlockSpec((tm, tn), lambda j, i: (i, j)),
        out_shape=jax.ShapeDtypeStruct((t, (n_a + n_b) * tn), F32),
        scratch_shapes=[pltpu.VMEM((k, tn), BF16)],
        compiler_params=_params(("arbitrary", "arbitrary")),
    )(x, w_a, w_b)


def _matmul_kernel(x_ref, w_ref, o_ref):
    o_ref[...] = _dot(x_ref[...], w_ref[...])


def _matmul(x, w, tm, tn):
    t, k = x.shape
    n = w.shape[1]
    return pl.pallas_call(
        _matmul_kernel,
        grid=(n // tn, t // tm),
        in_specs=[pl.BlockSpec((tm, k), lambda j, i: (i, 0)), pl.BlockSpec((k, tn), lambda j, i: (0, j))],
        out_specs=pl.BlockSpec((tm, tn), lambda j, i: (i, j)),
        out_shape=jax.ShapeDtypeStruct((t, n), F32),
        compiler_params=_params(("parallel", "arbitrary")),
    )(x, w)


def _chunk_masks(nb, sl):
    r = nb * sl
    shift = int(math.log2(sl))
    ri = lax.broadcasted_iota(jnp.int32, (r, r), 0)
    ci = lax.broadcasted_iota(jnp.int32, (r, r), 1)
    same = lax.shift_right_logical(ri, shift) == lax.shift_right_logical(ci, shift)
    return same & (ci <= ri), same & (ci < ri)


def _row_valid(nb, sl, n_valid):
    rowid = lax.broadcasted_iota(jnp.int32, (nb * sl, 1), 0)
    return (rowid & (sl - 1)) < n_valid


def _last_row_bcast(x, nb, sl):
    c = x.shape[-1]
    x3 = x.reshape(nb, sl, c)
    return jnp.broadcast_to(x3[:, sl - 1:sl, :], (nb, sl, c)).reshape(nb * sl, c)


def _gated_rmsnorm(o, w, gate):
    return o * lax.rsqrt(jnp.mean(o * o, axis=-1, keepdims=True) + EPS) * w * _silu(gate)


def _gdn_kernel(*refs, nb, sl, n_valid, has_init, mode, inv_mode, head_group):
    if has_init:
        (qkv_ref, z_ref, ba_ref, cw_ref, pv_ref, nw_ref, s0_ref, c0_ref,
         o_ref, sout_ref, s_scr, carry_scr) = refs
    else:
        (qkv_ref, z_ref, ba_ref, cw_ref, pv_ref, nw_ref,
         o_ref, sout_ref, s_scr, carry_scr) = refs
    c = pl.program_id(1)
    r = nb * sl
    dh = D_HEAD

    @pl.when(c == 0)
    def _():
        if has_init:
            s_scr[...] = s0_ref[...]
            carry_scr[...] = c0_ref[...]
        else:
            s_scr[...] = jnp.zeros_like(s_scr)
            carry_scr[...] = jnp.zeros_like(carry_scr)

    incl, strict = _chunk_masks(nb, sl)
    lmask = jnp.where(incl, 1.0, 0.0)
    offdiag = jnp.where(strict, 1.0, 0.0)
    valid = _row_valid(nb, sl, n_valid)
    masked = n_valid < sl
    rowid = lax.broadcasted_iota(jnp.int32, (r, 1), 0)
    row_seq = lax.shift_right_logical(rowid, int(math.log2(sl)))
    n_sq = int(math.log2(sl)) - 1

    ba = ba_ref[...].reshape(r, LANES)
    pv = pv_ref[...]
    beta_all = _sigmoid(ba)
    g_all = -jnp.exp(pv[0:1]) * _softplus(ba + pv[1:2])
    if masked:
        g_all = jnp.where(valid, g_all, 0.0)
    gcum = _masked_cumsum(lmask, g_all)
    gcum_t = gcum.T
    glast = _last_row_bcast(gcum, nb, sl)
    cw = cw_ref[...]
    nw = nw_ref[...]

    def conv_slice(c0):
        u = qkv_ref[:, :, c0:c0 + dh]
        prev = carry_scr[:, :, c0:c0 + dh]
        full = jnp.concatenate([prev, u], axis=1)
        acc = None
        for j in range(CONV_W):
            off = SUBLANES - (CONV_W - 1) + j
            term = full[:, off:off + sl, :] * cw[j:j + 1, c0:c0 + dh]
            acc = term if acc is None else acc + term
        return _silu(acc).reshape(r, dh)

    for h0 in range(0, N_HEADS, head_group):
        heads = range(h0, h0 + head_group)
        qs, ks, vs, bcs, gcs, gls, egs, decays = [], [], [], [], [], [], [], []
        for h in heads:
            q = conv_slice(h * dh)
            k = conv_slice(N_HEADS * dh + h * dh)
            v = conv_slice(2 * N_HEADS * dh + h * dh)
            q = q * lax.rsqrt(jnp.sum(q * q, axis=-1, keepdims=True) + EPS) * (dh ** -0.5)
            k = k * lax.rsqrt(jnp.sum(k * k, axis=-1, keepdims=True) + EPS)
            if masked:
                q = jnp.where(valid, q, 0.0)
                k = jnp.where(valid, k, 0.0)
                v = jnp.where(valid, v, 0.0)
            gc = gcum[:, N_HEADS + h:N_HEADS + h + 1]
            gr = gcum_t[N_HEADS + h:N_HEADS + h + 1, :]
            qs.append(q)
            ks.append(k)
            vs.append(v)
            bcs.append(beta_all[:, h:h + 1])
            gcs.append(gc)
            gls.append(glast[:, N_HEADS + h:N_HEADS + h + 1])
            egs.append(jnp.exp(gc))
            decays.append(jnp.exp(jnp.where(incl, gc - gr, NEG_BIG)))
        n = len(qs)
        qk_kk = [_dot_nt(jnp.concatenate([qs[i], ks[i]], axis=0), ks[i], mode) for i in range(n)]
        tm1 = [qk_kk[i][r:] * (decays[i] * offdiag) * (-bcs[i]) for i in range(n)]
        pw = list(tm1)
        for _ in range(n_sq):
            pw = [_dot(pw[i], pw[i], inv_mode) for i in range(n)]
            tm1 = [tm1[i] + pw[i] + _dot(tm1[i], pw[i], inv_mode) for i in range(n)]
        rhs = [jnp.concatenate([vs[i] * bcs[i], ks[i] * (bcs[i] * egs[i])], axis=1) for i in range(n)]
        uw = [rhs[i] + _dot(tm1[i], rhs[i], mode) for i in range(n)]
        vnew, ointer = [], []
        for i, h in enumerate(heads):
            u = uw[i][:, :dh]
            w = uw[i][:, dh:]
            qe = qs[i] * egs[i]
            vnew_parts, ointer_parts = [], []
            for b in range(nb):
                rows = slice(b * sl, (b + 1) * sl)
                ws = _dot(jnp.concatenate([w[rows], qe[rows]], axis=0), s_scr[b, h], mode)
                vnew_parts.append(u[rows] - ws[:sl])
                ointer_parts.append(ws[sl:])
            vnew.append(vnew_parts[0] if nb == 1 else jnp.concatenate(vnew_parts, axis=0))
            ointer.append(ointer_parts[0] if nb == 1 else jnp.concatenate(ointer_parts, axis=0))
        for i, h in enumerate(heads):
            attn = qk_kk[i][:r] * decays[i]
            o = ointer[i] + _dot(attn, vnew[i], mode)
            z = z_ref[:, :, h * dh:(h + 1) * dh].reshape(r, dh)
            o_ref[:, :, h * dh:(h + 1) * dh] = _gated_rmsnorm(o, nw, z).reshape(nb, sl, dh).astype(o_ref.dtype)
        for i, h in enumerate(heads):
            ktil = ks[i] * jnp.exp(gls[i] - gcs[i])
            for b in range(nb):
                kt_b = ktil if nb == 1 else jnp.where(row_seq == b, ktil, 0.0)
                gl_b = gls[i][b * sl:b * sl + 1, :]
                s_scr[b, h] = s_scr[b, h] * jnp.exp(gl_b) + _dot_tn(kt_b, vnew[i], mode)

    carry_scr[...] = qkv_ref[:, sl - SUBLANES:sl, :]

    @pl.when(c == pl.num_programs(1) - 1)
    def _():
        sout_ref[...] = s_scr[...]


def _hgrn_kernel(*refs, nb, sl, n_valid, has_init, mode, head_group):
    if has_init:
        (q_ref, f_ref, i_ref, g_ref, lb_ref, nw_ref, s0_ref, o_ref, sout_ref, s_scr) = refs
    else:
        (q_ref, f_ref, i_ref, g_ref, lb_ref, nw_ref, o_ref, sout_ref, s_scr) = refs
    c = pl.program_id(1)
    r = nb * sl
    dh = D_HEAD
    width = N_HEADS * dh

    @pl.when(c == 0)
    def _():
        if has_init:
            s_scr[...] = s0_ref[...]
        else:
            s_scr[...] = jnp.zeros_like(s_scr)

    incl, _ = _chunk_masks(nb, sl)
    lmask = jnp.where(incl, 1.0, 0.0)
    valid = _row_valid(nb, sl, n_valid)
    masked = n_valid < sl
    rowid = lax.broadcasted_iota(jnp.int32, (r, 1), 0)
    row_seq = lax.shift_right_logical(rowid, int(math.log2(sl)))
    sub = min(SUB, sl)
    nblk = r // sub
    sub_shift = int(math.log2(sub))
    ri = lax.broadcasted_iota(jnp.int32, (r, r), 0)
    ci = lax.broadcasted_iota(jnp.int32, (r, r), 1)
    same_blk = lax.shift_right_logical(ri, sub_shift) == lax.shift_right_logical(ci, sub_shift)
    diag_mask = incl & same_blk
    cross_mask = incl & jnp.logical_not(same_blk)

    lb = lb_ref[...]
    f = lb + (1.0 - lb) * _sigmoid(f_ref[...].reshape(r, width))
    lf = jnp.log(f)
    k_all = 1.0 - f
    if masked:
        lf = jnp.where(valid, lf, 0.0)
        k_all = jnp.where(valid, k_all, 0.0)
    bcum = _masked_cumsum(lmask, lf)
    nw = nw_ref[...]

    def head_inputs(h):
        cols = slice(h * dh, (h + 1) * dh)
        q = _silu(q_ref[:, :, cols].reshape(r, dh)) * (dh ** -0.5)
        v = i_ref[:, :, cols].reshape(r, dh)
        if masked:
            v = jnp.where(valid, v, 0.0)
        return q, k_all[:, cols], v, bcum[:, cols]

    def intra_attn(q, k, bh):
        bmid = jnp.broadcast_to(bh.reshape(nblk, sub, dh)[:, sub // 2:sub // 2 + 1, :],
                                (nblk, sub, dh)).reshape(r, dh)
        attn = jnp.where(diag_mask, _dot_nt(q * jnp.exp(bh - bmid), k * jnp.exp(bmid - bh), mode), 0.0)
        if sl > sub:
            parts = [jnp.zeros((sub, r), F32)]
            for blk in range(1, nblk):
                start = blk * sub
                bref = bh[start - 1:start, :]
                qc = q[start:start + sub] * jnp.exp(bh[start:start + sub] - bref)
                kc = k * jnp.exp(jnp.minimum(bref - bh, 0.0))
                parts.append(_dot_nt(qc, kc, mode))
            attn = attn + jnp.where(cross_mask, jnp.concatenate(parts, axis=0), 0.0)
        return attn

    for h0 in range(0, N_HEADS, head_group):
        heads = range(h0, h0 + head_group)
        ins = [head_inputs(h) for h in heads]
        attns = [intra_attn(q, k, bh) for (q, k, v, bh) in ins]
        for i, h in enumerate(heads):
            q, k, v, bh = ins[i]
            qe = q * jnp.exp(bh)
            ointer_parts = []
            for b in range(nb):
                rows = slice(b * sl, (b + 1) * sl)
                ointer_parts.append(_dot(qe[rows], s_scr[b, h], mode))
            ointer = ointer_parts[0] if nb == 1 else jnp.concatenate(ointer_parts, axis=0)
            o = ointer + _dot(attns[i], v, mode)
            cols = slice(h * dh, (h + 1) * dh)
            gate = g_ref[:, :, cols].reshape(r, dh)
            o_ref[:, :, cols] = _gated_rmsnorm(o, nw, gate).reshape(nb, sl, dh).astype(o_ref.dtype)
        for i, h in enumerate(heads):
            q, k, v, bh = ins[i]
            blast = _last_row_bcast(bh, nb, sl)
            ktil = k * jnp.exp(blast - bh)
            pad = [jnp.zeros((LANES - r, dh), F32)] if r < LANES else []
            tr = jnp.concatenate([blast] + pad, axis=0).T
            for b in range(nb):
                kt_b = ktil if nb == 1 else jnp.where(row_seq == b, ktil, 0.0)
                dec_col = jnp.exp(tr[:, b * sl:b * sl + 1])
                s_scr[b, h] = s_scr[b, h] * dec_col + _dot_tn(kt_b, v, mode)

    @pl.when(c == pl.num_programs(1) - 1)
    def _():
        sout_ref[...] = s_scr[...]


def _mixers(proj3, ba3, blk_off, nseq, nb, sl, n_valid, conv_w, pvec, gdn_nw, lb, hgrn_nw,
            s_gdn0, conv0, s_hgrn0, shared_init, out_dtype, mode, inv_mode):
    length = proj3.shape[1]
    has_init = s_gdn0 is not None
    width = N_HEADS * D_HEAD
    conv_ch = 3 * width
    grid = (nseq // nb, length // sl)
    state_spec = pl.BlockSpec((nb, N_HEADS, D_HEAD, D_HEAD), lambda g, c: (g, 0, 0, 0))
    state_shape = jax.ShapeDtypeStruct((nseq, N_HEADS, D_HEAD, D_HEAD), F32)
    init_idx = (lambda g: 0) if shared_init else (lambda g: g)
    init_state_spec = pl.BlockSpec((nb, N_HEADS, D_HEAD, D_HEAD), lambda g, c: (init_idx(g), 0, 0, 0))
    head_group = 8 if nb == 1 else 4

    def col_spec(w, idx):
        return pl.BlockSpec((nb, sl, w), lambda g, c: (g + blk_off, c, idx))

    def out_spec(w):
        return pl.BlockSpec((nb, sl, w), lambda g, c: (g, c, 0))

    def const_spec(shape):
        return pl.BlockSpec(shape, lambda g, c: (0,) * len(shape))

    gdn_in = [proj3, proj3, ba3, conv_w, pvec, gdn_nw]
    gdn_specs = [col_spec(conv_ch, 0), col_spec(width, 3), col_spec(LANES, 0),
                 const_spec(conv_w.shape), const_spec(pvec.shape), const_spec(gdn_nw.shape)]
    if has_init:
        gdn_in += [s_gdn0, conv0]
        gdn_specs += [init_state_spec, pl.BlockSpec((nb, SUBLANES, conv_ch), lambda g, c: (init_idx(g), 0, 0))]
    o_gdn, s_gdn = pl.pallas_call(
        functools.partial(_gdn_kernel, nb=nb, sl=sl, n_valid=n_valid, has_init=has_init, mode=mode,
                          inv_mode=inv_mode, head_group=head_group),
        grid=grid,
        in_specs=gdn_specs,
        out_specs=[out_spec(width), state_spec],
        out_shape=[jax.ShapeDtypeStruct((nseq, length, width), out_dtype), state_shape],
        scratch_shapes=[pltpu.VMEM((nb, N_HEADS, D_HEAD, D_HEAD), F32), pltpu.VMEM((nb, SUBLANES, conv_ch), F32)],
        compiler_params=_params(("parallel", "arbitrary")),
    )(*gdn_in)

    hgrn_in = [proj3, proj3, proj3, proj3, lb, hgrn_nw]
    hgrn_specs = [col_spec(width, 4), col_spec(width, 5), col_spec(width, 6), col_spec(width, 7),
                  const_spec(lb.shape), const_spec(hgrn_nw.shape)]
    if has_init:
        hgrn_in += [s_hgrn0]
        hgrn_specs += [init_state_spec]
    o_hgrn, s_hgrn = pl.pallas_call(
        functools.partial(_hgrn_kernel, nb=nb, sl=sl, n_valid=n_valid, has_init=has_init, mode=mode,
                          head_group=head_group),
        grid=grid,
        in_specs=hgrn_specs,
        out_specs=[out_spec(width), state_spec],
        out_shape=[jax.ShapeDtypeStruct((nseq, length, width), out_dtype), state_shape],
        scratch_shapes=[pltpu.VMEM((nb, N_HEADS, D_HEAD, D_HEAD), F32)],
        compiler_params=_params(("parallel", "arbitrary")),
    )(*hgrn_in)
    return o_gdn, o_hgrn, s_gdn, s_hgrn


def _outproj_router_kernel(oap_ref, obp_ref, hpp_ref, oas_ref, obs_ref, hps_ref, wo_ref, n2_ref, wr_ref, br_ref,
                           hp2_ref, xn2_ref, ri_ref, rw_ref, cnt_ref, carry_scr, *, tm, n_p):
    i = pl.program_id(0)

    @pl.when(i == 0)
    def _():
        carry_scr[...] = jnp.zeros_like(carry_scr)

    body = functools.partial(_outproj_router_tile, wo_ref=wo_ref, n2_ref=n2_ref, wr_ref=wr_ref, br_ref=br_ref,
                             hp2_ref=hp2_ref, xn2_ref=xn2_ref, ri_ref=ri_ref, rw_ref=rw_ref,
                             carry_scr=carry_scr, tm=tm)

    @pl.when(i < n_p)
    def _():
        body(oap_ref[...], obp_ref[...], hpp_ref[...])

    @pl.when(i >= n_p)
    def _():
        body(oas_ref[...], obs_ref[...], hps_ref[...])

    @pl.when(i == pl.num_programs(0) - 1)
    def _():
        cnt_ref[...] = carry_scr[...]


def _outproj_router_tile(oa, ob, hp, *, wo_ref, n2_ref, wr_ref, br_ref, hp2_ref, xn2_ref, ri_ref, rw_ref,
                         carry_scr, tm):
    half = oa.shape[-1]
    mix = _dot(oa.astype(BF16), wo_ref[:half, :]) + _dot(ob.astype(BF16), wo_ref[half:, :])
    hp2 = hp + mix
    hp2_ref[...] = hp2
    xn2 = hp2 * lax.rsqrt(jnp.mean(hp2 * hp2, axis=-1, keepdims=True) + EPS) * n2_ref[...]
    xn2_ref[...] = xn2
    logits = _dot(xn2, wr_ref[...], "bf16x3") + br_ref[...]

    lane = lax.broadcasted_iota(jnp.int32, (tm, LANES), 1)
    lane_f = lane.astype(F32)
    far = float(4 * LANES)
    is_g = (lane >= N_EXPERTS) & (lane < N_EXPERTS + N_GROUPS)
    lg = jnp.where(is_g, logits, -jnp.inf)
    gmax = jnp.max(lg, axis=-1, keepdims=True)
    gsel = jnp.min(jnp.where(lg == gmax, lane_f, far), axis=-1, keepdims=True).astype(jnp.int32) - N_EXPERTS
    p_top = 1.0 / jnp.sum(jnp.where(is_g, jnp.exp(logits - gmax), 0.0), axis=-1, keepdims=True)
    in_grp = (lane < N_EXPERTS) & (lax.shift_right_logical(lane, 3) == gsel)
    le = jnp.where(in_grp, logits, -jnp.inf)
    m1 = jnp.max(le, axis=-1, keepdims=True)
    i1 = jnp.min(jnp.where(le == m1, lane_f, far), axis=-1, keepdims=True).astype(jnp.int32)
    le2 = jnp.where(lane == i1, -jnp.inf, le)
    m2 = jnp.max(le2, axis=-1, keepdims=True)
    i2 = jnp.min(jnp.where(le2 == m2, lane_f, far), axis=-1, keepdims=True).astype(jnp.int32)
    e2 = jnp.exp(m2 - m1)
    w1 = p_top / (1.0 + e2)
    w2 = p_top * e2 / (1.0 + e2)

    onehot = (lane == i1) | (lane == i2)
    onehot_f = jnp.where(onehot, 1.0, 0.0)
    tri = (lax.broadcasted_iota(jnp.int32, (tm, tm), 1) < lax.broadcasted_iota(jnp.int32, (tm, tm), 0))
    before = _dot(jnp.where(tri, 1.0, 0.0).astype(BF16), onehot_f.astype(BF16)) + carry_scr[...]
    r1 = jnp.sum(jnp.where(lane == i1, before, 0.0), axis=-1, keepdims=True).astype(jnp.int32)
    r2 = jnp.sum(jnp.where(lane == i2, before, 0.0), axis=-1, keepdims=True).astype(jnp.int32)
    carry_scr[...] = carry_scr[...] + jnp.sum(onehot_f, axis=0, keepdims=True)

    ri_ref[...] = jnp.where(lane == 0, i1, jnp.where(lane == 1, i2, jnp.where(lane == 2, r1, jnp.where(lane == 3, r2, 0))))
    rw_ref[...] = jnp.where(lane == 0, w1, jnp.where(lane == 1, w2, 0.0))


def _outproj_router(oa_p, ob_p, hp_p, oa_s, ob_s, hp_s, w_out, norm2_w, w_r, b_r, tm):
    (tp, d), ts = hp_p.shape, hp_s.shape[0]
    half = oa_p.shape[1]
    n_p, n_s = tp // tm, ts // tm
    t = tp + ts
    prow = lambda w: pl.BlockSpec((tm, w), lambda i: (jnp.minimum(i, n_p - 1), 0))
    srow = lambda w: pl.BlockSpec((tm, w), lambda i: (jnp.maximum(i - n_p, 0), 0))
    row = lambda w: pl.BlockSpec((tm, w), lambda i: (i, 0))
    const = lambda shape: pl.BlockSpec(shape, lambda i: (0,) * len(shape))
    return pl.pallas_call(
        functools.partial(_outproj_router_kernel, tm=tm, n_p=n_p),
        grid=(n_p + n_s,),
        in_specs=[prow(half), prow(half), prow(d), srow(half), srow(half), srow(d),
                  const(w_out.shape), const((1, d)), const(w_r.shape), const((1, LANES))],
        out_specs=[row(d), row(d), row(LANES), row(LANES), const((1, LANES))],
        out_shape=[jax.ShapeDtypeStruct((t, d), F32), jax.ShapeDtypeStruct((t, d), F32),
                   jax.ShapeDtypeStruct((t, LANES), jnp.int32), jax.ShapeDtypeStruct((t, LANES), F32),
                   jax.ShapeDtypeStruct((1, LANES), F32)],
        scratch_shapes=[pltpu.VMEM((1, LANES), F32)],
        compiler_params=_params(("arbitrary",)),
    )(oa_p, ob_p, hp_p, oa_s, ob_s, hp_s, w_out, norm2_w.reshape(1, d), w_r, b_r)


def _start_row_gather(idx_ref, base, n, src_hbm, dst, sem):
    def issue(r, carry):
        pltpu.make_async_copy(src_hbm.at[pl.ds(idx_ref[base + r], 1), :], dst.at[pl.ds(r, 1), :], sem).start()
        return carry
    lax.fori_loop(0, n, issue, 0, unroll=8)


def _wait_row_gather(n, src_hbm, dst, sem):
    pltpu.make_async_copy(src_hbm.at[pl.ds(0, n), :], dst, sem).wait()


def _moe_kernel(te_ref, st_ref, nx_ref, nu_ref, x_hbm, wg_hbm, wu_hbm, wd_hbm, o_ref,
                xbuf, wg_f32, wu_f32, wd_f32, wg_scr, wu_scr, wd_scr, sems, wsems, *, tile):
    i = pl.program_id(0)
    n_used = nu_ref[0]
    slot = lax.rem(i, 2)
    weights = ((wg_hbm, wg_f32, wg_scr), (wu_hbm, wu_f32, wu_scr), (wd_hbm, wd_f32, wd_scr))

    def start_weights(e):
        for k, (w_hbm, w_f32, _) in enumerate(weights):
            pltpu.async_copy(w_hbm.at[e], w_f32, wsems.at[k], priority=1)

    @pl.when((i == 0) & (n_used > 0))
    def _():
        start_weights(te_ref[0])
        _start_row_gather(st_ref, 0, tile, x_hbm, xbuf.at[0], sems.at[0])

    @pl.when(i < n_used)
    def _():
        @pl.when(i + 1 < n_used)
        def _():
            _start_row_gather(st_ref, (i + 1) * tile, tile, x_hbm, xbuf.at[1 - slot], sems.at[1 - slot])

        e = te_ref[i]

        @pl.when((i == 0) | (e != te_ref[jnp.maximum(i - 1, 0)]))
        def _():
            for k, (w_hbm, w_f32, w_scr) in enumerate(weights):
                pltpu.make_async_copy(w_hbm.at[0], w_f32, wsems.at[k]).wait()
                w_scr[...] = w_f32[...].astype(BF16)
            nxt = nx_ref[e]

            @pl.when(nxt < N_EXPERTS)
            def _():
                start_weights(nxt)

        _wait_row_gather(tile, x_hbm, xbuf.at[slot], sems.at[slot])
        x = xbuf[slot].astype(BF16)
        g = _dot(x, wg_scr[...])
        u = _dot(x, wu_scr[...])
        o_ref[...] = _dot((_silu(g) * u).astype(BF16), wd_scr[...])

    @pl.when(i >= n_used)
    def _():
        o_ref[...] = jnp.zeros_like(o_ref)


def _moe(xn2, w_gate, w_up, w_down, tile_expert, slot_token, next_expert, n_used, tile):
    d = xn2.shape[1]
    n_tiles = tile_expert.shape[0]
    de = w_gate.shape[2]
    hbm = pl.BlockSpec(memory_space=pl.ANY)
    grid_spec = pltpu.PrefetchScalarGridSpec(
        num_scalar_prefetch=4,
        grid=(n_tiles,),
        in_specs=[hbm, hbm, hbm, hbm],
        out_specs=pl.BlockSpec((tile, d), lambda i, te, st, nx, nu: (i, 0)),
        scratch_shapes=[pltpu.VMEM((2, tile, d), F32),
                        pltpu.VMEM((d, de), F32), pltpu.VMEM((d, de), F32), pltpu.VMEM((de, d), F32),
                        pltpu.VMEM((d, de), BF16), pltpu.VMEM((d, de), BF16), pltpu.VMEM((de, d), BF16),
                        pltpu.SemaphoreType.DMA((2,)), pltpu.SemaphoreType.DMA((3,))],
    )
    return pl.pallas_call(
        functools.partial(_moe_kernel, tile=tile),
        grid_spec=grid_spec,
        out_shape=jax.ShapeDtypeStruct((n_tiles * tile, d), F32),
        compiler_params=_params(("arbitrary",)),
    )(tile_expert, slot_token, next_expert, n_used, xn2, w_gate, w_up, w_down)


def _combine_kernel(p0_ref, p1_ref, ys_hbm, hp2_ref, rw_ref, fw_ref, o_ref, buf0, buf1, sems, *, tm, tile_off):
    i = pl.program_id(0)
    slot = lax.rem(i, 2)

    def start(step, s):
        base = (step + tile_off) * tm
        _start_row_gather(p0_ref, base, tm, ys_hbm, buf0.at[s], sems.at[0, s])
        _start_row_gather(p1_ref, base, tm, ys_hbm, buf1.at[s], sems.at[1, s])

    @pl.when(i == 0)
    def _():
        start(0, 0)

    @pl.when(i + 1 < pl.num_programs(0))
    def _():
        start(i + 1, 1 - slot)

    _wait_row_gather(tm, ys_hbm, buf0.at[slot], sems.at[0, slot])
    _wait_row_gather(tm, ys_hbm, buf1.at[slot], sems.at[1, slot])
    rw = rw_ref[...]
    y = hp2_ref[...] + rw[:, 0:1] * buf0[slot] + rw[:, 1:2] * buf1[slot]
    o_ref[...] = y * lax.rsqrt(jnp.mean(y * y, axis=-1, keepdims=True) + EPS) * fw_ref[...]


def _combine(ys, hp2, route_w, final_w, pos0, pos1, tm, tile_off, n_tiles):
    _, d = hp2.shape
    grid_spec = pltpu.PrefetchScalarGridSpec(
        num_scalar_prefetch=2,
        grid=(n_tiles,),
        in_specs=[pl.BlockSpec(memory_space=pl.ANY),
                  pl.BlockSpec((tm, d), lambda i, p0, p1: (i + tile_off, 0)),
                  pl.BlockSpec((tm, LANES), lambda i, p0, p1: (i + tile_off, 0)),
                  pl.BlockSpec((1, d), lambda i, p0, p1: (0, 0))],
        out_specs=pl.BlockSpec((tm, d), lambda i, p0, p1: (i, 0)),
        scratch_shapes=[pltpu.VMEM((2, tm, d), F32), pltpu.VMEM((2, tm, d), F32),
                        pltpu.SemaphoreType.DMA((2, 2))],
    )
    return pl.pallas_call(
        functools.partial(_combine_kernel, tm=tm, tile_off=tile_off),
        grid_spec=grid_spec,
        out_shape=jax.ShapeDtypeStruct((n_tiles * tm, d), F32),
        compiler_params=_params(("arbitrary",)),
    )(pos0, pos1, ys, hp2, route_w, final_w.reshape(1, d))


def kernel(x_prompt, x_sample, state_gdn, state_conv, state_hgrn, meta_tokens, norm1_w, w_in, conv_w, a_log,
           dt_bias, gdn_norm_w, lb_logits, hgrn_norm_w, w_out, norm2_w, w_router_group, b_router_group,
           w_router_expert, b_router_expert, w_gate, w_up, w_down, final_norm_w):
    bp, seq, d = x_prompt.shape
    bs, dec_seq, _ = x_sample.shape
    assert w_in.shape[0] == 1, "single-layer trunk"
    width = N_HEADS * D_HEAD
    conv_ch = 3 * width
    tile = 256
    tn = 512
    sl_s = SUBLANES
    nb_s = CHUNK // sl_s
    assert seq % CHUNK == 0 and N_META <= CHUNK and dec_seq <= sl_s and bs % nb_s == 0
    tp, ts, ts_pad = bp * seq, bs * dec_seq, bs * sl_s
    t_small = ts_pad + CHUNK
    tm = _row_tile(math.gcd(tp, ts), 256)
    tm_p = _row_tile(tp, 1024)

    xp = x_prompt.reshape(tp, d)
    x_small = jnp.concatenate([jnp.pad(x_sample, ((0, 0), (0, sl_s - dec_seq), (0, 0))).reshape(ts_pad, d),
                               jnp.zeros((CHUNK - N_META, d), F32), meta_tokens.astype(F32)], axis=0)

    wi = w_in[0]
    n_a = 4 * width // tn
    w_b = wi[:, 4 * width + 2 * N_HEADS:]
    w_ba = jnp.pad(wi[:, 4 * width:4 * width + 2 * N_HEADS], ((0, 0), (0, LANES - 2 * N_HEADS))).astype(BF16)
    n_cols = 8 * width

    xn_p = _rmsnorm(xp, norm1_w[0], BF16, _row_tile(tp, 512))
    xn_s = _rmsnorm(x_small, norm1_w[0], BF16, _row_tile(t_small, 1024))
    w_a = wi[:, :4 * width]
    proj_p = _inproj(xn_p, w_a, n_a, w_b, tm_p, tn)
    proj_s = _inproj(xn_s, w_a, n_a, w_b, t_small, tn)
    ba_p = _matmul(xn_p, w_ba, tm_p, LANES)
    ba_s = _matmul(xn_s, w_ba, t_small, LANES)

    pvec = jnp.zeros((2, LANES), F32)
    pvec = pvec.at[0, N_HEADS:2 * N_HEADS].set(a_log[0]).at[1, N_HEADS:2 * N_HEADS].set(dt_bias[0])
    lb = jnp.cumsum(jax.nn.softmax(lb_logits.astype(F32), axis=0), axis=0)[0].reshape(1, width)
    gdn_nw = gdn_norm_w[0].reshape(1, D_HEAD)
    hgrn_nw = hgrn_norm_w[0].reshape(1, D_HEAD)
    cw = conv_w[0]
    mix_args = (cw, pvec, gdn_nw, lb, hgrn_nw)

    _, _, sg_m, sh_m = _mixers(proj_s.reshape(t_small // CHUNK, CHUNK, n_cols),
                               ba_s.reshape(t_small // CHUNK, CHUNK, LANES), ts_pad // CHUNK, 1, 1, CHUNK, CHUNK,
                               *mix_args, None, None, None, False, F32, MIX_MODE, INV_MODE)
    conv_m = proj_s[t_small - SUBLANES:, :conv_ch].reshape(1, SUBLANES, conv_ch)
    oa_p, ob_p, sg_p, sh_p = _mixers(proj_p.reshape(bp, seq, n_cols), ba_p.reshape(bp, seq, LANES), 0, bp, 1,
                                     CHUNK, CHUNK, *mix_args, sg_m, conv_m, sh_m, True, BF16, MIX_MODE, INV_MODE)
    conv0 = jnp.pad(state_conv[0], ((0, 0), (SUBLANES - (CONV_W - 1), 0), (0, 0)))
    oa_s, ob_s, sg_s, sh_s = _mixers(proj_s.reshape(t_small // sl_s, sl_s, n_cols),
                                     ba_s.reshape(t_small // sl_s, sl_s, LANES), 0, bs, nb_s, sl_s, dec_seq,
                                     *mix_args, state_gdn[0], conv0, state_hgrn[0], False, F32, MIX_MODE, INV_MODE)

    w_r = jnp.concatenate([w_router_expert[0], w_router_group[0],
                           jnp.zeros((d, LANES - N_EXPERTS - N_GROUPS), F32)], axis=1)
    b_r = jnp.concatenate([b_router_expert[0], b_router_group[0],
                           jnp.zeros((LANES - N_EXPERTS - N_GROUPS,), F32)]).reshape(1, LANES)
    t = tp + ts
    hp2, xn2, route_i, route_w, counts = _outproj_router(
        oa_p.reshape(tp, width), ob_p.reshape(tp, width), xp,
        oa_s[:, :dec_seq].reshape(ts, width), ob_s[:, :dec_seq].reshape(ts, width), x_sample.reshape(ts, d),
        w_out[0].astype(BF16), norm2_w[0], w_r, b_r, tm)

    cnt = counts[0, :N_EXPERTS].astype(jnp.int32)
    padded = (cnt + tile - 1) // tile * tile
    ends = jnp.cumsum(padded)
    offs = ends - padded
    eid = route_i[:, 0:2]
    expert_ids = jnp.arange(N_EXPERTS, dtype=jnp.int32)
    pos = jnp.sum(jnp.where(eid[:, :, None] == expert_ids, offs, 0), axis=-1) + route_i[:, 2:4]
    n_tiles = (2 * t) // tile + N_EXPERTS
    tile_start = jnp.arange(n_tiles, dtype=jnp.int32) * tile
    tile_expert = jnp.minimum(jnp.sum((ends[None, :] <= tile_start[:, None]).astype(jnp.int32), axis=1),
                              N_EXPERTS - 1)
    n_used = (ends[-1] // tile).astype(jnp.int32).reshape(1)
    tok = jnp.broadcast_to(jnp.arange(t, dtype=jnp.int32)[:, None], (t, 2))
    slot_token = jnp.zeros((n_tiles * tile,), jnp.int32).at[pos.reshape(-1)].set(
        tok.reshape(-1), unique_indices=True, mode="promise_in_bounds")

    later_active = (expert_ids[None, :] > expert_ids[:, None]) & (cnt[None, :] > 0)
    next_expert = jnp.min(jnp.where(later_active, expert_ids[None, :], N_EXPERTS), axis=1).astype(jnp.int32)
    ys = _moe(xn2, w_gate[0], w_up[0], w_down[0], tile_expert, slot_token, next_expert, n_used, tile)
    pos0, pos1 = pos[:, 0], pos[:, 1]
    y_prompt = _combine(ys, hp2, route_w, final_norm_w, pos0, pos1, tm, 0, tp // tm).reshape(bp, seq, d)
    y_sample = _combine(ys, hp2, route_w, final_norm_w, pos0, pos1, tm, tp // tm, ts // tm).reshape(bs, dec_seq, d)

    conv_p = proj_p.reshape(bp, seq, n_cols)[:, seq - (CONV_W - 1):, :conv_ch]
    u_s = proj_s.reshape(t_small // sl_s, sl_s, n_cols)[:bs, :dec_seq, :conv_ch]
    conv_s = jnp.concatenate([state_conv[0], u_s], axis=1)[:, dec_seq:]
    return (y_prompt, y_sample, sg_p[None], conv_p[None], sh_p[None], sg_s[None], conv_s[None], sh_s[None])
```

```python
import functools
import math

import jax
import jax.numpy as jnp
from jax import lax
from jax.experimental import pallas as pl
from jax.experimental.pallas import tpu as pltpu

F32 = jnp.float32
BF16 = jnp.bfloat16
HIGHEST = lax.Precision.HIGHEST

EPS = 1e-6
N_META = 16
CONV_W = 4
N_HEADS = 8
D_HEAD = 128
N_GROUPS = 4
EXPERTS_PER_GROUP = 8
N_EXPERTS = N_GROUPS * EXPERTS_PER_GROUP

LANES = 128
SUBLANES = 8
CHUNK = 64
SUB = 16
VMEM_LIMIT = 56 * 1024 * 1024
NEG_BIG = -1e30
MIX_MODE = "bf16"
INV_MODE = "bf16"


def _sigmoid(x):
    return 0.5 * jnp.tanh(0.5 * x) + 0.5


def _silu(x):
    return x * _sigmoid(x)


def _softplus(x):
    return jnp.maximum(x, 0.0) + jnp.log1p(jnp.exp(-jnp.abs(x)))


def _split_bf16(a, pieces):
    out = []
    for _ in range(pieces - 1):
        hi = a.astype(BF16)
        out.append(hi)
        a = a - hi.astype(F32)
    out.append(a.astype(BF16))
    return out


def _mm(a, b, dims, mode):
    dg = functools.partial(lax.dot_general, dimension_numbers=(dims, ((), ())), preferred_element_type=F32)
    if mode == "f32":
        return dg(a, b, precision=HIGHEST)
    if mode == "bf16":
        return dg(a.astype(BF16), b.astype(BF16))
    assert mode == "bf16x3"
    ah, al = _split_bf16(a, 2)
    bh, bl = _split_bf16(b, 2)
    return dg(ah, bh) + dg(ah, bl) + dg(al, bh)


def _dot(a, b, mode="bf16"):
    return _mm(a, b, ((1,), (0,)), mode)


def _dot_nt(a, b, mode="bf16"):
    return _mm(a, b, ((1,), (1,)), mode)


def _dot_tn(a, b, mode="bf16"):
    return _mm(a, b, ((0,), (0,)), mode)


def _masked_cumsum(lmask, x):
    lm = lmask.astype(BF16)
    return sum(lax.dot_general(lm, p, (((1,), (0,)), ((), ())), preferred_element_type=F32)
               for p in _split_bf16(x, 3))


def _params(sem):
    return pltpu.CompilerParams(dimension_semantics=sem, vmem_limit_bytes=VMEM_LIMIT)


def _row_tile(n, target):
    best = max(c for c in range(16, min(n, target) + 1, 16) if n % c == 0)
    return best


def _rmsnorm_kernel(x_ref, w_ref, o_ref):
    x = x_ref[...]
    ms = jnp.mean(x * x, axis=-1, keepdims=True)
    o_ref[...] = (x * lax.rsqrt(ms + EPS) * w_ref[...]).astype(o_ref.dtype)


def _rmsnorm(x, w, out_dtype, tm):
    t, d = x.shape
    return pl.pallas_call(
        _rmsnorm_kernel,
        grid=(t // tm,),
        in_specs=[pl.BlockSpec((tm, d), lambda i: (i, 0)), pl.BlockSpec((1, d), lambda i: (0, 0))],
        out_specs=pl.BlockSpec((tm, d), lambda i: (i, 0)),
        out_shape=jax.ShapeDtypeStruct((t, d), out_dtype),
        compiler_params=_params(("parallel",)),
    )(x, w.reshape(1, d))


def _inproj_kernel(x_ref, wa_ref, wb_ref, o_ref, w_scr, *, n_a):
    j = pl.program_id(0)
    i = pl.program_id(1)

    @pl.when((i == 0) & (j < n_a))
    def _():
        w_scr[...] = wa_ref[...].astype(BF16)

    @pl.when((i == 0) & (j >= n_a))
    def _():
        w_scr[...] = wb_ref[...].astype(BF16)

    o_ref[...] = _dot(x_ref[...], w_scr[...])


def _inproj(x, w_a, n_a, w_b, tm, tn):
    t, k = x.shape
    n_b = w_b.shape[1] // tn
    return pl.pallas_call(
        functools.partial(_inproj_kernel, n_a=n_a),
        grid=(n_a + n_b, t // tm),
        in_specs=[pl.BlockSpec((tm, k), lambda j, i: (i, 0)),
                  pl.BlockSpec((k, tn), lambda j, i: (0, jnp.minimum(j, n_a - 1))),
                  pl.BlockSpec((k, tn), lambda j, i: (0, jnp.maximum(j - n_a, 0)))],
        out_specs=pl.BlockSpec((tm, tn), lambda j, i: (i, j)),
        out_shape=jax.ShapeDtypeStruct((t, (n_a + n_b) * tn), F32),
        scratch_shapes=[pltpu.VMEM((k, tn), BF16)],
        compiler_params=_params(("arbitrary", "arbitrary")),
    )(x, w_a, w_b)


def _matmul_kernel(x_ref, w_ref, o_ref):
    o_ref[...] = _dot(x_ref[...], w_ref[...])


def _matmul(x, w, tm, tn):
    t, k = x.shape
    n = w.shape[1]
    return pl.pallas_call(
        _matmul_kernel,
        grid=(n // tn, t // tm),
        in_specs=[pl.BlockSpec((tm, k), lambda j, i: (i, 0)), pl.BlockSpec((k, tn), lambda j, i: (0, j))],
        out_specs=pl.BlockSpec((tm, tn), lambda j, i: (i, j)),
        out_shape=jax.ShapeDtypeStruct((t, n), F32),
        compiler_params=_params(("parallel", "arbitrary")),
    )(x, w)


def _chunk_masks(nb, sl):
    r = nb * sl
    shift = int(math.log2(sl))
    ri = lax.broadcasted_iota(jnp.int32, (r, r), 0)
    ci = lax.broadcasted_iota(jnp.int32, (r, r), 1)
    same = lax.shift_right_logical(ri, shift) == lax.shift_right_logical(ci, shift)
    return same & (ci <= ri), same & (ci < ri)


def _row_valid(nb, sl, n_valid):
    rowid = lax.broadcasted_iota(jnp.int32, (nb * sl, 1), 0)
    return (rowid & (sl - 1)) < n_valid


def _last_row_bcast(x, nb, sl):
    c = x.shape[-1]
    x3 = x.reshape(nb, sl, c)
    return jnp.broadcast_to(x3[:, sl - 1:sl, :], (nb, sl, c)).reshape(nb * sl, c)


def _gated_rmsnorm(o, w, gate):
    return o * lax.rsqrt(jnp.mean(o * o, axis=-1, keepdims=True) + EPS) * w * _silu(gate)


def _gdn_kernel(*refs, nb, sl, n_valid, has_init, mode, inv_mode, head_group):
    if has_init:
        (qkv_ref, z_ref, ba_ref, cw_ref, pv_ref, nw_ref, s0_ref, c0_ref,
         o_ref, sout_ref, s_scr, carry_scr) = refs
    else:
        (qkv_ref, z_ref, ba_ref, cw_ref, pv_ref, nw_ref,
         o_ref, sout_ref, s_scr, carry_scr) = refs
    c = pl.program_id(1)
    r = nb * sl
    dh = D_HEAD

    @pl.when(c == 0)
    def _():
        if has_init:
            s_scr[...] = s0_ref[...]
            carry_scr[...] = c0_ref[...]
        else:
            s_scr[...] = jnp.zeros_like(s_scr)
            carry_scr[...] = jnp.zeros_like(carry_scr)

    incl, strict = _chunk_masks(nb, sl)
    lmask = jnp.where(incl, 1.0, 0.0)
    offdiag = jnp.where(strict, 1.0, 0.0)
    valid = _row_valid(nb, sl, n_valid)
    masked = n_valid < sl
    rowid = lax.broadcasted_iota(jnp.int32, (r, 1), 0)
    row_seq = lax.shift_right_logical(rowid, int(math.log2(sl)))
    n_sq = int(math.log2(sl)) - 1

    ba = ba_ref[...].reshape(r, LANES)
    pv = pv_ref[...]
    beta_all = _sigmoid(ba)
    g_all = -jnp.exp(pv[0:1]) * _softplus(ba + pv[1:2])
    if masked:
        g_all = jnp.where(valid, g_all, 0.0)
    gcum = _masked_cumsum(lmask, g_all)
    gcum_t = gcum.T
    glast = _last_row_bcast(gcum, nb, sl)
    cw = cw_ref[...]
    nw = nw_ref[...]

    def conv_slice(c0):
        u = qkv_ref[:, :, c0:c0 + dh]
        prev = carry_scr[:, :, c0:c0 + dh]
        full = jnp.concatenate([prev, u], axis=1)
        acc = None
        for j in range(CONV_W):
            off = SUBLANES - (CONV_W - 1) + j
            term = full[:, off:off + sl, :] * cw[j:j + 1, c0:c0 + dh]
            acc = term if acc is None else acc + term
        return _silu(acc).reshape(r, dh)

    for h0 in range(0, N_HEADS, head_group):
        heads = range(h0, h0 + head_group)
        qs, ks, vs, bcs, gcs, gls, egs, decays = [], [], [], [], [], [], [], []
        for h in heads:
            q = conv_slice(h * dh)
            k = conv_slice(N_HEADS * dh + h * dh)
            v = conv_slice(2 * N_HEADS * dh + h * dh)
            q = q * lax.rsqrt(jnp.sum(q * q, axis=-1, keepdims=True) + EPS) * (dh ** -0.5)
            k = k * lax.rsqrt(jnp.sum(k * k, axis=-1, keepdims=True) + EPS)
            if masked:
                q = jnp.where(valid, q, 0.0)
                k = jnp.where(valid, k, 0.0)
                v = jnp.where(valid, v, 0.0)
            gc = gcum[:, N_HEADS + h:N_HEADS + h + 1]
            gr = gcum_t[N_HEADS + h:N_HEADS + h + 1, :]
            qs.append(q)
            ks.append(k)
            vs.append(v)
            bcs.append(beta_all[:, h:h + 1])
            gcs.append(gc)
            gls.append(glast[:, N_HEADS + h:N_HEADS + h + 1])
            egs.append(jnp.exp(gc))
            decays.append(jnp.exp(jnp.where(incl, gc - gr, NEG_BIG)))
        n = len(qs)
        qk_kk = [_dot_nt(jnp.concatenate([qs[i], ks[i]], axis=0), ks[i], mode) for i in range(n)]
        tm1 = [qk_kk[i][r:] * (decays[i] * offdiag) * (-bcs[i]) for i in range(n)]
        pw = list(tm1)
        for _ in range(n_sq):
            pw = [_dot(pw[i], pw[i], inv_mode) for i in range(n)]
            tm1 = [tm1[i] + pw[i] + _dot(tm1[i], pw[i], inv_mode) for i in range(n)]
        rhs = [jnp.concatenate([vs[i] * bcs[i], ks[i] * (bcs[i] * egs[i])], axis=1) for i in range(n)]
        uw = [rhs[i] + _dot(tm1[i], rhs[i], mode) for i in range(n)]
        vnew, ointer = [], []
        for i, h in enumerate(heads):
            u = uw[i][:, :dh]
            w = uw[i][:, dh:]
            qe = qs[i] * egs[i]
            vnew_parts, ointer_parts = [], []
            for b in range(nb):
                rows = slice(b * sl, (b + 1) * sl)
                ws = _dot(jnp.concatenate([w[rows], qe[rows]], axis=0), s_scr[b, h], mode)
                vnew_parts.append(u[rows] - ws[:sl])
                ointer_parts.append(ws[sl:])
            vnew.append(vnew_parts[0] if nb == 1 else jnp.concatenate(vnew_parts, axis=0))
            ointer.append(ointer_parts[0] if nb == 1 else jnp.concatenate(ointer_parts, axis=0))
        for i, h in enumerate(heads):
            attn = qk_kk[i][:r] * decays[i]
            o = ointer[i] + _dot(attn, vnew[i], mode)
            z = z_ref[:, :, h * dh:(h + 1) * dh].reshape(r, dh)
            o_ref[:, :, h * dh:(h + 1) * dh] = _gated_rmsnorm(o, nw, z).reshape(nb, sl, dh).astype(o_ref.dtype)
        for i, h in enumerate(heads):
            ktil = ks[i] * jnp.exp(gls[i] - gcs[i])
            for b in range(nb):
                kt_b = ktil if nb == 1 else jnp.where(row_seq == b, ktil, 0.0)
                gl_b = gls[i][b * sl:b * sl + 1, :]
                s_scr[b, h] = s_scr[b, h] * jnp.exp(gl_b) + _dot_tn(kt_b, vnew[i], mode)

    carry_scr[...] = qkv_ref[:, sl - SUBLANES:sl, :]

    @pl.when(c == pl.num_programs(1) - 1)
    def _():
        sout_ref[...] = s_scr[...]


def _hgrn_kernel(*refs, nb, sl, n_valid, has_init, mode, head_group):
    if has_init:
        (q_ref, f_ref, i_ref, g_ref, lb_ref, nw_ref, s0_ref, o_ref, sout_ref, s_scr) = refs
    else:
        (q_ref, f_ref, i_ref, g_ref, lb_ref, nw_ref, o_ref, sout_ref, s_scr) = refs
    c = pl.program_id(1)
    r = nb * sl
    dh = D_HEAD
    width = N_HEADS * dh

    @pl.when(c == 0)
    def _():
        if has_init:
            s_scr[...] = s0_ref[...]
        else:
            s_scr[...] = jnp.zeros_like(s_scr)

    incl, _ = _chunk_masks(nb, sl)
    lmask = jnp.where(incl, 1.0, 0.0)
    valid = _row_valid(nb, sl, n_valid)
    masked = n_valid < sl
    rowid = lax.broadcasted_iota(jnp.int32, (r, 1), 0)
    row_seq = lax.shift_right_logical(rowid, int(math.log2(sl)))
    sub = min(SUB, sl)
    nblk = r // sub
    sub_shift = int(math.log2(sub))
    ri = lax.broadcasted_iota(jnp.int32, (r, r), 0)
    ci = lax.broadcasted_iota(jnp.int32, (r, r), 1)
    same_blk = lax.shift_right_logical(ri, sub_shift) == lax.shift_right_logical(ci, sub_shift)
    diag_mask = incl & same_blk
    cross_mask = incl & jnp.logical_not(same_blk)

    lb = lb_ref[...]
    f = lb + (1.0 - lb) * _sigmoid(f_ref[...].reshape(r, width))
    lf = jnp.log(f)
    k_all = 1.0 - f
    if masked:
        lf = jnp.where(valid, lf, 0.0)
        k_all = jnp.where(valid, k_all, 0.0)
    bcum = _masked_cumsum(lmask, lf)
    nw = nw_ref[...]

    def head_inputs(h):
        cols = slice(h * dh, (h + 1) * dh)
        q = _silu(q_ref[:, :, cols].reshape(r, dh)) * (dh ** -0.5)
        v = i_ref[:, :, cols].reshape(r, dh)
        if masked:
            v = jnp.where(valid, v, 0.0)
        return q, k_all[:, cols], v, bcum[:, cols]

    def intra_attn(q, k, bh):
        bmid = jnp.broadcast_to(bh.reshape(nblk, sub, dh)[:, sub // 2:sub // 2 + 1, :],
                                (nblk, sub, dh)).reshape(r, dh)
        attn = jnp.where(diag_mask, _dot_nt(q * jnp.exp(bh - bmid), k * jnp.exp(bmid - bh), mode), 0.0)
        if sl > sub:
            parts = [jnp.zeros((sub, r), F32)]
            for blk in range(1, nblk):
                start = blk * sub
                bref = bh[start - 1:start, :]
                qc = q[start:start + sub] * jnp.exp(bh[start:start + sub] - bref)
                kc = k * jnp.exp(jnp.minimum(bref - bh, 0.0))
                parts.append(_dot_nt(qc, kc, mode))
            attn = attn + jnp.where(cross_mask, jnp.concatenate(parts, axis=0), 0.0)
        return attn

    for h0 in range(0, N_HEADS, head_group):
        heads = range(h0, h0 + head_group)
        ins = [head_inputs(h) for h in heads]
        attns = [intra_attn(q, k, bh) for (q, k, v, bh) in ins]
        for i, h in enumerate(heads):
            q, k, v, bh = ins[i]
            qe = q * jnp.exp(bh)
            ointer_parts = []
            for b in range(nb):
                rows = slice(b * sl, (b + 1) * sl)
                ointer_parts.append(_dot(qe[rows], s_scr[b, h], mode))
            ointer = ointer_parts[0] if nb == 1 else jnp.concatenate(ointer_parts, axis=0)
            o = ointer + _dot(attns[i], v, mode)
            cols = slice(h * dh, (h + 1) * dh)
            gate = g_ref[:, :, cols].reshape(r, dh)
            o_ref[:, :, cols] = _gated_rmsnorm(o, nw, gate).reshape(nb, sl, dh).astype(o_ref.dtype)
        for i, h in enumerate(heads):
            q, k, v, bh = ins[i]
            blast = _last_row_bcast(bh, nb, sl)
            ktil = k * jnp.exp(blast - bh)
            pad = [jnp.zeros((LANES - r, dh), F32)] if r < LANES else []
            tr = jnp.concatenate([blast] + pad, axis=0).T
            for b in range(nb):
                kt_b = ktil if nb == 1 else jnp.where(row_seq == b, ktil, 0.0)
                dec_col = jnp.exp(tr[:, b * sl:b * sl + 1])
                s_scr[b, h] = s_scr[b, h] * dec_col + _dot_tn(kt_b, v, mode)

    @pl.when(c == pl.num_programs(1) - 1)
    def _():
        sout_ref[...] = s_scr[...]


def _mixers(proj3, ba3, blk_off, nseq, nb, sl, n_valid, conv_w, pvec, gdn_nw, lb, hgrn_nw,
            s_gdn0, conv0, s_hgrn0, shared_init, out_dtype, mode, inv_mode):
    length = proj3.shape[1]
    has_init = s_gdn0 is not None
    width = N_HEADS * D_HEAD
    conv_ch = 3 * width
    grid = (nseq // nb, length // sl)
    state_spec = pl.BlockSpec((nb, N_HEADS, D_HEAD, D_HEAD), lambda g, c: (g, 0, 0, 0))
    state_shape = jax.ShapeDtypeStruct((nseq, N_HEADS, D_HEAD, D_HEAD), F32)
    init_idx = (lambda g: 0) if shared_init else (lambda g: g)
    init_state_spec = pl.BlockSpec((nb, N_HEADS, D_HEAD, D_HEAD), lambda g, c: (init_idx(g), 0, 0, 0))
    head_group = 8 if nb == 1 else 4

    def col_spec(w, idx):
        return pl.BlockSpec((nb, sl, w), lambda g, c: (g + blk_off, c, idx))

    def out_spec(w):
        return pl.BlockSpec((nb, sl, w), lambda g, c: (g, c, 0))

    def const_spec(shape):
        return pl.BlockSpec(shape, lambda g, c: (0,) * len(shape))

    gdn_in = [proj3, proj3, ba3, conv_w, pvec, gdn_nw]
    gdn_specs = [col_spec(conv_ch, 0), col_spec(width, 3), col_spec(LANES, 0),
                 const_spec(conv_w.shape), const_spec(pvec.shape), const_spec(gdn_nw.shape)]
    if has_init:
        gdn_in += [s_gdn0, conv0]
        gdn_specs += [init_state_spec, pl.BlockSpec((nb, SUBLANES, conv_ch), lambda g, c: (init_idx(g), 0, 0))]
    hgrn_in = [proj3, proj3, proj3, proj3, lb, hgrn_nw]
    hgrn_specs = [col_spec(width, 4), col_spec(width, 5), col_spec(width, 6), col_spec(width, 7),
                  const_spec(lb.shape), const_spec(hgrn_nw.shape)]
    if has_init:
        hgrn_in += [s_hgrn0]
        hgrn_specs += [init_state_spec]

    gdn_body = functools.partial(_gdn_kernel, nb=nb, sl=sl, n_valid=n_valid, has_init=has_init, mode=mode,
                                 inv_mode=inv_mode, head_group=head_group)
    hgrn_body = functools.partial(_hgrn_kernel, nb=nb, sl=sl, n_valid=n_valid, has_init=has_init, mode=mode,
                                  head_group=head_group)
    n_g, n_h = len(gdn_in), len(hgrn_in)

    def both(*refs):
        ins, outs, scr = refs[:n_g + n_h], refs[n_g + n_h:n_g + n_h + 4], refs[n_g + n_h + 4:]
        gdn_body(*ins[:n_g], outs[0], outs[1], scr[0], scr[1])
        hgrn_body(*ins[n_g:], outs[2], outs[3], scr[2])

    state_scr = pltpu.VMEM((nb, N_HEADS, D_HEAD, D_HEAD), F32)
    o_shape = jax.ShapeDtypeStruct((nseq, length, width), out_dtype)
    o_gdn, s_gdn, o_hgrn, s_hgrn = pl.pallas_call(
        both,
        grid=grid,
        in_specs=gdn_specs + hgrn_specs,
        out_specs=[out_spec(width), state_spec, out_spec(width), state_spec],
        out_shape=[o_shape, state_shape, o_shape, state_shape],
        scratch_shapes=[state_scr, pltpu.VMEM((nb, SUBLANES, conv_ch), F32), state_scr],
        compiler_params=_params(("parallel", "arbitrary")),
    )(*gdn_in, *hgrn_in)
    return o_gdn, o_hgrn, s_gdn, s_hgrn


def _outproj_router_kernel(oap_ref, obp_ref, hpp_ref, oas_ref, obs_ref, hps_ref, wo_ref, n2_ref, wr_ref, br_ref,
                           hp2_ref, xn2_ref, ri_ref, rw_ref, cnt_ref, carry_scr, *, tm, n_p):
    i = pl.program_id(0)

    @pl.when(i == 0)
    def _():
        carry_scr[...] = jnp.zeros_like(carry_scr)

    body = functools.partial(_outproj_router_tile, wo_ref=wo_ref, n2_ref=n2_ref, wr_ref=wr_ref, br_ref=br_ref,
                             hp2_ref=hp2_ref, xn2_ref=xn2_ref, ri_ref=ri_ref, rw_ref=rw_ref,
                             carry_scr=carry_scr, tm=tm)

    @pl.when(i < n_p)
    def _():
        body(oap_ref[...], obp_ref[...], hpp_ref[...])

    @pl.when(i >= n_p)
    def _():
        body(oas_ref[...], obs_ref[...], hps_ref[...])

    @pl.when(i == pl.num_programs(0) - 1)
    def _():
        cnt_ref[...] = carry_scr[...]


def _outproj_router_tile(oa, ob, hp, *, wo_ref, n2_ref, wr_ref, br_ref, hp2_ref, xn2_ref, ri_ref, rw_ref,
                         carry_scr, tm):
    half = oa.shape[-1]
    mix = _dot(oa.astype(BF16), wo_ref[:half, :]) + _dot(ob.astype(BF16), wo_ref[half:, :])
    hp2 = hp + mix
    hp2_ref[...] = hp2
    xn2 = hp2 * lax.rsqrt(jnp.mean(hp2 * hp2, axis=-1, keepdims=True) + EPS) * n2_ref[...]
    xn2_ref[...] = xn2
    logits = _dot(xn2, wr_ref[...], "bf16x3") + br_ref[...]

    lane = lax.broadcasted_iota(jnp.int32, (tm, LANES), 1)
    lane_f = lane.astype(F32)
    far = float(4 * LANES)
    is_g = (lane >= N_EXPERTS) & (lane < N_EXPERTS + N_GROUPS)
    lg = jnp.where(is_g, logits, -jnp.inf)
    gmax = jnp.max(lg, axis=-1, keepdims=True)
    gsel = jnp.min(jnp.where(lg == gmax, lane_f, far), axis=-1, keepdims=True).astype(jnp.int32) - N_EXPERTS
    p_top = 1.0 / jnp.sum(jnp.where(is_g, jnp.exp(logits - gmax), 0.0), axis=-1, keepdims=True)
    in_grp = (lane < N_EXPERTS) & (lax.shift_right_logical(lane, 3) == gsel)
    le = jnp.where(in_grp, logits, -jnp.inf)
    m1 = jnp.max(le, axis=-1, keepdims=True)
    i1 = jnp.min(jnp.where(le == m1, lane_f, far), axis=-1, keepdims=True).astype(jnp.int32)
    le2 = jnp.where(lane == i1, -jnp.inf, le)
    m2 = jnp.max(le2, axis=-1, keepdims=True)
    i2 = jnp.min(jnp.where(le2 == m2, lane_f, far), axis=-1, keepdims=True).astype(jnp.int32)
    e2 = jnp.exp(m2 - m1)
    w1 = p_top / (1.0 + e2)
    w2 = p_top * e2 / (1.0 + e2)

    onehot = (lane == i1) | (lane == i2)
    onehot_f = jnp.where(onehot, 1.0, 0.0)
    tri = (lax.broadcasted_iota(jnp.int32, (tm, tm), 1) < lax.broadcasted_iota(jnp.int32, (tm, tm), 0))
    before = _dot(jnp.where(tri, 1.0, 0.0).astype(BF16), onehot_f.astype(BF16)) + carry_scr[...]
    r1 = jnp.sum(jnp.where(lane == i1, before, 0.0), axis=-1, keepdims=True).astype(jnp.int32)
    r2 = jnp.sum(jnp.where(lane == i2, before, 0.0), axis=-1, keepdims=True).astype(jnp.int32)
    carry_scr[...] = carry_scr[...] + jnp.sum(onehot_f, axis=0, keepdims=True)

    ri_ref[...] = jnp.where(lane == 0, i1, jnp.where(lane == 1, i2, jnp.where(lane == 2, r1, jnp.where(lane == 3, r2, 0))))
    rw_ref[...] = jnp.where(lane == 0, w1, jnp.where(lane == 1, w2, 0.0))


def _outproj_router(oa_p, ob_p, hp_p, oa_s, ob_s, hp_s, w_out, norm2_w, w_r, b_r, tm):
    (tp, d), ts = hp_p.shape, hp_s.shape[0]
    half = oa_p.shape[1]
    n_p, n_s = tp // tm, ts // tm
    t = tp + ts
    prow = lambda w: pl.BlockSpec((tm, w), lambda i: (jnp.minimum(i, n_p - 1), 0))
    srow = lambda w: pl.BlockSpec((tm, w), lambda i: (jnp.maximum(i - n_p, 0), 0))
    row = lambda w: pl.BlockSpec((tm, w), lambda i: (i, 0))
    const = lambda shape: pl.BlockSpec(shape, lambda i: (0,) * len(shape))
    return pl.pallas_call(
        functools.partial(_outproj_router_kernel, tm=tm, n_p=n_p),
        grid=(n_p + n_s,),
        in_specs=[prow(half), prow(half), prow(d), srow(half), srow(half), srow(d),
                  const(w_out.shape), const((1, d)), const(w_r.shape), const((1, LANES))],
        out_specs=[row(d), row(d), row(LANES), row(LANES), const((1, LANES))],
        out_shape=[jax.ShapeDtypeStruct((t, d), F32), jax.ShapeDtypeStruct((t, d), F32),
                   jax.ShapeDtypeStruct((t, LANES), jnp.int32), jax.ShapeDtypeStruct((t, LANES), F32),
                   jax.ShapeDtypeStruct((1, LANES), F32)],
        scratch_shapes=[pltpu.VMEM((1, LANES), F32)],
        compiler_params=_params(("arbitrary",)),
    )(oa_p, ob_p, hp_p, oa_s, ob_s, hp_s, w_out, norm2_w.reshape(1, d), w_r, b_r)


def _start_row_gather(idx_ref, base, n, src_hbm, dst, sem):
    def issue(r, carry):
        pltpu.make_async_copy(src_hbm.at[pl.ds(idx_ref[base + r], 1), :], dst.at[pl.ds(r, 1), :], sem).start()
        return carry
    lax.fori_loop(0, n, issue, 0, unroll=8)


def _wait_row_gather(n, src_hbm, dst, sem):
    pltpu.make_async_copy(src_hbm.at[pl.ds(0, n), :], dst, sem).wait()


def _moe_kernel(te_ref, st_ref, nx_ref, nu_ref, x_hbm, wg_hbm, wu_hbm, wd_hbm, o_ref,
                xbuf, wg_f32, wu_f32, wd_f32, wg_scr, wu_scr, wd_scr, sems, wsems, *, tile):
    i = pl.program_id(0)
    n_used = nu_ref[0]
    slot = lax.rem(i, 2)
    weights = ((wg_hbm, wg_f32, wg_scr), (wu_hbm, wu_f32, wu_scr), (wd_hbm, wd_f32, wd_scr))

    def start_weights(e):
        for k, (w_hbm, w_f32, _) in enumerate(weights):
            pltpu.async_copy(w_hbm.at[e], w_f32, wsems.at[k], priority=1)

    @pl.when((i == 0) & (n_used > 0))
    def _():
        start_weights(te_ref[0])
        _start_row_gather(st_ref, 0, tile, x_hbm, xbuf.at[0], sems.at[0])

    @pl.when(i < n_used)
    def _():
        @pl.when(i + 1 < n_used)
        def _():
            _start_row_gather(st_ref, (i + 1) * tile, tile, x_hbm, xbuf.at[1 - slot], sems.at[1 - slot])

        e = te_ref[i]

        @pl.when((i == 0) | (e != te_ref[jnp.maximum(i - 1, 0)]))
        def _():
            for k, (w_hbm, w_f32, w_scr) in enumerate(weights):
                pltpu.make_async_copy(w_hbm.at[0], w_f32, wsems.at[k]).wait()
                w_scr[...] = w_f32[...].astype(BF16)
            nxt = nx_ref[e]

            @pl.when(nxt < N_EXPERTS)
            def _():
                start_weights(nxt)

        _wait_row_gather(tile, x_hbm, xbuf.at[slot], sems.at[slot])
        x = xbuf[slot].astype(BF16)
        g = _dot(x, wg_scr[...])
        u = _dot(x, wu_scr[...])
        o_ref[...] = _dot((_silu(g) * u).astype(BF16), wd_scr[...])

    @pl.when(i >= n_used)
    def _():
        o_ref[...] = jnp.zeros_like(o_ref)


def _moe(xn2, w_gate, w_up, w_down, tile_expert, slot_token, next_expert, n_used, tile):
    d = xn2.shape[1]
    n_tiles = tile_expert.shape[0]
    de = w_gate.shape[2]
    hbm = pl.BlockSpec(memory_space=pl.ANY)
    grid_spec = pltpu.PrefetchScalarGridSpec(
        num_scalar_prefetch=4,
        grid=(n_tiles,),
        in_specs=[hbm, hbm, hbm, hbm],
        out_specs=pl.BlockSpec((tile, d), lambda i, te, st, nx, nu: (i, 0)),
        scratch_shapes=[pltpu.VMEM((2, tile, d), F32),
                        pltpu.VMEM((d, de), F32), pltpu.VMEM((d, de), F32), pltpu.VMEM((de, d), F32),
                        pltpu.VMEM((d, de), BF16), pltpu.VMEM((d, de), BF16), pltpu.VMEM((de, d), BF16),
                        pltpu.SemaphoreType.DMA((2,)), pltpu.SemaphoreType.DMA((3,))],
    )
    return pl.pallas_call(
        functools.partial(_moe_kernel, tile=tile),
        grid_spec=grid_spec,
        out_shape=jax.ShapeDtypeStruct((n_tiles * tile, d), F32),
        compiler_params=_params(("arbitrary",)),
    )(tile_expert, slot_token, next_expert, n_used, xn2, w_gate, w_up, w_down)


def _combine_kernel(p0_ref, p1_ref, ys_hbm, hp2_ref, rw_ref, fw_ref, o_ref, buf0, buf1, sems, *, tm, tile_off):
    i = pl.program_id(0)
    slot = lax.rem(i, 2)

    def start(step, s):
        base = (step + tile_off) * tm
        _start_row_gather(p0_ref, base, tm, ys_hbm, buf0.at[s], sems.at[0, s])
        _start_row_gather(p1_ref, base, tm, ys_hbm, buf1.at[s], sems.at[1, s])

    @pl.when(i == 0)
    def _():
        start(0, 0)

    @pl.when(i + 1 < pl.num_programs(0))
    def _():
        start(i + 1, 1 - slot)

    _wait_row_gather(tm, ys_hbm, buf0.at[slot], sems.at[0, slot])
    _wait_row_gather(tm, ys_hbm, buf1.at[slot], sems.at[1, slot])
    rw = rw_ref[...]
    y = hp2_ref[...] + rw[:, 0:1] * buf0[slot] + rw[:, 1:2] * buf1[slot]
    o_ref[...] = y * lax.rsqrt(jnp.mean(y * y, axis=-1, keepdims=True) + EPS) * fw_ref[...]


def _combine(ys, hp2, route_w, final_w, pos0, pos1, tm, tile_off, n_tiles):
    _, d = hp2.shape
    grid_spec = pltpu.PrefetchScalarGridSpec(
        num_scalar_prefetch=2,
        grid=(n_tiles,),
        in_specs=[pl.BlockSpec(memory_space=pl.ANY),
                  pl.BlockSpec((tm, d), lambda i, p0, p1: (i + tile_off, 0)),
                  pl.BlockSpec((tm, LANES), lambda i, p0, p1: (i + tile_off, 0)),
                  pl.BlockSpec((1, d), lambda i, p0, p1: (0, 0))],
        out_specs=pl.BlockSpec((tm, d), lambda i, p0, p1: (i, 0)),
        scratch_shapes=[pltpu.VMEM((2, tm, d), F32), pltpu.VMEM((2, tm, d), F32),
                        pltpu.SemaphoreType.DMA((2, 2))],
    )
    return pl.pallas_call(
        functools.partial(_combine_kernel, tm=tm, tile_off=tile_off),
        grid_spec=grid_spec,
        out_shape=jax.ShapeDtypeStruct((n_tiles * tm, d), F32),
        compiler_params=_params(("arbitrary",)),
    )(pos0, pos1, ys, hp2, route_w, final_w.reshape(1, d))


def kernel(x_prompt, x_sample, state_gdn, state_conv, state_hgrn, meta_tokens, norm1_w, w_in, conv_w, a_log,
           dt_bias, gdn_norm_w, lb_logits, hgrn_norm_w, w_out, norm2_w, w_router_group, b_router_group,
           w_router_expert, b_router_expert, w_gate, w_up, w_down, final_norm_w):
    bp, seq, d = x_prompt.shape
    bs, dec_seq, _ = x_sample.shape
    assert w_in.shape[0] == 1, "single-layer trunk"
    width = N_HEADS * D_HEAD
    conv_ch = 3 * width
    tile = 256
    tn = 512
    sl_s = SUBLANES
    nb_s = CHUNK // sl_s
    assert seq % CHUNK == 0 and N_META <= CHUNK and dec_seq <= sl_s and bs % nb_s == 0
    tp, ts, ts_pad = bp * seq, bs * dec_seq, bs * sl_s
    t_small = ts_pad + CHUNK
    tm = _row_tile(math.gcd(tp, ts), 256)
    tm_p = _row_tile(tp, 1024)

    xp = x_prompt.reshape(tp, d)
    x_small = jnp.concatenate([jnp.pad(x_sample, ((0, 0), (0, sl_s - dec_seq), (0, 0))).reshape(ts_pad, d),
                               jnp.zeros((CHUNK - N_META, d), F32), meta_tokens.astype(F32)], axis=0)

    wi = w_in[0]
    n_a = 4 * width // tn
    w_b = wi[:, 4 * width + 2 * N_HEADS:]
    w_ba = jnp.pad(wi[:, 4 * width:4 * width + 2 * N_HEADS], ((0, 0), (0, LANES - 2 * N_HEADS))).astype(BF16)
    n_cols = 8 * width

    xn_p = _rmsnorm(xp, norm1_w[0], BF16, _row_tile(tp, 512))
    xn_s = _rmsnorm(x_small, norm1_w[0], BF16, _row_tile(t_small, 1024))
    w_a = wi[:, :4 * width]
    proj_p = _inproj(xn_p, w_a, n_a, w_b, tm_p, tn)
    proj_s = _inproj(xn_s, w_a, n_a, w_b, t_small, tn)
    ba_p = _matmul(xn_p, w_ba, tm_p, LANES)
    ba_s = _matmul(xn_s, w_ba, t_small, LANES)

    pvec = jnp.zeros((2, LANES), F32)
    pvec = pvec.at[0, N_HEADS:2 * N_HEADS].set(a_log[0]).at[1, N_HEADS:2 * N_HEADS].set(dt_bias[0])
    lb = jnp.cumsum(jax.nn.softmax(lb_logits.astype(F32), axis=0), axis=0)[0].reshape(1, width)
    gdn_nw = gdn_norm_w[0].reshape(1, D_HEAD)
    hgrn_nw = hgrn_norm_w[0].reshape(1, D_HEAD)
    cw = conv_w[0]
    mix_args = (cw, pvec, gdn_nw, lb, hgrn_nw)

    _, _, sg_m, sh_m = _mixers(proj_s.reshape(t_small // CHUNK, CHUNK, n_cols),
                               ba_s.reshape(t_small // CHUNK, CHUNK, LANES), ts_pad // CHUNK, 1, 1, CHUNK, CHUNK,
                               *mix_args, None, None, None, False, F32, MIX_MODE, INV_MODE)
    conv_m = proj_s[t_small - SUBLANES:, :conv_ch].reshape(1, SUBLANES, conv_ch)
    oa_p, ob_p, sg_p, sh_p = _mixers(proj_p.reshape(bp, seq, n_cols), ba_p.reshape(bp, seq, LANES), 0, bp, 1,
                                     CHUNK, CHUNK, *mix_args, sg_m, conv_m, sh_m, True, BF16, MIX_MODE, INV_MODE)
    conv0 = jnp.pad(state_conv[0], ((0, 0), (SUBLANES - (CONV_W - 1), 0), (0, 0)))
    oa_s, ob_s, sg_s, sh_s = _mixers(proj_s.reshape(t_small // sl_s, sl_s, n_cols),
                                     ba_s.reshape(t_small // sl_s, sl_s, LANES), 0, bs, nb_s, sl_s, dec_seq,
                                     *mix_args, state_gdn[0], conv0, state_hgrn[0], False, F32, MIX_MODE, INV_MODE)

    w_r = jnp.concatenate([w_router_expert[0], w_router_group[0],
                           jnp.zeros((d, LANES - N_EXPERTS - N_GROUPS), F32)], axis=1)
    b_r = jnp.concatenate([b_router_expert[0], b_router_group[0],
                           jnp.zeros((LANES - N_EXPERTS - N_GROUPS,), F32)]).reshape(1, LANES)
    t = tp + ts
    hp2, xn2, route_i, route_w, counts = _outproj_router(
        oa_p.reshape(tp, width), ob_p.reshape(tp, width), xp,
        oa_s[:, :dec_seq].reshape(ts, width), ob_s[:, :dec_seq].reshape(ts, width), x_sample.reshape(ts, d),
        w_out[0].astype(BF16), norm2_w[0], w_r, b_r, tm)

    cnt = counts[0, :N_EXPERTS].astype(jnp.int32)
    padded = (cnt + tile - 1) // tile * tile
    ends = jnp.cumsum(padded)
    offs = ends - padded
    eid = route_i[:, 0:2]
    expert_ids = jnp.arange(N_EXPERTS, dtype=jnp.int32)
    pos = jnp.sum(jnp.where(eid[:, :, None] == expert_ids, offs, 0), axis=-1) + route_i[:, 2:4]
    n_tiles = (2 * t) // tile + N_EXPERTS
    tile_start = jnp.arange(n_tiles, dtype=jnp.int32) * tile
    tile_expert = jnp.minimum(jnp.sum((ends[None, :] <= tile_start[:, None]).astype(jnp.int32), axis=1),
                              N_EXPERTS - 1)
    n_used = (ends[-1] // tile).astype(jnp.int32).reshape(1)
    tok = jnp.broadcast_to(jnp.arange(t, dtype=jnp.int32)[:, None], (t, 2))
    slot_token = jnp.zeros((n_tiles * tile,), jnp.int32).at[pos.reshape(-1)].set(
        tok.reshape(-1), unique_indices=True, mode="promise_in_bounds")

    later_active = (expert_ids[None, :] > expert_ids[:, None]) & (cnt[None, :] > 0)
    next_expert = jnp.min(jnp.where(later_active, expert_ids[None, :], N_EXPERTS), axis=1).astype(jnp.int32)
    ys = _moe(xn2, w_gate[0], w_up[0], w_down[0], tile_expert, slot_token, next_expert, n_used, tile)
    pos0, pos1 = pos[:, 0], pos[:, 1]
    y_prompt = _combine(ys, hp2, route_w, final_norm_w, pos0, pos1, tm, 0, tp // tm).reshape(bp, seq, d)
    y_sample = _combine(ys, hp2, route_w, final_norm_w, pos0, pos1, tm, tp // tm, ts // tm).reshape(bs, dec_seq, d)

    conv_p = proj_p.reshape(bp, seq, n_cols)[:, seq - (CONV_W - 1):, :conv_ch]
    u_s = proj_s.reshape(t_small // sl_s, sl_s, n_cols)[:bs, :dec_seq, :conv_ch]
    conv_s = jnp.concatenate([state_conv[0], u_s], axis=1)[:, dec_seq:]
    return (y_prompt, y_sample, sg_p[None], conv_p[None], sh_p[None], sg_s[None], conv_s[None], sh_s[None])
```

```python
import functools
import math

import jax
import jax.numpy as jnp
from jax import lax
from jax.experimental import pallas as pl
from jax.experimental.pallas import tpu as pltpu

F32 = jnp.float32
BF16 = jnp.bfloat16
HIGHEST = lax.Precision.HIGHEST

EPS = 1e-6
N_META = 16
CONV_W = 4
N_HEADS = 8
D_HEAD = 128
N_GROUPS = 4
EXPERTS_PER_GROUP = 8
N_EXPERTS = N_GROUPS * EXPERTS_PER_GROUP

LANES = 128
SUBLANES = 8
CHUNK = 64
SUB = 16
VMEM_LIMIT = 56 * 1024 * 1024
NEG_BIG = -1e30
MIX_MODE = "bf16"
INV_MODE = "bf16"


def _sigmoid(x):
    return 0.5 * jnp.tanh(0.5 * x) + 0.5


def _silu(x):
    return x * _sigmoid(x)


def _softplus(x):
    return jnp.maximum(x, 0.0) + jnp.log1p(jnp.exp(-jnp.abs(x)))


def _split_bf16(a, pieces):
    out = []
    for _ in range(pieces - 1):
        hi = a.astype(BF16)
        out.append(hi)
        a = a - hi.astype(F32)
    out.append(a.astype(BF16))
    return out


def _mm(a, b, dims, mode):
    dg = functools.partial(lax.dot_general, dimension_numbers=(dims, ((), ())), preferred_element_type=F32)
    if mode == "f32":
        return dg(a, b, precision=HIGHEST)
    if mode == "bf16":
        return dg(a.astype(BF16), b.astype(BF16))
    assert mode == "bf16x3"
    ah, al = _split_bf16(a, 2)
    bh, bl = _split_bf16(b, 2)
    return dg(ah, bh) + dg(ah, bl) + dg(al, bh)


def _dot(a, b, mode="bf16"):
    return _mm(a, b, ((1,), (0,)), mode)


def _dot_nt(a, b, mode="bf16"):
    return _mm(a, b, ((1,), (1,)), mode)


def _dot_tn(a, b, mode="bf16"):
    return _mm(a, b, ((0,), (0,)), mode)


def _masked_cumsum(lmask, x):
    lm = lmask.astype(BF16)
    return sum(lax.dot_general(lm, p, (((1,), (0,)), ((), ())), preferred_element_type=F32)
               for p in _split_bf16(x, 3))


def _params(sem):
    return pltpu.CompilerParams(dimension_semantics=sem, vmem_limit_bytes=VMEM_LIMIT)


def _row_tile(n, target):
    best = max(c for c in range(16, min(n, target) + 1, 16) if n % c == 0)
    return best


def _rmsnorm_kernel(x_ref, w_ref, o_ref):
    x = x_ref[...]
    ms = jnp.mean(x * x, axis=-1, keepdims=True)
    o_ref[...] = (x * lax.rsqrt(ms + EPS) * w_ref[...]).astype(o_ref.dtype)


def _rmsnorm(x, w, out_dtype, tm):
    t, d = x.shape
    return pl.pallas_call(
        _rmsnorm_kernel,
        grid=(t // tm,),
        in_specs=[pl.BlockSpec((tm, d), lambda i: (i, 0)), pl.BlockSpec((1, d), lambda i: (0, 0))],
        out_specs=pl.BlockSpec((tm, d), lambda i: (i, 0)),
        out_shape=jax.ShapeDtypeStruct((t, d), out_dtype),
        compiler_params=_params(("parallel",)),
    )(x, w.reshape(1, d))


def _inproj_kernel(x_ref, wa_ref, wb_ref, o_ref, w_scr, *, n_a):
    j = pl.program_id(0)
    i = pl.program_id(1)

    @pl.when((i == 0) & (j < n_a))
    def _():
        w_scr[...] = wa_ref[...].astype(BF16)

    @pl.when((i == 0) & (j >= n_a))
    def _():
        w_scr[...] = wb_ref[...].astype(BF16)

    o_ref[...] = _dot(x_ref[...], w_scr[...])


def _inproj(x, w_a, n_a, w_b, tm, tn):
    t, k = x.shape
    n_b = w_b.shape[1] // tn
    return pl.pallas_call(
        functools.partial(_inproj_kernel, n_a=n_a),
        grid=(n_a + n_b, t // tm),
        in_specs=[pl.BlockSpec((tm, k), lambda j, i: (i, 0)),
                  pl.BlockSpec((k, tn), lambda j, i: (0, jnp.minimum(j, n_a - 1))),
                  pl.BlockSpec((k, tn), lambda j, i: (0, jnp.maximum(j - n_a, 0)))],
        out_specs=pl.BlockSpec((tm, tn), lambda j, i: (i, j)),
        out_shape=jax.ShapeDtypeStruct((t, (n_a + n_b) * tn), F32),
        scratch_shapes=[pltpu.VMEM((k, tn), BF16)],
        compiler_params=_params(("arbitrary", "arbitrary")),
    )(x, w_a, w_b)


def _matmul_kernel(x_ref, w_ref, o_ref):
    o_ref[...] = _dot(x_ref[...], w_ref[...])


def _matmul(x, w, tm, tn):
    t, k = x.shape
    n = w.shape[1]
    return pl.pallas_call(
        _matmul_kernel,
        grid=(n // tn, t // tm),
        in_specs=[pl.BlockSpec((tm, k), lambda j, i: (i, 0)), pl.BlockSpec((k, tn), lambda j, i: (0, j))],
        out_specs=pl.BlockSpec((tm, tn), lambda j, i: (i, j)),
        out_shape=jax.ShapeDtypeStruct((t, n), F32),
        compiler_params=_params(("parallel", "arbitrary")),
    )(x, w)


def _chunk_masks(nb, sl):
    r = nb * sl
    shift = int(math.log2(sl))
    ri = lax.broadcasted_iota(jnp.int32, (r, r), 0)
    ci = lax.broadcasted_iota(jnp.int32, (r, r), 1)
    same = lax.shift_right_logical(ri, shift) == lax.shift_right_logical(ci, shift)
    return same & (ci <= ri), same & (ci < ri)


def _row_valid(nb, sl, n_valid):
    rowid = lax.broadcasted_iota(jnp.int32, (nb * sl, 1), 0)
    return (rowid & (sl - 1)) < n_valid


def _last_row_bcast(x, nb, sl):
    c = x.shape[-1]
    x3 = x.reshape(nb, sl, c)
    return jnp.broadcast_to(x3[:, sl - 1:sl, :], (nb, sl, c)).reshape(nb * sl, c)


def _gated_rmsnorm(o, w, gate):
    return o * lax.rsqrt(jnp.mean(o * o, axis=-1, keepdims=True) + EPS) * w * _silu(gate)


def _gdn_kernel(*refs, nb, sl, n_valid, has_init, mode, inv_mode, head_group):
    if has_init:
        (qkv_ref, z_ref, ba_ref, cw_ref, pv_ref, nw_ref, s0_ref, c0_ref,
         o_ref, sout_ref, s_scr, carry_scr) = refs
    else:
        (qkv_ref, z_ref, ba_ref, cw_ref, pv_ref, nw_ref,
         o_ref, sout_ref, s_scr, carry_scr) = refs
    c = pl.program_id(1)
    r = nb * sl
    dh = D_HEAD

    @pl.when(c == 0)
    def _():
        if has_init:
            s_scr[...] = s0_ref[...]
            carry_scr[...] = c0_ref[...]
        else:
            s_scr[...] = jnp.zeros_like(s_scr)
            carry_scr[...] = jnp.zeros_like(carry_scr)

    incl, strict = _chunk_masks(nb, sl)
    lmask = jnp.where(incl, 1.0, 0.0)
    offdiag = jnp.where(strict, 1.0, 0.0)
    valid = _row_valid(nb, sl, n_valid)
    masked = n_valid < sl
    rowid = lax.broadcasted_iota(jnp.int32, (r, 1), 0)
    row_seq = lax.shift_right_logical(rowid, int(math.log2(sl)))
    n_sq = int(math.log2(sl)) - 1

    ba = ba_ref[...].reshape(r, LANES)
    pv = pv_ref[...]
    beta_all = _sigmoid(ba)
    g_all = -jnp.exp(pv[0:1]) * _softplus(ba + pv[1:2])
    if masked:
        g_all = jnp.where(valid, g_all, 0.0)
    gcum = _masked_cumsum(lmask, g_all)
    gcum_t = gcum.T
    glast = _last_row_bcast(gcum, nb, sl)
    cw = cw_ref[...]
    nw = nw_ref[...]

    def conv_slice(c0):
        u = qkv_ref[:, :, c0:c0 + dh]
        prev = carry_scr[:, :, c0:c0 + dh]
        full = jnp.concatenate([prev, u], axis=1)
        acc = None
        for j in range(CONV_W):
            off = SUBLANES - (CONV_W - 1) + j
            term = full[:, off:off + sl, :] * cw[j:j + 1, c0:c0 + dh]
            acc = term if acc is None else acc + term
        return _silu(acc).reshape(r, dh)

    for h0 in range(0, N_HEADS, head_group):
        heads = range(h0, h0 + head_group)
        qs, ks, vs, bcs, gcs, gls, egs, decays = [], [], [], [], [], [], [], []
        for h in heads:
            q = conv_slice(h * dh)
            k = conv_slice(N_HEADS * dh + h * dh)
            v = conv_slice(2 * N_HEADS * dh + h * dh)
            q = q * lax.rsqrt(jnp.sum(q * q, axis=-1, keepdims=True) + EPS) * (dh ** -0.5)
            k = k * lax.rsqrt(jnp.sum(k * k, axis=-1, keepdims=True) + EPS)
            if masked:
                q = jnp.where(valid, q, 0.0)
                k = jnp.where(valid, k, 0.0)
                v = jnp.where(valid, v, 0.0)
            gc = gcum[:, N_HEADS + h:N_HEADS + h + 1]
            gr = gcum_t[N_HEADS + h:N_HEADS + h + 1, :]
            qs.append(q)
            ks.append(k)
            vs.append(v)
            bcs.append(beta_all[:, h:h + 1])
            gcs.append(gc)
            gls.append(glast[:, N_HEADS + h:N_HEADS + h + 1])
            egs.append(jnp.exp(gc))
            decays.append(jnp.exp(jnp.where(incl, gc - gr, NEG_BIG)))
        n = len(qs)
        qk_kk = [_dot_nt(jnp.concatenate([qs[i], ks[i]], axis=0), ks[i], mode) for i in range(n)]
        tm1 = [qk_kk[i][r:] * (decays[i] * offdiag) * (-bcs[i]) for i in range(n)]
        pw = list(tm1)
        for _ in range(n_sq):
            pw = [_dot(pw[i], pw[i], inv_mode) for i in range(n)]
            tm1 = [tm1[i] + pw[i] + _dot(tm1[i], pw[i], inv_mode) for i in range(n)]
        rhs = [jnp.concatenate([vs[i] * bcs[i], ks[i] * (bcs[i] * egs[i])], axis=1) for i in range(n)]
        uw = [rhs[i] + _dot(tm1[i], rhs[i], mode) for i in range(n)]
        vnew, ointer = [], []
        for i, h in enumerate(heads):
            u = uw[i][:, :dh]
            w = uw[i][:, dh:]
            qe = qs[i] * egs[i]
            vnew_parts, ointer_parts = [], []
            for b in range(nb):
                rows = slice(b * sl, (b + 1) * sl)
                ws = _dot(jnp.concatenate([w[rows], qe[rows]], axis=0), s_scr[b, h], mode)
                vnew_parts.append(u[rows] - ws[:sl])
                ointer_parts.append(ws[sl:])
            vnew.append(vnew_parts[0] if nb == 1 else jnp.concatenate(vnew_parts, axis=0))
            ointer.append(ointer_parts[0] if nb == 1 else jnp.concatenate(ointer_parts, axis=0))
        for i, h in enumerate(heads):
            attn = qk_kk[i][:r] * decays[i]
            o = ointer[i] + _dot(attn, vnew[i], mode)
            z = z_ref[:, :, h * dh:(h + 1) * dh].reshape(r, dh)
            o_ref[:, :, h * dh:(h + 1) * dh] = _gated_rmsnorm(o, nw, z).reshape(nb, sl, dh).astype(o_ref.dtype)
        for i, h in enumerate(heads):
            ktil = ks[i] * jnp.exp(gls[i] - gcs[i])
            for b in range(nb):
                kt_b = ktil if nb == 1 else jnp.where(row_seq == b, ktil, 0.0)
                gl_b = gls[i][b * sl:b * sl + 1, :]
                s_scr[b, h] = s_scr[b, h] * jnp.exp(gl_b) + _dot_tn(kt_b, vnew[i], mode)

    carry_scr[...] = qkv_ref[:, sl - SUBLANES:sl, :]

    @pl.when(c == pl.num_programs(1) - 1)
    def _():
        sout_ref[...] = s_scr[...]


def _hgrn_kernel(*refs, nb, sl, n_valid, has_init, mode, head_group):
    if has_init:
        (q_ref, f_ref, i_ref, g_ref, lb_ref, nw_ref, s0_ref, o_ref, sout_ref, s_scr) = refs
    else:
        (q_ref, f_ref, i_ref, g_ref, lb_ref, nw_ref, o_ref, sout_ref, s_scr) = refs
    c = pl.program_id(1)
    r = nb * sl
    dh = D_HEAD
    width = N_HEADS * dh

    @pl.when(c == 0)
    def _():
        if has_init:
            s_scr[...] = s0_ref[...]
        else:
            s_scr[...] = jnp.zeros_like(s_scr)

    incl, _ = _chunk_masks(nb, sl)
    lmask = jnp.where(incl, 1.0, 0.0)
    valid = _row_valid(nb, sl, n_valid)
    masked = n_valid < sl
    rowid = lax.broadcasted_iota(jnp.int32, (r, 1), 0)
    row_seq = lax.shift_right_logical(rowid, int(math.log2(sl)))
    sub = min(SUB, sl)
    nblk = r // sub
    sub_shift = int(math.log2(sub))
    ri = lax.broadcasted_iota(jnp.int32, (r, r), 0)
    ci = lax.broadcasted_iota(jnp.int32, (r, r), 1)
    same_blk = lax.shift_right_logical(ri, sub_shift) == lax.shift_right_logical(ci, sub_shift)
    diag_mask = incl & same_blk
    cross_mask = incl & jnp.logical_not(same_blk)

    lb = lb_ref[...]
    f = lb + (1.0 - lb) * _sigmoid(f_ref[...].reshape(r, width))
    lf = jnp.log(f)
    k_all = 1.0 - f
    if masked:
        lf = jnp.where(valid, lf, 0.0)
        k_all = jnp.where(valid, k_all, 0.0)
    bcum = _masked_cumsum(lmask, lf)
    nw = nw_ref[...]

    def head_inputs(h):
        cols = slice(h * dh, (h + 1) * dh)
        q = _silu(q_ref[:, :, cols].reshape(r, dh)) * (dh ** -0.5)
        v = i_ref[:, :, cols].reshape(r, dh)
        if masked:
            v = jnp.where(valid, v, 0.0)
        return q, k_all[:, cols], v, bcum[:, cols]

    def intra_attn(q, k, bh):
        bmid = jnp.broadcast_to(bh.reshape(nblk, sub, dh)[:, sub // 2:sub // 2 + 1, :],
                                (nblk, sub, dh)).reshape(r, dh)
        attn = jnp.where(diag_mask, _dot_nt(q * jnp.exp(bh - bmid), k * jnp.exp(bmid - bh), mode), 0.0)
        if sl > sub:
            parts = [jnp.zeros((sub, r), F32)]
            for blk in range(1, nblk):
                start = blk * sub
                bref = bh[start - 1:start, :]
                qc = q[start:start + sub] * jnp.exp(bh[start:start + sub] - bref)
                kc = k * jnp.exp(jnp.minimum(bref - bh, 0.0))
                parts.append(_dot_nt(qc, kc, mode))
            attn = attn + jnp.where(cross_mask, jnp.concatenate(parts, axis=0), 0.0)
        return attn

    for h0 in range(0, N_HEADS, head_group):
        heads = range(h0, h0 + head_group)
        ins = [head_inputs(h) for h in heads]
        attns = [intra_attn(q, k, bh) for (q, k, v, bh) in ins]
        for i, h in enumerate(heads):
            q, k, v, bh = ins[i]
            qe = q * jnp.exp(bh)
            ointer_parts = []
            for b in range(nb):
                rows = slice(b * sl, (b + 1) * sl)
                ointer_parts.append(_dot(qe[rows], s_scr[b, h], mode))
            ointer = ointer_parts[0] if nb == 1 else jnp.concatenate(ointer_parts, axis=0)
            o = ointer + _dot(attns[i], v, mode)
            cols = slice(h * dh, (h + 1) * dh)
            gate = g_ref[:, :, cols].reshape(r, dh)
            o_ref[:, :, cols] = _gated_rmsnorm(o, nw, gate).reshape(nb, sl, dh).astype(o_ref.dtype)
        for i, h in enumerate(heads):
            q, k, v, bh = ins[i]
            blast = _last_row_bcast(bh, nb, sl)
            ktil = k * jnp.exp(blast - bh)
            pad = [jnp.zeros((LANES - r, dh), F32)] if r < LANES else []
            tr = jnp.concatenate([blast] + pad, axis=0).T
            for b in range(nb):
                kt_b = ktil if nb == 1 else jnp.where(row_seq == b, ktil, 0.0)
                dec_col = jnp.exp(tr[:, b * sl:b * sl + 1])
                s_scr[b, h] = s_scr[b, h] * dec_col + _dot_tn(kt_b, v, mode)

    @pl.when(c == pl.num_programs(1) - 1)
    def _():
        sout_ref[...] = s_scr[...]


def _mixers(proj3, ba3, blk_off, nseq, nb, sl, n_valid, conv_w, pvec, gdn_nw, lb, hgrn_nw,
            s_gdn0, conv0, s_hgrn0, shared_init, out_dtype, mode, inv_mode):
    length = proj3.shape[1]
    has_init = s_gdn0 is not None
    width = N_HEADS * D_HEAD
    conv_ch = 3 * width
    grid = (nseq // nb, length // sl)
    state_spec = pl.BlockSpec((nb, N_HEADS, D_HEAD, D_HEAD), lambda g, c: (g, 0, 0, 0))
    state_shape = jax.ShapeDtypeStruct((nseq, N_HEADS, D_HEAD, D_HEAD), F32)
    init_idx = (lambda g: 0) if shared_init else (lambda g: g)
    init_state_spec = pl.BlockSpec((nb, N_HEADS, D_HEAD, D_HEAD), lambda g, c: (init_idx(g), 0, 0, 0))
    head_group = 8 if nb == 1 else 4

    def col_spec(w, idx):
        return pl.BlockSpec((nb, sl, w), lambda g, c: (g + blk_off, c, idx))

    def out_spec(w):
        return pl.BlockSpec((nb, sl, w), lambda g, c: (g, c, 0))

    def const_spec(shape):
        return pl.BlockSpec(shape, lambda g, c: (0,) * len(shape))

    gdn_in = [proj3, proj3, ba3, conv_w, pvec, gdn_nw]
    gdn_specs = [col_spec(conv_ch, 0), col_spec(width, 3), col_spec(LANES, 0),
                 const_spec(conv_w.shape), const_spec(pvec.shape), const_spec(gdn_nw.shape)]
    if has_init:
        gdn_in += [s_gdn0, conv0]
        gdn_specs += [init_state_spec, pl.BlockSpec((nb, SUBLANES, conv_ch), lambda g, c: (init_idx(g), 0, 0))]
    hgrn_in = [proj3, proj3, proj3, proj3, lb, hgrn_nw]
    hgrn_specs = [col_spec(width, 4), col_spec(width, 5), col_spec(width, 6), col_spec(width, 7),
                  const_spec(lb.shape), const_spec(hgrn_nw.shape)]
    if has_init:
        hgrn_in += [s_hgrn0]
        hgrn_specs += [init_state_spec]

    gdn_body = functools.partial(_gdn_kernel, nb=nb, sl=sl, n_valid=n_valid, has_init=has_init, mode=mode,
                                 inv_mode=inv_mode, head_group=head_group)
    hgrn_body = functools.partial(_hgrn_kernel, nb=nb, sl=sl, n_valid=n_valid, has_init=has_init, mode=mode,
                                  head_group=head_group)
    n_g, n_h = len(gdn_in), len(hgrn_in)

    def both(*refs):
        ins, outs, scr = refs[:n_g + n_h], refs[n_g + n_h:n_g + n_h + 4], refs[n_g + n_h + 4:]
        gdn_body(*ins[:n_g], outs[0], outs[1], scr[0], scr[1])
        hgrn_body(*ins[n_g:], outs[2], outs[3], scr[2])

    state_scr = pltpu.VMEM((nb, N_HEADS, D_HEAD, D_HEAD), F32)
    o_shape = jax.ShapeDtypeStruct((nseq, length, width), out_dtype)
    o_gdn, s_gdn, o_hgrn, s_hgrn = pl.pallas_call(
        both,
        grid=grid,
        in_specs=gdn_specs + hgrn_specs,
        out_specs=[out_spec(width), state_spec, out_spec(width), state_spec],
        out_shape=[o_shape, state_shape, o_shape, state_shape],
        scratch_shapes=[state_scr, pltpu.VMEM((nb, SUBLANES, conv_ch), F32), state_scr],
        compiler_params=_params(("parallel", "arbitrary")),
    )(*gdn_in, *hgrn_in)
    return o_gdn, o_hgrn, s_gdn, s_hgrn


def _outproj_router_kernel(oap_ref, obp_ref, hpp_ref, oas_ref, obs_ref, hps_ref, wo_ref, n2_ref, wr_ref, br_ref,
                           hp2_ref, xloc_ref, ri_ref, rw_ref, m8_ref, *, tm, n_p):
    i = pl.program_id(0)
    body = functools.partial(_outproj_router_tile, wo_ref=wo_ref, n2_ref=n2_ref, wr_ref=wr_ref, br_ref=br_ref,
                             hp2_ref=hp2_ref, xloc_ref=xloc_ref, ri_ref=ri_ref, rw_ref=rw_ref, m8_ref=m8_ref, tm=tm)

    @pl.when(i < n_p)
    def _():
        body(oap_ref[...], obp_ref[...], hpp_ref[...])

    @pl.when(i >= n_p)
    def _():
        body(oas_ref[...], obs_ref[...], hps_ref[...])


def _outproj_router_tile(oa, ob, hp, *, wo_ref, n2_ref, wr_ref, br_ref, hp2_ref, xloc_ref, ri_ref, rw_ref,
                         m8_ref, tm):
    half = oa.shape[-1]
    mix = _dot(oa.astype(BF16), wo_ref[:half, :]) + _dot(ob.astype(BF16), wo_ref[half:, :])
    hp2 = hp + mix
    hp2_ref[...] = hp2
    xn2 = hp2 * lax.rsqrt(jnp.mean(hp2 * hp2, axis=-1, keepdims=True) + EPS) * n2_ref[...]
    logits = _dot(xn2, wr_ref[...], "bf16x3") + br_ref[...]

    lane = lax.broadcasted_iota(jnp.int32, (tm, LANES), 1)
    lane_f = lane.astype(F32)
    far = float(4 * LANES)
    is_g = (lane >= N_EXPERTS) & (lane < N_EXPERTS + N_GROUPS)
    lg = jnp.where(is_g, logits, -jnp.inf)
    gmax = jnp.max(lg, axis=-1, keepdims=True)
    gsel = jnp.min(jnp.where(lg == gmax, lane_f, far), axis=-1, keepdims=True).astype(jnp.int32) - N_EXPERTS
    p_top = 1.0 / jnp.sum(jnp.where(is_g, jnp.exp(logits - gmax), 0.0), axis=-1, keepdims=True)
    in_grp = (lane < N_EXPERTS) & (lax.shift_right_logical(lane, 3) == gsel)
    le = jnp.where(in_grp, logits, -jnp.inf)
    m1 = jnp.max(le, axis=-1, keepdims=True)
    i1 = jnp.min(jnp.where(le == m1, lane_f, far), axis=-1, keepdims=True).astype(jnp.int32)
    le2 = jnp.where(lane == i1, -jnp.inf, le)
    m2 = jnp.max(le2, axis=-1, keepdims=True)
    i2 = jnp.min(jnp.where(le2 == m2, lane_f, far), axis=-1, keepdims=True).astype(jnp.int32)
    e2 = jnp.exp(m2 - m1)
    w1 = p_top / (1.0 + e2)
    w2 = p_top * e2 / (1.0 + e2)

    onehot = (lane == i1) | (lane == i2)
    onehot_f = jnp.where(onehot, 1.0, 0.0)
    tri = (lax.broadcasted_iota(jnp.int32, (tm, tm), 1) < lax.broadcasted_iota(jnp.int32, (tm, tm), 0))
    rank = _dot(jnp.where(tri, 1.0, 0.0).astype(BF16), onehot_f.astype(BF16))
    cnt = jnp.sum(onehot_f, axis=0, keepdims=True)
    m8 = jnp.floor((cnt + (SUBLANES - 1)) * (1.0 / SUBLANES))
    upper = (lax.broadcasted_iota(jnp.int32, (LANES, LANES), 0) < lax.broadcasted_iota(jnp.int32, (LANES, LANES), 1))
    goff = _dot(jnp.broadcast_to(m8, (SUBLANES, LANES)).astype(BF16), jnp.where(upper, 1.0, 0.0).astype(BF16))[0:1]
    local = goff * SUBLANES + rank
    lr1 = jnp.sum(jnp.where(lane == i1, rank, 0.0), axis=-1, keepdims=True)
    lr2 = jnp.sum(jnp.where(lane == i2, rank, 0.0), axis=-1, keepdims=True)
    lp1 = jnp.sum(jnp.where(lane == i1, local, 0.0), axis=-1, keepdims=True)
    lp2 = jnp.sum(jnp.where(lane == i2, local, 0.0), axis=-1, keepdims=True)
    lp_rows = jnp.where(lane == 0, lp1, jnp.where(lane == 1, lp2, -1.0)).T
    n_loc = xloc_ref.shape[0]
    row = lax.broadcasted_iota(jnp.int32, (n_loc, tm), 0).astype(F32)
    perm = (row == lp_rows[0:1, :]) | (row == lp_rows[1:2, :])
    xloc_ref[...] = _dot(jnp.where(perm, 1.0, 0.0).astype(BF16), xn2.astype(BF16))

    ri_ref[...] = jnp.where(lane == 0, i1, jnp.where(lane == 1, i2, jnp.where(
        lane == 2, lr1.astype(jnp.int32), jnp.where(lane == 3, lr2.astype(jnp.int32), 0))))
    rw_ref[...] = jnp.where(lane == 0, w1, jnp.where(lane == 1, w2, 0.0))
    m8_ref[...] = jnp.broadcast_to(m8, m8_ref.shape)


def _outproj_router(oa_p, ob_p, hp_p, oa_s, ob_s, hp_s, w_out, norm2_w, w_r, b_r, tm):
    (tp, d), ts = hp_p.shape, hp_s.shape[0]
    half = oa_p.shape[1]
    n_p, n_s = tp // tm, ts // tm
    t = tp + ts
    n_loc = _local_rows(tm)
    prow = lambda w: pl.BlockSpec((tm, w), lambda i: (jnp.minimum(i, n_p - 1), 0))
    srow = lambda w: pl.BlockSpec((tm, w), lambda i: (jnp.maximum(i - n_p, 0), 0))
    row = lambda w: pl.BlockSpec((tm, w), lambda i: (i, 0))
    const = lambda shape: pl.BlockSpec(shape, lambda i: (0,) * len(shape))
    return pl.pallas_call(
        functools.partial(_outproj_router_kernel, tm=tm, n_p=n_p),
        grid=(n_p + n_s,),
        in_specs=[prow(half), prow(half), prow(d), srow(half), srow(half), srow(d),
                  const(w_out.shape), const((1, d)), const(w_r.shape), const((1, LANES))],
        out_specs=[row(d), pl.BlockSpec((n_loc, d), lambda i: (i, 0)), row(LANES), row(LANES),
                   pl.BlockSpec((None, SUBLANES, LANES), lambda i: (i, 0, 0))],
        out_shape=[jax.ShapeDtypeStruct((t, d), F32), jax.ShapeDtypeStruct(((n_p + n_s) * n_loc, d), F32),
                   jax.ShapeDtypeStruct((t, LANES), jnp.int32), jax.ShapeDtypeStruct((t, LANES), F32),
                   jax.ShapeDtypeStruct((n_p + n_s, SUBLANES, LANES), F32)],
        compiler_params=_params(("parallel",)),
    )(oa_p, ob_p, hp_p, oa_s, ob_s, hp_s, w_out, norm2_w.reshape(1, d), w_r, b_r)


def _local_rows(tm):
    return 2 * tm + SUBLANES * N_EXPERTS


def _start_row_gather(idx_ref, base, n, src_hbm, dst, sem):
    def issue(r, carry):
        pltpu.make_async_copy(src_hbm.at[pl.ds(idx_ref[base + r], 1), :], dst.at[pl.ds(r, 1), :], sem).start()
        return carry
    lax.fori_loop(0, n, issue, 0, unroll=8)


def _start_group_gather(idx_ref, base, n_groups, src_hbm, dst, sem):
    def issue(r, carry):
        src_row = pl.multiple_of(idx_ref[base + r] * SUBLANES, SUBLANES)
        dst_row = pl.multiple_of(r * SUBLANES, SUBLANES)
        pltpu.make_async_copy(src_hbm.at[pl.ds(src_row, SUBLANES), :], dst.at[pl.ds(dst_row, SUBLANES), :], sem).start()
        return carry
    lax.fori_loop(0, n_groups, issue, 0, unroll=8)


def _wait_row_gather(n, src_hbm, dst, sem):
    pltpu.make_async_copy(src_hbm.at[pl.ds(0, n), :], dst, sem).wait()


def _moe_kernel(te_ref, st_ref, nx_ref, nu_ref, x_hbm, wg_hbm, wu_hbm, wd_hbm, o_ref,
                xbuf, wg_f32, wu_f32, wd_f32, wg_scr, wu_scr, wd_scr, sems, wsems, *, tile):
    i = pl.program_id(0)
    n_used = nu_ref[0]
    slot = lax.rem(i, 2)
    weights = ((wg_hbm, wg_f32, wg_scr), (wu_hbm, wu_f32, wu_scr), (wd_hbm, wd_f32, wd_scr))

    def start_weights(e):
        for k, (w_hbm, w_f32, _) in enumerate(weights):
            pltpu.async_copy(w_hbm.at[e], w_f32, wsems.at[k], priority=1)

    @pl.when((i == 0) & (n_used > 0))
    def _():
        start_weights(te_ref[0])
        _start_group_gather(st_ref, 0, tile // SUBLANES, x_hbm, xbuf.at[0], sems.at[0])

    @pl.when(i < n_used)
    def _():
        @pl.when(i + 1 < n_used)
        def _():
            _start_group_gather(st_ref, (i + 1) * (tile // SUBLANES), tile // SUBLANES, x_hbm,
                                xbuf.at[1 - slot], sems.at[1 - slot])

        e = te_ref[i]

        @pl.when((i == 0) | (e != te_ref[jnp.maximum(i - 1, 0)]))
        def _():
            for k, (w_hbm, w_f32, w_scr) in enumerate(weights):
                pltpu.make_async_copy(w_hbm.at[0], w_f32, wsems.at[k]).wait()
                w_scr[...] = w_f32[...].astype(BF16)
            nxt = nx_ref[e]

            @pl.when(nxt < N_EXPERTS)
            def _():
                start_weights(nxt)

        _wait_row_gather(tile, x_hbm, xbuf.at[slot], sems.at[slot])
        x = xbuf[slot].astype(BF16)
        g = _dot(x, wg_scr[...])
        u = _dot(x, wu_scr[...])
        o_ref[...] = _dot((_silu(g) * u).astype(BF16), wd_scr[...])

    @pl.when(i >= n_used)
    def _():
        o_ref[...] = jnp.zeros_like(o_ref)


def _moe(xn2, w_gate, w_up, w_down, tile_expert, slot_token, next_expert, n_used, tile):
    d = xn2.shape[1]
    n_tiles = tile_expert.shape[0]
    de = w_gate.shape[2]
    hbm = pl.BlockSpec(memory_space=pl.ANY)
    grid_spec = pltpu.PrefetchScalarGridSpec(
        num_scalar_prefetch=4,
        grid=(n_tiles,),
        in_specs=[hbm, hbm, hbm, hbm],
        out_specs=pl.BlockSpec((tile, d), lambda i, te, st, nx, nu: (i, 0)),
        scratch_shapes=[pltpu.VMEM((2, tile, d), F32),
                        pltpu.VMEM((d, de), F32), pltpu.VMEM((d, de), F32), pltpu.VMEM((de, d), F32),
                        pltpu.VMEM((d, de), BF16), pltpu.VMEM((d, de), BF16), pltpu.VMEM((de, d), BF16),
                        pltpu.SemaphoreType.DMA((2,)), pltpu.SemaphoreType.DMA((3,))],
    )
    return pl.pallas_call(
        functools.partial(_moe_kernel, tile=tile),
        grid_spec=grid_spec,
        out_shape=jax.ShapeDtypeStruct((n_tiles * tile, d), F32),
        compiler_params=_params(("arbitrary",)),
    )(tile_expert, slot_token, next_expert, n_used, xn2, w_gate, w_up, w_down)


def _combine_kernel(p0_ref, p1_ref, ys_hbm, hp2_ref, rw_ref, fw_ref, o_ref, buf0, buf1, sems, *, tm, tile_off):
    i = pl.program_id(0)
    slot = lax.rem(i, 2)

    def start(step, s):
        base = (step + tile_off) * tm
        _start_row_gather(p0_ref, base, tm, ys_hbm, buf0.at[s], sems.at[0, s])
        _start_row_gather(p1_ref, base, tm, ys_hbm, buf1.at[s], sems.at[1, s])

    @pl.when(i == 0)
    def _():
        start(0, 0)

    @pl.when(i + 1 < pl.num_programs(0))
    def _():
        start(i + 1, 1 - slot)

    _wait_row_gather(tm, ys_hbm, buf0.at[slot], sems.at[0, slot])
    _wait_row_gather(tm, ys_hbm, buf1.at[slot], sems.at[1, slot])
    rw = rw_ref[...]
    y = hp2_ref[...] + rw[:, 0:1] * buf0[slot] + rw[:, 1:2] * buf1[slot]
    o_ref[...] = y * lax.rsqrt(jnp.mean(y * y, axis=-1, keepdims=True) + EPS) * fw_ref[...]


def _combine(ys, hp2, route_w, final_w, pos0, pos1, tm, tile_off, n_tiles):
    _, d = hp2.shape
    grid_spec = pltpu.PrefetchScalarGridSpec(
        num_scalar_prefetch=2,
        grid=(n_tiles,),
        in_specs=[pl.BlockSpec(memory_space=pl.ANY),
                  pl.BlockSpec((tm, d), lambda i, p0, p1: (i + tile_off, 0)),
                  pl.BlockSpec((tm, LANES), lambda i, p0, p1: (i + tile_off, 0)),
                  pl.BlockSpec((1, d), lambda i, p0, p1: (0, 0))],
        out_specs=pl.BlockSpec((tm, d), lambda i, p0, p1: (i, 0)),
        scratch_shapes=[pltpu.VMEM((2, tm, d), F32), pltpu.VMEM((2, tm, d), F32),
                        pltpu.SemaphoreType.DMA((2, 2))],
    )
    return pl.pallas_call(
        functools.partial(_combine_kernel, tm=tm, tile_off=tile_off),
        grid_spec=grid_spec,
        out_shape=jax.ShapeDtypeStruct((n_tiles * tm, d), F32),
        compiler_params=_params(("arbitrary",)),
    )(pos0, pos1, ys, hp2, route_w, final_w.reshape(1, d))


def kernel(x_prompt, x_sample, state_gdn, state_conv, state_hgrn, meta_tokens, norm1_w, w_in, conv_w, a_log,
           dt_bias, gdn_norm_w, lb_logits, hgrn_norm_w, w_out, norm2_w, w_router_group, b_router_group,
           w_router_expert, b_router_expert, w_gate, w_up, w_down, final_norm_w):
    bp, seq, d = x_prompt.shape
    bs, dec_seq, _ = x_sample.shape
    assert w_in.shape[0] == 1, "single-layer trunk"
    width = N_HEADS * D_HEAD
    conv_ch = 3 * width
    tile = 256
    tn = 512
    sl_s = SUBLANES
    nb_s = CHUNK // sl_s
    assert seq % CHUNK == 0 and N_META <= CHUNK and dec_seq <= sl_s and bs % nb_s == 0
    tp, ts, ts_pad = bp * seq, bs * dec_seq, bs * sl_s
    t_small = ts_pad + CHUNK
    tm = _row_tile(math.gcd(tp, ts), 256)
    tm_p = _row_tile(tp, 1024)

    xp = x_prompt.reshape(tp, d)
    x_small = jnp.concatenate([jnp.pad(x_sample, ((0, 0), (0, sl_s - dec_seq), (0, 0))).reshape(ts_pad, d),
                               jnp.zeros((CHUNK - N_META, d), F32), meta_tokens.astype(F32)], axis=0)

    wi = w_in[0]
    n_a = 4 * width // tn
    w_b = wi[:, 4 * width + 2 * N_HEADS:]
    w_ba = jnp.pad(wi[:, 4 * width:4 * width + 2 * N_HEADS], ((0, 0), (0, LANES - 2 * N_HEADS))).astype(BF16)
    n_cols = 8 * width

    xn_p = _rmsnorm(xp, norm1_w[0], BF16, _row_tile(tp, 512))
    xn_s = _rmsnorm(x_small, norm1_w[0], BF16, _row_tile(t_small, 1024))
    w_a = wi[:, :4 * width]
    proj_p = _inproj(xn_p, w_a, n_a, w_b, tm_p, tn)
    proj_s = _inproj(xn_s, w_a, n_a, w_b, t_small, tn)
    ba_p = _matmul(xn_p, w_ba, tm_p, LANES)
    ba_s = _matmul(xn_s, w_ba, t_small, LANES)

    pvec = jnp.zeros((2, LANES), F32)
    pvec = pvec.at[0, N_HEADS:2 * N_HEADS].set(a_log[0]).at[1, N_HEADS:2 * N_HEADS].set(dt_bias[0])
    lb = jnp.cumsum(jax.nn.softmax(lb_logits.astype(F32), axis=0), axis=0)[0].reshape(1, width)
    gdn_nw = gdn_norm_w[0].reshape(1, D_HEAD)
    hgrn_nw = hgrn_norm_w[0].reshape(1, D_HEAD)
    cw = conv_w[0]
    mix_args = (cw, pvec, gdn_nw, lb, hgrn_nw)

    _, _, sg_m, sh_m = _mixers(proj_s.reshape(t_small // CHUNK, CHUNK, n_cols),
                               ba_s.reshape(t_small // CHUNK, CHUNK, LANES), ts_pad // CHUNK, 1, 1, CHUNK, CHUNK,
                               *mix_args, None, None, None, False, F32, MIX_MODE, INV_MODE)
    conv_m = proj_s[t_small - SUBLANES:, :conv_ch].reshape(1, SUBLANES, conv_ch)
    oa_p, ob_p, sg_p, sh_p = _mixers(proj_p.reshape(bp, seq, n_cols), ba_p.reshape(bp, seq, LANES), 0, bp, 1,
                                     CHUNK, CHUNK, *mix_args, sg_m, conv_m, sh_m, True, BF16, MIX_MODE, INV_MODE)
    conv0 = jnp.pad(state_conv[0], ((0, 0), (SUBLANES - (CONV_W - 1), 0), (0, 0)))
    oa_s, ob_s, sg_s, sh_s = _mixers(proj_s.reshape(t_small // sl_s, sl_s, n_cols),
                                     ba_s.reshape(t_small // sl_s, sl_s, LANES), 0, bs, nb_s, sl_s, dec_seq,
                                     *mix_args, state_gdn[0], conv0, state_hgrn[0], False, F32, MIX_MODE, INV_MODE)

    w_r = jnp.concatenate([w_router_expert[0], w_router_group[0],
                           jnp.zeros((d, LANES - N_EXPERTS - N_GROUPS), F32)], axis=1)
    b_r = jnp.concatenate([b_router_expert[0], b_router_group[0],
                           jnp.zeros((LANES - N_EXPERTS - N_GROUPS,), F32)]).reshape(1, LANES)
    t = tp + ts
    hp2, xloc, route_i, route_w, tile_m8 = _outproj_router(
        oa_p.reshape(tp, width), ob_p.reshape(tp, width), xp,
        oa_s[:, :dec_seq].reshape(ts, width), ob_s[:, :dec_seq].reshape(ts, width), x_sample.reshape(ts, d),
        w_out[0].astype(BF16), norm2_w[0], w_r, b_r, tm)

    n_tt = t // tm
    gpt = tile // SUBLANES
    loc_g = _local_rows(tm) // SUBLANES
    expert_ids = jnp.arange(N_EXPERTS, dtype=jnp.int32)
    m8 = tile_m8[:, 0, :N_EXPERTS].astype(jnp.int32)
    before = jnp.cumsum(m8, axis=0) - m8
    goff = jnp.cumsum(m8, axis=1) - m8
    groups = jnp.sum(m8, axis=0)
    padded_g = (groups + gpt - 1) // gpt * gpt
    ends_g = jnp.cumsum(padded_g)
    offs_g = ends_g - padded_g
    eid = route_i[:, 0:2]
    base_rows = SUBLANES * (offs_g[None, :] + before)
    base_tok = jnp.repeat(base_rows, tm, axis=0)
    pos = jnp.sum(jnp.where(eid[:, :, None] == expert_ids, base_tok[:, None, :], 0), axis=-1) + route_i[:, 2:4]

    n_tiles = -(-(2 * t + n_tt * N_EXPERTS * (SUBLANES - 1)) // tile) + N_EXPERTS
    tile_expert = jnp.minimum(
        jnp.sum((ends_g[None, :] <= (jnp.arange(n_tiles, dtype=jnp.int32) * gpt)[:, None]).astype(jnp.int32), axis=1),
        N_EXPERTS - 1)
    n_used = (ends_g[-1] // gpt).astype(jnp.int32).reshape(1)
    q = jnp.arange(n_tiles * gpt, dtype=jnp.int32)
    e_q = jnp.repeat(tile_expert, gpt)
    sel = (e_q[:, None] == expert_ids).astype(jnp.int32)
    u = q - sel @ offs_g
    run_end = sel @ (before + m8).T
    j_q = jnp.sum((run_end <= u[:, None]).astype(jnp.int32), axis=1)
    hit = (jnp.arange(n_tt, dtype=jnp.int32) == j_q[:, None]).astype(jnp.int32)
    src_in_tile = jnp.sum(hit * (sel @ (goff - before).T), axis=1) + u
    slot_group = jnp.where(j_q < n_tt, j_q * loc_g + src_in_tile, 0).astype(jnp.int32)

    later_active = (expert_ids[None, :] > expert_ids[:, None]) & (groups[None, :] > 0)
    next_expert = jnp.min(jnp.where(later_active, expert_ids[None, :], N_EXPERTS), axis=1).astype(jnp.int32)
    ys = _moe(xloc, w_gate[0], w_up[0], w_down[0], tile_expert, slot_group, next_expert, n_used, tile)
    pos0, pos1 = pos[:, 0], pos[:, 1]
    y_prompt = _combine(ys, hp2, route_w, final_norm_w, pos0, pos1, tm, 0, tp // tm).reshape(bp, seq, d)
    y_sample = _combine(ys, hp2, route_w, final_norm_w, pos0, pos1, tm, tp // tm, ts // tm).reshape(bs, dec_seq, d)

    conv_p = proj_p.reshape(bp, seq, n_cols)[:, seq - (CONV_W - 1):, :conv_ch]
    u_s = proj_s.reshape(t_small // sl_s, sl_s, n_cols)[:bs, :dec_seq, :conv_ch]
    conv_s = jnp.concatenate([state_conv[0], u_s], axis=1)[:, dec_seq:]
    return (y_prompt, y_sample, sg_p[None], conv_p[None], sh_p[None], sg_s[None], conv_s[None], sh_s[None])
```

```python
import functools
import math

import jax
import jax.numpy as jnp
from jax import lax
from jax.experimental import pallas as pl
from jax.experimental.pallas import tpu as pltpu

F32 = jnp.float32
BF16 = jnp.bfloat16
HIGHEST = lax.Precision.HIGHEST

EPS = 1e-6
N_META = 16
CONV_W = 4
N_HEADS = 8
D_HEAD = 128
N_GROUPS = 4
EXPERTS_PER_GROUP = 8
N_EXPERTS = N_GROUPS * EXPERTS_PER_GROUP

LANES = 128
SUBLANES = 8
CHUNK = 64
SUB = 16
VMEM_LIMIT = 56 * 1024 * 1024
NEG_BIG = -1e30
MIX_MODE = "bf16"
INV_MODE = "bf16"


def _sigmoid(x):
    return 0.5 * jnp.tanh(0.5 * x) + 0.5


def _silu(x):
    return x * _sigmoid(x)


def _softplus(x):
    return jnp.maximum(x, 0.0) + jnp.log1p(jnp.exp(-jnp.abs(x)))


def _split_bf16(a, pieces):
    out = []
    for _ in range(pieces - 1):
        hi = a.astype(BF16)
        out.append(hi)
        a = a - hi.astype(F32)
    out.append(a.astype(BF16))
    return out


def _mm(a, b, dims, mode):
    dg = functools.partial(lax.dot_general, dimension_numbers=(dims, ((), ())), preferred_element_type=F32)
    if mode == "f32":
        return dg(a, b, precision=HIGHEST)
    if mode == "bf16":
        return dg(a.astype(BF16), b.astype(BF16))
    assert mode == "bf16x3"
    ah, al = _split_bf16(a, 2)
    bh, bl = _split_bf16(b, 2)
    return dg(ah, bh) + dg(ah, bl) + dg(al, bh)


def _dot(a, b, mode="bf16"):
    return _mm(a, b, ((1,), (0,)), mode)


def _dot_nt(a, b, mode="bf16"):
    return _mm(a, b, ((1,), (1,)), mode)


def _dot_tn(a, b, mode="bf16"):
    return _mm(a, b, ((0,), (0,)), mode)


def _masked_cumsum(lmask, x):
    lm = lmask.astype(BF16)
    return sum(lax.dot_general(lm, p, (((1,), (0,)), ((), ())), preferred_element_type=F32)
               for p in _split_bf16(x, 3))


def _params(sem):
    return pltpu.CompilerParams(dimension_semantics=sem, vmem_limit_bytes=VMEM_LIMIT)


def _row_tile(n, target):
    best = max(c for c in range(16, min(n, target) + 1, 16) if n % c == 0)
    return best


def _rmsnorm_kernel(x_ref, w_ref, o_ref):
    x = x_ref[...]
    ms = jnp.mean(x * x, axis=-1, keepdims=True)
    o_ref[...] = (x * lax.rsqrt(ms + EPS) * w_ref[...]).astype(o_ref.dtype)


def _rmsnorm(x, w, out_dtype, tm):
    t, d = x.shape
    return pl.pallas_call(
        _rmsnorm_kernel,
        grid=(t // tm,),
        in_specs=[pl.BlockSpec((tm, d), lambda i: (i, 0)), pl.BlockSpec((1, d), lambda i: (0, 0))],
        out_specs=pl.BlockSpec((tm, d), lambda i: (i, 0)),
        out_shape=jax.ShapeDtypeStruct((t, d), out_dtype),
        compiler_params=_params(("parallel",)),
    )(x, w.reshape(1, d))


def _inproj_kernel(x_ref, wa_ref, wb_ref, o_ref, w_scr, *, n_a):
    j = pl.program_id(0)
    i = pl.program_id(1)

    @pl.when((i == 0) & (j < n_a))
    def _():
        w_scr[...] = wa_ref[...].astype(BF16)

    @pl.when((i == 0) & (j >= n_a))
    def _():
        w_scr[...] = wb_ref[...].astype(BF16)

    o_ref[...] = _dot(x_ref[...], w_scr[...])


def _inproj(x, w_a, n_a, w_b, tm, tn):
    t, k = x.shape
    n_b = w_b.shape[1] // tn
    return pl.pallas_call(
        functools.partial(_inproj_kernel, n_a=n_a),
        grid=(n_a + n_b, t // tm),
        in_specs=[pl.BlockSpec((tm, k), lambda j, i: (i, 0)),
                  pl.BlockSpec((k, tn), lambda j, i: (0, jnp.minimum(j, n_a - 1))),
                  pl.BlockSpec((k, tn), lambda j, i: (0, jnp.maximum(j - n_a, 0)))],
        out_specs=pl.BlockSpec((tm, tn), lambda j, i: (i, j)),
        out_shape=jax.ShapeDtypeStruct((t, (n_a + n_b) * tn), F32),
        scratch_shapes=[pltpu.VMEM((k, tn), BF16)],
        compiler_params=_params(("arbitrary", "arbitrary")),
    )(x, w_a, w_b)


def _matmul_kernel(x_ref, w_ref, o_ref):
    o_ref[...] = _dot(x_ref[...], w_ref[...])


def _matmul(x, w, tm, tn):
    t, k = x.shape
    n = w.shape[1]
    return pl.pallas_call(
        _matmul_kernel,
        grid=(n // tn, t // tm),
        in_specs=[pl.BlockSpec((tm, k), lambda j, i: (i, 0)), pl.BlockSpec((k, tn), lambda j, i: (0, j))],
        out_specs=pl.BlockSpec((tm, tn), lambda j, i: (i, j)),
        out_shape=jax.ShapeDtypeStruct((t, n), F32),
        compiler_params=_params(("parallel", "arbitrary")),
    )(x, w)


def _chunk_masks(nb, sl):
    r = nb * sl
    shift = int(math.log2(sl))
    ri = lax.broadcasted_iota(jnp.int32, (r, r), 0)
    ci = lax.broadcasted_iota(jnp.int32, (r, r), 1)
    same = lax.shift_right_logical(ri, shift) == lax.shift_right_logical(ci, shift)
    return same & (ci <= ri), same & (ci < ri)


def _row_valid(nb, sl, n_valid):
    rowid = lax.broadcasted_iota(jnp.int32, (nb * sl, 1), 0)
    return (rowid & (sl - 1)) < n_valid


def _last_row_bcast(x, nb, sl):
    c = x.shape[-1]
    x3 = x.reshape(nb, sl, c)
    return jnp.broadcast_to(x3[:, sl - 1:sl, :], (nb, sl, c)).reshape(nb * sl, c)


def _gated_rmsnorm(o, w, gate):
    return o * lax.rsqrt(jnp.mean(o * o, axis=-1, keepdims=True) + EPS) * w * _silu(gate)


def _gdn_kernel(*refs, nb, sl, n_valid, has_init, mode, inv_mode, head_group):
    if has_init:
        (qkv_ref, z_ref, ba_ref, cw_ref, pv_ref, nw_ref, s0_ref, c0_ref,
         o_ref, sout_ref, s_scr, carry_scr) = refs
    else:
        (qkv_ref, z_ref, ba_ref, cw_ref, pv_ref, nw_ref,
         o_ref, sout_ref, s_scr, carry_scr) = refs
    c = pl.program_id(1)
    r = nb * sl
    dh = D_HEAD

    @pl.when(c == 0)
    def _():
        if has_init:
            s_scr[...] = s0_ref[...]
            carry_scr[...] = c0_ref[...]
        else:
            s_scr[...] = jnp.zeros_like(s_scr)
            carry_scr[...] = jnp.zeros_like(carry_scr)

    incl, strict = _chunk_masks(nb, sl)
    lmask = jnp.where(incl, 1.0, 0.0)
    offdiag = jnp.where(strict, 1.0, 0.0)
    valid = _row_valid(nb, sl, n_valid)
    masked = n_valid < sl
    rowid = lax.broadcasted_iota(jnp.int32, (r, 1), 0)
    row_seq = lax.shift_right_logical(rowid, int(math.log2(sl)))
    n_sq = int(math.log2(sl)) - 1

    ba = ba_ref[...].reshape(r, LANES)
    pv = pv_ref[...]
    beta_all = _sigmoid(ba)
    g_all = -jnp.exp(pv[0:1]) * _softplus(ba + pv[1:2])
    if masked:
        g_all = jnp.where(valid, g_all, 0.0)
    gcum = _masked_cumsum(lmask, g_all)
    gcum_t = gcum.T
    glast = _last_row_bcast(gcum, nb, sl)
    cw = cw_ref[...]
    nw = nw_ref[...]

    def conv_slice(c0):
        u = qkv_ref[:, :, c0:c0 + dh]
        prev = carry_scr[:, :, c0:c0 + dh]
        full = jnp.concatenate([prev, u], axis=1)
        acc = None
        for j in range(CONV_W):
            off = SUBLANES - (CONV_W - 1) + j
            term = full[:, off:off + sl, :] * cw[j:j + 1, c0:c0 + dh]
            acc = term if acc is None else acc + term
        return _silu(acc).reshape(r, dh)

    for h0 in range(0, N_HEADS, head_group):
        heads = range(h0, h0 + head_group)
        qs, ks, vs, bcs, gcs, gls, egs, decays = [], [], [], [], [], [], [], []
        for h in heads:
            q = conv_slice(h * dh)
            k = conv_slice(N_HEADS * dh + h * dh)
            v = conv_slice(2 * N_HEADS * dh + h * dh)
            q = q * lax.rsqrt(jnp.sum(q * q, axis=-1, keepdims=True) + EPS) * (dh ** -0.5)
            k = k * lax.rsqrt(jnp.sum(k * k, axis=-1, keepdims=True) + EPS)
            if masked:
                q = jnp.where(valid, q, 0.0)
                k = jnp.where(valid, k, 0.0)
                v = jnp.where(valid, v, 0.0)
            gc = gcum[:, N_HEADS + h:N_HEADS + h + 1]
            gr = gcum_t[N_HEADS + h:N_HEADS + h + 1, :]
            qs.append(q)
            ks.append(k)
            vs.append(v)
            bcs.append(beta_all[:, h:h + 1])
            gcs.append(gc)
            gls.append(glast[:, N_HEADS + h:N_HEADS + h + 1])
            egs.append(jnp.exp(gc))
            decays.append(jnp.exp(jnp.where(incl, gc - gr, NEG_BIG)))
        n = len(qs)
        qk_kk = [_dot_nt(jnp.concatenate([qs[i], ks[i]], axis=0), ks[i], mode) for i in range(n)]
        tm1 = [qk_kk[i][r:] * (decays[i] * offdiag) * (-bcs[i]) for i in range(n)]
        pw = list(tm1)
        for _ in range(n_sq):
            pw = [_dot(pw[i], pw[i], inv_mode) for i in range(n)]
            tm1 = [tm1[i] + pw[i] + _dot(tm1[i], pw[i], inv_mode) for i in range(n)]
        rhs = [jnp.concatenate([vs[i] * bcs[i], ks[i] * (bcs[i] * egs[i])], axis=1) for i in range(n)]
        uw = [rhs[i] + _dot(tm1[i], rhs[i], mode) for i in range(n)]
        vnew, ointer = [], []
        for i, h in enumerate(heads):
            u = uw[i][:, :dh]
            w = uw[i][:, dh:]
            qe = qs[i] * egs[i]
            vnew_parts, ointer_parts = [], []
            for b in range(nb):
                rows = slice(b * sl, (b + 1) * sl)
                ws = _dot(jnp.concatenate([w[rows], qe[rows]], axis=0), s_scr[b, h], mode)
                vnew_parts.append(u[rows] - ws[:sl])
                ointer_parts.append(ws[sl:])
            vnew.append(vnew_parts[0] if nb == 1 else jnp.concatenate(vnew_parts, axis=0))
            ointer.append(ointer_parts[0] if nb == 1 else jnp.concatenate(ointer_parts, axis=0))
        for i, h in enumerate(heads):
            attn = qk_kk[i][:r] * decays[i]
            o = ointer[i] + _dot(attn, vnew[i], mode)
            z = z_ref[:, :, h * dh:(h + 1) * dh].reshape(r, dh)
            o_ref[:, :, h * dh:(h + 1) * dh] = _gated_rmsnorm(o, nw, z).reshape(nb, sl, dh).astype(o_ref.dtype)
        for i, h in enumerate(heads):
            ktil = ks[i] * jnp.exp(gls[i] - gcs[i])
            for b in range(nb):
                kt_b = ktil if nb == 1 else jnp.where(row_seq == b, ktil, 0.0)
                gl_b = gls[i][b * sl:b * sl + 1, :]
                s_scr[b, h] = s_scr[b, h] * jnp.exp(gl_b) + _dot_tn(kt_b, vnew[i], mode)

    carry_scr[...] = qkv_ref[:, sl - SUBLANES:sl, :]

    @pl.when(c == pl.num_programs(1) - 1)
    def _():
        sout_ref[...] = s_scr[...]


def _hgrn_kernel(*refs, nb, sl, n_valid, has_init, mode, head_group):
    if has_init:
        (q_ref, f_ref, i_ref, g_ref, lb_ref, nw_ref, s0_ref, o_ref, sout_ref, s_scr) = refs
    else:
        (q_ref, f_ref, i_ref, g_ref, lb_ref, nw_ref, o_ref, sout_ref, s_scr) = refs
    c = pl.program_id(1)
    r = nb * sl
    dh = D_HEAD
    width = N_HEADS * dh

    @pl.when(c == 0)
    def _():
        if has_init:
            s_scr[...] = s0_ref[...]
        else:
            s_scr[...] = jnp.zeros_like(s_scr)

    incl, _ = _chunk_masks(nb, sl)
    lmask = jnp.where(incl, 1.0, 0.0)
    valid = _row_valid(nb, sl, n_valid)
    masked = n_valid < sl
    rowid = lax.broadcasted_iota(jnp.int32, (r, 1), 0)
    row_seq = lax.shift_right_logical(rowid, int(math.log2(sl)))
    sub = min(SUB, sl)
    nblk = r // sub
    sub_shift = int(math.log2(sub))
    ri = lax.broadcasted_iota(jnp.int32, (r, r), 0)
    ci = lax.broadcasted_iota(jnp.int32, (r, r), 1)
    same_blk = lax.shift_right_logical(ri, sub_shift) == lax.shift_right_logical(ci, sub_shift)
    diag_mask = incl & same_blk
    cross_mask = incl & jnp.logical_not(same_blk)

    lb = lb_ref[...]
    f = lb + (1.0 - lb) * _sigmoid(f_ref[...].reshape(r, width))
    lf = jnp.log(f)
    k_all = 1.0 - f
    if masked:
        lf = jnp.where(valid, lf, 0.0)
        k_all = jnp.where(valid, k_all, 0.0)
    bcum = _masked_cumsum(lmask, lf)
    nw = nw_ref[...]

    def head_inputs(h):
        cols = slice(h * dh, (h + 1) * dh)
        q = _silu(q_ref[:, :, cols].reshape(r, dh)) * (dh ** -0.5)
        v = i_ref[:, :, cols].reshape(r, dh)
        if masked:
            v = jnp.where(valid, v, 0.0)
        return q, k_all[:, cols], v, bcum[:, cols]

    def intra_attn(q, k, bh):
        bmid = jnp.broadcast_to(bh.reshape(nblk, sub, dh)[:, sub // 2:sub // 2 + 1, :],
                                (nblk, sub, dh)).reshape(r, dh)
        attn = jnp.where(diag_mask, _dot_nt(q * jnp.exp(bh - bmid), k * jnp.exp(bmid - bh), mode), 0.0)
        if sl > sub:
            parts = [jnp.zeros((sub, r), F32)]
            for blk in range(1, nblk):
                start = blk * sub
                bref = bh[start - 1:start, :]
                qc = q[start:start + sub] * jnp.exp(bh[start:start + sub] - bref)
                kc = k * jnp.exp(jnp.minimum(bref - bh, 0.0))
                parts.append(_dot_nt(qc, kc, mode))
            attn = attn + jnp.where(cross_mask, jnp.concatenate(parts, axis=0), 0.0)
        return attn

    for h0 in range(0, N_HEADS, head_group):
        heads = range(h0, h0 + head_group)
        ins = [head_inputs(h) for h in heads]
        attns = [intra_attn(q, k, bh) for (q, k, v, bh) in ins]
        for i, h in enumerate(heads):
            q, k, v, bh = ins[i]
            qe = q * jnp.exp(bh)
            ointer_parts = []
            for b in range(nb):
                rows = slice(b * sl, (b + 1) * sl)
                ointer_parts.append(_dot(qe[rows], s_scr[b, h], mode))
            ointer = ointer_parts[0] if nb == 1 else jnp.concatenate(ointer_parts, axis=0)
            o = ointer + _dot(attns[i], v, mode)
            cols = slice(h * dh, (h + 1) * dh)
            gate = g_ref[:, :, cols].reshape(r, dh)
            o_ref[:, :, cols] = _gated_rmsnorm(o, nw, gate).reshape(nb, sl, dh).astype(o_ref.dtype)
        for i, h in enumerate(heads):
            q, k, v, bh = ins[i]
            blast = _last_row_bcast(bh, nb, sl)
            ktil = k * jnp.exp(blast - bh)
            pad = [jnp.zeros((LANES - r, dh), F32)] if r < LANES else []
            tr = jnp.concatenate([blast] + pad, axis=0).T
            for b in range(nb):
                kt_b = ktil if nb == 1 else jnp.where(row_seq == b, ktil, 0.0)
                dec_col = jnp.exp(tr[:, b * sl:b * sl + 1])
                s_scr[b, h] = s_scr[b, h] * dec_col + _dot_tn(kt_b, v, mode)

    @pl.when(c == pl.num_programs(1) - 1)
    def _():
        sout_ref[...] = s_scr[...]


def _mixers(proj3, ba3, blk_off, nseq, nb, sl, n_valid, conv_w, pvec, gdn_nw, lb, hgrn_nw,
            s_gdn0, conv0, s_hgrn0, shared_init, out_dtype, mode, inv_mode):
    length = proj3.shape[1]
    has_init = s_gdn0 is not None
    width = N_HEADS * D_HEAD
    conv_ch = 3 * width
    grid = (nseq // nb, length // sl)
    state_spec = pl.BlockSpec((nb, N_HEADS, D_HEAD, D_HEAD), lambda g, c: (g, 0, 0, 0))
    state_shape = jax.ShapeDtypeStruct((nseq, N_HEADS, D_HEAD, D_HEAD), F32)
    init_idx = (lambda g: 0) if shared_init else (lambda g: g)
    init_state_spec = pl.BlockSpec((nb, N_HEADS, D_HEAD, D_HEAD), lambda g, c: (init_idx(g), 0, 0, 0))
    head_group = 8 if nb == 1 else 4

    def col_spec(w, idx):
        return pl.BlockSpec((nb, sl, w), lambda g, c: (g + blk_off, c, idx))

    def out_spec(w):
        return pl.BlockSpec((nb, sl, w), lambda g, c: (g, c, 0))

    def const_spec(shape):
        return pl.BlockSpec(shape, lambda g, c: (0,) * len(shape))

    gdn_in = [proj3, proj3, ba3, conv_w, pvec, gdn_nw]
    gdn_specs = [col_spec(conv_ch, 0), col_spec(width, 3), col_spec(LANES, 0),
                 const_spec(conv_w.shape), const_spec(pvec.shape), const_spec(gdn_nw.shape)]
    if has_init:
        gdn_in += [s_gdn0, conv0]
        gdn_specs += [init_state_spec, pl.BlockSpec((nb, SUBLANES, conv_ch), lambda g, c: (init_idx(g), 0, 0))]
    hgrn_in = [proj3, proj3, proj3, proj3, lb, hgrn_nw]
    hgrn_specs = [col_spec(width, 4), col_spec(width, 5), col_spec(width, 6), col_spec(width, 7),
                  const_spec(lb.shape), const_spec(hgrn_nw.shape)]
    if has_init:
        hgrn_in += [s_hgrn0]
        hgrn_specs += [init_state_spec]

    gdn_body = functools.partial(_gdn_kernel, nb=nb, sl=sl, n_valid=n_valid, has_init=has_init, mode=mode,
                                 inv_mode=inv_mode, head_group=head_group)
    hgrn_body = functools.partial(_hgrn_kernel, nb=nb, sl=sl, n_valid=n_valid, has_init=has_init, mode=mode,
                                  head_group=head_group)
    n_g, n_h = len(gdn_in), len(hgrn_in)

    def both(*refs):
        ins, outs, scr = refs[:n_g + n_h], refs[n_g + n_h:n_g + n_h + 4], refs[n_g + n_h + 4:]
        gdn_body(*ins[:n_g], outs[0], outs[1], scr[0], scr[1])
        hgrn_body(*ins[n_g:], outs[2], outs[3], scr[2])

    state_scr = pltpu.VMEM((nb, N_HEADS, D_HEAD, D_HEAD), F32)
    o_shape = jax.ShapeDtypeStruct((nseq, length, width), out_dtype)
    o_gdn, s_gdn, o_hgrn, s_hgrn = pl.pallas_call(
        both,
        grid=grid,
        in_specs=gdn_specs + hgrn_specs,
        out_specs=[out_spec(width), state_spec, out_spec(width), state_spec],
        out_shape=[o_shape, state_shape, o_shape, state_shape],
        scratch_shapes=[state_scr, pltpu.VMEM((nb, SUBLANES, conv_ch), F32), state_scr],
        compiler_params=_params(("parallel", "arbitrary")),
    )(*gdn_in, *hgrn_in)
    return o_gdn, o_hgrn, s_gdn, s_hgrn


def _outproj_router_kernel(oap_ref, obp_ref, hpp_ref, oas_ref, obs_ref, hps_ref, wo_ref, n2_ref, wr_ref, br_ref,
                           hp2_ref, xloc_ref, ri_ref, rw_ref, m8_ref, *, tm, n_p):
    i = pl.program_id(0)
    body = functools.partial(_outproj_router_tile, wo_ref=wo_ref, n2_ref=n2_ref, wr_ref=wr_ref, br_ref=br_ref,
                             hp2_ref=hp2_ref, xloc_ref=xloc_ref, ri_ref=ri_ref, rw_ref=rw_ref, m8_ref=m8_ref, tm=tm)

    @pl.when(i < n_p)
    def _():
        body(oap_ref[...], obp_ref[...], hpp_ref[...])

    @pl.when(i >= n_p)
    def _():
        body(oas_ref[...], obs_ref[...], hps_ref[...])


def _outproj_router_tile(oa, ob, hp, *, wo_ref, n2_ref, wr_ref, br_ref, hp2_ref, xloc_ref, ri_ref, rw_ref,
                         m8_ref, tm):
    half = oa.shape[-1]
    mix = _dot(oa.astype(BF16), wo_ref[:half, :]) + _dot(ob.astype(BF16), wo_ref[half:, :])
    hp2 = hp + mix
    hp2_ref[...] = hp2
    xn2 = hp2 * lax.rsqrt(jnp.mean(hp2 * hp2, axis=-1, keepdims=True) + EPS) * n2_ref[...]
    logits = _dot(xn2, wr_ref[...], "bf16x3") + br_ref[...]

    lane = lax.broadcasted_iota(jnp.int32, (tm, LANES), 1)
    lane_f = lane.astype(F32)
    far = float(4 * LANES)
    is_g = (lane >= N_EXPERTS) & (lane < N_EXPERTS + N_GROUPS)
    lg = jnp.where(is_g, logits, -jnp.inf)
    gmax = jnp.max(lg, axis=-1, keepdims=True)
    gsel = jnp.min(jnp.where(lg == gmax, lane_f, far), axis=-1, keepdims=True).astype(jnp.int32) - N_EXPERTS
    p_top = 1.0 / jnp.sum(jnp.where(is_g, jnp.exp(logits - gmax), 0.0), axis=-1, keepdims=True)
    in_grp = (lane < N_EXPERTS) & (lax.shift_right_logical(lane, 3) == gsel)
    le = jnp.where(in_grp, logits, -jnp.inf)
    m1 = jnp.max(le, axis=-1, keepdims=True)
    i1 = jnp.min(jnp.where(le == m1, lane_f, far), axis=-1, keepdims=True).astype(jnp.int32)
    le2 = jnp.where(lane == i1, -jnp.inf, le)
    m2 = jnp.max(le2, axis=-1, keepdims=True)
    i2 = jnp.min(jnp.where(le2 == m2, lane_f, far), axis=-1, keepdims=True).astype(jnp.int32)
    e2 = jnp.exp(m2 - m1)
    w1 = p_top / (1.0 + e2)
    w2 = p_top * e2 / (1.0 + e2)

    onehot = (lane == i1) | (lane == i2)
    onehot_f = jnp.where(onehot, 1.0, 0.0)
    tri = (lax.broadcasted_iota(jnp.int32, (tm, tm), 1) < lax.broadcasted_iota(jnp.int32, (tm, tm), 0))
    rank = _dot(jnp.where(tri, 1.0, 0.0).astype(BF16), onehot_f.astype(BF16))
    cnt = jnp.sum(onehot_f, axis=0, keepdims=True)
    m8 = jnp.floor((cnt + (SUBLANES - 1)) * (1.0 / SUBLANES))
    upper = (lax.broadcasted_iota(jnp.int32, (LANES, LANES), 0) < lax.broadcasted_iota(jnp.int32, (LANES, LANES), 1))
    goff = _dot(jnp.broadcast_to(m8, (SUBLANES, LANES)).astype(BF16), jnp.where(upper, 1.0, 0.0).astype(BF16))[0:1]
    local = goff * SUBLANES + rank
    lr1 = jnp.sum(jnp.where(lane == i1, rank, 0.0), axis=-1, keepdims=True)
    lr2 = jnp.sum(jnp.where(lane == i2, rank, 0.0), axis=-1, keepdims=True)
    lp1 = jnp.sum(jnp.where(lane == i1, local, 0.0), axis=-1, keepdims=True)
    lp2 = jnp.sum(jnp.where(lane == i2, local, 0.0), axis=-1, keepdims=True)
    lp_rows = jnp.where(lane == 0, lp1, jnp.where(lane == 1, lp2, -1.0)).T
    n_loc = xloc_ref.shape[0]
    row = lax.broadcasted_iota(jnp.int32, (n_loc, tm), 0).astype(F32)
    perm = (row == lp_rows[0:1, :]) | (row == lp_rows[1:2, :])
    xloc_ref[...] = _dot(jnp.where(perm, 1.0, 0.0).astype(BF16), xn2.astype(BF16))

    ints = (i1, i2, lr1.astype(jnp.int32), lr2.astype(jnp.int32), lp1.astype(jnp.int32), lp2.astype(jnp.int32))
    packed = jnp.zeros((tm, LANES), jnp.int32)
    for k, v in enumerate(ints):
        packed = jnp.where(lane == k, v, packed)
    ri_ref[...] = packed
    rw_ref[...] = jnp.where(lane == 0, w1, jnp.where(lane == 1, w2, 0.0))
    m8_ref[...] = jnp.broadcast_to(m8, m8_ref.shape)


def _outproj_router(oa_p, ob_p, hp_p, oa_s, ob_s, hp_s, w_out, norm2_w, w_r, b_r, tm):
    (tp, d), ts = hp_p.shape, hp_s.shape[0]
    half = oa_p.shape[1]
    n_p, n_s = tp // tm, ts // tm
    t = tp + ts
    n_loc = _local_rows(tm)
    prow = lambda w: pl.BlockSpec((tm, w), lambda i: (jnp.minimum(i, n_p - 1), 0))
    srow = lambda w: pl.BlockSpec((tm, w), lambda i: (jnp.maximum(i - n_p, 0), 0))
    row = lambda w: pl.BlockSpec((tm, w), lambda i: (i, 0))
    const = lambda shape: pl.BlockSpec(shape, lambda i: (0,) * len(shape))
    return pl.pallas_call(
        functools.partial(_outproj_router_kernel, tm=tm, n_p=n_p),
        grid=(n_p + n_s,),
        in_specs=[prow(half), prow(half), prow(d), srow(half), srow(half), srow(d),
                  const(w_out.shape), const((1, d)), const(w_r.shape), const((1, LANES))],
        out_specs=[row(d), pl.BlockSpec((n_loc, d), lambda i: (i, 0)), row(LANES), row(LANES),
                   pl.BlockSpec((None, SUBLANES, LANES), lambda i: (i, 0, 0))],
        out_shape=[jax.ShapeDtypeStruct((t, d), F32), jax.ShapeDtypeStruct(((n_p + n_s) * n_loc, d), F32),
                   jax.ShapeDtypeStruct((t, LANES), jnp.int32), jax.ShapeDtypeStruct((t, LANES), F32),
                   jax.ShapeDtypeStruct((n_p + n_s, SUBLANES, LANES), F32)],
        compiler_params=_params(("parallel",)),
    )(oa_p, ob_p, hp_p, oa_s, ob_s, hp_s, w_out, norm2_w.reshape(1, d), w_r, b_r)


def _local_rows(tm):
    return 2 * tm + SUBLANES * N_EXPERTS


def _start_row_gather(idx_ref, base, n, src_hbm, dst, sem):
    def issue(r, carry):
        pltpu.make_async_copy(src_hbm.at[pl.ds(idx_ref[base + r], 1), :], dst.at[pl.ds(r, 1), :], sem).start()
        return carry
    lax.fori_loop(0, n, issue, 0, unroll=8)


def _start_group_gather(idx_ref, base, n_groups, src_hbm, dst, sem):
    def issue(r, carry):
        src_row = pl.multiple_of(idx_ref[base + r] * SUBLANES, SUBLANES)
        dst_row = pl.multiple_of(r * SUBLANES, SUBLANES)
        pltpu.make_async_copy(src_hbm.at[pl.ds(src_row, SUBLANES), :], dst.at[pl.ds(dst_row, SUBLANES), :], sem).start()
        return carry
    lax.fori_loop(0, n_groups, issue, 0, unroll=8)


def _wait_row_gather(n, src_hbm, dst, sem):
    pltpu.make_async_copy(src_hbm.at[pl.ds(0, n), :], dst, sem).wait()


def _moe_kernel(te_ref, st_ref, nx_ref, nu_ref, x_hbm, wg_hbm, wu_hbm, wd_hbm, o_ref,
                xbuf, wg_f32, wu_f32, wd_f32, wg_scr, wu_scr, wd_scr, sems, wsems, *, tile):
    i = pl.program_id(0)
    n_used = nu_ref[0]
    slot = lax.rem(i, 2)
    weights = ((wg_hbm, wg_f32, wg_scr), (wu_hbm, wu_f32, wu_scr), (wd_hbm, wd_f32, wd_scr))

    def start_weights(e):
        for k, (w_hbm, w_f32, _) in enumerate(weights):
            pltpu.async_copy(w_hbm.at[e], w_f32, wsems.at[k], priority=1)

    @pl.when((i == 0) & (n_used > 0))
    def _():
        start_weights(te_ref[0])
        _start_group_gather(st_ref, 0, tile // SUBLANES, x_hbm, xbuf.at[0], sems.at[0])

    @pl.when(i < n_used)
    def _():
        @pl.when(i + 1 < n_used)
        def _():
            _start_group_gather(st_ref, (i + 1) * (tile // SUBLANES), tile // SUBLANES, x_hbm,
                                xbuf.at[1 - slot], sems.at[1 - slot])

        e = te_ref[i]

        @pl.when((i == 0) | (e != te_ref[jnp.maximum(i - 1, 0)]))
        def _():
            for k, (w_hbm, w_f32, w_scr) in enumerate(weights):
                pltpu.make_async_copy(w_hbm.at[0], w_f32, wsems.at[k]).wait()
                w_scr[...] = w_f32[...].astype(BF16)
            nxt = nx_ref[e]

            @pl.when(nxt < N_EXPERTS)
            def _():
                start_weights(nxt)

        _wait_row_gather(tile, x_hbm, xbuf.at[slot], sems.at[slot])
        x = xbuf[slot].astype(BF16)
        g = _dot(x, wg_scr[...])
        u = _dot(x, wu_scr[...])
        o_ref[...] = _dot((_silu(g) * u).astype(BF16), wd_scr[...])

    @pl.when(i >= n_used)
    def _():
        o_ref[...] = jnp.zeros_like(o_ref)


def _moe(xn2, w_gate, w_up, w_down, tile_expert, slot_token, next_expert, n_used, tile):
    d = xn2.shape[1]
    n_tiles = tile_expert.shape[0]
    de = w_gate.shape[2]
    hbm = pl.BlockSpec(memory_space=pl.ANY)
    grid_spec = pltpu.PrefetchScalarGridSpec(
        num_scalar_prefetch=4,
        grid=(n_tiles,),
        in_specs=[hbm, hbm, hbm, hbm],
        out_specs=pl.BlockSpec((tile, d), lambda i, te, st, nx, nu: (i, 0)),
        scratch_shapes=[pltpu.VMEM((2, tile, d), F32),
                        pltpu.VMEM((d, de), F32), pltpu.VMEM((d, de), F32), pltpu.VMEM((de, d), F32),
                        pltpu.VMEM((d, de), BF16), pltpu.VMEM((d, de), BF16), pltpu.VMEM((de, d), BF16),
                        pltpu.SemaphoreType.DMA((2,)), pltpu.SemaphoreType.DMA((3,))],
    )
    return pl.pallas_call(
        functools.partial(_moe_kernel, tile=tile),
        grid_spec=grid_spec,
        out_shape=jax.ShapeDtypeStruct((n_tiles * tile, d), F32),
        compiler_params=_params(("arbitrary",)),
    )(tile_expert, slot_token, next_expert, n_used, xn2, w_gate, w_up, w_down)


def _combine_kernel(yg_ref, ys_hbm, hp2_ref, ri_ref, rw_ref, fw_ref, o_ref, ybuf, sems, *, tm, tile_off, n_loc):
    i = pl.program_id(0)
    slot = lax.rem(i, 2)
    n_grp = n_loc // SUBLANES

    def start(step, s):
        _start_group_gather(yg_ref, (step + tile_off) * n_grp, n_grp, ys_hbm, ybuf.at[s], sems.at[s])

    @pl.when(i == 0)
    def _():
        start(0, 0)

    @pl.when(i + 1 < pl.num_programs(0))
    def _():
        start(i + 1, 1 - slot)

    _wait_row_gather(n_loc, ys_hbm, ybuf.at[slot], sems.at[slot])
    ri = ri_ref[...]
    rw = rw_ref[...]
    loc = lax.broadcasted_iota(jnp.int32, (tm, n_loc), 1)
    wmat = jnp.where(loc == ri[:, 4:5], rw[:, 0:1], 0.0) + jnp.where(loc == ri[:, 5:6], rw[:, 1:2], 0.0)
    w_hi, w_lo = _split_bf16(wmat, 2)
    yb = ybuf[slot].astype(BF16)
    y = hp2_ref[...] + _dot(w_hi, yb) + _dot(w_lo, yb)
    o_ref[...] = y * lax.rsqrt(jnp.mean(y * y, axis=-1, keepdims=True) + EPS) * fw_ref[...]


def _combine(ys, hp2, route_i, route_w, final_w, ys_group, tm, tile_off, n_tiles):
    _, d = hp2.shape
    n_loc = _local_rows(tm)
    row = lambda w: pl.BlockSpec((tm, w), lambda i, yg: (i + tile_off, 0))
    grid_spec = pltpu.PrefetchScalarGridSpec(
        num_scalar_prefetch=1,
        grid=(n_tiles,),
        in_specs=[pl.BlockSpec(memory_space=pl.ANY), row(d), row(LANES), row(LANES),
                  pl.BlockSpec((1, d), lambda i, yg: (0, 0))],
        out_specs=pl.BlockSpec((tm, d), lambda i, yg: (i, 0)),
        scratch_shapes=[pltpu.VMEM((2, n_loc, d), F32), pltpu.SemaphoreType.DMA((2,))],
    )
    return pl.pallas_call(
        functools.partial(_combine_kernel, tm=tm, tile_off=tile_off, n_loc=n_loc),
        grid_spec=grid_spec,
        out_shape=jax.ShapeDtypeStruct((n_tiles * tm, d), F32),
        compiler_params=_params(("arbitrary",)),
    )(ys_group, ys, hp2, route_i, route_w, final_w.reshape(1, d))


def kernel(x_prompt, x_sample, state_gdn, state_conv, state_hgrn, meta_tokens, norm1_w, w_in, conv_w, a_log,
           dt_bias, gdn_norm_w, lb_logits, hgrn_norm_w, w_out, norm2_w, w_router_group, b_router_group,
           w_router_expert, b_router_expert, w_gate, w_up, w_down, final_norm_w):
    bp, seq, d = x_prompt.shape
    bs, dec_seq, _ = x_sample.shape
    assert w_in.shape[0] == 1, "single-layer trunk"
    width = N_HEADS * D_HEAD
    conv_ch = 3 * width
    tile = 256
    tn = 512
    sl_s = SUBLANES
    nb_s = CHUNK // sl_s
    assert seq % CHUNK == 0 and N_META <= CHUNK and dec_seq <= sl_s and bs % nb_s == 0
    tp, ts, ts_pad = bp * seq, bs * dec_seq, bs * sl_s
    t_small = ts_pad + CHUNK
    tm = _row_tile(math.gcd(tp, ts), 256)
    tm_p = _row_tile(tp, 1024)

    xp = x_prompt.reshape(tp, d)
    x_small = jnp.concatenate([jnp.pad(x_sample, ((0, 0), (0, sl_s - dec_seq), (0, 0))).reshape(ts_pad, d),
                               jnp.zeros((CHUNK - N_META, d), F32), meta_tokens.astype(F32)], axis=0)

    wi = w_in[0]
    n_a = 4 * width // tn
    w_b = wi[:, 4 * width + 2 * N_HEADS:]
    w_ba = jnp.pad(wi[:, 4 * width:4 * width + 2 * N_HEADS], ((0, 0), (0, LANES - 2 * N_HEADS))).astype(BF16)
    n_cols = 8 * width

    xn_p = _rmsnorm(xp, norm1_w[0], BF16, _row_tile(tp, 512))
    xn_s = _rmsnorm(x_small, norm1_w[0], BF16, _row_tile(t_small, 1024))
    w_a = wi[:, :4 * width]
    proj_p = _inproj(xn_p, w_a, n_a, w_b, tm_p, tn)
    proj_s = _inproj(xn_s, w_a, n_a, w_b, t_small, tn)
    ba_p = _matmul(xn_p, w_ba, tm_p, LANES)
    ba_s = _matmul(xn_s, w_ba, t_small, LANES)

    pvec = jnp.zeros((2, LANES), F32)
    pvec = pvec.at[0, N_HEADS:2 * N_HEADS].set(a_log[0]).at[1, N_HEADS:2 * N_HEADS].set(dt_bias[0])
    lb = jnp.cumsum(jax.nn.softmax(lb_logits.astype(F32), axis=0), axis=0)[0].reshape(1, width)
    gdn_nw = gdn_norm_w[0].reshape(1, D_HEAD)
    hgrn_nw = hgrn_norm_w[0].reshape(1, D_HEAD)
    cw = conv_w[0]
    mix_args = (cw, pvec, gdn_nw, lb, hgrn_nw)

    _, _, sg_m, sh_m = _mixers(proj_s.reshape(t_small // CHUNK, CHUNK, n_cols),
                               ba_s.reshape(t_small // CHUNK, CHUNK, LANES), ts_pad // CHUNK, 1, 1, CHUNK, CHUNK,
                               *mix_args, None, None, None, False, F32, MIX_MODE, INV_MODE)
    conv_m = proj_s[t_small - SUBLANES:, :conv_ch].reshape(1, SUBLANES, conv_ch)
    oa_p, ob_p, sg_p, sh_p = _mixers(proj_p.reshape(bp, seq, n_cols), ba_p.reshape(bp, seq, LANES), 0, bp, 1,
                                     CHUNK, CHUNK, *mix_args, sg_m, conv_m, sh_m, True, BF16, MIX_MODE, INV_MODE)
    conv0 = jnp.pad(state_conv[0], ((0, 0), (SUBLANES - (CONV_W - 1), 0), (0, 0)))
    oa_s, ob_s, sg_s, sh_s = _mixers(proj_s.reshape(t_small // sl_s, sl_s, n_cols),
                                     ba_s.reshape(t_small // sl_s, sl_s, LANES), 0, bs, nb_s, sl_s, dec_seq,
                                     *mix_args, state_gdn[0], conv0, state_hgrn[0], False, F32, MIX_MODE, INV_MODE)

    w_r = jnp.concatenate([w_router_expert[0], w_router_group[0],
                           jnp.zeros((d, LANES - N_EXPERTS - N_GROUPS), F32)], axis=1)
    b_r = jnp.concatenate([b_router_expert[0], b_router_group[0],
                           jnp.zeros((LANES - N_EXPERTS - N_GROUPS,), F32)]).reshape(1, LANES)
    t = tp + ts
    hp2, xloc, route_i, route_w, tile_m8 = _outproj_router(
        oa_p.reshape(tp, width), ob_p.reshape(tp, width), xp,
        oa_s[:, :dec_seq].reshape(ts, width), ob_s[:, :dec_seq].reshape(ts, width), x_sample.reshape(ts, d),
        w_out[0].astype(BF16), norm2_w[0], w_r, b_r, tm)

    n_tt = t // tm
    gpt = tile // SUBLANES
    loc_g = _local_rows(tm) // SUBLANES
    expert_ids = jnp.arange(N_EXPERTS, dtype=jnp.int32)
    m8 = tile_m8[:, 0, :N_EXPERTS].astype(jnp.int32)
    before = jnp.cumsum(m8, axis=0) - m8
    goff = jnp.cumsum(m8, axis=1) - m8
    groups = jnp.sum(m8, axis=0)
    padded_g = (groups + gpt - 1) // gpt * gpt
    ends_g = jnp.cumsum(padded_g)
    offs_g = ends_g - padded_g
    n_tiles = -(-(2 * t + n_tt * N_EXPERTS * (SUBLANES - 1)) // tile) + N_EXPERTS
    tile_expert = jnp.minimum(
        jnp.sum((ends_g[None, :] <= (jnp.arange(n_tiles, dtype=jnp.int32) * gpt)[:, None]).astype(jnp.int32), axis=1),
        N_EXPERTS - 1)
    n_used = (ends_g[-1] // gpt).astype(jnp.int32).reshape(1)
    q = jnp.arange(n_tiles * gpt, dtype=jnp.int32)
    e_q = jnp.repeat(tile_expert, gpt)
    sel = (e_q[:, None] == expert_ids).astype(jnp.int32)
    u = q - sel @ offs_g
    run_end = sel @ (before + m8).T
    j_q = jnp.sum((run_end <= u[:, None]).astype(jnp.int32), axis=1)
    hit = (jnp.arange(n_tt, dtype=jnp.int32) == j_q[:, None]).astype(jnp.int32)
    src_in_tile = jnp.sum(hit * (sel @ (goff - before).T), axis=1) + u
    slot_group = jnp.where(j_q < n_tt, j_q * loc_g + src_in_tile, 0).astype(jnp.int32)

    later_active = (expert_ids[None, :] > expert_ids[:, None]) & (groups[None, :] > 0)
    next_expert = jnp.min(jnp.where(later_active, expert_ids[None, :], N_EXPERTS), axis=1).astype(jnp.int32)
    ys = _moe(xloc, w_gate[0], w_up[0], w_down[0], tile_expert, slot_group, next_expert, n_used, tile)
    lg = jnp.arange(loc_g, dtype=jnp.int32)
    e_l = jnp.sum(((goff + m8)[:, None, :] <= lg[None, :, None]).astype(jnp.int32), axis=-1)
    pick = (e_l[:, :, None] == expert_ids).astype(jnp.int32)
    ys_group = (jnp.sum(pick * (offs_g[None, :] + before - goff)[:, None, :], axis=-1)
                + jnp.where(e_l < N_EXPERTS, lg[None, :], 0)).astype(jnp.int32).reshape(-1)
    y_prompt = _combine(ys, hp2, route_i, route_w, final_norm_w, ys_group, tm, 0, tp // tm).reshape(bp, seq, d)
    y_sample = _combine(ys, hp2, route_i, route_w, final_norm_w, ys_group, tm, tp // tm, ts // tm
                        ).reshape(bs, dec_seq, d)

    conv_p = proj_p.reshape(bp, seq, n_cols)[:, seq - (CONV_W - 1):, :conv_ch]
    u_s = proj_s.reshape(t_small // sl_s, sl_s, n_cols)[:bs, :dec_seq, :conv_ch]
    conv_s = jnp.concatenate([state_conv[0], u_s], axis=1)[:, dec_seq:]
    return (y_prompt, y_sample, sg_p[None], conv_p[None], sh_p[None], sg_s[None], conv_s[None], sh_s[None])
```

```python
import functools
import math

import jax
import jax.numpy as jnp
from jax import lax
from jax.experimental import pallas as pl
from jax.experimental.pallas import tpu as pltpu

F32 = jnp.float32
BF16 = jnp.bfloat16

EPS = 1e-6
N_META = 16
CONV_W = 4
N_HEADS = 8
D_HEAD = 128
N_GROUPS = 4
EXPERTS_PER_GROUP = 8
N_EXPERTS = N_GROUPS * EXPERTS_PER_GROUP

LANES = 128
SUBLANES = 8
CHUNK = 64
SUB = 16
VMEM_LIMIT = 56 * 1024 * 1024
NEG_BIG = -1e30
MIX_MODE = "bf16"
INV_MODE = "bf16"


def _sigmoid(x):
    return 0.5 * jnp.tanh(0.5 * x) + 0.5


def _silu(x):
    return x * _sigmoid(x)


def _softplus(x):
    return jnp.maximum(x, 0.0) + jnp.log1p(jnp.exp(-jnp.abs(x)))


def _split_bf16(a, pieces):
    out = []
    for _ in range(pieces - 1):
        hi = a.astype(BF16)
        out.append(hi)
        a = a - hi.astype(F32)
    out.append(a.astype(BF16))
    return out


def _mm(a, b, dims, mode):
    dg = functools.partial(lax.dot_general, dimension_numbers=(dims, ((), ())), preferred_element_type=F32)
    if mode == "bf16":
        return dg(a.astype(BF16), b.astype(BF16))
    assert mode == "bf16x3"
    ah, al = _split_bf16(a, 2)
    bh, bl = _split_bf16(b, 2)
    return dg(ah, bh) + dg(ah, bl) + dg(al, bh)


def _dot(a, b, mode="bf16"):
    return _mm(a, b, ((1,), (0,)), mode)


def _dot_nt(a, b, mode="bf16"):
    return _mm(a, b, ((1,), (1,)), mode)


def _dot_tn(a, b, mode="bf16"):
    return _mm(a, b, ((0,), (0,)), mode)


def _masked_cumsum(lmask, x):
    lm = lmask.astype(BF16)
    return sum(lax.dot_general(lm, p, (((1,), (0,)), ((), ())), preferred_element_type=F32)
               for p in _split_bf16(x, 3))


def _params(sem):
    return pltpu.CompilerParams(dimension_semantics=sem, vmem_limit_bytes=VMEM_LIMIT)


def _row_tile(n, target):
    best = max(c for c in range(16, min(n, target) + 1, 16) if n % c == 0)
    return best


def _rmsnorm_kernel(x_ref, w_ref, o_ref):
    x = x_ref[...]
    ms = jnp.mean(x * x, axis=-1, keepdims=True)
    o_ref[...] = (x * lax.rsqrt(ms + EPS) * w_ref[...]).astype(o_ref.dtype)


def _rmsnorm(x, w, out_dtype, tm):
    t, d = x.shape
    return pl.pallas_call(
        _rmsnorm_kernel,
        grid=(t // tm,),
        in_specs=[pl.BlockSpec((tm, d), lambda i: (i, 0)), pl.BlockSpec((1, d), lambda i: (0, 0))],
        out_specs=pl.BlockSpec((tm, d), lambda i: (i, 0)),
        out_shape=jax.ShapeDtypeStruct((t, d), out_dtype),
        compiler_params=_params(("parallel",)),
    )(x, w.reshape(1, d))


def _inproj_kernel(x_ref, wa_ref, wb_ref, o_ref, w_scr, *, n_a):
    j = pl.program_id(0)
    i = pl.program_id(1)

    @pl.when((i == 0) & (j < n_a))
    def _():
        w_scr[...] = wa_ref[...].astype(BF16)

    @pl.when((i == 0) & (j >= n_a))
    def _():
        w_scr[...] = wb_ref[...].astype(BF16)

    o_ref[...] = _dot(x_ref[...], w_scr[...])


def _inproj(x, w_a, n_a, w_b, tm, tn):
    t, k = x.shape
    n_b = w_b.shape[1] // tn
    return pl.pallas_call(
        functools.partial(_inproj_kernel, n_a=n_a),
        grid=(n_a + n_b, t // tm),
        in_specs=[pl.BlockSpec((tm, k), lambda j, i: (i, 0)),
                  pl.BlockSpec((k, tn), lambda j, i: (0, jnp.minimum(j, n_a - 1))),
                  pl.BlockSpec((k, tn), lambda j, i: (0, jnp.maximum(j - n_a, 0)))],
        out_specs=pl.BlockSpec((tm, tn), lambda j, i: (i, j)),
        out_shape=jax.ShapeDtypeStruct((t, (n_a + n_b) * tn), F32),
        scratch_shapes=[pltpu.VMEM((k, tn), BF16)],
        compiler_params=_params(("arbitrary", "arbitrary")),
    )(x, w_a, w_b)


def _matmul_kernel(x_ref, w_ref, o_ref):
    o_ref[...] = _dot(x_ref[...], w_ref[...])


def _matmul(x, w, tm, tn):
    t, k = x.shape
    n = w.shape[1]
    return pl.pallas_call(
        _matmul_kernel,
        grid=(n // tn, t // tm),
        in_specs=[pl.BlockSpec((tm, k), lambda j, i: (i, 0)), pl.BlockSpec((k, tn), lambda j, i: (0, j))],
        out_specs=pl.BlockSpec((tm, tn), lambda j, i: (i, j)),
        out_shape=jax.ShapeDtypeStruct((t, n), F32),
        compiler_params=_params(("parallel", "arbitrary")),
    )(x, w)


def _chunk_masks(nb, sl):
    r = nb * sl
    shift = int(math.log2(sl))
    ri = lax.broadcasted_iota(jnp.int32, (r, r), 0)
    ci = lax.broadcasted_iota(jnp.int32, (r, r), 1)
    same = lax.shift_right_logical(ri, shift) == lax.shift_right_logical(ci, shift)
    return same & (ci <= ri), same & (ci < ri)


def _row_valid(nb, sl, n_valid):
    rowid = lax.broadcasted_iota(jnp.int32, (nb * sl, 1), 0)
    return (rowid & (sl - 1)) < n_valid


def _last_row_bcast(x, nb, sl):
    c = x.shape[-1]
    x3 = x.reshape(nb, sl, c)
    return jnp.broadcast_to(x3[:, sl - 1:sl, :], (nb, sl, c)).reshape(nb * sl, c)


def _gated_rmsnorm(o, w, gate):
    return o * lax.rsqrt(jnp.mean(o * o, axis=-1, keepdims=True) + EPS) * w * _silu(gate)


def _gdn_kernel(*refs, nb, sl, n_valid, has_init, mode, inv_mode, head_group):
    if has_init:
        (qkv_ref, z_ref, ba_ref, cw_ref, pv_ref, nw_ref, s0_ref, c0_ref,
         o_ref, sout_ref, s_scr, carry_scr) = refs
    else:
        (qkv_ref, z_ref, ba_ref, cw_ref, pv_ref, nw_ref,
         o_ref, sout_ref, s_scr, carry_scr) = refs
    c = pl.program_id(1)
    r = nb * sl
    dh = D_HEAD

    @pl.when(c == 0)
    def _():
        if has_init:
            s_scr[...] = s0_ref[...]
            carry_scr[...] = c0_ref[...]
        else:
            s_scr[...] = jnp.zeros_like(s_scr)
            carry_scr[...] = jnp.zeros_like(carry_scr)

    incl, strict = _chunk_masks(nb, sl)
    lmask = jnp.where(incl, 1.0, 0.0)
    offdiag = jnp.where(strict, 1.0, 0.0)
    valid = _row_valid(nb, sl, n_valid)
    masked = n_valid < sl
    rowid = lax.broadcasted_iota(jnp.int32, (r, 1), 0)
    row_seq = lax.shift_right_logical(rowid, int(math.log2(sl)))
    n_sq = int(math.log2(sl)) - 1

    ba = ba_ref[...].reshape(r, LANES)
    pv = pv_ref[...]
    beta_all = _sigmoid(ba)
    g_all = -jnp.exp(pv[0:1]) * _softplus(ba + pv[1:2])
    if masked:
        g_all = jnp.where(valid, g_all, 0.0)
    gcum = _masked_cumsum(lmask, g_all)
    gcum_t = gcum.T
    glast = _last_row_bcast(gcum, nb, sl)
    cw = cw_ref[...]
    nw = nw_ref[...]

    def conv_slice(c0):
        u = qkv_ref[:, :, c0:c0 + dh]
        prev = carry_scr[:, :, c0:c0 + dh]
        full = jnp.concatenate([prev, u], axis=1)
        acc = None
        for j in range(CONV_W):
            off = SUBLANES - (CONV_W - 1) + j
            term = full[:, off:off + sl, :] * cw[j:j + 1, c0:c0 + dh]
            acc = term if acc is None else acc + term
        return _silu(acc).reshape(r, dh)

    for h0 in range(0, N_HEADS, head_group):
        heads = range(h0, h0 + head_group)
        qs, ks, vs, bcs, gcs, gls, egs, decays = [], [], [], [], [], [], [], []
        for h in heads:
            q = conv_slice(h * dh)
            k = conv_slice(N_HEADS * dh + h * dh)
            v = conv_slice(2 * N_HEADS * dh + h * dh)
            q = q * lax.rsqrt(jnp.sum(q * q, axis=-1, keepdims=True) + EPS) * (dh ** -0.5)
            k = k * lax.rsqrt(jnp.sum(k * k, axis=-1, keepdims=True) + EPS)
            if masked:
                q = jnp.where(valid, q, 0.0)
                k = jnp.where(valid, k, 0.0)
                v = jnp.where(valid, v, 0.0)
            gc = gcum[:, N_HEADS + h:N_HEADS + h + 1]
            gr = gcum_t[N_HEADS + h:N_HEADS + h + 1, :]
            qs.append(q)
            ks.append(k)
            vs.append(v)
            bcs.append(beta_all[:, h:h + 1])
            gcs.append(gc)
            gls.append(glast[:, N_HEADS + h:N_HEADS + h + 1])
            egs.append(jnp.exp(gc))
            decays.append(jnp.exp(jnp.where(incl, gc - gr, NEG_BIG)))
        n = len(qs)
        qk_kk = [_dot_nt(jnp.concatenate([qs[i], ks[i]], axis=0), ks[i], mode) for i in range(n)]
        tm1 = [qk_kk[i][r:] * (decays[i] * offdiag) * (-bcs[i]) for i in range(n)]
        pw = list(tm1)
        for _ in range(n_sq):
            pw = [_dot(pw[i], pw[i], inv_mode) for i in range(n)]
            tm1 = [tm1[i] + pw[i] + _dot(tm1[i], pw[i], inv_mode) for i in range(n)]
        rhs = [jnp.concatenate([vs[i] * bcs[i], ks[i] * (bcs[i] * egs[i])], axis=1) for i in range(n)]
        uw = [rhs[i] + _dot(tm1[i], rhs[i], mode) for i in range(n)]
        vnew, ointer = [], []
        for i, h in enumerate(heads):
            u = uw[i][:, :dh]
            w = uw[i][:, dh:]
            qe = qs[i] * egs[i]
            vnew_parts, ointer_parts = [], []
            for b in range(nb):
                rows = slice(b * sl, (b + 1) * sl)
                ws = _dot(jnp.concatenate([w[rows], qe[rows]], axis=0), s_scr[b, h], mode)
                vnew_parts.append(u[rows] - ws[:sl])
                ointer_parts.append(ws[sl:])
            vnew.append(vnew_parts[0] if nb == 1 else jnp.concatenate(vnew_parts, axis=0))
            ointer.append(ointer_parts[0] if nb == 1 else jnp.concatenate(ointer_parts, axis=0))
        for i, h in enumerate(heads):
            attn = qk_kk[i][:r] * decays[i]
            o = ointer[i] + _dot(attn, vnew[i], mode)
            z = z_ref[:, :, h * dh:(h + 1) * dh].reshape(r, dh)
            o_ref[:, :, h * dh:(h + 1) * dh] = _gated_rmsnorm(o, nw, z).reshape(nb, sl, dh).astype(o_ref.dtype)
        for i, h in enumerate(heads):
            ktil = ks[i] * jnp.exp(gls[i] - gcs[i])
            for b in range(nb):
                kt_b = ktil if nb == 1 else jnp.where(row_seq == b, ktil, 0.0)
                gl_b = gls[i][b * sl:b * sl + 1, :]
                s_scr[b, h] = s_scr[b, h] * jnp.exp(gl_b) + _dot_tn(kt_b, vnew[i], mode)

    carry_scr[...] = qkv_ref[:, sl - SUBLANES:sl, :]

    @pl.when(c == pl.num_programs(1) - 1)
    def _():
        sout_ref[...] = s_scr[...]


def _hgrn_kernel(*refs, nb, sl, n_valid, has_init, mode, head_group):
    if has_init:
        (q_ref, f_ref, i_ref, g_ref, lb_ref, nw_ref, s0_ref, o_ref, sout_ref, s_scr) = refs
    else:
        (q_ref, f_ref, i_ref, g_ref, lb_ref, nw_ref, o_ref, sout_ref, s_scr) = refs
    c = pl.program_id(1)
    r = nb * sl
    dh = D_HEAD
    width = N_HEADS * dh

    @pl.when(c == 0)
    def _():
        if has_init:
            s_scr[...] = s0_ref[...]
        else:
            s_scr[...] = jnp.zeros_like(s_scr)

    incl, _ = _chunk_masks(nb, sl)
    lmask = jnp.where(incl, 1.0, 0.0)
    valid = _row_valid(nb, sl, n_valid)
    masked = n_valid < sl
    rowid = lax.broadcasted_iota(jnp.int32, (r, 1), 0)
    row_seq = lax.shift_right_logical(rowid, int(math.log2(sl)))
    sub = min(SUB, sl)
    nblk = r // sub
    sub_shift = int(math.log2(sub))
    ri = lax.broadcasted_iota(jnp.int32, (r, r), 0)
    ci = lax.broadcasted_iota(jnp.int32, (r, r), 1)
    same_blk = lax.shift_right_logical(ri, sub_shift) == lax.shift_right_logical(ci, sub_shift)
    diag_mask = incl & same_blk
    cross_mask = incl & jnp.logical_not(same_blk)

    lb = lb_ref[...]
    f = lb + (1.0 - lb) * _sigmoid(f_ref[...].reshape(r, width))
    lf = jnp.log(f)
    k_all = 1.0 - f
    if masked:
        lf = jnp.where(valid, lf, 0.0)
        k_all = jnp.where(valid, k_all, 0.0)
    bcum = _masked_cumsum(lmask, lf)
    nw = nw_ref[...]

    def head_inputs(h):
        cols = slice(h * dh, (h + 1) * dh)
        q = _silu(q_ref[:, :, cols].reshape(r, dh)) * (dh ** -0.5)
        v = i_ref[:, :, cols].reshape(r, dh)
        if masked:
            v = jnp.where(valid, v, 0.0)
        return q, k_all[:, cols], v, bcum[:, cols]

    def intra_attn(q, k, bh):
        bmid = jnp.broadcast_to(bh.reshape(nblk, sub, dh)[:, sub // 2:sub // 2 + 1, :],
                                (nblk, sub, dh)).reshape(r, dh)
        attn = jnp.where(diag_mask, _dot_nt(q * jnp.exp(bh - bmid), k * jnp.exp(bmid - bh), mode), 0.0)
        if sl > sub:
            parts = [jnp.zeros((sub, r), F32)]
            for blk in range(1, nblk):
                start = blk * sub
                bref = bh[start - 1:start, :]
                qc = q[start:start + sub] * jnp.exp(bh[start:start + sub] - bref)
                kc = k * jnp.exp(jnp.minimum(bref - bh, 0.0))
                parts.append(_dot_nt(qc, kc, mode))
            attn = attn + jnp.where(cross_mask, jnp.concatenate(parts, axis=0), 0.0)
        return attn

    for h0 in range(0, N_HEADS, head_group):
        heads = range(h0, h0 + head_group)
        ins = [head_inputs(h) for h in heads]
        attns = [intra_attn(q, k, bh) for (q, k, v, bh) in ins]
        for i, h in enumerate(heads):
            q, k, v, bh = ins[i]
            qe = q * jnp.exp(bh)
            ointer_parts = []
            for b in range(nb):
                rows = slice(b * sl, (b + 1) * sl)
                ointer_parts.append(_dot(qe[rows], s_scr[b, h], mode))
            ointer = ointer_parts[0] if nb == 1 else jnp.concatenate(ointer_parts, axis=0)
            o = ointer + _dot(attns[i], v, mode)
            cols = slice(h * dh, (h + 1) * dh)
            gate = g_ref[:, :, cols].reshape(r, dh)
            o_ref[:, :, cols] = _gated_rmsnorm(o, nw, gate).reshape(nb, sl, dh).astype(o_ref.dtype)
        for i, h in enumerate(heads):
            q, k, v, bh = ins[i]
            blast = _last_row_bcast(bh, nb, sl)
            ktil = k * jnp.exp(blast - bh)
            pad = [jnp.zeros((LANES - r, dh), F32)] if r < LANES else []
            tr = jnp.concatenate([blast] + pad, axis=0).T
            for b in range(nb):
                kt_b = ktil if nb == 1 else jnp.where(row_seq == b, ktil, 0.0)
                dec_col = jnp.exp(tr[:, b * sl:b * sl + 1])
                s_scr[b, h] = s_scr[b, h] * dec_col + _dot_tn(kt_b, v, mode)

    @pl.when(c == pl.num_programs(1) - 1)
    def _():
        sout_ref[...] = s_scr[...]


def _mixers(proj3, ba3, blk_off, nseq, nb, sl, n_valid, conv_w, pvec, gdn_nw, lb, hgrn_nw,
            s_gdn0, conv0, s_hgrn0, shared_init, out_dtype, mode, inv_mode):
    length = proj3.shape[1]
    has_init = s_gdn0 is not None
    width = N_HEADS * D_HEAD
    conv_ch = 3 * width
    grid = (nseq // nb, length // sl)
    state_spec = pl.BlockSpec((nb, N_HEADS, D_HEAD, D_HEAD), lambda g, c: (g, 0, 0, 0))
    state_shape = jax.ShapeDtypeStruct((nseq, N_HEADS, D_HEAD, D_HEAD), F32)
    init_idx = (lambda g: 0) if shared_init else (lambda g: g)
    init_state_spec = pl.BlockSpec((nb, N_HEADS, D_HEAD, D_HEAD), lambda g, c: (init_idx(g), 0, 0, 0))
    head_group = N_HEADS

    def col_spec(w, idx):
        return pl.BlockSpec((nb, sl, w), lambda g, c: (g + blk_off, c, idx))

    def out_spec(w):
        return pl.BlockSpec((nb, sl, w), lambda g, c: (g, c, 0))

    def const_spec(shape):
        return pl.BlockSpec(shape, lambda g, c: (0,) * len(shape))

    gdn_in = [proj3, proj3, ba3, conv_w, pvec, gdn_nw]
    gdn_specs = [col_spec(conv_ch, 0), col_spec(width, 3), col_spec(LANES, 0),
                 const_spec(conv_w.shape), const_spec(pvec.shape), const_spec(gdn_nw.shape)]
    if has_init:
        gdn_in += [s_gdn0, conv0]
        gdn_specs += [init_state_spec, pl.BlockSpec((nb, SUBLANES, conv_ch), lambda g, c: (init_idx(g), 0, 0))]
    hgrn_in = [proj3, proj3, proj3, proj3, lb, hgrn_nw]
    hgrn_specs = [col_spec(width, 4), col_spec(width, 5), col_spec(width, 6), col_spec(width, 7),
                  const_spec(lb.shape), const_spec(hgrn_nw.shape)]
    if has_init:
        hgrn_in += [s_hgrn0]
        hgrn_specs += [init_state_spec]

    gdn_body = functools.partial(_gdn_kernel, nb=nb, sl=sl, n_valid=n_valid, has_init=has_init, mode=mode,
                                 inv_mode=inv_mode, head_group=head_group)
    hgrn_body = functools.partial(_hgrn_kernel, nb=nb, sl=sl, n_valid=n_valid, has_init=has_init, mode=mode,
                                  head_group=head_group)
    n_g, n_h = len(gdn_in), len(hgrn_in)

    def both(*refs):
        ins, outs, scr = refs[:n_g + n_h], refs[n_g + n_h:n_g + n_h + 4], refs[n_g + n_h + 4:]
        gdn_body(*ins[:n_g], outs[0], outs[1], scr[0], scr[1])
        hgrn_body(*ins[n_g:], outs[2], outs[3], scr[2])

    state_scr = pltpu.VMEM((nb, N_HEADS, D_HEAD, D_HEAD), F32)
    o_shape = jax.ShapeDtypeStruct((nseq, length, width), out_dtype)
    o_gdn, s_gdn, o_hgrn, s_hgrn = pl.pallas_call(
        both,
        grid=grid,
        in_specs=gdn_specs + hgrn_specs,
        out_specs=[out_spec(width), state_spec, out_spec(width), state_spec],
        out_shape=[o_shape, state_shape, o_shape, state_shape],
        scratch_shapes=[state_scr, pltpu.VMEM((nb, SUBLANES, conv_ch), F32), state_scr],
        compiler_params=_params(("parallel", "arbitrary")),
    )(*gdn_in, *hgrn_in)
    return o_gdn, o_hgrn, s_gdn, s_hgrn


def _outproj_router_kernel(oap_ref, obp_ref, hpp_ref, oas_ref, obs_ref, hps_ref, wo_ref, n2_ref, wr_ref, br_ref,
                           hp2_ref, xloc_ref, ri_ref, rw_ref, m8_ref, *, tm, n_p):
    i = pl.program_id(0)
    body = functools.partial(_outproj_router_tile, wo_ref=wo_ref, n2_ref=n2_ref, wr_ref=wr_ref, br_ref=br_ref,
                             hp2_ref=hp2_ref, xloc_ref=xloc_ref, ri_ref=ri_ref, rw_ref=rw_ref, m8_ref=m8_ref, tm=tm)

    @pl.when(i < n_p)
    def _():
        body(oap_ref[...], obp_ref[...], hpp_ref[...])

    @pl.when(i >= n_p)
    def _():
        body(oas_ref[...], obs_ref[...], hps_ref[...])


def _outproj_router_tile(oa, ob, hp, *, wo_ref, n2_ref, wr_ref, br_ref, hp2_ref, xloc_ref, ri_ref, rw_ref,
                         m8_ref, tm):
    half = oa.shape[-1]
    mix = _dot(oa.astype(BF16), wo_ref[:half, :]) + _dot(ob.astype(BF16), wo_ref[half:, :])
    hp2 = hp + mix
    hp2_ref[...] = hp2
    xn2 = hp2 * lax.rsqrt(jnp.mean(hp2 * hp2, axis=-1, keepdims=True) + EPS) * n2_ref[...]
    logits = _dot(xn2, wr_ref[...], "bf16x3") + br_ref[...]

    lane = lax.broadcasted_iota(jnp.int32, (tm, LANES), 1)
    lane_f = lane.astype(F32)
    far = float(4 * LANES)
    is_g = (lane >= N_EXPERTS) & (lane < N_EXPERTS + N_GROUPS)
    lg = jnp.where(is_g, logits, -jnp.inf)
    gmax = jnp.max(lg, axis=-1, keepdims=True)
    gsel = jnp.min(jnp.where(lg == gmax, lane_f, far), axis=-1, keepdims=True).astype(jnp.int32) - N_EXPERTS
    p_top = 1.0 / jnp.sum(jnp.where(is_g, jnp.exp(logits - gmax), 0.0), axis=-1, keepdims=True)
    in_grp = (lane < N_EXPERTS) & (lax.shift_right_logical(lane, 3) == gsel)
    le = jnp.where(in_grp, logits, -jnp.inf)
    m1 = jnp.max(le, axis=-1, keepdims=True)
    i1 = jnp.min(jnp.where(le == m1, lane_f, far), axis=-1, keepdims=True).astype(jnp.int32)
    le2 = jnp.where(lane == i1, -jnp.inf, le)
    m2 = jnp.max(le2, axis=-1, keepdims=True)
    i2 = jnp.min(jnp.where(le2 == m2, lane_f, far), axis=-1, keepdims=True).astype(jnp.int32)
    e2 = jnp.exp(m2 - m1)
    w1 = p_top / (1.0 + e2)
    w2 = p_top * e2 / (1.0 + e2)

    onehot = (lane == i1) | (lane == i2)
    onehot_f = jnp.where(onehot, 1.0, 0.0)
    tri = (lax.broadcasted_iota(jnp.int32, (tm, tm), 1) < lax.broadcasted_iota(jnp.int32, (tm, tm), 0))
    rank = _dot(jnp.where(tri, 1.0, 0.0).astype(BF16), onehot_f.astype(BF16))
    cnt = jnp.sum(onehot_f, axis=0, keepdims=True)
    m8 = jnp.floor((cnt + (SUBLANES - 1)) * (1.0 / SUBLANES))
    upper = (lax.broadcasted_iota(jnp.int32, (LANES, LANES), 0) < lax.broadcasted_iota(jnp.int32, (LANES, LANES), 1))
    goff = _dot(jnp.broadcast_to(m8, (SUBLANES, LANES)).astype(BF16), jnp.where(upper, 1.0, 0.0).astype(BF16))[0:1]
    local = goff * SUBLANES + rank
    lr1 = jnp.sum(jnp.where(lane == i1, rank, 0.0), axis=-1, keepdims=True)
    lr2 = jnp.sum(jnp.where(lane == i2, rank, 0.0), axis=-1, keepdims=True)
    lp1 = jnp.sum(jnp.where(lane == i1, local, 0.0), axis=-1, keepdims=True)
    lp2 = jnp.sum(jnp.where(lane == i2, local, 0.0), axis=-1, keepdims=True)
    lp_rows = jnp.where(lane == 0, lp1, jnp.where(lane == 1, lp2, -1.0)).T
    n_loc = xloc_ref.shape[0]
    row = lax.broadcasted_iota(jnp.int32, (n_loc, tm), 0).astype(F32)
    perm = (row == lp_rows[0:1, :]) | (row == lp_rows[1:2, :])
    xloc_ref[...] = _dot(jnp.where(perm, 1.0, 0.0).astype(BF16), xn2.astype(BF16))

    ri_ref[...] = jnp.where(lane == 0, i1, jnp.where(lane == 1, i2, jnp.where(
        lane == 2, lr1.astype(jnp.int32), jnp.where(lane == 3, lr2.astype(jnp.int32), 0))))
    rw_ref[...] = jnp.where(lane == 0, w1, jnp.where(lane == 1, w2, 0.0))
    m8_ref[...] = jnp.broadcast_to(m8, m8_ref.shape)


def _outproj_router(oa_p, ob_p, hp_p, oa_s, ob_s, hp_s, w_out, norm2_w, w_r, b_r, tm):
    (tp, d), ts = hp_p.shape, hp_s.shape[0]
    half = oa_p.shape[1]
    n_p, n_s = tp // tm, ts // tm
    t = tp + ts
    n_loc = _local_rows(tm)
    prow = lambda w: pl.BlockSpec((tm, w), lambda i: (jnp.minimum(i, n_p - 1), 0))
    srow = lambda w: pl.BlockSpec((tm, w), lambda i: (jnp.maximum(i - n_p, 0), 0))
    row = lambda w: pl.BlockSpec((tm, w), lambda i: (i, 0))
    const = lambda shape: pl.BlockSpec(shape, lambda i: (0,) * len(shape))
    return pl.pallas_call(
        functools.partial(_outproj_router_kernel, tm=tm, n_p=n_p),
        grid=(n_p + n_s,),
        in_specs=[prow(half), prow(half), prow(d), srow(half), srow(half), srow(d),
                  const(w_out.shape), const((1, d)), const(w_r.shape), const((1, LANES))],
        out_specs=[row(d), pl.BlockSpec((n_loc, d), lambda i: (i, 0)), row(LANES), row(LANES),
                   pl.BlockSpec((None, SUBLANES, LANES), lambda i: (i, 0, 0))],
        out_shape=[jax.ShapeDtypeStruct((t, d), F32), jax.ShapeDtypeStruct(((n_p + n_s) * n_loc, d), F32),
                   jax.ShapeDtypeStruct((t, LANES), jnp.int32), jax.ShapeDtypeStruct((t, LANES), F32),
                   jax.ShapeDtypeStruct((n_p + n_s, SUBLANES, LANES), F32)],
        compiler_params=_params(("parallel",)),
    )(oa_p, ob_p, hp_p, oa_s, ob_s, hp_s, w_out, norm2_w.reshape(1, d), w_r, b_r)


def _local_rows(tm):
    return 2 * tm + SUBLANES * N_EXPERTS


def _start_row_gather(idx_ref, base, n, src_hbm, dst, sem):
    def issue(r, carry):
        pltpu.make_async_copy(src_hbm.at[pl.ds(idx_ref[base + r], 1), :], dst.at[pl.ds(r, 1), :], sem).start()
        return carry
    lax.fori_loop(0, n, issue, 0, unroll=8)


def _start_group_gather(idx_ref, base, n_groups, src_hbm, dst, sem):
    def issue(r, carry):
        src_row = pl.multiple_of(idx_ref[base + r] * SUBLANES, SUBLANES)
        dst_row = pl.multiple_of(r * SUBLANES, SUBLANES)
        pltpu.make_async_copy(src_hbm.at[pl.ds(src_row, SUBLANES), :], dst.at[pl.ds(dst_row, SUBLANES), :], sem).start()
        return carry
    lax.fori_loop(0, n_groups, issue, 0, unroll=8)


def _wait_row_gather(n, src_hbm, dst, sem):
    pltpu.make_async_copy(src_hbm.at[pl.ds(0, n), :], dst, sem).wait()


def _moe_kernel(te_ref, st_ref, nx_ref, nu_ref, x_hbm, wg_hbm, wu_hbm, wd_hbm, o_ref,
                xbuf, wg_f32, wu_f32, wd_f32, wg_scr, wu_scr, wd_scr, sems, wsems, *, tile):
    i = pl.program_id(0)
    n_used = nu_ref[0]
    slot = lax.rem(i, 2)
    weights = ((wg_hbm, wg_f32, wg_scr), (wu_hbm, wu_f32, wu_scr), (wd_hbm, wd_f32, wd_scr))

    def start_weights(e):
        for k, (w_hbm, w_f32, _) in enumerate(weights):
            pltpu.async_copy(w_hbm.at[e], w_f32, wsems.at[k], priority=1)

    @pl.when((i == 0) & (n_used > 0))
    def _():
        start_weights(te_ref[0])
        _start_group_gather(st_ref, 0, tile // SUBLANES, x_hbm, xbuf.at[0], sems.at[0])

    @pl.when(i < n_used)
    def _():
        @pl.when(i + 1 < n_used)
        def _():
            _start_group_gather(st_ref, (i + 1) * (tile // SUBLANES), tile // SUBLANES, x_hbm,
                                xbuf.at[1 - slot], sems.at[1 - slot])

        e = te_ref[i]

        @pl.when((i == 0) | (e != te_ref[jnp.maximum(i - 1, 0)]))
        def _():
            for k, (w_hbm, w_f32, w_scr) in enumerate(weights):
                pltpu.make_async_copy(w_hbm.at[0], w_f32, wsems.at[k]).wait()
                w_scr[...] = w_f32[...].astype(BF16)
            nxt = nx_ref[e]

            @pl.when(nxt < N_EXPERTS)
            def _():
                start_weights(nxt)

        _wait_row_gather(tile, x_hbm, xbuf.at[slot], sems.at[slot])
        x = xbuf[slot].astype(BF16)
        g = _dot(x, wg_scr[...])
        u = _dot(x, wu_scr[...])
        o_ref[...] = _dot((_silu(g) * u).astype(BF16), wd_scr[...])

    @pl.when(i >= n_used)
    def _():
        o_ref[...] = jnp.zeros_like(o_ref)


def _moe(xn2, w_gate, w_up, w_down, tile_expert, slot_token, next_expert, n_used, tile):
    d = xn2.shape[1]
    n_tiles = tile_expert.shape[0]
    de = w_gate.shape[2]
    hbm = pl.BlockSpec(memory_space=pl.ANY)
    grid_spec = pltpu.PrefetchScalarGridSpec(
        num_scalar_prefetch=4,
        grid=(n_tiles,),
        in_specs=[hbm, hbm, hbm, hbm],
        out_specs=pl.BlockSpec((tile, d), lambda i, te, st, nx, nu: (i, 0)),
        scratch_shapes=[pltpu.VMEM((2, tile, d), F32),
                        pltpu.VMEM((d, de), F32), pltpu.VMEM((d, de), F32), pltpu.VMEM((de, d), F32),
                        pltpu.VMEM((d, de), BF16), pltpu.VMEM((d, de), BF16), pltpu.VMEM((de, d), BF16),
                        pltpu.SemaphoreType.DMA((2,)), pltpu.SemaphoreType.DMA((3,))],
    )
    return pl.pallas_call(
        functools.partial(_moe_kernel, tile=tile),
        grid_spec=grid_spec,
        out_shape=jax.ShapeDtypeStruct((n_tiles * tile, d), F32),
        compiler_params=_params(("arbitrary",)),
    )(tile_expert, slot_token, next_expert, n_used, xn2, w_gate, w_up, w_down)


def _combine_kernel(p0_ref, p1_ref, ys_hbm, hp2_ref, rw_ref, fw_ref, o_ref, buf0, buf1, sems, *, tm, tile_off):
    i = pl.program_id(0)
    slot = lax.rem(i, 2)

    def start(step, s):
        base = (step + tile_off) * tm
        _start_row_gather(p0_ref, base, tm, ys_hbm, buf0.at[s], sems.at[0, s])
        _start_row_gather(p1_ref, base, tm, ys_hbm, buf1.at[s], sems.at[1, s])

    @pl.when(i == 0)
    def _():
        start(0, 0)

    @pl.when(i + 1 < pl.num_programs(0))
    def _():
        start(i + 1, 1 - slot)

    _wait_row_gather(tm, ys_hbm, buf0.at[slot], sems.at[0, slot])
    _wait_row_gather(tm, ys_hbm, buf1.at[slot], sems.at[1, slot])
    rw = rw_ref[...]
    y = hp2_ref[...] + rw[:, 0:1] * buf0[slot] + rw[:, 1:2] * buf1[slot]
    o_ref[...] = y * lax.rsqrt(jnp.mean(y * y, axis=-1, keepdims=True) + EPS) * fw_ref[...]


def _combine(ys, hp2, route_w, final_w, pos0, pos1, tm, tile_off, n_tiles):
    _, d = hp2.shape
    grid_spec = pltpu.PrefetchScalarGridSpec(
        num_scalar_prefetch=2,
        grid=(n_tiles,),
        in_specs=[pl.BlockSpec(memory_space=pl.ANY),
                  pl.BlockSpec((tm, d), lambda i, p0, p1: (i + tile_off, 0)),
                  pl.BlockSpec((tm, LANES), lambda i, p0, p1: (i + tile_off, 0)),
                  pl.BlockSpec((1, d), lambda i, p0, p1: (0, 0))],
        out_specs=pl.BlockSpec((tm, d), lambda i, p0, p1: (i, 0)),
        scratch_shapes=[pltpu.VMEM((2, tm, d), F32), pltpu.VMEM((2, tm, d), F32),
                        pltpu.SemaphoreType.DMA((2, 2))],
    )
    return pl.pallas_call(
        functools.partial(_combine_kernel, tm=tm, tile_off=tile_off),
        grid_spec=grid_spec,
        out_shape=jax.ShapeDtypeStruct((n_tiles * tm, d), F32),
        compiler_params=_params(("arbitrary",)),
    )(pos0, pos1, ys, hp2, route_w, final_w.reshape(1, d))


def kernel(x_prompt, x_sample, state_gdn, state_conv, state_hgrn, meta_tokens, norm1_w, w_in, conv_w, a_log,
           dt_bias, gdn_norm_w, lb_logits, hgrn_norm_w, w_out, norm2_w, w_router_group, b_router_group,
           w_router_expert, b_router_expert, w_gate, w_up, w_down, final_norm_w):
    bp, seq, d = x_prompt.shape
    bs, dec_seq, _ = x_sample.shape
    assert w_in.shape[0] == 1, "single-layer trunk"
    width = N_HEADS * D_HEAD
    conv_ch = 3 * width
    tile = 256
    tn = 512
    sl_s = SUBLANES
    nb_s = CHUNK // sl_s
    assert seq % CHUNK == 0 and N_META <= CHUNK and dec_seq <= sl_s and bs % nb_s == 0
    tp, ts, ts_pad = bp * seq, bs * dec_seq, bs * sl_s
    t_small = ts_pad + CHUNK
    tm = _row_tile(math.gcd(tp, ts), 256)
    tm_p = _row_tile(tp, 2048)

    xp = x_prompt.reshape(tp, d)
    x_small = jnp.concatenate([jnp.pad(x_sample, ((0, 0), (0, sl_s - dec_seq), (0, 0))).reshape(ts_pad, d),
                               jnp.zeros((CHUNK - N_META, d), F32), meta_tokens.astype(F32)], axis=0)

    wi = w_in[0]
    n_a = 4 * width // tn
    w_b = wi[:, 4 * width + 2 * N_HEADS:]
    w_ba = jnp.pad(wi[:, 4 * width:4 * width + 2 * N_HEADS], ((0, 0), (0, LANES - 2 * N_HEADS))).astype(BF16)
    n_cols = 8 * width

    xn_p = _rmsnorm(xp, norm1_w[0], BF16, _row_tile(tp, 512))
    xn_s = _rmsnorm(x_small, norm1_w[0], BF16, _row_tile(t_small, 1024))
    w_a = wi[:, :4 * width]
    proj_p = _inproj(xn_p, w_a, n_a, w_b, tm_p, tn)
    proj_s = _inproj(xn_s, w_a, n_a, w_b, t_small, tn)
    ba_p = _matmul(xn_p, w_ba, tm_p, LANES)
    ba_s = _matmul(xn_s, w_ba, t_small, LANES)

    pvec = jnp.zeros((2, LANES), F32)
    pvec = pvec.at[0, N_HEADS:2 * N_HEADS].set(a_log[0]).at[1, N_HEADS:2 * N_HEADS].set(dt_bias[0])
    lb = jnp.cumsum(jax.nn.softmax(lb_logits.astype(F32), axis=0), axis=0)[0].reshape(1, width)
    gdn_nw = gdn_norm_w[0].reshape(1, D_HEAD)
    hgrn_nw = hgrn_norm_w[0].reshape(1, D_HEAD)
    cw = conv_w[0]
    mix_args = (cw, pvec, gdn_nw, lb, hgrn_nw)

    _, _, sg_m, sh_m = _mixers(proj_s.reshape(t_small // CHUNK, CHUNK, n_cols),
                               ba_s.reshape(t_small // CHUNK, CHUNK, LANES), ts_pad // CHUNK, 1, 1, CHUNK, CHUNK,
                               *mix_args, None, None, None, False, F32, MIX_MODE, INV_MODE)
    conv_m = proj_s[t_small - SUBLANES:, :conv_ch].reshape(1, SUBLANES, conv_ch)
    oa_p, ob_p, sg_p, sh_p = _mixers(proj_p.reshape(bp, seq, n_cols), ba_p.reshape(bp, seq, LANES), 0, bp, 1,
                                     CHUNK, CHUNK, *mix_args, sg_m, conv_m, sh_m, True, BF16, MIX_MODE, INV_MODE)
    conv0 = jnp.pad(state_conv[0], ((0, 0), (SUBLANES - (CONV_W - 1), 0), (0, 0)))
    oa_s, ob_s, sg_s, sh_s = _mixers(proj_s.reshape(t_small // sl_s, sl_s, n_cols),
                                     ba_s.reshape(t_small // sl_s, sl_s, LANES), 0, bs, nb_s, sl_s, dec_seq,
                                     *mix_args, state_gdn[0], conv0, state_hgrn[0], False, F32, MIX_MODE, INV_MODE)

    w_r = jnp.concatenate([w_router_expert[0], w_router_group[0],
                           jnp.zeros((d, LANES - N_EXPERTS - N_GROUPS), F32)], axis=1)
    b_r = jnp.concatenate([b_router_expert[0], b_router_group[0],
                           jnp.zeros((LANES - N_EXPERTS - N_GROUPS,), F32)]).reshape(1, LANES)
    t = tp + ts
    hp2, xloc, route_i, route_w, tile_m8 = _outproj_router(
        oa_p.reshape(tp, width), ob_p.reshape(tp, width), xp,
        oa_s[:, :dec_seq].reshape(ts, width), ob_s[:, :dec_seq].reshape(ts, width), x_sample.reshape(ts, d),
        w_out[0].astype(BF16), norm2_w[0], w_r, b_r, tm)

    n_tt = t // tm
    gpt = tile // SUBLANES
    loc_g = _local_rows(tm) // SUBLANES
    expert_ids = jnp.arange(N_EXPERTS, dtype=jnp.int32)
    m8 = tile_m8[:, 0, :N_EXPERTS].astype(jnp.int32)
    before = jnp.cumsum(m8, axis=0) - m8
    goff = jnp.cumsum(m8, axis=1) - m8
    groups = jnp.sum(m8, axis=0)
    padded_g = (groups + gpt - 1) // gpt * gpt
    ends_g = jnp.cumsum(padded_g)
    offs_g = ends_g - padded_g
    eid = route_i[:, 0:2]
    base_rows = SUBLANES * (offs_g[None, :] + before)
    base_tok = jnp.repeat(base_rows, tm, axis=0)
    pos = jnp.sum(jnp.where(eid[:, :, None] == expert_ids, base_tok[:, None, :], 0), axis=-1) + route_i[:, 2:4]

    n_tiles = -(-(2 * t + n_tt * N_EXPERTS * (SUBLANES - 1)) // tile) + N_EXPERTS
    tile_expert = jnp.minimum(
        jnp.sum((ends_g[None, :] <= (jnp.arange(n_tiles, dtype=jnp.int32) * gpt)[:, None]).astype(jnp.int32), axis=1),
        N_EXPERTS - 1)
    n_used = (ends_g[-1] // gpt).astype(jnp.int32).reshape(1)
    q = jnp.arange(n_tiles * gpt, dtype=jnp.int32)
    e_q = jnp.repeat(tile_expert, gpt)
    sel = (e_q[:, None] == expert_ids).astype(jnp.int32)
    u = q - sel @ offs_g
    run_end = sel @ (before + m8).T
    j_q = jnp.sum((run_end <= u[:, None]).astype(jnp.int32), axis=1)
    hit = (jnp.arange(n_tt, dtype=jnp.int32) == j_q[:, None]).astype(jnp.int32)
    src_in_tile = jnp.sum(hit * (sel @ (goff - before).T), axis=1) + u
    slot_group = jnp.where(j_q < n_tt, j_q * loc_g + src_in_tile, 0).astype(jnp.int32)

    later_active = (expert_ids[None, :] > expert_ids[:, None]) & (groups[None, :] > 0)
    next_expert = jnp.min(jnp.where(later_active, expert_ids[None, :], N_EXPERTS), axis=1).astype(jnp.int32)
    ys = _moe(xloc, w_gate[0], w_up[0], w_down[0], tile_expert, slot_group, next_expert, n_used, tile)
    pos0, pos1 = pos[:, 0], pos[:, 1]
    y_prompt = _combine(ys, hp2, route_w, final_norm_w, pos0, pos1, tm, 0, tp // tm).reshape(bp, seq, d)
    y_sample = _combine(ys, hp2, route_w, final_norm_w, pos0, pos1, tm, tp // tm, ts // tm).reshape(bs, dec_seq, d)

    conv_p = proj_p.reshape(bp, seq, n_cols)[:, seq - (CONV_W - 1):, :conv_ch]
    u_s = proj_s.reshape(t_small // sl_s, sl_s, n_cols)[:bs, :dec_seq, :conv_ch]
    conv_s = jnp.concatenate([state_conv[0], u_s], axis=1)[:, dec_seq:]
    return (y_prompt, y_sample, sg_p[None], conv_p[None], sh_p[None], sg_s[None], conv_s[None], sh_s[None])
```

```python
import functools
import math

import jax
import jax.numpy as jnp
from jax import lax
from jax.experimental import pallas as pl
from jax.experimental.pallas import tpu as pltpu

F32 = jnp.float32
BF16 = jnp.bfloat16

EPS = 1e-6
N_META = 16
CONV_W = 4
N_HEADS = 8
D_HEAD = 128
N_GROUPS = 4
EXPERTS_PER_GROUP = 8
N_EXPERTS = N_GROUPS * EXPERTS_PER_GROUP

LANES = 128
SUBLANES = 8
CHUNK = 64
SUB = 16
VMEM_LIMIT = 56 * 1024 * 1024
NEG_BIG = -1e30
MIX_MODE = "bf16"
INV_MODE = "bf16"


def _sigmoid(x):
    return 0.5 * jnp.tanh(0.5 * x) + 0.5


def _silu(x):
    return x * _sigmoid(x)


def _softplus(x):
    return jnp.maximum(x, 0.0) + jnp.log1p(jnp.exp(-jnp.abs(x)))


def _split_bf16(a, pieces):
    out = []
    for _ in range(pieces - 1):
        hi = a.astype(BF16)
        out.append(hi)
        a = a - hi.astype(F32)
    out.append(a.astype(BF16))
    return out


def _mm(a, b, dims, mode):
    dg = functools.partial(lax.dot_general, dimension_numbers=(dims, ((), ())), preferred_element_type=F32)
    if mode == "bf16":
        return dg(a.astype(BF16), b.astype(BF16))
    assert mode == "bf16x3"
    ah, al = _split_bf16(a, 2)
    bh, bl = _split_bf16(b, 2)
    return dg(ah, bh) + dg(ah, bl) + dg(al, bh)


def _dot(a, b, mode="bf16"):
    return _mm(a, b, ((1,), (0,)), mode)


def _dot_nt(a, b, mode="bf16"):
    return _mm(a, b, ((1,), (1,)), mode)


def _dot_tn(a, b, mode="bf16"):
    return _mm(a, b, ((0,), (0,)), mode)


def _masked_cumsum(lmask, x):
    lm = lmask.astype(BF16)
    return sum(lax.dot_general(lm, p, (((1,), (0,)), ((), ())), preferred_element_type=F32)
               for p in _split_bf16(x, 3))


def _params(sem):
    return pltpu.CompilerParams(dimension_semantics=sem, vmem_limit_bytes=VMEM_LIMIT)


def _row_tile(n, target):
    best = max(c for c in range(16, min(n, target) + 1, 16) if n % c == 0)
    return best


def _rmsnorm_kernel(x_ref, w_ref, o_ref):
    x = x_ref[...]
    ms = jnp.mean(x * x, axis=-1, keepdims=True)
    o_ref[...] = (x * lax.rsqrt(ms + EPS) * w_ref[...]).astype(o_ref.dtype)


def _rmsnorm(x, w, out_dtype, tm):
    t, d = x.shape
    return pl.pallas_call(
        _rmsnorm_kernel,
        grid=(t // tm,),
        in_specs=[pl.BlockSpec((tm, d), lambda i: (i, 0)), pl.BlockSpec((1, d), lambda i: (0, 0))],
        out_specs=pl.BlockSpec((tm, d), lambda i: (i, 0)),
        out_shape=jax.ShapeDtypeStruct((t, d), out_dtype),
        compiler_params=_params(("parallel",)),
    )(x, w.reshape(1, d))


def _inproj_kernel(x_ref, wa_ref, wb_ref, o_ref, w_scr, *, n_a):
    j = pl.program_id(0)
    i = pl.program_id(1)

    @pl.when((i == 0) & (j < n_a))
    def _():
        w_scr[...] = wa_ref[...].astype(BF16)

    @pl.when((i == 0) & (j >= n_a))
    def _():
        w_scr[...] = wb_ref[...].astype(BF16)

    o_ref[...] = _dot(x_ref[...], w_scr[...])


def _inproj(x, w_a, n_a, w_b, tm, tn):
    t, k = x.shape
    n_b = w_b.shape[1] // tn
    return pl.pallas_call(
        functools.partial(_inproj_kernel, n_a=n_a),
        grid=(n_a + n_b, t // tm),
        in_specs=[pl.BlockSpec((tm, k), lambda j, i: (i, 0)),
                  pl.BlockSpec((k, tn), lambda j, i: (0, jnp.minimum(j, n_a - 1))),
                  pl.BlockSpec((k, tn), lambda j, i: (0, jnp.maximum(j - n_a, 0)))],
        out_specs=pl.BlockSpec((tm, tn), lambda j, i: (i, j)),
        out_shape=jax.ShapeDtypeStruct((t, (n_a + n_b) * tn), F32),
        scratch_shapes=[pltpu.VMEM((k, tn), BF16)],
        compiler_params=_params(("arbitrary", "arbitrary")),
    )(x, w_a, w_b)


def _matmul_kernel(x_ref, w_ref, o_ref):
    o_ref[...] = _dot(x_ref[...], w_ref[...])


def _matmul(x, w, tm, tn):
    t, k = x.shape
    n = w.shape[1]
    return pl.pallas_call(
        _matmul_kernel,
        grid=(n // tn, t // tm),
        in_specs=[pl.BlockSpec((tm, k), lambda j, i: (i, 0)), pl.BlockSpec((k, tn), lambda j, i: (0, j))],
        out_specs=pl.BlockSpec((tm, tn), lambda j, i: (i, j)),
        out_shape=jax.ShapeDtypeStruct((t, n), F32),
        compiler_params=_params(("parallel", "arbitrary")),
    )(x, w)


def _chunk_masks(nb, sl):
    r = nb * sl
    shift = int(math.log2(sl))
    ri = lax.broadcasted_iota(jnp.int32, (r, r), 0)
    ci = lax.broadcasted_iota(jnp.int32, (r, r), 1)
    same = lax.shift_right_logical(ri, shift) == lax.shift_right_logical(ci, shift)
    return same & (ci <= ri), same & (ci < ri)


def _row_valid(nb, sl, n_valid):
    rowid = lax.broadcasted_iota(jnp.int32, (nb * sl, 1), 0)
    return (rowid & (sl - 1)) < n_valid


def _last_row_bcast(x, nb, sl):
    c = x.shape[-1]
    x3 = x.reshape(nb, sl, c)
    return jnp.broadcast_to(x3[:, sl - 1:sl, :], (nb, sl, c)).reshape(nb * sl, c)


def _gated_rmsnorm(o, w, gate):
    return o * lax.rsqrt(jnp.mean(o * o, axis=-1, keepdims=True) + EPS) * w * _silu(gate)


def _gdn_kernel(*refs, nb, sl, n_valid, has_init, mode, inv_mode, head_group):
    if has_init:
        (qkv_ref, z_ref, ba_ref, cw_ref, pv_ref, nw_ref, s0_ref, c0_ref,
         o_ref, sout_ref, s_scr, carry_scr) = refs
    else:
        (qkv_ref, z_ref, ba_ref, cw_ref, pv_ref, nw_ref,
         o_ref, sout_ref, s_scr, carry_scr) = refs
    c = pl.program_id(1)
    r = nb * sl
    dh = D_HEAD

    @pl.when(c == 0)
    def _():
        if has_init:
            s_scr[...] = s0_ref[...]
            carry_scr[...] = c0_ref[...]
        else:
            s_scr[...] = jnp.zeros_like(s_scr)
            carry_scr[...] = jnp.zeros_like(carry_scr)

    incl, strict = _chunk_masks(nb, sl)
    lmask = jnp.where(incl, 1.0, 0.0)
    offdiag = jnp.where(strict, 1.0, 0.0)
    valid = _row_valid(nb, sl, n_valid)
    masked = n_valid < sl
    rowid = lax.broadcasted_iota(jnp.int32, (r, 1), 0)
    row_seq = lax.shift_right_logical(rowid, int(math.log2(sl)))
    n_sq = int(math.log2(sl)) - 1

    ba = ba_ref[...].reshape(r, LANES)
    pv = pv_ref[...]
    beta_all = _sigmoid(ba)
    g_all = -jnp.exp(pv[0:1]) * _softplus(ba + pv[1:2])
    if masked:
        g_all = jnp.where(valid, g_all, 0.0)
    gcum = _masked_cumsum(lmask, g_all)
    gcum_t = gcum.T
    glast = _last_row_bcast(gcum, nb, sl)
    cw = cw_ref[...]
    nw = nw_ref[...]

    def conv_slice(c0):
        u = qkv_ref[:, :, c0:c0 + dh]
        prev = carry_scr[:, :, c0:c0 + dh]
        full = jnp.concatenate([prev, u], axis=1)
        acc = None
        for j in range(CONV_W):
            off = SUBLANES - (CONV_W - 1) + j
            term = full[:, off:off + sl, :] * cw[j:j + 1, c0:c0 + dh]
            acc = term if acc is None else acc + term
        return _silu(acc).reshape(r, dh)

    for h0 in range(0, N_HEADS, head_group):
        heads = range(h0, h0 + head_group)
        qs, ks, vs, bcs, gcs, gls, egs, decays = [], [], [], [], [], [], [], []
        for h in heads:
            q = conv_slice(h * dh)
            k = conv_slice(N_HEADS * dh + h * dh)
            v = conv_slice(2 * N_HEADS * dh + h * dh)
            q = q * lax.rsqrt(jnp.sum(q * q, axis=-1, keepdims=True) + EPS) * (dh ** -0.5)
            k = k * lax.rsqrt(jnp.sum(k * k, axis=-1, keepdims=True) + EPS)
            if masked:
                q = jnp.where(valid, q, 0.0)
                k = jnp.where(valid, k, 0.0)
                v = jnp.where(valid, v, 0.0)
            gc = gcum[:, N_HEADS + h:N_HEADS + h + 1]
            gr = gcum_t[N_HEADS + h:N_HEADS + h + 1, :]
            qs.append(q)
            ks.append(k)
            vs.append(v)
            bcs.append(beta_all[:, h:h + 1])
            gcs.append(gc)
            gls.append(glast[:, N_HEADS + h:N_HEADS + h + 1])
            egs.append(jnp.exp(gc))
            decays.append(jnp.exp(jnp.where(incl, gc - gr, NEG_BIG)))
        n = len(qs)
        qk_kk = [_dot_nt(jnp.concatenate([qs[i], ks[i]], axis=0), ks[i], mode) for i in range(n)]
        tm1 = [qk_kk[i][r:] * (decays[i] * offdiag) * (-bcs[i]) for i in range(n)]
        pw = list(tm1)
        for _ in range(n_sq):
            pw = [_dot(pw[i], pw[i], inv_mode) for i in range(n)]
            tm1 = [tm1[i] + pw[i] + _dot(tm1[i], pw[i], inv_mode) for i in range(n)]
        rhs = [jnp.concatenate([vs[i] * bcs[i], ks[i] * (bcs[i] * egs[i])], axis=1) for i in range(n)]
        uw = [rhs[i] + _dot(tm1[i], rhs[i], mode) for i in range(n)]
        vnew, ointer = [], []
        for i, h in enumerate(heads):
            u = uw[i][:, :dh]
            w = uw[i][:, dh:]
            qe = qs[i] * egs[i]
            vnew_parts, ointer_parts = [], []
            for b in range(nb):
                rows = slice(b * sl, (b + 1) * sl)
                ws = _dot(jnp.concatenate([w[rows], qe[rows]], axis=0), s_scr[b, h], mode)
                vnew_parts.append(u[rows] - ws[:sl])
                ointer_parts.append(ws[sl:])
            vnew.append(vnew_parts[0] if nb == 1 else jnp.concatenate(vnew_parts, axis=0))
            ointer.append(ointer_parts[0] if nb == 1 else jnp.concatenate(ointer_parts, axis=0))
        for i, h in enumerate(heads):
            attn = qk_kk[i][:r] * decays[i]
            o = ointer[i] + _dot(attn, vnew[i], mode)
            z = z_ref[:, :, h * dh:(h + 1) * dh].reshape(r, dh)
            o_ref[:, :, h * dh:(h + 1) * dh] = _gated_rmsnorm(o, nw, z).reshape(nb, sl, dh).astype(o_ref.dtype)
        for i, h in enumerate(heads):
            ktil = ks[i] * jnp.exp(gls[i] - gcs[i])
            for b in range(nb):
                kt_b = ktil if nb == 1 else jnp.where(row_seq == b, ktil, 0.0)
                gl_b = gls[i][b * sl:b * sl + 1, :]
                s_scr[b, h] = s_scr[b, h] * jnp.exp(gl_b) + _dot_tn(kt_b, vnew[i], mode)

    carry_scr[...] = qkv_ref[:, sl - SUBLANES:sl, :]

    @pl.when(c == pl.num_programs(1) - 1)
    def _():
        sout_ref[...] = s_scr[...]


def _hgrn_kernel(*refs, nb, sl, n_valid, has_init, mode, head_group):
    if has_init:
        (q_ref, f_ref, i_ref, g_ref, lb_ref, nw_ref, s0_ref, o_ref, sout_ref, s_scr) = refs
    else:
        (q_ref, f_ref, i_ref, g_ref, lb_ref, nw_ref, o_ref, sout_ref, s_scr) = refs
    c = pl.program_id(1)
    r = nb * sl
    dh = D_HEAD
    width = N_HEADS * dh

    @pl.when(c == 0)
    def _():
        if has_init:
            s_scr[...] = s0_ref[...]
        else:
            s_scr[...] = jnp.zeros_like(s_scr)

    incl, _ = _chunk_masks(nb, sl)
    lmask = jnp.where(incl, 1.0, 0.0)
    valid = _row_valid(nb, sl, n_valid)
    masked = n_valid < sl
    rowid = lax.broadcasted_iota(jnp.int32, (r, 1), 0)
    row_seq = lax.shift_right_logical(rowid, int(math.log2(sl)))
    sub = min(SUB, sl)
    nblk = r // sub
    sub_shift = int(math.log2(sub))
    ri = lax.broadcasted_iota(jnp.int32, (r, r), 0)
    ci = lax.broadcasted_iota(jnp.int32, (r, r), 1)
    same_blk = lax.shift_right_logical(ri, sub_shift) == lax.shift_right_logical(ci, sub_shift)
    diag_mask = incl & same_blk
    cross_mask = incl & jnp.logical_not(same_blk)

    lb = lb_ref[...]
    f = lb + (1.0 - lb) * _sigmoid(f_ref[...].reshape(r, width))
    lf = jnp.log(f)
    k_all = 1.0 - f
    if masked:
        lf = jnp.where(valid, lf, 0.0)
        k_all = jnp.where(valid, k_all, 0.0)
    bcum = _masked_cumsum(lmask, lf)
    nw = nw_ref[...]

    def head_inputs(h):
        cols = slice(h * dh, (h + 1) * dh)
        q = _silu(q_ref[:, :, cols].reshape(r, dh)) * (dh ** -0.5)
        v = i_ref[:, :, cols].reshape(r, dh)
        if masked:
            v = jnp.where(valid, v, 0.0)
        return q, k_all[:, cols], v, bcum[:, cols]

    def intra_attn(q, k, bh):
        bmid = jnp.broadcast_to(bh.reshape(nblk, sub, dh)[:, sub // 2:sub // 2 + 1, :],
                                (nblk, sub, dh)).reshape(r, dh)
        attn = jnp.where(diag_mask, _dot_nt(q * jnp.exp(bh - bmid), k * jnp.exp(bmid - bh), mode), 0.0)
        if sl > sub:
            parts = [jnp.zeros((sub, r), F32)]
            for blk in range(1, nblk):
                start = blk * sub
                bref = bh[start - 1:start, :]
                qc = q[start:start + sub] * jnp.exp(bh[start:start + sub] - bref)
                kc = k * jnp.exp(jnp.minimum(bref - bh, 0.0))
                parts.append(_dot_nt(qc, kc, mode))
            attn = attn + jnp.where(cross_mask, jnp.concatenate(parts, axis=0), 0.0)
        return attn

    for h0 in range(0, N_HEADS, head_group):
        heads = range(h0, h0 + head_group)
        ins = [head_inputs(h) for h in heads]
        attns = [intra_attn(q, k, bh) for (q, k, v, bh) in ins]
        for i, h in enumerate(heads):
            q, k, v, bh = ins[i]
            qe = q * jnp.exp(bh)
            ointer_parts = []
            for b in range(nb):
                rows = slice(b * sl, (b + 1) * sl)
                ointer_parts.append(_dot(qe[rows], s_scr[b, h], mode))
            ointer = ointer_parts[0] if nb == 1 else jnp.concatenate(ointer_parts, axis=0)
            o = ointer + _dot(attns[i], v, mode)
            cols = slice(h * dh, (h + 1) * dh)
            gate = g_ref[:, :, cols].reshape(r, dh)
            o_ref[:, :, cols] = _gated_rmsnorm(o, nw, gate).reshape(nb, sl, dh).astype(o_ref.dtype)
        for i, h in enumerate(heads):
            q, k, v, bh = ins[i]
            blast = _last_row_bcast(bh, nb, sl)
            ktil = k * jnp.exp(blast - bh)
            pad = [jnp.zeros((LANES - r, dh), F32)] if r < LANES else []
            tr = jnp.concatenate([blast] + pad, axis=0).T
            for b in range(nb):
                kt_b = ktil if nb == 1 else jnp.where(row_seq == b, ktil, 0.0)
                dec_col = jnp.exp(tr[:, b * sl:b * sl + 1])
                s_scr[b, h] = s_scr[b, h] * dec_col + _dot_tn(kt_b, v, mode)

    @pl.when(c == pl.num_programs(1) - 1)
    def _():
        sout_ref[...] = s_scr[...]


def _mixers(proj3, ba3, blk_off, nseq, nb, sl, n_valid, conv_w, pvec, gdn_nw, lb, hgrn_nw,
            s_gdn0, conv0, s_hgrn0, shared_init, out_dtype, mode, inv_mode):
    length = proj3.shape[1]
    has_init = s_gdn0 is not None
    width = N_HEADS * D_HEAD
    conv_ch = 3 * width
    grid = (nseq // nb, length // sl)
    state_spec = pl.BlockSpec((nb, N_HEADS, D_HEAD, D_HEAD), lambda g, c: (g, 0, 0, 0))
    state_shape = jax.ShapeDtypeStruct((nseq, N_HEADS, D_HEAD, D_HEAD), F32)
    init_idx = (lambda g: 0) if shared_init else (lambda g: g)
    init_state_spec = pl.BlockSpec((nb, N_HEADS, D_HEAD, D_HEAD), lambda g, c: (init_idx(g), 0, 0, 0))
    head_group = N_HEADS if nb == 1 else N_HEADS // 2

    def col_spec(w, idx):
        return pl.BlockSpec((nb, sl, w), lambda g, c: (g + blk_off, c, idx))

    def out_spec(w):
        return pl.BlockSpec((nb, sl, w), lambda g, c: (g, c, 0))

    def const_spec(shape):
        return pl.BlockSpec(shape, lambda g, c: (0,) * len(shape))

    gdn_in = [proj3, proj3, ba3, conv_w, pvec, gdn_nw]
    gdn_specs = [col_spec(conv_ch, 0), col_spec(width, 3), col_spec(LANES, 0),
                 const_spec(conv_w.shape), const_spec(pvec.shape), const_spec(gdn_nw.shape)]
    if has_init:
        gdn_in += [s_gdn0, conv0]
        gdn_specs += [init_state_spec, pl.BlockSpec((nb, SUBLANES, conv_ch), lambda g, c: (init_idx(g), 0, 0))]
    hgrn_in = [proj3, proj3, proj3, proj3, lb, hgrn_nw]
    hgrn_specs = [col_spec(width, 4), col_spec(width, 5), col_spec(width, 6), col_spec(width, 7),
                  const_spec(lb.shape), const_spec(hgrn_nw.shape)]
    if has_init:
        hgrn_in += [s_hgrn0]
        hgrn_specs += [init_state_spec]

    gdn_body = functools.partial(_gdn_kernel, nb=nb, sl=sl, n_valid=n_valid, has_init=has_init, mode=mode,
                                 inv_mode=inv_mode, head_group=head_group)
    hgrn_body = functools.partial(_hgrn_kernel, nb=nb, sl=sl, n_valid=n_valid, has_init=has_init, mode=mode,
                                  head_group=head_group)
    n_g, n_h = len(gdn_in), len(hgrn_in)

    def both(*refs):
        ins, outs, scr = refs[:n_g + n_h], refs[n_g + n_h:n_g + n_h + 4], refs[n_g + n_h + 4:]
        gdn_body(*ins[:n_g], outs[0], outs[1], scr[0], scr[1])
        hgrn_body(*ins[n_g:], outs[2], outs[3], scr[2])

    state_scr = pltpu.VMEM((nb, N_HEADS, D_HEAD, D_HEAD), F32)
    o_shape = jax.ShapeDtypeStruct((nseq, length, width), out_dtype)
    o_gdn, s_gdn, o_hgrn, s_hgrn = pl.pallas_call(
        both,
        grid=grid,
        in_specs=gdn_specs + hgrn_specs,
        out_specs=[out_spec(width), state_spec, out_spec(width), state_spec],
        out_shape=[o_shape, state_shape, o_shape, state_shape],
        scratch_shapes=[state_scr, pltpu.VMEM((nb, SUBLANES, conv_ch), F32), state_scr],
        compiler_params=_params(("parallel", "arbitrary")),
    )(*gdn_in, *hgrn_in)
    return o_gdn, o_hgrn, s_gdn, s_hgrn


def _outproj_kernel(oap_ref, obp_ref, hpp_ref, oas_ref, obs_ref, hps_ref, wo_ref, hp2_ref, *, n_p):
    i = pl.program_id(0)
    half = oap_ref.shape[-1]

    def tile(oa, ob, hp):
        mix = _dot(oa.astype(BF16), wo_ref[:half, :]) + _dot(ob.astype(BF16), wo_ref[half:, :])
        hp2_ref[...] = hp + mix

    @pl.when(i < n_p)
    def _():
        tile(oap_ref[...], obp_ref[...], hpp_ref[...])

    @pl.when(i >= n_p)
    def _():
        tile(oas_ref[...], obs_ref[...], hps_ref[...])


def _outproj(oa_p, ob_p, hp_p, oa_s, ob_s, hp_s, w_out, tm):
    (tp, d), ts = hp_p.shape, hp_s.shape[0]
    half = oa_p.shape[1]
    n_p, n_s = tp // tm, ts // tm
    prow = lambda w: pl.BlockSpec((tm, w), lambda i: (jnp.minimum(i, n_p - 1), 0))
    srow = lambda w: pl.BlockSpec((tm, w), lambda i: (jnp.maximum(i - n_p, 0), 0))
    return pl.pallas_call(
        functools.partial(_outproj_kernel, n_p=n_p),
        grid=(n_p + n_s,),
        in_specs=[prow(half), prow(half), prow(d), srow(half), srow(half), srow(d),
                  pl.BlockSpec(w_out.shape, lambda i: (0, 0))],
        out_specs=pl.BlockSpec((tm, d), lambda i: (i, 0)),
        out_shape=jax.ShapeDtypeStruct((tp + ts, d), F32),
        compiler_params=_params(("parallel",)),
    )(oa_p, ob_p, hp_p, oa_s, ob_s, hp_s, w_out)


def _router_kernel(hp2_ref, n2_ref, wr_ref, br_ref, xloc_ref, ri_ref, rw_ref, m8_ref, *, tm):
    hp2 = hp2_ref[...]
    xn2 = hp2 * lax.rsqrt(jnp.mean(hp2 * hp2, axis=-1, keepdims=True) + EPS) * n2_ref[...]
    logits = _dot(xn2, wr_ref[...], "bf16x3") + br_ref[...]

    lane = lax.broadcasted_iota(jnp.int32, (tm, LANES), 1)
    lane_f = lane.astype(F32)
    far = float(4 * LANES)
    is_g = (lane >= N_EXPERTS) & (lane < N_EXPERTS + N_GROUPS)
    lg = jnp.where(is_g, logits, -jnp.inf)
    gmax = jnp.max(lg, axis=-1, keepdims=True)
    gsel = jnp.min(jnp.where(lg == gmax, lane_f, far), axis=-1, keepdims=True).astype(jnp.int32) - N_EXPERTS
    p_top = 1.0 / jnp.sum(jnp.where(is_g, jnp.exp(logits - gmax), 0.0), axis=-1, keepdims=True)
    in_grp = (lane < N_EXPERTS) & (lax.shift_right_logical(lane, 3) == gsel)
    le = jnp.where(in_grp, logits, -jnp.inf)
    m1 = jnp.max(le, axis=-1, keepdims=True)
    i1 = jnp.min(jnp.where(le == m1, lane_f, far), axis=-1, keepdims=True).astype(jnp.int32)
    le2 = jnp.where(lane == i1, -jnp.inf, le)
    m2 = jnp.max(le2, axis=-1, keepdims=True)
    i2 = jnp.min(jnp.where(le2 == m2, lane_f, far), axis=-1, keepdims=True).astype(jnp.int32)
    e2 = jnp.exp(m2 - m1)
    w1 = p_top / (1.0 + e2)
    w2 = p_top * e2 / (1.0 + e2)

    onehot = (lane == i1) | (lane == i2)
    onehot_f = jnp.where(onehot, 1.0, 0.0)
    tri = (lax.broadcasted_iota(jnp.int32, (tm, tm), 1) < lax.broadcasted_iota(jnp.int32, (tm, tm), 0))
    rank = _dot(jnp.where(tri, 1.0, 0.0).astype(BF16), onehot_f.astype(BF16))
    cnt = jnp.sum(onehot_f, axis=0, keepdims=True)
    m8 = jnp.floor((cnt + (SUBLANES - 1)) * (1.0 / SUBLANES))
    upper = (lax.broadcasted_iota(jnp.int32, (LANES, LANES), 0) < lax.broadcasted_iota(jnp.int32, (LANES, LANES), 1))
    goff = _dot(jnp.broadcast_to(m8, (SUBLANES, LANES)).astype(BF16), jnp.where(upper, 1.0, 0.0).astype(BF16))[0:1]
    local = goff * SUBLANES + rank
    lr1 = jnp.sum(jnp.where(lane == i1, rank, 0.0), axis=-1, keepdims=True)
    lr2 = jnp.sum(jnp.where(lane == i2, rank, 0.0), axis=-1, keepdims=True)
    lp1 = jnp.sum(jnp.where(lane == i1, local, 0.0), axis=-1, keepdims=True)
    lp2 = jnp.sum(jnp.where(lane == i2, local, 0.0), axis=-1, keepdims=True)
    lp_rows = jnp.where(lane == 0, lp1, jnp.where(lane == 1, lp2, -1.0)).T
    n_loc = xloc_ref.shape[0]
    row = lax.broadcasted_iota(jnp.int32, (n_loc, tm), 0).astype(F32)
    perm = (row == lp_rows[0:1, :]) | (row == lp_rows[1:2, :])
    xloc_ref[...] = _dot(jnp.where(perm, 1.0, 0.0).astype(BF16), xn2.astype(BF16))

    ri_ref[...] = jnp.where(lane == 0, i1, jnp.where(lane == 1, i2, jnp.where(
        lane == 2, lr1.astype(jnp.int32), jnp.where(lane == 3, lr2.astype(jnp.int32), 0))))
    rw_ref[...] = jnp.where(lane == 0, w1, jnp.where(lane == 1, w2, 0.0))
    m8_ref[...] = jnp.broadcast_to(m8, m8_ref.shape)


def _router(hp2, norm2_w, w_r, b_r, tm):
    t, d = hp2.shape
    n_tt = t // tm
    n_loc = _local_rows(tm)
    row = lambda w: pl.BlockSpec((tm, w), lambda i: (i, 0))
    const = lambda shape: pl.BlockSpec(shape, lambda i: (0,) * len(shape))
    return pl.pallas_call(
        functools.partial(_router_kernel, tm=tm),
        grid=(n_tt,),
        in_specs=[row(d), const((1, d)), const(w_r.shape), const((1, LANES))],
        out_specs=[pl.BlockSpec((n_loc, d), lambda i: (i, 0)), row(LANES), row(LANES),
                   pl.BlockSpec((None, SUBLANES, LANES), lambda i: (i, 0, 0))],
        out_shape=[jax.ShapeDtypeStruct((n_tt * n_loc, d), F32),
                   jax.ShapeDtypeStruct((t, LANES), jnp.int32), jax.ShapeDtypeStruct((t, LANES), F32),
                   jax.ShapeDtypeStruct((n_tt, SUBLANES, LANES), F32)],
        compiler_params=_params(("parallel",)),
    )(hp2, norm2_w.reshape(1, d), w_r, b_r)


def _local_rows(tm):
    return 2 * tm + SUBLANES * N_EXPERTS


def _start_row_gather(idx_ref, base, n, src_hbm, dst, sem):
    def issue(r, carry):
        pltpu.make_async_copy(src_hbm.at[pl.ds(idx_ref[base + r], 1), :], dst.at[pl.ds(r, 1), :], sem).start()
        return carry
    lax.fori_loop(0, n, issue, 0, unroll=8)


def _start_group_gather(idx_ref, base, n_groups, src_hbm, dst, sem):
    def issue(r, carry):
        src_row = pl.multiple_of(idx_ref[base + r] * SUBLANES, SUBLANES)
        dst_row = pl.multiple_of(r * SUBLANES, SUBLANES)
        pltpu.make_async_copy(src_hbm.at[pl.ds(src_row, SUBLANES), :], dst.at[pl.ds(dst_row, SUBLANES), :], sem).start()
        return carry
    lax.fori_loop(0, n_groups, issue, 0, unroll=8)


def _wait_row_gather(n, src_hbm, dst, sem):
    pltpu.make_async_copy(src_hbm.at[pl.ds(0, n), :], dst, sem).wait()


def _moe_kernel(te_ref, st_ref, nx_ref, nu_ref, x_hbm, wg_hbm, wu_hbm, wd_hbm, o_ref,
                xbuf, wg_f32, wu_f32, wd_f32, wg_scr, wu_scr, wd_scr, sems, wsems, *, tile):
    i = pl.program_id(0)
    n_used = nu_ref[0]
    slot = lax.rem(i, 2)
    weights = ((wg_hbm, wg_f32, wg_scr), (wu_hbm, wu_f32, wu_scr), (wd_hbm, wd_f32, wd_scr))

    def start_weights(e):
        for k, (w_hbm, w_f32, _) in enumerate(weights):
            pltpu.async_copy(w_hbm.at[e], w_f32, wsems.at[k], priority=1)

    @pl.when((i == 0) & (n_used > 0))
    def _():
        start_weights(te_ref[0])
        _start_group_gather(st_ref, 0, tile // SUBLANES, x_hbm, xbuf.at[0], sems.at[0])

    @pl.when(i < n_used)
    def _():
        @pl.when(i + 1 < n_used)
        def _():
            _start_group_gather(st_ref, (i + 1) * (tile // SUBLANES), tile // SUBLANES, x_hbm,
                                xbuf.at[1 - slot], sems.at[1 - slot])

        e = te_ref[i]

        @pl.when((i == 0) | (e != te_ref[jnp.maximum(i - 1, 0)]))
        def _():
            for k, (w_hbm, w_f32, w_scr) in enumerate(weights):
                pltpu.make_async_copy(w_hbm.at[0], w_f32, wsems.at[k]).wait()
                w_scr[...] = w_f32[...].astype(BF16)
            nxt = nx_ref[e]

            @pl.when(nxt < N_EXPERTS)
            def _():
                start_weights(nxt)

        _wait_row_gather(tile, x_hbm, xbuf.at[slot], sems.at[slot])
        x = xbuf[slot].astype(BF16)
        g = _dot(x, wg_scr[...])
        u = _dot(x, wu_scr[...])
        o_ref[...] = _dot((_silu(g) * u).astype(BF16), wd_scr[...])

    @pl.when(i >= n_used)
    def _():
        o_ref[...] = jnp.zeros_like(o_ref)


def _moe(xn2, w_gate, w_up, w_down, tile_expert, slot_token, next_expert, n_used, tile):
    d = xn2.shape[1]
    n_tiles = tile_expert.shape[0]
    de = w_gate.shape[2]
    hbm = pl.BlockSpec(memory_space=pl.ANY)
    grid_spec = pltpu.PrefetchScalarGridSpec(
        num_scalar_prefetch=4,
        grid=(n_tiles,),
        in_specs=[hbm, hbm, hbm, hbm],
        out_specs=pl.BlockSpec((tile, d), lambda i, te, st, nx, nu: (i, 0)),
        scratch_shapes=[pltpu.VMEM((2, tile, d), F32),
                        pltpu.VMEM((d, de), F32), pltpu.VMEM((d, de), F32), pltpu.VMEM((de, d), F32),
                        pltpu.VMEM((d, de), BF16), pltpu.VMEM((d, de), BF16), pltpu.VMEM((de, d), BF16),
                        pltpu.SemaphoreType.DMA((2,)), pltpu.SemaphoreType.DMA((3,))],
    )
    return pl.pallas_call(
        functools.partial(_moe_kernel, tile=tile),
        grid_spec=grid_spec,
        out_shape=jax.ShapeDtypeStruct((n_tiles * tile, d), F32),
        compiler_params=_params(("arbitrary",)),
    )(tile_expert, slot_token, next_expert, n_used, xn2, w_gate, w_up, w_down)


def _combine_kernel(p0_ref, p1_ref, ys_hbm, hp2_ref, rw_ref, fw_ref, o_ref, buf0, buf1, sems, *, tm, tile_off):
    i = pl.program_id(0)
    slot = lax.rem(i, 2)

    def start(step, s):
        base = (step + tile_off) * tm
        _start_row_gather(p0_ref, base, tm, ys_hbm, buf0.at[s], sems.at[0, s])
        _start_row_gather(p1_ref, base, tm, ys_hbm, buf1.at[s], sems.at[1, s])

    @pl.when(i == 0)
    def _():
        start(0, 0)

    @pl.when(i + 1 < pl.num_programs(0))
    def _():
        start(i + 1, 1 - slot)

    _wait_row_gather(tm, ys_hbm, buf0.at[slot], sems.at[0, slot])
    _wait_row_gather(tm, ys_hbm, buf1.at[slot], sems.at[1, slot])
    rw = rw_ref[...]
    y = hp2_ref[...] + rw[:, 0:1] * buf0[slot] + rw[:, 1:2] * buf1[slot]
    o_ref[...] = y * lax.rsqrt(jnp.mean(y * y, axis=-1, keepdims=True) + EPS) * fw_ref[...]


def _combine(ys, hp2, route_w, final_w, pos0, pos1, tm, tile_off, n_tiles):
    _, d = hp2.shape
    grid_spec = pltpu.PrefetchScalarGridSpec(
        num_scalar_prefetch=2,
        grid=(n_tiles,),
        in_specs=[pl.BlockSpec(memory_space=pl.ANY),
                  pl.BlockSpec((tm, d), lambda i, p0, p1: (i + tile_off, 0)),
                  pl.BlockSpec((tm, LANES), lambda i, p0, p1: (i + tile_off, 0)),
                  pl.BlockSpec((1, d), lambda i, p0, p1: (0, 0))],
        out_specs=pl.BlockSpec((tm, d), lambda i, p0, p1: (i, 0)),
        scratch_shapes=[pltpu.VMEM((2, tm, d), F32), pltpu.VMEM((2, tm, d), F32),
                        pltpu.SemaphoreType.DMA((2, 2))],
    )
    return pl.pallas_call(
        functools.partial(_combine_kernel, tm=tm, tile_off=tile_off),
        grid_spec=grid_spec,
        out_shape=jax.ShapeDtypeStruct((n_tiles * tm, d), F32),
        compiler_params=_params(("arbitrary",)),
    )(pos0, pos1, ys, hp2, route_w, final_w.reshape(1, d))


def kernel(x_prompt, x_sample, state_gdn, state_conv, state_hgrn, meta_tokens, norm1_w, w_in, conv_w, a_log,
           dt_bias, gdn_norm_w, lb_logits, hgrn_norm_w, w_out, norm2_w, w_router_group, b_router_group,
           w_router_expert, b_router_expert, w_gate, w_up, w_down, final_norm_w):
    bp, seq, d = x_prompt.shape
    bs, dec_seq, _ = x_sample.shape
    assert w_in.shape[0] == 1, "single-layer trunk"
    width = N_HEADS * D_HEAD
    conv_ch = 3 * width
    tile = 256
    tn = 512
    sl_s = SUBLANES
    nb_s = CHUNK // sl_s
    assert seq % CHUNK == 0 and N_META <= CHUNK and dec_seq <= sl_s and bs % nb_s == 0
    tp, ts, ts_pad = bp * seq, bs * dec_seq, bs * sl_s
    t_small = ts_pad + CHUNK
    tm = _row_tile(math.gcd(tp, ts), 256)
    tm_p = _row_tile(tp, 2048)

    xp = x_prompt.reshape(tp, d)
    x_small = jnp.concatenate([jnp.pad(x_sample, ((0, 0), (0, sl_s - dec_seq), (0, 0))).reshape(ts_pad, d),
                               jnp.zeros((CHUNK - N_META, d), F32), meta_tokens.astype(F32)], axis=0)

    wi = w_in[0]
    n_a = 4 * width // tn
    w_b = wi[:, 4 * width + 2 * N_HEADS:]
    w_ba = jnp.pad(wi[:, 4 * width:4 * width + 2 * N_HEADS], ((0, 0), (0, LANES - 2 * N_HEADS))).astype(BF16)
    n_cols = 8 * width

    xn_p = _rmsnorm(xp, norm1_w[0], BF16, _row_tile(tp, 512))
    xn_s = _rmsnorm(x_small, norm1_w[0], BF16, _row_tile(t_small, 1024))
    w_a = wi[:, :4 * width]
    proj_p = _inproj(xn_p, w_a, n_a, w_b, tm_p, tn)
    proj_s = _inproj(xn_s, w_a, n_a, w_b, t_small, tn)
    ba_p = _matmul(xn_p, w_ba, tm_p, LANES)
    ba_s = _matmul(xn_s, w_ba, t_small, LANES)

    pvec = jnp.zeros((2, LANES), F32)
    pvec = pvec.at[0, N_HEADS:2 * N_HEADS].set(a_log[0]).at[1, N_HEADS:2 * N_HEADS].set(dt_bias[0])
    lb = jnp.cumsum(jax.nn.softmax(lb_logits.astype(F32), axis=0), axis=0)[0].reshape(1, width)
    gdn_nw = gdn_norm_w[0].reshape(1, D_HEAD)
    hgrn_nw = hgrn_norm_w[0].reshape(1, D_HEAD)
    cw = conv_w[0]
    mix_args = (cw, pvec, gdn_nw, lb, hgrn_nw)

    _, _, sg_m, sh_m = _mixers(proj_s.reshape(t_small // CHUNK, CHUNK, n_cols),
                               ba_s.reshape(t_small // CHUNK, CHUNK, LANES), ts_pad // CHUNK, 1, 1, CHUNK, CHUNK,
                               *mix_args, None, None, None, False, F32, MIX_MODE, INV_MODE)
    conv_m = proj_s[t_small - SUBLANES:, :conv_ch].reshape(1, SUBLANES, conv_ch)
    oa_p, ob_p, sg_p, sh_p = _mixers(proj_p.reshape(bp, seq, n_cols), ba_p.reshape(bp, seq, LANES), 0, bp, 1,
                                     CHUNK, CHUNK, *mix_args, sg_m, conv_m, sh_m, True, BF16, MIX_MODE, INV_MODE)
    conv0 = jnp.pad(state_conv[0], ((0, 0), (SUBLANES - (CONV_W - 1), 0), (0, 0)))
    oa_s, ob_s, sg_s, sh_s = _mixers(proj_s.reshape(t_small // sl_s, sl_s, n_cols),
                                     ba_s.reshape(t_small // sl_s, sl_s, LANES), 0, bs, nb_s, sl_s, dec_seq,
                                     *mix_args, state_gdn[0], conv0, state_hgrn[0], False, F32, MIX_MODE, INV_MODE)

    w_r = jnp.concatenate([w_router_expert[0], w_router_group[0],
                           jnp.zeros((d, LANES - N_EXPERTS - N_GROUPS), F32)], axis=1)
    b_r = jnp.concatenate([b_router_expert[0], b_router_group[0],
                           jnp.zeros((LANES - N_EXPERTS - N_GROUPS,), F32)]).reshape(1, LANES)
    t = tp + ts
    hp2 = _outproj(oa_p.reshape(tp, width), ob_p.reshape(tp, width), xp,
                   oa_s[:, :dec_seq].reshape(ts, width), ob_s[:, :dec_seq].reshape(ts, width),
                   x_sample.reshape(ts, d), w_out[0].astype(BF16), _row_tile(math.gcd(tp, ts), 512))
    xloc, route_i, route_w, tile_m8 = _router(hp2, norm2_w[0], w_r, b_r, tm)

    n_tt = t // tm
    gpt = tile // SUBLANES
    loc_g = _local_rows(tm) // SUBLANES
    expert_ids = jnp.arange(N_EXPERTS, dtype=jnp.int32)
    m8 = tile_m8[:, 0, :N_EXPERTS].astype(jnp.int32)
    before = jnp.cumsum(m8, axis=0) - m8
    goff = jnp.cumsum(m8, axis=1) - m8
    groups = jnp.sum(m8, axis=0)
    padded_g = (groups + gpt - 1) // gpt * gpt
    ends_g = jnp.cumsum(padded_g)
    offs_g = ends_g - padded_g
    eid = route_i[:, 0:2]
    base_rows = SUBLANES * (offs_g[None, :] + before)
    base_tok = jnp.repeat(base_rows, tm, axis=0)
    pos = jnp.sum(jnp.where(eid[:, :, None] == expert_ids, base_tok[:, None, :], 0), axis=-1) + route_i[:, 2:4]

    n_tiles = -(-(2 * t + n_tt * N_EXPERTS * (SUBLANES - 1)) // tile) + N_EXPERTS
    tile_expert = jnp.minimum(
        jnp.sum((ends_g[None, :] <= (jnp.arange(n_tiles, dtype=jnp.int32) * gpt)[:, None]).astype(jnp.int32), axis=1),
        N_EXPERTS - 1)
    n_used = (ends_g[-1] // gpt).astype(jnp.int32).reshape(1)
    q = jnp.arange(n_tiles * gpt, dtype=jnp.int32)
    e_q = jnp.repeat(tile_expert, gpt)
    sel = (e_q[:, None] == expert_ids).astype(jnp.int32)
    u = q - sel @ offs_g
    run_end = sel @ (before + m8).T
    j_q = jnp.sum((run_end <= u[:, None]).astype(jnp.int32), axis=1)
    hit = (jnp.arange(n_tt, dtype=jnp.int32) == j_q[:, None]).astype(jnp.int32)
    src_in_tile = jnp.sum(hit * (sel @ (goff - before).T), axis=1) + u
    slot_group = jnp.where(j_q < n_tt, j_q * loc_g + src_in_tile, 0).astype(jnp.int32)

    later_active = (expert_ids[None, :] > expert_ids[:, None]) & (groups[None, :] > 0)
    next_expert = jnp.min(jnp.where(later_active, expert_ids[None, :], N_EXPERTS), axis=1).astype(jnp.int32)
    ys = _moe(xloc, w_gate[0], w_up[0], w_down[0], tile_expert, slot_group, next_expert, n_used, tile)
    pos0, pos1 = pos[:, 0], pos[:, 1]
    y_prompt = _combine(ys, hp2, route_w, final_norm_w, pos0, pos1, tm, 0, tp // tm).reshape(bp, seq, d)
    y_sample = _combine(ys, hp2, route_w, final_norm_w, pos0, pos1, tm, tp // tm, ts // tm).reshape(bs, dec_seq, d)

    conv_p = proj_p.reshape(bp, seq, n_cols)[:, seq - (CONV_W - 1):, :conv_ch]
    u_s = proj_s.reshape(t_small // sl_s, sl_s, n_cols)[:bs, :dec_seq, :conv_ch]
    conv_s = jnp.concatenate([state_conv[0], u_s], axis=1)[:, dec_seq:]
    return (y_prompt, y_sample, sg_p[None], conv_p[None], sh_p[None], sg_s[None], conv_s[None], sh_s[None])
```

```python
import functools
import math

import jax
import jax.numpy as jnp
from jax import lax
from jax.experimental import pallas as pl
from jax.experimental.pallas import tpu as pltpu

F32 = jnp.float32
BF16 = jnp.bfloat16

EPS = 1e-6
N_META = 16
CONV_W = 4
N_HEADS = 8
D_HEAD = 128
N_GROUPS = 4
EXPERTS_PER_GROUP = 8
N_EXPERTS = N_GROUPS * EXPERTS_PER_GROUP

LANES = 128
SUBLANES = 8
CHUNK = 64
SUB = 16
VMEM_LIMIT = 56 * 1024 * 1024
MIX_MODE = "bf16"
INV_MODE = "bf16"


def _sigmoid(x):
    return 0.5 * jnp.tanh(0.5 * x) + 0.5


def _silu(x):
    return x * _sigmoid(x)


def _softplus(x):
    return jnp.maximum(x, 0.0) + jnp.log1p(jnp.exp(-jnp.abs(x)))


def _split_bf16(a, pieces):
    out = []
    for _ in range(pieces - 1):
        hi = a.astype(BF16)
        out.append(hi)
        a = a - hi.astype(F32)
    out.append(a.astype(BF16))
    return out


def _mm(a, b, dims, mode):
    dg = functools.partial(lax.dot_general, dimension_numbers=(dims, ((), ())), preferred_element_type=F32)
    if mode == "bf16":
        return dg(a.astype(BF16), b.astype(BF16))
    assert mode == "bf16x3"
    ah, al = _split_bf16(a, 2)
    bh, bl = _split_bf16(b, 2)
    return dg(ah, bh) + dg(ah, bl) + dg(al, bh)


def _dot(a, b, mode="bf16"):
    return _mm(a, b, ((1,), (0,)), mode)


def _dot_nt(a, b, mode="bf16"):
    return _mm(a, b, ((1,), (1,)), mode)


def _dot_tn(a, b, mode="bf16"):
    return _mm(a, b, ((0,), (0,)), mode)


def _masked_cumsum(lmask, x):
    lm = lmask.astype(BF16)
    return sum(lax.dot_general(lm, p, (((1,), (0,)), ((), ())), preferred_element_type=F32)
               for p in _split_bf16(x, 3))


def _params(sem):
    return pltpu.CompilerParams(dimension_semantics=sem, vmem_limit_bytes=VMEM_LIMIT)


def _row_tile(n, target):
    best = max(c for c in range(16, min(n, target) + 1, 16) if n % c == 0)
    return best


def _rmsnorm_kernel(x_ref, w_ref, o_ref):
    x = x_ref[...]
    ms = jnp.mean(x * x, axis=-1, keepdims=True)
    o_ref[...] = (x * lax.rsqrt(ms + EPS) * w_ref[...]).astype(o_ref.dtype)


def _rmsnorm(x, w, out_dtype, tm):
    t, d = x.shape
    return pl.pallas_call(
        _rmsnorm_kernel,
        grid=(t // tm,),
        in_specs=[pl.BlockSpec((tm, d), lambda i: (i, 0)), pl.BlockSpec((1, d), lambda i: (0, 0))],
        out_specs=pl.BlockSpec((tm, d), lambda i: (i, 0)),
        out_shape=jax.ShapeDtypeStruct((t, d), out_dtype),
        compiler_params=_params(("parallel",)),
    )(x, w.reshape(1, d))


def _inproj_kernel(x_ref, w_ref, o_ref, w_scr):
    @pl.when(pl.program_id(1) == 0)
    def _():
        w_scr[...] = w_ref[...].astype(BF16)

    o_ref[...] = _dot(x_ref[...], w_scr[...])


def _inproj(x, w, tm, tn):
    t, k = x.shape
    n = w.shape[1]
    return pl.pallas_call(
        _inproj_kernel,
        grid=(n // tn, t // tm),
        in_specs=[pl.BlockSpec((tm, k), lambda j, i: (i, 0)), pl.BlockSpec((k, tn), lambda j, i: (0, j))],
        out_specs=pl.BlockSpec((tm, tn), lambda j, i: (i, j)),
        out_shape=jax.ShapeDtypeStruct((t, n), F32),
        scratch_shapes=[pltpu.VMEM((k, tn), BF16)],
        compiler_params=_params(("arbitrary", "arbitrary")),
    )(x, w)


def _matmul_kernel(x_ref, w_ref, o_ref):
    o_ref[...] = _dot(x_ref[...], w_ref[...])


def _matmul(x, w, tm, tn):
    t, k = x.shape
    n = w.shape[1]
    return pl.pallas_call(
        _matmul_kernel,
        grid=(n // tn, t // tm),
        in_specs=[pl.BlockSpec((tm, k), lambda j, i: (i, 0)), pl.BlockSpec((k, tn), lambda j, i: (0, j))],
        out_specs=pl.BlockSpec((tm, tn), lambda j, i: (i, j)),
        out_shape=jax.ShapeDtypeStruct((t, n), F32),
        compiler_params=_params(("parallel", "arbitrary")),
    )(x, w)


def _chunk_masks(nb, sl):
    r = nb * sl
    shift = int(math.log2(sl))
    ri = lax.broadcasted_iota(jnp.int32, (r, r), 0)
    ci = lax.broadcasted_iota(jnp.int32, (r, r), 1)
    same = lax.shift_right_logical(ri, shift) == lax.shift_right_logical(ci, shift)
    return same & (ci <= ri), same & (ci < ri)


def _row_valid(nb, sl, n_valid):
    rowid = lax.broadcasted_iota(jnp.int32, (nb * sl, 1), 0)
    return (rowid & (sl - 1)) < n_valid


def _last_row_bcast(x, nb, sl):
    c = x.shape[-1]
    x3 = x.reshape(nb, sl, c)
    return jnp.broadcast_to(x3[:, sl - 1:sl, :], (nb, sl, c)).reshape(nb * sl, c)


def _gated_rmsnorm(o, w, gate):
    return o * lax.rsqrt(jnp.mean(o * o, axis=-1, keepdims=True) + EPS) * w * _silu(gate)


def _gdn_kernel(*refs, nb, sl, n_valid, has_init, mode, inv_mode, head_group):
    if has_init:
        (qkv_ref, z_ref, ba_ref, cw_ref, pv_ref, nw_ref, s0_ref, c0_ref,
         o_ref, sout_ref, s_scr, carry_scr) = refs
    else:
        (qkv_ref, z_ref, ba_ref, cw_ref, pv_ref, nw_ref,
         o_ref, sout_ref, s_scr, carry_scr) = refs
    c = pl.program_id(1)
    r = nb * sl
    dh = D_HEAD

    @pl.when(c == 0)
    def _():
        if has_init:
            s_scr[...] = s0_ref[...]
            carry_scr[...] = c0_ref[...]
        else:
            s_scr[...] = jnp.zeros_like(s_scr)
            carry_scr[...] = jnp.zeros_like(carry_scr)

    incl, strict = _chunk_masks(nb, sl)
    lmask = jnp.where(incl, 1.0, 0.0)
    offdiag = jnp.where(strict, 1.0, 0.0)
    valid = _row_valid(nb, sl, n_valid)
    masked = n_valid < sl
    rowid = lax.broadcasted_iota(jnp.int32, (r, 1), 0)
    row_seq = lax.shift_right_logical(rowid, int(math.log2(sl)))
    n_sq = int(math.log2(sl)) - 1

    ba = ba_ref[...].reshape(r, LANES)
    pv = pv_ref[...]
    beta_all = _sigmoid(ba)
    g_all = -jnp.exp(pv[0:1]) * _softplus(ba + pv[1:2])
    if masked:
        g_all = jnp.where(valid, g_all, 0.0)
    gcum = _masked_cumsum(lmask, g_all)
    gcum_t = gcum.T
    glast = _last_row_bcast(gcum, nb, sl)
    cw = cw_ref[...]
    nw = nw_ref[...]

    def conv_slice(c0):
        u = qkv_ref[:, :, c0:c0 + dh]
        prev = carry_scr[:, :, c0:c0 + dh]
        full = jnp.concatenate([prev, u], axis=1)
        acc = None
        for j in range(CONV_W):
            off = SUBLANES - (CONV_W - 1) + j
            term = full[:, off:off + sl, :] * cw[j:j + 1, c0:c0 + dh]
            acc = term if acc is None else acc + term
        return _silu(acc).reshape(r, dh)

    for h0 in range(0, N_HEADS, head_group):
        heads = range(h0, h0 + head_group)
        qs, ks, vs, bcs, gcs, gls, egs, decays = [], [], [], [], [], [], [], []
        for h in heads:
            q = conv_slice(h * dh)
            k = conv_slice(N_HEADS * dh + h * dh)
            v = conv_slice(2 * N_HEADS * dh + h * dh)
            q = q * lax.rsqrt(jnp.sum(q * q, axis=-1, keepdims=True) + EPS) * (dh ** -0.5)
            k = k * lax.rsqrt(jnp.sum(k * k, axis=-1, keepdims=True) + EPS)
            if masked:
                q = jnp.where(valid, q, 0.0)
                k = jnp.where(valid, k, 0.0)
                v = jnp.where(valid, v, 0.0)
            gc = gcum[:, N_HEADS + h:N_HEADS + h + 1]
            gr = gcum_t[N_HEADS + h:N_HEADS + h + 1, :]
            qs.append(q)
            ks.append(k)
            vs.append(v)
            bcs.append(beta_all[:, h:h + 1])
            gcs.append(gc)
            gls.append(glast[:, N_HEADS + h:N_HEADS + h + 1])
            egs.append(jnp.exp(gc))
            decays.append(jnp.exp(jnp.minimum(gc - gr, 0.0)) * lmask)
        n = len(qs)
        qk_kk = [_dot_nt(jnp.concatenate([qs[i], ks[i]], axis=0), ks[i], mode) for i in range(n)]
        tm1 = [qk_kk[i][r:] * (decays[i] * offdiag) * (-bcs[i]) for i in range(n)]
        pw = list(tm1)
        for _ in range(n_sq):
            pw = [_dot(pw[i], pw[i], inv_mode) for i in range(n)]
            tm1 = [tm1[i] + pw[i] + _dot(tm1[i], pw[i], inv_mode) for i in range(n)]
        rhs = [jnp.concatenate([vs[i] * bcs[i], ks[i] * (bcs[i] * egs[i])], axis=1) for i in range(n)]
        uw = [rhs[i] + _dot(tm1[i], rhs[i], mode) for i in range(n)]
        vnew, ointer = [], []
        for i, h in enumerate(heads):
            u = uw[i][:, :dh]
            w = uw[i][:, dh:]
            qe = qs[i] * egs[i]
            vnew_parts, ointer_parts = [], []
            for b in range(nb):
                rows = slice(b * sl, (b + 1) * sl)
                ws = _dot(jnp.concatenate([w[rows], qe[rows]], axis=0), s_scr[b, h], mode)
                vnew_parts.append(u[rows] - ws[:sl])
                ointer_parts.append(ws[sl:])
            vnew.append(vnew_parts[0] if nb == 1 else jnp.concatenate(vnew_parts, axis=0))
            ointer.append(ointer_parts[0] if nb == 1 else jnp.concatenate(ointer_parts, axis=0))
        for i, h in enumerate(heads):
            attn = qk_kk[i][:r] * decays[i]
            o = ointer[i] + _dot(attn, vnew[i], mode)
            z = z_ref[:, :, h * dh:(h + 1) * dh].reshape(r, dh)
            o_ref[:, :, h * dh:(h + 1) * dh] = _gated_rmsnorm(o, nw, z).reshape(nb, sl, dh).astype(o_ref.dtype)
        for i, h in enumerate(heads):
            ktil = ks[i] * jnp.exp(gls[i] - gcs[i])
            for b in range(nb):
                kt_b = ktil if nb == 1 else jnp.where(row_seq == b, ktil, 0.0)
                gl_b = gls[i][b * sl:b * sl + 1, :]
                s_scr[b, h] = s_scr[b, h] * jnp.exp(gl_b) + _dot_tn(kt_b, vnew[i], mode)

    carry_scr[...] = qkv_ref[:, sl - SUBLANES:sl, :]

    @pl.when(c == pl.num_programs(1) - 1)
    def _():
        sout_ref[...] = s_scr[...]


def _hgrn_kernel(*refs, nb, sl, n_valid, has_init, mode, head_group):
    if has_init:
        (q_ref, f_ref, i_ref, g_ref, lb_ref, nw_ref, s0_ref, o_ref, sout_ref, s_scr) = refs
    else:
        (q_ref, f_ref, i_ref, g_ref, lb_ref, nw_ref, o_ref, sout_ref, s_scr) = refs
    c = pl.program_id(1)
    r = nb * sl
    dh = D_HEAD
    width = N_HEADS * dh

    @pl.when(c == 0)
    def _():
        if has_init:
            s_scr[...] = s0_ref[...]
        else:
            s_scr[...] = jnp.zeros_like(s_scr)

    incl, _ = _chunk_masks(nb, sl)
    lmask = jnp.where(incl, 1.0, 0.0)
    valid = _row_valid(nb, sl, n_valid)
    masked = n_valid < sl
    rowid = lax.broadcasted_iota(jnp.int32, (r, 1), 0)
    row_seq = lax.shift_right_logical(rowid, int(math.log2(sl)))
    sub = min(SUB, sl)
    nblk = r // sub
    sub_shift = int(math.log2(sub))
    ri = lax.broadcasted_iota(jnp.int32, (r, r), 0)
    ci = lax.broadcasted_iota(jnp.int32, (r, r), 1)
    same_blk = lax.shift_right_logical(ri, sub_shift) == lax.shift_right_logical(ci, sub_shift)
    diag_mask = jnp.where(incl & same_blk, 1.0, 0.0)
    cross_mask = jnp.where(incl & jnp.logical_not(same_blk), 1.0, 0.0)

    lb = lb_ref[...]
    f = lb + (1.0 - lb) * _sigmoid(f_ref[...].reshape(r, width))
    lf = jnp.log(f)
    k_all = 1.0 - f
    if masked:
        lf = jnp.where(valid, lf, 0.0)
        k_all = jnp.where(valid, k_all, 0.0)
    bcum = _masked_cumsum(lmask, lf)
    nw = nw_ref[...]

    def head_inputs(h):
        cols = slice(h * dh, (h + 1) * dh)
        q = _silu(q_ref[:, :, cols].reshape(r, dh)) * (dh ** -0.5)
        v = i_ref[:, :, cols].reshape(r, dh)
        if masked:
            v = jnp.where(valid, v, 0.0)
        return q, k_all[:, cols], v, bcum[:, cols]

    def intra_attn(q, k, bh):
        bmid = jnp.broadcast_to(bh.reshape(nblk, sub, dh)[:, sub // 2:sub // 2 + 1, :],
                                (nblk, sub, dh)).reshape(r, dh)
        attn = _dot_nt(q * jnp.exp(bh - bmid), k * jnp.exp(bmid - bh), mode) * diag_mask
        if sl > sub:
            parts = [jnp.zeros((sub, r), F32)]
            for blk in range(1, nblk):
                start = blk * sub
                bref = bh[start - 1:start, :]
                qc = q[start:start + sub] * jnp.exp(bh[start:start + sub] - bref)
                kc = k * jnp.exp(jnp.minimum(bref - bh, 0.0))
                parts.append(_dot_nt(qc, kc, mode))
            attn = attn + jnp.concatenate(parts, axis=0) * cross_mask
        return attn

    for h0 in range(0, N_HEADS, head_group):
        heads = range(h0, h0 + head_group)
        ins = [head_inputs(h) for h in heads]
        attns = [intra_attn(q, k, bh) for (q, k, v, bh) in ins]
        for i, h in enumerate(heads):
            q, k, v, bh = ins[i]
            qe = q * jnp.exp(bh)
            ointer_parts = []
            for b in range(nb):
                rows = slice(b * sl, (b + 1) * sl)
                ointer_parts.append(_dot(qe[rows], s_scr[b, h], mode))
            ointer = ointer_parts[0] if nb == 1 else jnp.concatenate(ointer_parts, axis=0)
            o = ointer + _dot(attns[i], v, mode)
            cols = slice(h * dh, (h + 1) * dh)
            gate = g_ref[:, :, cols].reshape(r, dh)
            o_ref[:, :, cols] = _gated_rmsnorm(o, nw, gate).reshape(nb, sl, dh).astype(o_ref.dtype)
        for i, h in enumerate(heads):
            q, k, v, bh = ins[i]
            blast = _last_row_bcast(bh, nb, sl)
            ktil = k * jnp.exp(blast - bh)
            pad = [jnp.zeros((LANES - r, dh), F32)] if r < LANES else []
            tr = jnp.concatenate([blast] + pad, axis=0).T
            for b in range(nb):
                kt_b = ktil if nb == 1 else jnp.where(row_seq == b, ktil, 0.0)
                dec_col = jnp.exp(tr[:, b * sl:b * sl + 1])
                s_scr[b, h] = s_scr[b, h] * dec_col + _dot_tn(kt_b, v, mode)

    @pl.when(c == pl.num_programs(1) - 1)
    def _():
        sout_ref[...] = s_scr[...]


def _mixers(proj3, ba3, blk_off, nseq, nb, sl, n_valid, conv_w, pvec, gdn_nw, lb, hgrn_nw,
            s_gdn0, conv0, s_hgrn0, shared_init, out_dtype, mode, inv_mode):
    length = proj3.shape[1]
    has_init = s_gdn0 is not None
    width = N_HEADS * D_HEAD
    conv_ch = 3 * width
    grid = (nseq // nb, length // sl)
    state_spec = pl.BlockSpec((nb, N_HEADS, D_HEAD, D_HEAD), lambda g, c: (g, 0, 0, 0))
    state_shape = jax.ShapeDtypeStruct((nseq, N_HEADS, D_HEAD, D_HEAD), F32)
    init_idx = (lambda g: 0) if shared_init else (lambda g: g)
    init_state_spec = pl.BlockSpec((nb, N_HEADS, D_HEAD, D_HEAD), lambda g, c: (init_idx(g), 0, 0, 0))
    head_group = N_HEADS if nb == 1 else N_HEADS // 2

    def col_spec(w, idx):
        return pl.BlockSpec((nb, sl, w), lambda g, c: (g + blk_off, c, idx))

    def out_spec(w):
        return pl.BlockSpec((nb, sl, w), lambda g, c: (g, c, 0))

    def const_spec(shape):
        return pl.BlockSpec(shape, lambda g, c: (0,) * len(shape))

    gdn_in = [proj3, proj3, ba3, conv_w, pvec, gdn_nw]
    gdn_specs = [col_spec(conv_ch, 0), col_spec(width, 3), col_spec(LANES, 0),
                 const_spec(conv_w.shape), const_spec(pvec.shape), const_spec(gdn_nw.shape)]
    if has_init:
        gdn_in += [s_gdn0, conv0]
        gdn_specs += [init_state_spec, pl.BlockSpec((nb, SUBLANES, conv_ch), lambda g, c: (init_idx(g), 0, 0))]
    hgrn_in = [proj3, proj3, proj3, proj3, lb, hgrn_nw]
    hgrn_specs = [col_spec(width, 4), col_spec(width, 5), col_spec(width, 6), col_spec(width, 7),
                  const_spec(lb.shape), const_spec(hgrn_nw.shape)]
    if has_init:
        hgrn_in += [s_hgrn0]
        hgrn_specs += [init_state_spec]

    gdn_body = functools.partial(_gdn_kernel, nb=nb, sl=sl, n_valid=n_valid, has_init=has_init, mode=mode,
                                 inv_mode=inv_mode, head_group=head_group)
    hgrn_body = functools.partial(_hgrn_kernel, nb=nb, sl=sl, n_valid=n_valid, has_init=has_init, mode=mode,
                                  head_group=head_group)
    n_g, n_h = len(gdn_in), len(hgrn_in)

    def both(*refs):
        ins, outs, scr = refs[:n_g + n_h], refs[n_g + n_h:n_g + n_h + 4], refs[n_g + n_h + 4:]
        gdn_body(*ins[:n_g], outs[0], outs[1], scr[0], scr[1])
        hgrn_body(*ins[n_g:], outs[2], outs[3], scr[2])

    state_scr = pltpu.VMEM((nb, N_HEADS, D_HEAD, D_HEAD), F32)
    o_shape = jax.ShapeDtypeStruct((nseq, length, width), out_dtype)
    o_gdn, s_gdn, o_hgrn, s_hgrn = pl.pallas_call(
        both,
        grid=grid,
        in_specs=gdn_specs + hgrn_specs,
        out_specs=[out_spec(width), state_spec, out_spec(width), state_spec],
        out_shape=[o_shape, state_shape, o_shape, state_shape],
        scratch_shapes=[state_scr, pltpu.VMEM((nb, SUBLANES, conv_ch), F32), state_scr],
        compiler_params=_params(("parallel", "arbitrary")),
    )(*gdn_in, *hgrn_in)
    return o_gdn, o_hgrn, s_gdn, s_hgrn


def _outproj_kernel(oap_ref, obp_ref, hpp_ref, oas_ref, obs_ref, hps_ref, wo_ref, hp2_ref, *, n_p):
    i = pl.program_id(0)
    half = oap_ref.shape[-1]

    def tile(oa, ob, hp):
        mix = _dot(oa.astype(BF16), wo_ref[:half, :]) + _dot(ob.astype(BF16), wo_ref[half:, :])
        hp2_ref[...] = hp + mix

    @pl.when(i < n_p)
    def _():
        tile(oap_ref[...], obp_ref[...], hpp_ref[...])

    @pl.when(i >= n_p)
    def _():
        tile(oas_ref[...], obs_ref[...], hps_ref[...])


def _outproj(oa_p, ob_p, hp_p, oa_s, ob_s, hp_s, w_out, tm):
    (tp, d), ts = hp_p.shape, hp_s.shape[0]
    half = oa_p.shape[1]
    n_p, n_s = tp // tm, ts // tm
    prow = lambda w: pl.BlockSpec((tm, w), lambda i: (jnp.minimum(i, n_p - 1), 0))
    srow = lambda w: pl.BlockSpec((tm, w), lambda i: (jnp.maximum(i - n_p, 0), 0))
    return pl.pallas_call(
        functools.partial(_outproj_kernel, n_p=n_p),
        grid=(n_p + n_s,),
        in_specs=[prow(half), prow(half), prow(d), srow(half), srow(half), srow(d),
                  pl.BlockSpec(w_out.shape, lambda i: (0, 0))],
        out_specs=pl.BlockSpec((tm, d), lambda i: (i, 0)),
        out_shape=jax.ShapeDtypeStruct((tp + ts, d), F32),
        compiler_params=_params(("parallel",)),
    )(oa_p, ob_p, hp_p, oa_s, ob_s, hp_s, w_out)


def _router_kernel(hp2_ref, n2_ref, wr_ref, br_ref, xloc_ref, ri_ref, rw_ref, m8_ref, *, tm):
    hp2 = hp2_ref[...]
    xn2 = hp2 * lax.rsqrt(jnp.mean(hp2 * hp2, axis=-1, keepdims=True) + EPS) * n2_ref[...]
    logits = _dot(xn2, wr_ref[...], "bf16x3") + br_ref[...]

    lane = lax.broadcasted_iota(jnp.int32, (tm, LANES), 1)
    lane_f = lane.astype(F32)
    far = float(4 * LANES)
    is_g = (lane >= N_EXPERTS) & (lane < N_EXPERTS + N_GROUPS)
    lg = jnp.where(is_g, logits, -jnp.inf)
    gmax = jnp.max(lg, axis=-1, keepdims=True)
    gsel = jnp.min(jnp.where(lg == gmax, lane_f, far), axis=-1, keepdims=True).astype(jnp.int32) - N_EXPERTS
    p_top = 1.0 / jnp.sum(jnp.where(is_g, jnp.exp(logits - gmax), 0.0), axis=-1, keepdims=True)
    in_grp = (lane < N_EXPERTS) & (lax.shift_right_logical(lane, 3) == gsel)
    le = jnp.where(in_grp, logits, -jnp.inf)
    m1 = jnp.max(le, axis=-1, keepdims=True)
    i1 = jnp.min(jnp.where(le == m1, lane_f, far), axis=-1, keepdims=True).astype(jnp.int32)
    le2 = jnp.where(lane == i1, -jnp.inf, le)
    m2 = jnp.max(le2, axis=-1, keepdims=True)
    i2 = jnp.min(jnp.where(le2 == m2, lane_f, far), axis=-1, keepdims=True).astype(jnp.int32)
    e2 = jnp.exp(m2 - m1)
    w1 = p_top / (1.0 + e2)
    w2 = p_top * e2 / (1.0 + e2)

    onehot = (lane == i1) | (lane == i2)
    onehot_f = jnp.where(onehot, 1.0, 0.0)
    tri = (lax.broadcasted_iota(jnp.int32, (tm, tm), 1) < lax.broadcasted_iota(jnp.int32, (tm, tm), 0))
    rank = _dot(jnp.where(tri, 1.0, 0.0).astype(BF16), onehot_f.astype(BF16))
    cnt = jnp.sum(onehot_f, axis=0, keepdims=True)
    m8 = jnp.floor((cnt + (SUBLANES - 1)) * (1.0 / SUBLANES))
    upper = (lax.broadcasted_iota(jnp.int32, (LANES, LANES), 0) < lax.broadcasted_iota(jnp.int32, (LANES, LANES), 1))
    goff = _dot(jnp.broadcast_to(m8, (SUBLANES, LANES)).astype(BF16), jnp.where(upper, 1.0, 0.0).astype(BF16))[0:1]
    local = goff * SUBLANES + rank
    lr1 = jnp.sum(jnp.where(lane == i1, rank, 0.0), axis=-1, keepdims=True)
    lr2 = jnp.sum(jnp.where(lane == i2, rank, 0.0), axis=-1, keepdims=True)
    lp1 = jnp.sum(jnp.where(lane == i1, local, 0.0), axis=-1, keepdims=True)
    lp2 = jnp.sum(jnp.where(lane == i2, local, 0.0), axis=-1, keepdims=True)
    lp_rows = jnp.where(lane == 0, lp1, jnp.where(lane == 1, lp2, -1.0)).T
    n_loc = xloc_ref.shape[0]
    row = lax.broadcasted_iota(jnp.int32, (n_loc, tm), 0).astype(F32)
    perm = (row == lp_rows[0:1, :]) | (row == lp_rows[1:2, :])
    xloc_ref[...] = _dot(jnp.where(perm, 1.0, 0.0).astype(BF16), xn2.astype(BF16))

    ri_ref[...] = jnp.where(lane == 0, i1, jnp.where(lane == 1, i2, jnp.where(
        lane == 2, lr1.astype(jnp.int32), jnp.where(lane == 3, lr2.astype(jnp.int32), 0))))
    rw_ref[...] = jnp.where(lane == 0, w1, jnp.where(lane == 1, w2, 0.0))
    m8_ref[...] = jnp.broadcast_to(m8, m8_ref.shape)


def _router(hp2, norm2_w, w_r, b_r, tm):
    t, d = hp2.shape
    n_tt = t // tm
    n_loc = _local_rows(tm)
    row = lambda w: pl.BlockSpec((tm, w), lambda i: (i, 0))
    const = lambda shape: pl.BlockSpec(shape, lambda i: (0,) * len(shape))
    return pl.pallas_call(
        functools.partial(_router_kernel, tm=tm),
        grid=(n_tt,),
        in_specs=[row(d), const((1, d)), const(w_r.shape), const((1, LANES))],
        out_specs=[pl.BlockSpec((n_loc, d), lambda i: (i, 0)), row(LANES), row(LANES),
                   pl.BlockSpec((None, SUBLANES, LANES), lambda i: (i, 0, 0))],
        out_shape=[jax.ShapeDtypeStruct((n_tt * n_loc, d), F32),
                   jax.ShapeDtypeStruct((t, LANES), jnp.int32), jax.ShapeDtypeStruct((t, LANES), F32),
                   jax.ShapeDtypeStruct((n_tt, SUBLANES, LANES), F32)],
        compiler_params=_params(("parallel",)),
    )(hp2, norm2_w.reshape(1, d), w_r, b_r)


def _local_rows(tm):
    return 2 * tm + SUBLANES * N_EXPERTS


def _start_row_gather(idx_ref, base, n, src_hbm, dst, sem):
    def issue(r, carry):
        pltpu.make_async_copy(src_hbm.at[pl.ds(idx_ref[base + r], 1), :], dst.at[pl.ds(r, 1), :], sem).start()
        return carry
    lax.fori_loop(0, n, issue, 0, unroll=8)


def _start_group_gather(idx_ref, base, n_groups, src_hbm, dst, sem):
    def issue(r, carry):
        src_row = pl.multiple_of(idx_ref[base + r] * SUBLANES, SUBLANES)
        dst_row = pl.multiple_of(r * SUBLANES, SUBLANES)
        pltpu.make_async_copy(src_hbm.at[pl.ds(src_row, SUBLANES), :], dst.at[pl.ds(dst_row, SUBLANES), :], sem).start()
        return carry
    lax.fori_loop(0, n_groups, issue, 0, unroll=8)


def _wait_row_gather(n, src_hbm, dst, sem):
    pltpu.make_async_copy(src_hbm.at[pl.ds(0, n), :], dst, sem).wait()


def _moe_kernel(te_ref, st_ref, nx_ref, nu_ref, x_hbm, wg_hbm, wu_hbm, wd_hbm, o_ref,
                xbuf, wg_f32, wu_f32, wd_f32, wg_scr, wu_scr, wd_scr, sems, wsems, *, tile):
    i = pl.program_id(0)
    n_used = nu_ref[0]
    slot = lax.rem(i, 2)
    weights = ((wg_hbm, wg_f32, wg_scr), (wu_hbm, wu_f32, wu_scr), (wd_hbm, wd_f32, wd_scr))

    def start_weights(e):
        for k, (w_hbm, w_f32, _) in enumerate(weights):
            pltpu.async_copy(w_hbm.at[e], w_f32, wsems.at[k], priority=1)

    @pl.when((i == 0) & (n_used > 0))
    def _():
        start_weights(te_ref[0])
        _start_group_gather(st_ref, 0, tile // SUBLANES, x_hbm, xbuf.at[0], sems.at[0])

    @pl.when(i < n_used)
    def _():
        @pl.when(i + 1 < n_used)
        def _():
            _start_group_gather(st_ref, (i + 1) * (tile // SUBLANES), tile // SUBLANES, x_hbm,
                                xbuf.at[1 - slot], sems.at[1 - slot])

        e = te_ref[i]

        @pl.when((i == 0) | (e != te_ref[jnp.maximum(i - 1, 0)]))
        def _():
            for k, (w_hbm, w_f32, w_scr) in enumerate(weights):
                pltpu.make_async_copy(w_hbm.at[0], w_f32, wsems.at[k]).wait()
                w_scr[...] = w_f32[...].astype(BF16)
            nxt = nx_ref[e]

            @pl.when(nxt < N_EXPERTS)
            def _():
                start_weights(nxt)

        _wait_row_gather(tile, x_hbm, xbuf.at[slot], sems.at[slot])
        x = xbuf[slot].astype(BF16)
        g = _dot(x, wg_scr[...])
        u = _dot(x, wu_scr[...])
        o_ref[...] = _dot((_silu(g) * u).astype(BF16), wd_scr[...])

    @pl.when(i >= n_used)
    def _():
        o_ref[...] = jnp.zeros_like(o_ref)


def _moe(xn2, w_gate, w_up, w_down, tile_expert, slot_token, next_expert, n_used, tile):
    d = xn2.shape[1]
    n_tiles = tile_expert.shape[0]
    de = w_gate.shape[2]
    hbm = pl.BlockSpec(memory_space=pl.ANY)
    grid_spec = pltpu.PrefetchScalarGridSpec(
        num_scalar_prefetch=4,
        grid=(n_tiles,),
        in_specs=[hbm, hbm, hbm, hbm],
        out_specs=pl.BlockSpec((tile, d), lambda i, te, st, nx, nu: (i, 0)),
        scratch_shapes=[pltpu.VMEM((2, tile, d), F32),
                        pltpu.VMEM((d, de), F32), pltpu.VMEM((d, de), F32), pltpu.VMEM((de, d), F32),
                        pltpu.VMEM((d, de), BF16), pltpu.VMEM((d, de), BF16), pltpu.VMEM((de, d), BF16),
                        pltpu.SemaphoreType.DMA((2,)), pltpu.SemaphoreType.DMA((3,))],
    )
    return pl.pallas_call(
        functools.partial(_moe_kernel, tile=tile),
        grid_spec=grid_spec,
        out_shape=jax.ShapeDtypeStruct((n_tiles * tile, d), F32),
        compiler_params=_params(("arbitrary",)),
    )(tile_expert, slot_token, next_expert, n_used, xn2, w_gate, w_up, w_down)


def _combine_kernel(p0_ref, p1_ref, ys_hbm, hp2_ref, rw_ref, fw_ref, o_ref, buf0, buf1, sems, *, tm, tile_off):
    i = pl.program_id(0)
    slot = lax.rem(i, 2)

    def start(step, s):
        base = (step + tile_off) * tm
        _start_row_gather(p0_ref, base, tm, ys_hbm, buf0.at[s], sems.at[0, s])
        _start_row_gather(p1_ref, base, tm, ys_hbm, buf1.at[s], sems.at[1, s])

    @pl.when(i == 0)
    def _():
        start(0, 0)

    @pl.when(i + 1 < pl.num_programs(0))
    def _():
        start(i + 1, 1 - slot)

    _wait_row_gather(tm, ys_hbm, buf0.at[slot], sems.at[0, slot])
    _wait_row_gather(tm, ys_hbm, buf1.at[slot], sems.at[1, slot])
    rw = rw_ref[...]
    y = hp2_ref[...] + rw[:, 0:1] * buf0[slot] + rw[:, 1:2] * buf1[slot]
    o_ref[...] = y * lax.rsqrt(jnp.mean(y * y, axis=-1, keepdims=True) + EPS) * fw_ref[...]


def _combine(ys, hp2, route_w, final_w, pos0, pos1, tm, tile_off, n_tiles):
    _, d = hp2.shape
    grid_spec = pltpu.PrefetchScalarGridSpec(
        num_scalar_prefetch=2,
        grid=(n_tiles,),
        in_specs=[pl.BlockSpec(memory_space=pl.ANY),
                  pl.BlockSpec((tm, d), lambda i, p0, p1: (i + tile_off, 0)),
                  pl.BlockSpec((tm, LANES), lambda i, p0, p1: (i + tile_off, 0)),
                  pl.BlockSpec((1, d), lambda i, p0, p1: (0, 0))],
        out_specs=pl.BlockSpec((tm, d), lambda i, p0, p1: (i, 0)),
        scratch_shapes=[pltpu.VMEM((2, tm, d), F32), pltpu.VMEM((2, tm, d), F32),
                        pltpu.SemaphoreType.DMA((2, 2))],
    )
    return pl.pallas_call(
        functools.partial(_combine_kernel, tm=tm, tile_off=tile_off),
        grid_spec=grid_spec,
        out_shape=jax.ShapeDtypeStruct((n_tiles * tm, d), F32),
        compiler_params=_params(("arbitrary",)),
    )(pos0, pos1, ys, hp2, route_w, final_w.reshape(1, d))


def kernel(x_prompt, x_sample, state_gdn, state_conv, state_hgrn, meta_tokens, norm1_w, w_in, conv_w, a_log,
           dt_bias, gdn_norm_w, lb_logits, hgrn_norm_w, w_out, norm2_w, w_router_group, b_router_group,
           w_router_expert, b_router_expert, w_gate, w_up, w_down, final_norm_w):
    bp, seq, d = x_prompt.shape
    bs, dec_seq, _ = x_sample.shape
    assert w_in.shape[0] == 1, "single-layer trunk"
    width = N_HEADS * D_HEAD
    conv_ch = 3 * width
    tile = 256
    tn = 1024
    sl_s = SUBLANES
    nb_s = CHUNK // sl_s
    assert seq % CHUNK == 0 and N_META <= CHUNK and dec_seq <= sl_s and bs % nb_s == 0
    tp, ts, ts_pad = bp * seq, bs * dec_seq, bs * sl_s
    t_small = ts_pad + CHUNK
    tm = _row_tile(math.gcd(tp, ts), 256)
    tm_p = _row_tile(tp, 1024)

    xp = x_prompt.reshape(tp, d)
    x_small = jnp.concatenate([jnp.pad(x_sample, ((0, 0), (0, sl_s - dec_seq), (0, 0))).reshape(ts_pad, d),
                               jnp.zeros((CHUNK - N_META, d), F32), meta_tokens.astype(F32)], axis=0)

    wi = w_in[0]
    w_main = jnp.concatenate([wi[:, :4 * width], wi[:, 4 * width + 2 * N_HEADS:]], axis=1)
    w_ba = jnp.pad(wi[:, 4 * width:4 * width + 2 * N_HEADS], ((0, 0), (0, LANES - 2 * N_HEADS))).astype(BF16)
    n_cols = 8 * width

    xn_p = _rmsnorm(xp, norm1_w[0], BF16, _row_tile(tp, 512))
    xn_s = _rmsnorm(x_small, norm1_w[0], BF16, _row_tile(t_small, 1024))
    proj_p = _inproj(xn_p, w_main, tm_p, tn)
    proj_s = _inproj(xn_s, w_main, t_small, tn)
    ba_p = _matmul(xn_p, w_ba, tm_p, LANES)
    ba_s = _matmul(xn_s, w_ba, t_small, LANES)

    pvec = jnp.zeros((2, LANES), F32)
    pvec = pvec.at[0, N_HEADS:2 * N_HEADS].set(a_log[0]).at[1, N_HEADS:2 * N_HEADS].set(dt_bias[0])
    lb = jnp.cumsum(jax.nn.softmax(lb_logits.astype(F32), axis=0), axis=0)[0].reshape(1, width)
    gdn_nw = gdn_norm_w[0].reshape(1, D_HEAD)
    hgrn_nw = hgrn_norm_w[0].reshape(1, D_HEAD)
    cw = conv_w[0]
    mix_args = (cw, pvec, gdn_nw, lb, hgrn_nw)

    _, _, sg_m, sh_m = _mixers(proj_s.reshape(t_small // CHUNK, CHUNK, n_cols),
                               ba_s.reshape(t_small // CHUNK, CHUNK, LANES), ts_pad // CHUNK, 1, 1, CHUNK, CHUNK,
                               *mix_args, None, None, None, False, F32, MIX_MODE, INV_MODE)
    conv_m = proj_s[t_small - SUBLANES:, :conv_ch].reshape(1, SUBLANES, conv_ch)
    oa_p, ob_p, sg_p, sh_p = _mixers(proj_p.reshape(bp, seq, n_cols), ba_p.reshape(bp, seq, LANES), 0, bp, 1,
                                     CHUNK, CHUNK, *mix_args, sg_m, conv_m, sh_m, True, BF16, MIX_MODE, INV_MODE)
    conv0 = jnp.pad(state_conv[0], ((0, 0), (SUBLANES - (CONV_W - 1), 0), (0, 0)))
    oa_s, ob_s, sg_s, sh_s = _mixers(proj_s.reshape(t_small // sl_s, sl_s, n_cols),
                                     ba_s.reshape(t_small // sl_s, sl_s, LANES), 0, bs, nb_s, sl_s, dec_seq,
                                     *mix_args, state_gdn[0], conv0, state_hgrn[0], False, F32, MIX_MODE, INV_MODE)

    w_r = jnp.concatenate([w_router_expert[0], w_router_group[0],
                           jnp.zeros((d, LANES - N_EXPERTS - N_GROUPS), F32)], axis=1)
    b_r = jnp.concatenate([b_router_expert[0], b_router_group[0],
                           jnp.zeros((LANES - N_EXPERTS - N_GROUPS,), F32)]).reshape(1, LANES)
    t = tp + ts
    hp2 = _outproj(oa_p.reshape(tp, width), ob_p.reshape(tp, width), xp,
                   oa_s[:, :dec_seq].reshape(ts, width), ob_s[:, :dec_seq].reshape(ts, width),
                   x_sample.reshape(ts, d), w_out[0].astype(BF16), _row_tile(math.gcd(tp, ts), 512))
    xloc, route_i, route_w, tile_m8 = _router(hp2, norm2_w[0], w_r, b_r, tm)

    n_tt = t // tm
    gpt = tile // SUBLANES
    loc_g = _local_rows(tm) // SUBLANES
    expert_ids = jnp.arange(N_EXPERTS, dtype=jnp.int32)
    m8 = tile_m8[:, 0, :N_EXPERTS].astype(jnp.int32)
    before = jnp.cumsum(m8, axis=0) - m8
    goff = jnp.cumsum(m8, axis=1) - m8
    groups = jnp.sum(m8, axis=0)
    padded_g = (groups + gpt - 1) // gpt * gpt
    ends_g = jnp.cumsum(padded_g)
    offs_g = ends_g - padded_g
    eid = route_i[:, 0:2]
    base_rows = SUBLANES * (offs_g[None, :] + before)
    base_tok = jnp.repeat(base_rows, tm, axis=0)
    pos = jnp.sum(jnp.where(eid[:, :, None] == expert_ids, base_tok[:, None, :], 0), axis=-1) + route_i[:, 2:4]

    n_tiles = -(-(2 * t + n_tt * N_EXPERTS * (SUBLANES - 1)) // tile) + N_EXPERTS
    tile_expert = jnp.minimum(
        jnp.sum((ends_g[None, :] <= (jnp.arange(n_tiles, dtype=jnp.int32) * gpt)[:, None]).astype(jnp.int32), axis=1),
        N_EXPERTS - 1)
    n_used = (ends_g[-1] // gpt).astype(jnp.int32).reshape(1)
    q = jnp.arange(n_tiles * gpt, dtype=jnp.int32)
    e_q = jnp.repeat(tile_expert, gpt)
    sel = (e_q[:, None] == expert_ids).astype(jnp.int32)
    u = q - sel @ offs_g
    run_end = sel @ (before + m8).T
    j_q = jnp.sum((run_end <= u[:, None]).astype(jnp.int32), axis=1)
    hit = (jnp.arange(n_tt, dtype=jnp.int32) == j_q[:, None]).astype(jnp.int32)
    src_in_tile = jnp.sum(hit * (sel @ (goff - before).T), axis=1) + u
    slot_group = jnp.where(j_q < n_tt, j_q * loc_g + src_in_tile, 0).astype(jnp.int32)

    later_active = (expert_ids[None, :] > expert_ids[:, None]) & (groups[None, :] > 0)
    next_expert = jnp.min(jnp.where(later_active, expert_ids[None, :], N_EXPERTS), axis=1).astype(jnp.int32)
    ys = _moe(xloc, w_gate[0], w_up[0], w_down[0], tile_expert, slot_group, next_expert, n_used, tile)
    pos0, pos1 = pos[:, 0], pos[:, 1]
    y_prompt = _combine(ys, hp2, route_w, final_norm_w, pos0, pos1, tm, 0, tp // tm).reshape(bp, seq, d)
    y_sample = _combine(ys, hp2, route_w, final_norm_w, pos0, pos1, tm, tp // tm, ts // tm).reshape(bs, dec_seq, d)

    conv_p = proj_p.reshape(bp, seq, n_cols)[:, seq - (CONV_W - 1):, :conv_ch]
    u_s = proj_s.reshape(t_small // sl_s, sl_s, n_cols)[:bs, :dec_seq, :conv_ch]
    conv_s = jnp.concatenate([state_conv[0], u_s], axis=1)[:, dec_seq:]
    return (y_prompt, y_sample, sg_p[None], conv_p[None], sh_p[None], sg_s[None], conv_s[None], sh_s[None])
```

```python
import functools
import math

import jax
import jax.numpy as jnp
from jax import lax
from jax.experimental import pallas as pl
from jax.experimental.pallas import tpu as pltpu

F32 = jnp.float32
BF16 = jnp.bfloat16

EPS = 1e-6
N_META = 16
CONV_W = 4
N_HEADS = 8
D_HEAD = 128
N_GROUPS = 4
EXPERTS_PER_GROUP = 8
N_EXPERTS = N_GROUPS * EXPERTS_PER_GROUP

LANES = 128
SUBLANES = 8
CHUNK = 64
SUB = 16
VMEM_LIMIT = 56 * 1024 * 1024
NEG_BIG = -1e30
MIX_MODE = "bf16"
INV_MODE = "bf16"


def _sigmoid(x):
    return 0.5 * jnp.tanh(0.5 * x) + 0.5


def _silu(x):
    return x * _sigmoid(x)


def _softplus(x):
    return jnp.maximum(x, 0.0) + jnp.log1p(jnp.exp(-jnp.abs(x)))


def _split_bf16(a, pieces):
    out = []
    for _ in range(pieces - 1):
        hi = a.astype(BF16)
        out.append(hi)
        a = a - hi.astype(F32)
    out.append(a.astype(BF16))
    return out


def _mm(a, b, dims, mode):
    dg = functools.partial(lax.dot_general, dimension_numbers=(dims, ((), ())), preferred_element_type=F32)
    if mode == "bf16":
        return dg(a.astype(BF16), b.astype(BF16))
    assert mode == "bf16x3"
    ah, al = _split_bf16(a, 2)
    bh, bl = _split_bf16(b, 2)
    return dg(ah, bh) + dg(ah, bl) + dg(al, bh)


def _dot(a, b, mode="bf16"):
    return _mm(a, b, ((1,), (0,)), mode)


def _dot_nt(a, b, mode="bf16"):
    return _mm(a, b, ((1,), (1,)), mode)


def _dot_tn(a, b, mode="bf16"):
    return _mm(a, b, ((0,), (0,)), mode)


def _masked_cumsum(lmask, x):
    lm = lmask.astype(BF16)
    return sum(lax.dot_general(lm, p, (((1,), (0,)), ((), ())), preferred_element_type=F32)
               for p in _split_bf16(x, 3))


def _params(sem):
    return pltpu.CompilerParams(dimension_semantics=sem, vmem_limit_bytes=VMEM_LIMIT)


def _row_tile(n, target):
    best = max(c for c in range(16, min(n, target) + 1, 16) if n % c == 0)
    return best


def _rmsnorm_kernel(x_ref, w_ref, o_ref):
    x = x_ref[...]
    ms = jnp.mean(x * x, axis=-1, keepdims=True)
    o_ref[...] = (x * lax.rsqrt(ms + EPS) * w_ref[...]).astype(o_ref.dtype)


def _rmsnorm(x, w, out_dtype, tm):
    t, d = x.shape
    return pl.pallas_call(
        _rmsnorm_kernel,
        grid=(t // tm,),
        in_specs=[pl.BlockSpec((tm, d), lambda i: (i, 0)), pl.BlockSpec((1, d), lambda i: (0, 0))],
        out_specs=pl.BlockSpec((tm, d), lambda i: (i, 0)),
        out_shape=jax.ShapeDtypeStruct((t, d), out_dtype),
        compiler_params=_params(("parallel",)),
    )(x, w.reshape(1, d))


def _inproj_kernel(x_ref, wa_ref, wb_ref, o_ref, w_scr, *, n_a):
    j = pl.program_id(0)
    i = pl.program_id(1)

    @pl.when((i == 0) & (j < n_a))
    def _():
        w_scr[...] = wa_ref[...].astype(BF16)

    @pl.when((i == 0) & (j >= n_a))
    def _():
        w_scr[...] = wb_ref[...].astype(BF16)

    o_ref[...] = _dot(x_ref[...], w_scr[...])


def _inproj(x, w_a, n_a, w_b, tm, tn):
    t, k = x.shape
    n_b = w_b.shape[1] // tn
    return pl.pallas_call(
        functools.partial(_inproj_kernel, n_a=n_a),
        grid=(n_a + n_b, t // tm),
        in_specs=[pl.BlockSpec((tm, k), lambda j, i: (i, 0)),
                  pl.BlockSpec((k, tn), lambda j, i: (0, jnp.minimum(j, n_a - 1))),
                  pl.BlockSpec((k, tn), lambda j, i: (0, jnp.maximum(j - n_a, 0)))],
        out_specs=pl.BlockSpec((tm, tn), lambda j, i: (i, j)),
        out_shape=jax.ShapeDtypeStruct((t, (n_a + n_b) * tn), F32),
        scratch_shapes=[pltpu.VMEM((k, tn), BF16)],
        compiler_params=_params(("arbitrary", "arbitrary")),
    )(x, w_a, w_b)


def _matmul_kernel(x_ref, w_ref, o_ref):
    o_ref[...] = _dot(x_ref[...], w_ref[...])


def _matmul(x, w, tm, tn):
    t, k = x.shape
    n = w.shape[1]
    return pl.pallas_call(
        _matmul_kernel,
        grid=(n // tn, t // tm),
        in_specs=[pl.BlockSpec((tm, k), lambda j, i: (i, 0)), pl.BlockSpec((k, tn), lambda j, i: (0, j))],
        out_specs=pl.BlockSpec((tm, tn), lambda j, i: (i, j)),
        out_shape=jax.ShapeDtypeStruct((t, n), F32),
        compiler_params=_params(("parallel", "arbitrary")),
    )(x, w)


def _chunk_masks(nb, sl):
    r = nb * sl
    shift = int(math.log2(sl))
    ri = lax.broadcasted_iota(jnp.int32, (r, r), 0)
    ci = lax.broadcasted_iota(jnp.int32, (r, r), 1)
    same = lax.shift_right_logical(ri, shift) == lax.shift_right_logical(ci, shift)
    return same & (ci <= ri), same & (ci < ri)


def _row_valid(nb, sl, n_valid):
    rowid = lax.broadcasted_iota(jnp.int32, (nb * sl, 1), 0)
    return (rowid & (sl - 1)) < n_valid


def _last_row_bcast(x, nb, sl):
    c = x.shape[-1]
    x3 = x.reshape(nb, sl, c)
    return jnp.broadcast_to(x3[:, sl - 1:sl, :], (nb, sl, c)).reshape(nb * sl, c)


def _gated_rmsnorm(o, w, gate):
    return o * lax.rsqrt(jnp.mean(o * o, axis=-1, keepdims=True) + EPS) * w * _silu(gate)


def _gdn_kernel(*refs, nb, sl, n_valid, has_init, mode, inv_mode, head_group):
    if has_init:
        (qkv_ref, z_ref, ba_ref, cw_ref, pv_ref, nw_ref, s0_ref, c0_ref,
         o_ref, sout_ref, s_scr, carry_scr) = refs
    else:
        (qkv_ref, z_ref, ba_ref, cw_ref, pv_ref, nw_ref,
         o_ref, sout_ref, s_scr, carry_scr) = refs
    c = pl.program_id(1)
    r = nb * sl
    dh = D_HEAD

    @pl.when(c == 0)
    def _():
        if has_init:
            s_scr[...] = s0_ref[...]
            carry_scr[...] = c0_ref[...]
        else:
            s_scr[...] = jnp.zeros_like(s_scr)
            carry_scr[...] = jnp.zeros_like(carry_scr)

    incl, strict = _chunk_masks(nb, sl)
    lmask = jnp.where(incl, 1.0, 0.0)
    offdiag = jnp.where(strict, 1.0, 0.0)
    valid = _row_valid(nb, sl, n_valid)
    masked = n_valid < sl
    rowid = lax.broadcasted_iota(jnp.int32, (r, 1), 0)
    row_seq = lax.shift_right_logical(rowid, int(math.log2(sl)))
    n_sq = int(math.log2(sl)) - 1

    ba = ba_ref[...].reshape(r, LANES)
    pv = pv_ref[...]
    beta_all = _sigmoid(ba)
    g_all = -jnp.exp(pv[0:1]) * _softplus(ba + pv[1:2])
    if masked:
        g_all = jnp.where(valid, g_all, 0.0)
    gcum = _masked_cumsum(lmask, g_all)
    gcum_t = gcum.T
    glast = _last_row_bcast(gcum, nb, sl)
    cw = cw_ref[...]
    nw = nw_ref[...]

    def conv_slice(c0):
        u = qkv_ref[:, :, c0:c0 + dh]
        prev = carry_scr[:, :, c0:c0 + dh]
        full = jnp.concatenate([prev, u], axis=1)
        acc = None
        for j in range(CONV_W):
            off = SUBLANES - (CONV_W - 1) + j
            term = full[:, off:off + sl, :] * cw[j:j + 1, c0:c0 + dh]
            acc = term if acc is None else acc + term
        return _silu(acc).reshape(r, dh)

    for h0 in range(0, N_HEADS, head_group):
        heads = range(h0, h0 + head_group)
        qs, ks, vs, bcs, gcs, gls, egs, decays = [], [], [], [], [], [], [], []
        for h in heads:
            q = conv_slice(h * dh)
            k = conv_slice(N_HEADS * dh + h * dh)
            v = conv_slice(2 * N_HEADS * dh + h * dh)
            q = q * lax.rsqrt(jnp.sum(q * q, axis=-1, keepdims=True) + EPS) * (dh ** -0.5)
            k = k * lax.rsqrt(jnp.sum(k * k, axis=-1, keepdims=True) + EPS)
            if masked:
                q = jnp.where(valid, q, 0.0)
                k = jnp.where(valid, k, 0.0)
                v = jnp.where(valid, v, 0.0)
            gc = gcum[:, N_HEADS + h:N_HEADS + h + 1]
            gr = gcum_t[N_HEADS + h:N_HEADS + h + 1, :]
            qs.append(q)
            ks.append(k)
            vs.append(v)
            bcs.append(beta_all[:, h:h + 1])
            gcs.append(gc)
            gls.append(glast[:, N_HEADS + h:N_HEADS + h + 1])
            egs.append(jnp.exp(gc))
            decays.append(jnp.exp(jnp.where(incl, gc - gr, NEG_BIG)))
        n = len(qs)
        qk_kk = [_dot_nt(jnp.concatenate([qs[i], ks[i]], axis=0), ks[i], mode) for i in range(n)]
        tm1 = [qk_kk[i][r:] * (decays[i] * offdiag) * (-bcs[i]) for i in range(n)]
        pw = list(tm1)
        for _ in range(n_sq):
            pw = [_dot(pw[i], pw[i], inv_mode) for i in range(n)]
            tm1 = [tm1[i] + pw[i] + _dot(tm1[i], pw[i], inv_mode) for i in range(n)]
        rhs = [jnp.concatenate([vs[i] * bcs[i], ks[i] * (bcs[i] * egs[i])], axis=1) for i in range(n)]
        uw = [rhs[i] + _dot(tm1[i], rhs[i], mode) for i in range(n)]
        vnew, ointer = [], []
        for i, h in enumerate(heads):
            u = uw[i][:, :dh]
            w = uw[i][:, dh:]
            qe = qs[i] * egs[i]
            vnew_parts, ointer_parts = [], []
            for b in range(nb):
                rows = slice(b * sl, (b + 1) * sl)
                ws = _dot(jnp.concatenate([w[rows], qe[rows]], axis=0), s_scr[b, h], mode)
                vnew_parts.append(u[rows] - ws[:sl])
                ointer_parts.append(ws[sl:])
            vnew.append(vnew_parts[0] if nb == 1 else jnp.concatenate(vnew_parts, axis=0))
            ointer.append(ointer_parts[0] if nb == 1 else jnp.concatenate(ointer_parts, axis=0))
        for i, h in enumerate(heads):
            attn = qk_kk[i][:r] * decays[i]
            o = ointer[i] + _dot(attn, vnew[i], mode)
            z = z_ref[:, :, h * dh:(h + 1) * dh].reshape(r, dh)
            o_ref[:, :, h * dh:(h + 1) * dh] = _gated_rmsnorm(o, nw, z).reshape(nb, sl, dh).astype(o_ref.dtype)
        for i, h in enumerate(heads):
            ktil = ks[i] * jnp.exp(gls[i] - gcs[i])
            for b in range(nb):
                kt_b = ktil if nb == 1 else jnp.where(row_seq == b, ktil, 0.0)
                gl_b = gls[i][b * sl:b * sl + 1, :]
                s_scr[b, h] = s_scr[b, h] * jnp.exp(gl_b) + _dot_tn(kt_b, vnew[i], mode)

    carry_scr[...] = qkv_ref[:, sl - SUBLANES:sl, :]

    @pl.when(c == pl.num_programs(1) - 1)
    def _():
        sout_ref[...] = s_scr[...]


def _hgrn_kernel(*refs, nb, sl, n_valid, has_init, mode, head_group):
    if has_init:
        (q_ref, f_ref, i_ref, g_ref, lb_ref, nw_ref, s0_ref, o_ref, sout_ref, s_scr) = refs
    else:
        (q_ref, f_ref, i_ref, g_ref, lb_ref, nw_ref, o_ref, sout_ref, s_scr) = refs
    c = pl.program_id(1)
    r = nb * sl
    dh = D_HEAD
    width = N_HEADS * dh

    @pl.when(c == 0)
    def _():
        if has_init:
            s_scr[...] = s0_ref[...]
        else:
            s_scr[...] = jnp.zeros_like(s_scr)

    incl, _ = _chunk_masks(nb, sl)
    lmask = jnp.where(incl, 1.0, 0.0)
    valid = _row_valid(nb, sl, n_valid)
    masked = n_valid < sl
    rowid = lax.broadcasted_iota(jnp.int32, (r, 1), 0)
    row_seq = lax.shift_right_logical(rowid, int(math.log2(sl)))
    sub = min(SUB, sl)
    nblk = r // sub
    sub_shift = int(math.log2(sub))
    ri = lax.broadcasted_iota(jnp.int32, (r, r), 0)
    ci = lax.broadcasted_iota(jnp.int32, (r, r), 1)
    same_blk = lax.shift_right_logical(ri, sub_shift) == lax.shift_right_logical(ci, sub_shift)
    diag_mask = incl & same_blk
    cross_mask = incl & jnp.logical_not(same_blk)

    lb = lb_ref[...]
    f = lb + (1.0 - lb) * _sigmoid(f_ref[...].reshape(r, width))
    lf = jnp.log(f)
    k_all = 1.0 - f
    if masked:
        lf = jnp.where(valid, lf, 0.0)
        k_all = jnp.where(valid, k_all, 0.0)
    bcum = _masked_cumsum(lmask, lf)
    nw = nw_ref[...]

    def head_inputs(h):
        cols = slice(h * dh, (h + 1) * dh)
        q = _silu(q_ref[:, :, cols].reshape(r, dh)) * (dh ** -0.5)
        v = i_ref[:, :, cols].reshape(r, dh)
        if masked:
            v = jnp.where(valid, v, 0.0)
        return q, k_all[:, cols], v, bcum[:, cols]

    def intra_attn(q, k, bh):
        bmid = jnp.broadcast_to(bh.reshape(nblk, sub, dh)[:, sub // 2:sub // 2 + 1, :],
                                (nblk, sub, dh)).reshape(r, dh)
        attn = jnp.where(diag_mask, _dot_nt(q * jnp.exp(bh - bmid), k * jnp.exp(bmid - bh), mode), 0.0)
        if sl > sub:
            parts = [jnp.zeros((sub, r), F32)]
            for blk in range(1, nblk):
                start = blk * sub
                bref = bh[start - 1:start, :]
                qc = q[start:start + sub] * jnp.exp(bh[start:start + sub] - bref)
                kc = k * jnp.exp(jnp.minimum(bref - bh, 0.0))
                parts.append(_dot_nt(qc, kc, mode))
            attn = attn + jnp.where(cross_mask, jnp.concatenate(parts, axis=0), 0.0)
        return attn

    for h0 in range(0, N_HEADS, head_group):
        heads = range(h0, h0 + head_group)
        ins = [head_inputs(h) for h in heads]
        attns = [intra_attn(q, k, bh) for (q, k, v, bh) in ins]
        for i, h in enumerate(heads):
            q, k, v, bh = ins[i]
            qe = q * jnp.exp(bh)
            ointer_parts = []
            for b in range(nb):
                rows = slice(b * sl, (b + 1) * sl)
                ointer_parts.append(_dot(qe[rows], s_scr[b, h], mode))
            ointer = ointer_parts[0] if nb == 1 else jnp.concatenate(ointer_parts, axis=0)
            o = ointer + _dot(attns[i], v, mode)
            cols = slice(h * dh, (h + 1) * dh)
            gate = g_ref[:, :, cols].reshape(r, dh)
            o_ref[:, :, cols] = _gated_rmsnorm(o, nw, gate).reshape(nb, sl, dh).astype(o_ref.dtype)
        for i, h in enumerate(heads):
            q, k, v, bh = ins[i]
            blast = _last_row_bcast(bh, nb, sl)
            ktil = k * jnp.exp(blast - bh)
            pad = [jnp.zeros((LANES - r, dh), F32)] if r < LANES else []
            tr = jnp.concatenate([blast] + pad, axis=0).T
            for b in range(nb):
                kt_b = ktil if nb == 1 else jnp.where(row_seq == b, ktil, 0.0)
                dec_col = jnp.exp(tr[:, b * sl:b * sl + 1])
                s_scr[b, h] = s_scr[b, h] * dec_col + _dot_tn(kt_b, v, mode)

    @pl.when(c == pl.num_programs(1) - 1)
    def _():
        sout_ref[...] = s_scr[...]


def _mixers(proj3, ba3, blk_off, nseq, nb, sl, n_valid, conv_w, pvec, gdn_nw, lb, hgrn_nw,
            s_gdn0, conv0, s_hgrn0, shared_init, out_dtype, mode, inv_mode):
    length = proj3.shape[1]
    has_init = s_gdn0 is not None
    width = N_HEADS * D_HEAD
    conv_ch = 3 * width
    grid = (nseq // nb, length // sl)
    state_spec = pl.BlockSpec((nb, N_HEADS, D_HEAD, D_HEAD), lambda g, c: (g, 0, 0, 0))
    state_shape = jax.ShapeDtypeStruct((nseq, N_HEADS, D_HEAD, D_HEAD), F32)
    init_idx = (lambda g: 0) if shared_init else (lambda g: g)
    init_state_spec = pl.BlockSpec((nb, N_HEADS, D_HEAD, D_HEAD), lambda g, c: (init_idx(g), 0, 0, 0))
    head_group = N_HEADS if nb == 1 else N_HEADS // 2

    def col_spec(w, idx):
        return pl.BlockSpec((nb, sl, w), lambda g, c: (g + blk_off, c, idx))

    def out_spec(w):
        return pl.BlockSpec((nb, sl, w), lambda g, c: (g, c, 0))

    def const_spec(shape):
        return pl.BlockSpec(shape, lambda g, c: (0,) * len(shape))

    gdn_in = [proj3, proj3, ba3, conv_w, pvec, gdn_nw]
    gdn_specs = [col_spec(conv_ch, 0), col_spec(width, 3), col_spec(LANES, 0),
                 const_spec(conv_w.shape), const_spec(pvec.shape), const_spec(gdn_nw.shape)]
    if has_init:
        gdn_in += [s_gdn0, conv0]
        gdn_specs += [init_state_spec, pl.BlockSpec((nb, SUBLANES, conv_ch), lambda g, c: (init_idx(g), 0, 0))]
    hgrn_in = [proj3, proj3, proj3, proj3, lb, hgrn_nw]
    hgrn_specs = [col_spec(width, 4), col_spec(width, 5), col_spec(width, 6), col_spec(width, 7),
                  const_spec(lb.shape), const_spec(hgrn_nw.shape)]
    if has_init:
        hgrn_in += [s_hgrn0]
        hgrn_specs += [init_state_spec]

    gdn_body = functools.partial(_gdn_kernel, nb=nb, sl=sl, n_valid=n_valid, has_init=has_init, mode=mode,
                                 inv_mode=inv_mode, head_group=head_group)
    hgrn_body = functools.partial(_hgrn_kernel, nb=nb, sl=sl, n_valid=n_valid, has_init=has_init, mode=mode,
                                  head_group=head_group)
    n_g, n_h = len(gdn_in), len(hgrn_in)

    def both(*refs):
        ins, outs, scr = refs[:n_g + n_h], refs[n_g + n_h:n_g + n_h + 4], refs[n_g + n_h + 4:]
        gdn_body(*ins[:n_g], outs[0], outs[1], scr[0], scr[1])
        hgrn_body(*ins[n_g:], outs[2], outs[3], scr[2])

    state_scr = pltpu.VMEM((nb, N_HEADS, D_HEAD, D_HEAD), F32)
    o_shape = jax.ShapeDtypeStruct((nseq, length, width), out_dtype)
    o_gdn, s_gdn, o_hgrn, s_hgrn = pl.pallas_call(
        both,
        grid=grid,
        in_specs=gdn_specs + hgrn_specs,
        out_specs=[out_spec(width), state_spec, out_spec(width), state_spec],
        out_shape=[o_shape, state_shape, o_shape, state_shape],
        scratch_shapes=[state_scr, pltpu.VMEM((nb, SUBLANES, conv_ch), F32), state_scr],
        compiler_params=_params(("parallel", "arbitrary")),
    )(*gdn_in, *hgrn_in)
    return o_gdn, o_hgrn, s_gdn, s_hgrn


def _outproj_router_kernel(oap_ref, obp_ref, hpp_ref, oas_ref, obs_ref, hps_ref, wo_ref, n2_ref, wr_ref, br_ref,
                           hp2_ref, xloc_ref, ri_ref, rw_ref, m8_ref, *, tm, n_p):
    i = pl.program_id(0)
    body = functools.partial(_outproj_router_tile, wo_ref=wo_ref, n2_ref=n2_ref, wr_ref=wr_ref, br_ref=br_ref,
                             hp2_ref=hp2_ref, xloc_ref=xloc_ref, ri_ref=ri_ref, rw_ref=rw_ref, m8_ref=m8_ref, tm=tm)

    @pl.when(i < n_p)
    def _():
        body(oap_ref[...], obp_ref[...], hpp_ref[...])

    @pl.when(i >= n_p)
    def _():
        body(oas_ref[...], obs_ref[...], hps_ref[...])


def _outproj_router_tile(oa, ob, hp, *, wo_ref, n2_ref, wr_ref, br_ref, hp2_ref, xloc_ref, ri_ref, rw_ref,
                         m8_ref, tm):
    half = oa.shape[-1]
    mix = _dot(oa.astype(BF16), wo_ref[:half, :]) + _dot(ob.astype(BF16), wo_ref[half:, :])
    hp2 = hp + mix
    hp2_ref[...] = hp2
    xn2 = hp2 * lax.rsqrt(jnp.mean(hp2 * hp2, axis=-1, keepdims=True) + EPS) * n2_ref[...]
    logits = _dot(xn2, wr_ref[...], "bf16x3") + br_ref[...]

    lane = lax.broadcasted_iota(jnp.int32, (tm, LANES), 1)
    lane_f = lane.astype(F32)
    far = float(4 * LANES)
    is_g = (lane >= N_EXPERTS) & (lane < N_EXPERTS + N_GROUPS)
    lg = jnp.where(is_g, logits, -jnp.inf)
    gmax = jnp.max(lg, axis=-1, keepdims=True)
    gsel = jnp.min(jnp.where(lg == gmax, lane_f, far), axis=-1, keepdims=True).astype(jnp.int32) - N_EXPERTS
    p_top = 1.0 / jnp.sum(jnp.where(is_g, jnp.exp(logits - gmax), 0.0), axis=-1, keepdims=True)
    in_grp = (lane < N_EXPERTS) & (lax.shift_right_logical(lane, 3) == gsel)
    le = jnp.where(in_grp, logits, -jnp.inf)
    m1 = jnp.max(le, axis=-1, keepdims=True)
    i1 = jnp.min(jnp.where(le == m1, lane_f, far), axis=-1, keepdims=True).astype(jnp.int32)
    le2 = jnp.where(lane == i1, -jnp.inf, le)
    m2 = jnp.max(le2, axis=-1, keepdims=True)
    i2 = jnp.min(jnp.where(le2 == m2, lane_f, far), axis=-1, keepdims=True).astype(jnp.int32)
    e2 = jnp.exp(m2 - m1)
    w1 = p_top / (1.0 + e2)
    w2 = p_top * e2 / (1.0 + e2)

    onehot = (lane == i1) | (lane == i2)
    onehot_f = jnp.where(onehot, 1.0, 0.0)
    tri = (lax.broadcasted_iota(jnp.int32, (tm, tm), 1) < lax.broadcasted_iota(jnp.int32, (tm, tm), 0))
    rank = _dot(jnp.where(tri, 1.0, 0.0).astype(BF16), onehot_f.astype(BF16))
    cnt = jnp.sum(onehot_f, axis=0, keepdims=True)
    m8 = jnp.floor((cnt + (SUBLANES - 1)) * (1.0 / SUBLANES))
    upper = (lax.broadcasted_iota(jnp.int32, (LANES, LANES), 0) < lax.broadcasted_iota(jnp.int32, (LANES, LANES), 1))
    goff = _dot(jnp.broadcast_to(m8, (SUBLANES, LANES)).astype(BF16), jnp.where(upper, 1.0, 0.0).astype(BF16))[0:1]
    local = goff * SUBLANES + rank
    lr1 = jnp.sum(jnp.where(lane == i1, rank, 0.0), axis=-1, keepdims=True)
    lr2 = jnp.sum(jnp.where(lane == i2, rank, 0.0), axis=-1, keepdims=True)
    lp1 = jnp.sum(jnp.where(lane == i1, local, 0.0), axis=-1, keepdims=True)
    lp2 = jnp.sum(jnp.where(lane == i2, local, 0.0), axis=-1, keepdims=True)
    lp_rows = jnp.where(lane == 0, lp1, jnp.where(lane == 1, lp2, -1.0)).T
    n_loc = xloc_ref.shape[0]
    row = lax.broadcasted_iota(jnp.int32, (n_loc, tm), 0).astype(F32)
    perm = (row == lp_rows[0:1, :]) | (row == lp_rows[1:2, :])
    xloc_ref[...] = _dot(jnp.where(perm, 1.0, 0.0).astype(BF16), xn2.astype(BF16))

    ri_ref[...] = jnp.where(lane == 0, i1, jnp.where(lane == 1, i2, jnp.where(
        lane == 2, lr1.astype(jnp.int32), jnp.where(lane == 3, lr2.astype(jnp.int32), 0))))
    rw_ref[...] = jnp.where(lane == 0, w1, jnp.where(lane == 1, w2, 0.0))
    m8_ref[...] = jnp.broadcast_to(m8, m8_ref.shape)


def _outproj_router(oa_p, ob_p, hp_p, oa_s, ob_s, hp_s, w_out, norm2_w, w_r, b_r, tm):
    (tp, d), ts = hp_p.shape, hp_s.shape[0]
    half = oa_p.shape[1]
    n_p, n_s = tp // tm, ts // tm
    t = tp + ts
    n_loc = _local_rows(tm)
    prow = lambda w: pl.BlockSpec((tm, w), lambda i: (jnp.minimum(i, n_p - 1), 0))
    srow = lambda w: pl.BlockSpec((tm, w), lambda i: (jnp.maximum(i - n_p, 0), 0))
    row = lambda w: pl.BlockSpec((tm, w), lambda i: (i, 0))
    const = lambda shape: pl.BlockSpec(shape, lambda i: (0,) * len(shape))
    return pl.pallas_call(
        functools.partial(_outproj_router_kernel, tm=tm, n_p=n_p),
        grid=(n_p + n_s,),
        in_specs=[prow(half), prow(half), prow(d), srow(half), srow(half), srow(d),
                  const(w_out.shape), const((1, d)), const(w_r.shape), const((1, LANES))],
        out_specs=[row(d), pl.BlockSpec((n_loc, d), lambda i: (i, 0)), row(LANES), row(LANES),
                   pl.BlockSpec((None, SUBLANES, LANES), lambda i: (i, 0, 0))],
        out_shape=[jax.ShapeDtypeStruct((t, d), F32), jax.ShapeDtypeStruct(((n_p + n_s) * n_loc, d), F32),
                   jax.ShapeDtypeStruct((t, LANES), jnp.int32), jax.ShapeDtypeStruct((t, LANES), F32),
                   jax.ShapeDtypeStruct((n_p + n_s, SUBLANES, LANES), F32)],
        compiler_params=_params(("parallel",)),
    )(oa_p, ob_p, hp_p, oa_s, ob_s, hp_s, w_out, norm2_w.reshape(1, d), w_r, b_r)


def _local_rows(tm):
    return 2 * tm + SUBLANES * N_EXPERTS


def _start_row_gather(idx_ref, base, n, src_hbm, dst, sem):
    def issue(r, carry):
        pltpu.make_async_copy(src_hbm.at[pl.ds(idx_ref[base + r], 1), :], dst.at[pl.ds(r, 1), :], sem).start()
        return carry
    lax.fori_loop(0, n, issue, 0, unroll=8)


def _start_group_gather(idx_ref, base, n_groups, src_hbm, dst, sem):
    def issue(r, carry):
        src_row = pl.multiple_of(idx_ref[base + r] * SUBLANES, SUBLANES)
        dst_row = pl.multiple_of(r * SUBLANES, SUBLANES)
        pltpu.make_async_copy(src_hbm.at[pl.ds(src_row, SUBLANES), :], dst.at[pl.ds(dst_row, SUBLANES), :], sem).start()
        return carry
    lax.fori_loop(0, n_groups, issue, 0, unroll=8)


def _wait_row_gather(n, src_hbm, dst, sem):
    pltpu.make_async_copy(src_hbm.at[pl.ds(0, n), :], dst, sem).wait()


def _moe_kernel(te_ref, st_ref, nx_ref, nu_ref, x_hbm, wg_hbm, wu_hbm, wd_hbm, o_ref,
                xbuf, wg_f32, wu_f32, wd_f32, wg_scr, wu_scr, wd_scr, sems, wsems, *, tile):
    i = pl.program_id(0)
    n_used = nu_ref[0]
    slot = lax.rem(i, 2)
    weights = ((wg_hbm, wg_f32, wg_scr), (wu_hbm, wu_f32, wu_scr), (wd_hbm, wd_f32, wd_scr))

    def start_weights(e):
        for k, (w_hbm, w_f32, _) in enumerate(weights):
            pltpu.async_copy(w_hbm.at[e], w_f32, wsems.at[k], priority=1)

    @pl.when((i == 0) & (n_used > 0))
    def _():
        start_weights(te_ref[0])
        _start_group_gather(st_ref, 0, tile // SUBLANES, x_hbm, xbuf.at[0], sems.at[0])

    @pl.when(i < n_used)
    def _():
        @pl.when(i + 1 < n_used)
        def _():
            _start_group_gather(st_ref, (i + 1) * (tile // SUBLANES), tile // SUBLANES, x_hbm,
                                xbuf.at[1 - slot], sems.at[1 - slot])

        e = te_ref[i]

        @pl.when((i == 0) | (e != te_ref[jnp.maximum(i - 1, 0)]))
        def _():
            for k, (w_hbm, w_f32, w_scr) in enumerate(weights):
                pltpu.make_async_copy(w_hbm.at[0], w_f32, wsems.at[k]).wait()
                w_scr[...] = w_f32[...].astype(BF16)
            nxt = nx_ref[e]

            @pl.when(nxt < N_EXPERTS)
            def _():
                start_weights(nxt)

        _wait_row_gather(tile, x_hbm, xbuf.at[slot], sems.at[slot])
        x = xbuf[slot].astype(BF16)
        g = _dot(x, wg_scr[...])
        u = _dot(x, wu_scr[...])
        o_ref[...] = _dot((_silu(g) * u).astype(BF16), wd_scr[...])

    @pl.when(i >= n_used)
    def _():
        o_ref[...] = jnp.zeros_like(o_ref)


def _moe(xn2, w_gate, w_up, w_down, tile_expert, slot_token, next_expert, n_used, tile):
    d = xn2.shape[1]
    n_tiles = tile_expert.shape[0]
    de = w_gate.shape[2]
    hbm = pl.BlockSpec(memory_space=pl.ANY)
    grid_spec = pltpu.PrefetchScalarGridSpec(
        num_scalar_prefetch=4,
        grid=(n_tiles,),
        in_specs=[hbm, hbm, hbm, hbm],
        out_specs=pl.BlockSpec((tile, d), lambda i, te, st, nx, nu: (i, 0)),
        scratch_shapes=[pltpu.VMEM((2, tile, d), F32),
                        pltpu.VMEM((d, de), F32), pltpu.VMEM((d, de), F32), pltpu.VMEM((de, d), F32),
                        pltpu.VMEM((d, de), BF16), pltpu.VMEM((d, de), BF16), pltpu.VMEM((de, d), BF16),
                        pltpu.SemaphoreType.DMA((2,)), pltpu.SemaphoreType.DMA((3,))],
    )
    return pl.pallas_call(
        functools.partial(_moe_kernel, tile=tile),
        grid_spec=grid_spec,
        out_shape=jax.ShapeDtypeStruct((n_tiles * tile, d), F32),
        compiler_params=_params(("arbitrary",)),
    )(tile_expert, slot_token, next_expert, n_used, xn2, w_gate, w_up, w_down)


def _combine_kernel(p0_ref, p1_ref, ys_hbm, hp2_ref, rw_ref, fw_ref, o_ref, buf0, buf1, sems, *, tm, tile_off):
    i = pl.program_id(0)
    slot = lax.rem(i, 2)

    def start(step, s):
        base = (step + tile_off) * tm
        _start_row_gather(p0_ref, base, tm, ys_hbm, buf0.at[s], sems.at[0, s])
        _start_row_gather(p1_ref, base, tm, ys_hbm, buf1.at[s], sems.at[1, s])

    @pl.when(i == 0)
    def _():
        start(0, 0)

    @pl.when(i + 1 < pl.num_programs(0))
    def _():
        start(i + 1, 1 - slot)

    _wait_row_gather(tm, ys_hbm, buf0.at[slot], sems.at[0, slot])
    _wait_row_gather(tm, ys_hbm, buf1.at[slot], sems.at[1, slot])
    rw = rw_ref[...]
    y = hp2_ref[...] + rw[:, 0:1] * buf0[slot] + rw[:, 1:2] * buf1[slot]
    o_ref[...] = y * lax.rsqrt(jnp.mean(y * y, axis=-1, keepdims=True) + EPS) * fw_ref[...]


def _combine(ys, hp2, route_w, final_w, pos0, pos1, tm, tile_off, n_tiles):
    _, d = hp2.shape
    grid_spec = pltpu.PrefetchScalarGridSpec(
        num_scalar_prefetch=2,
        grid=(n_tiles,),
        in_specs=[pl.BlockSpec(memory_space=pl.ANY),
                  pl.BlockSpec((tm, d), lambda i, p0, p1: (i + tile_off, 0)),
                  pl.BlockSpec((tm, LANES), lambda i, p0, p1: (i + tile_off, 0)),
                  pl.BlockSpec((1, d), lambda i, p0, p1: (0, 0))],
        out_specs=pl.BlockSpec((tm, d), lambda i, p0, p1: (i, 0)),
        scratch_shapes=[pltpu.VMEM((2, tm, d), F32), pltpu.VMEM((2, tm, d), F32),
                        pltpu.SemaphoreType.DMA((2, 2))],
    )
    return pl.pallas_call(
        functools.partial(_combine_kernel, tm=tm, tile_off=tile_off),
        grid_spec=grid_spec,
        out_shape=jax.ShapeDtypeStruct((n_tiles * tm, d), F32),
        compiler_params=_params(("arbitrary",)),
    )(pos0, pos1, ys, hp2, route_w, final_w.reshape(1, d))


def kernel(x_prompt, x_sample, state_gdn, state_conv, state_hgrn, meta_tokens, norm1_w, w_in, conv_w, a_log,
           dt_bias, gdn_norm_w, lb_logits, hgrn_norm_w, w_out, norm2_w, w_router_group, b_router_group,
           w_router_expert, b_router_expert, w_gate, w_up, w_down, final_norm_w):
    bp, seq, d = x_prompt.shape
    bs, dec_seq, _ = x_sample.shape
    assert w_in.shape[0] == 1, "single-layer trunk"
    width = N_HEADS * D_HEAD
    conv_ch = 3 * width
    tile = 256
    tn = 512
    sl_s = SUBLANES
    nb_s = CHUNK // sl_s
    assert seq % CHUNK == 0 and N_META <= CHUNK and dec_seq <= sl_s and bs % nb_s == 0
    tp, ts, ts_pad = bp * seq, bs * dec_seq, bs * sl_s
    t_small = ts_pad + CHUNK
    tm = _row_tile(math.gcd(tp, ts), 256)
    tm_p = _row_tile(tp, 2048)

    xp = x_prompt.reshape(tp, d)
    x_small = jnp.concatenate([jnp.pad(x_sample, ((0, 0), (0, sl_s - dec_seq), (0, 0))).reshape(ts_pad, d),
                               jnp.zeros((CHUNK - N_META, d), F32), meta_tokens.astype(F32)], axis=0)

    wi = w_in[0]
    n_a = 4 * width // tn
    w_b = wi[:, 4 * width + 2 * N_HEADS:]
    w_ba = jnp.pad(wi[:, 4 * width:4 * width + 2 * N_HEADS], ((0, 0), (0, LANES - 2 * N_HEADS))).astype(BF16)
    n_cols = 8 * width

    xn_p = _rmsnorm(xp, norm1_w[0], BF16, _row_tile(tp, 512))
    xn_s = _rmsnorm(x_small, norm1_w[0], BF16, _row_tile(t_small, 1024))
    w_a = wi[:, :4 * width]
    proj_p = _inproj(xn_p, w_a, n_a, w_b, tm_p, tn)
    proj_s = _inproj(xn_s, w_a, n_a, w_b, t_small, tn)
    ba_p = _matmul(xn_p, w_ba, tm_p, LANES)
    ba_s = _matmul(xn_s, w_ba, t_small, LANES)

    pvec = jnp.zeros((2, LANES), F32)
    pvec = pvec.at[0, N_HEADS:2 * N_HEADS].set(a_log[0]).at[1, N_HEADS:2 * N_HEADS].set(dt_bias[0])
    lb = jnp.cumsum(jax.nn.softmax(lb_logits.astype(F32), axis=0), axis=0)[0].reshape(1, width)
    gdn_nw = gdn_norm_w[0].reshape(1, D_HEAD)
    hgrn_nw = hgrn_norm_w[0].reshape(1, D_HEAD)
    cw = conv_w[0]
    mix_args = (cw, pvec, gdn_nw, lb, hgrn_nw)

    _, _, sg_m, sh_m = _mixers(proj_s.reshape(t_small // CHUNK, CHUNK, n_cols),
                               ba_s.reshape(t_small // CHUNK, CHUNK, LANES), ts_pad // CHUNK, 1, 1, CHUNK, CHUNK,
                               *mix_args, None, None, None, False, F32, MIX_MODE, INV_MODE)
    conv_m = proj_s[t_small - SUBLANES:, :conv_ch].reshape(1, SUBLANES, conv_ch)
    oa_p, ob_p, sg_p, sh_p = _mixers(proj_p.reshape(bp, seq, n_cols), ba_p.reshape(bp, seq, LANES), 0, bp, 1,
                                     CHUNK, CHUNK, *mix_args, sg_m, conv_m, sh_m, True, BF16, MIX_MODE, INV_MODE)
    conv0 = jnp.pad(state_conv[0], ((0, 0), (SUBLANES - (CONV_W - 1), 0), (0, 0)))
    oa_s, ob_s, sg_s, sh_s = _mixers(proj_s.reshape(t_small // sl_s, sl_s, n_cols),
                                     ba_s.reshape(t_small // sl_s, sl_s, LANES), 0, bs, nb_s, sl_s, dec_seq,
                                     *mix_args, state_gdn[0], conv0, state_hgrn[0], False, F32, MIX_MODE, INV_MODE)

    w_r = jnp.concatenate([w_router_expert[0], w_router_group[0],
                           jnp.zeros((d, LANES - N_EXPERTS - N_GROUPS), F32)], axis=1)
    b_r = jnp.concatenate([b_router_expert[0], b_router_group[0],
                           jnp.zeros((LANES - N_EXPERTS - N_GROUPS,), F32)]).reshape(1, LANES)
    t = tp + ts
    hp2, xloc, route_i, route_w, tile_m8 = _outproj_router(
        oa_p.reshape(tp, width), ob_p.reshape(tp, width), xp,
        oa_s[:, :dec_seq].reshape(ts, width), ob_s[:, :dec_seq].reshape(ts, width), x_sample.reshape(ts, d),
        w_out[0].astype(BF16), norm2_w[0], w_r, b_r, tm)

    n_tt = t // tm
    gpt = tile // SUBLANES
    loc_g = _local_rows(tm) // SUBLANES
    expert_ids = jnp.arange(N_EXPERTS, dtype=jnp.int32)
    m8 = tile_m8[:, 0, :N_EXPERTS].astype(jnp.int32)
    before = jnp.cumsum(m8, axis=0) - m8
    goff = jnp.cumsum(m8, axis=1) - m8
    groups = jnp.sum(m8, axis=0)
    padded_g = (groups + gpt - 1) // gpt * gpt
    ends_g = jnp.cumsum(padded_g)
    offs_g = ends_g - padded_g
    eid = route_i[:, 0:2]
    base_rows = SUBLANES * (offs_g[None, :] + before)
    base_tok = jnp.repeat(base_rows, tm, axis=0)
    pos = jnp.sum(jnp.where(eid[:, :, None] == expert_ids, base_tok[:, None, :], 0), axis=-1) + route_i[:, 2:4]

    n_tiles = -(-(2 * t + n_tt * N_EXPERTS * (SUBLANES - 1)) // tile) + N_EXPERTS
    tile_expert = jnp.minimum(
        jnp.sum((ends_g[None, :] <= (jnp.arange(n_tiles, dtype=jnp.int32) * gpt)[:, None]).astype(jnp.int32), axis=1),
        N_EXPERTS - 1)
    n_used = (ends_g[-1] // gpt).astype(jnp.int32).reshape(1)
    q = jnp.arange(n_tiles * gpt, dtype=jnp.int32)
    e_q = jnp.repeat(tile_expert, gpt)
    sel = (e_q[:, None] == expert_ids).astype(jnp.int32)
    u = q - sel @ offs_g
    run_end = sel @ (before + m8).T
    j_q = jnp.sum((run_end <= u[:, None]).astype(jnp.int32), axis=1)
    hit = (jnp.arange(n_tt, dtype=jnp.int32) == j_q[:, None]).astype(jnp.int32)
    src_in_tile = jnp.sum(hit * (sel @ (goff - before).T), axis=1) + u
    slot_group = jnp.where(j_q < n_tt, j_q * loc_g + src_in_tile, 0).astype(jnp.int32)

    later_active = (expert_ids[None, :] > expert_ids[:, None]) & (groups[None, :] > 0)
    next_expert = jnp.min(jnp.where(later_active, expert_ids[None, :], N_EXPERTS), axis=1).astype(jnp.int32)
    ys = _moe(xloc, w_gate[0], w_up[0], w_down[0], tile_expert, slot_group, next_expert, n_used, tile)
    pos0, pos1 = pos[:, 0], pos[:, 1]
    y_prompt = _combine(ys, hp2, route_w, final_norm_w, pos0, pos1, tm, 0, tp // tm).reshape(bp, seq, d)
    y_sample = _combine(ys, hp2, route_w, final_norm_w, pos0, pos1, tm, tp // tm, ts // tm).reshape(bs, dec_seq, d)

    conv_p = proj_p.reshape(bp, seq, n_cols)[:, seq - (CONV_W - 1):, :conv_ch]
    u_s = proj_s.reshape(t_small // sl_s, sl_s, n_cols)[:bs, :dec_seq, :conv_ch]
    conv_s = jnp.concatenate([state_conv[0], u_s], axis=1)[:, dec_seq:]
    return (y_prompt, y_sample, sg_p[None], conv_p[None], sh_p[None], sg_s[None], conv_s[None], sh_s[None])
```

```python
import functools
import math

import jax
import jax.numpy as jnp
from jax import lax
from jax.experimental import pallas as pl
from jax.experimental.pallas import tpu as pltpu

F32 = jnp.float32
BF16 = jnp.bfloat16

EPS = 1e-6
N_META = 16
CONV_W = 4
N_HEADS = 8
D_HEAD = 128
N_GROUPS = 4
EXPERTS_PER_GROUP = 8
N_EXPERTS = N_GROUPS * EXPERTS_PER_GROUP

LANES = 128
SUBLANES = 8
CHUNK = 64
SUB = 16
VMEM_LIMIT = 56 * 1024 * 1024
NEG_BIG = -1e30
MIX_MODE = "bf16"
INV_MODE = "bf16"


def _sigmoid(x):
    return 0.5 * jnp.tanh(0.5 * x) + 0.5


def _silu(x):
    return x * _sigmoid(x)


def _softplus(x):
    return jnp.maximum(x, 0.0) + jnp.log1p(jnp.exp(-jnp.abs(x)))


def _split_bf16(a, pieces):
    out = []
    for _ in range(pieces - 1):
        hi = a.astype(BF16)
        out.append(hi)
        a = a - hi.astype(F32)
    out.append(a.astype(BF16))
    return out


def _mm(a, b, dims, mode):
    dg = functools.partial(lax.dot_general, dimension_numbers=(dims, ((), ())), preferred_element_type=F32)
    if mode == "bf16":
        return dg(a.astype(BF16), b.astype(BF16))
    assert mode == "bf16x3"
    ah, al = _split_bf16(a, 2)
    bh, bl = _split_bf16(b, 2)
    return dg(ah, bh) + dg(ah, bl) + dg(al, bh)


def _dot(a, b, mode="bf16"):
    return _mm(a, b, ((1,), (0,)), mode)


def _dot_nt(a, b, mode="bf16"):
    return _mm(a, b, ((1,), (1,)), mode)


def _dot_tn(a, b, mode="bf16"):
    return _mm(a, b, ((0,), (0,)), mode)


def _masked_cumsum(lmask, x):
    lm = lmask.astype(BF16)
    return sum(lax.dot_general(lm, p, (((1,), (0,)), ((), ())), preferred_element_type=F32)
               for p in _split_bf16(x, 3))


def _params(sem):
    return pltpu.CompilerParams(dimension_semantics=sem, vmem_limit_bytes=VMEM_LIMIT)


def _row_tile(n, target):
    best = max(c for c in range(16, min(n, target) + 1, 16) if n % c == 0)
    return best


def _rmsnorm_kernel(x_ref, w_ref, o_ref):
    x = x_ref[...]
    ms = jnp.mean(x * x, axis=-1, keepdims=True)
    o_ref[...] = (x * lax.rsqrt(ms + EPS) * w_ref[...]).astype(o_ref.dtype)


def _rmsnorm(x, w, out_dtype, tm):
    t, d = x.shape
    return pl.pallas_call(
        _rmsnorm_kernel,
        grid=(t // tm,),
        in_specs=[pl.BlockSpec((tm, d), lambda i: (i, 0)), pl.BlockSpec((1, d), lambda i: (0, 0))],
        out_specs=pl.BlockSpec((tm, d), lambda i: (i, 0)),
        out_shape=jax.ShapeDtypeStruct((t, d), out_dtype),
        compiler_params=_params(("parallel",)),
    )(x, w.reshape(1, d))


def _inproj_kernel(x_ref, wa_ref, wb_ref, o_ref, w_scr, *, n_a):
    j = pl.program_id(0)
    i = pl.program_id(1)

    @pl.when((i == 0) & (j < n_a))
    def _():
        w_scr[...] = wa_ref[...].astype(BF16)

    @pl.when((i == 0) & (j >= n_a))
    def _():
        w_scr[...] = wb_ref[...].astype(BF16)

    o_ref[...] = _dot(x_ref[...], w_scr[...])


def _inproj(x, w_a, n_a, w_b, tm, tn):
    t, k = x.shape
    n_b = w_b.shape[1] // tn
    return pl.pallas_call(
        functools.partial(_inproj_kernel, n_a=n_a),
        grid=(n_a + n_b, t // tm),
        in_specs=[pl.BlockSpec((tm, k), lambda j, i: (i, 0)),
                  pl.BlockSpec((k, tn), lambda j, i: (0, jnp.minimum(j, n_a - 1))),
                  pl.BlockSpec((k, tn), lambda j, i: (0, jnp.maximum(j - n_a, 0)))],
        out_specs=pl.BlockSpec((tm, tn), lambda j, i: (i, j)),
        out_shape=jax.ShapeDtypeStruct((t, (n_a + n_b) * tn), F32),
        scratch_shapes=[pltpu.VMEM((k, tn), BF16)],
        compiler_params=_params(("arbitrary", "arbitrary")),
    )(x, w_a, w_b)


def _matmul_kernel(x_ref, w_ref, o_ref):
    o_ref[...] = _dot(x_ref[...], w_ref[...])


def _matmul(x, w, tm, tn):
    t, k = x.shape
    n = w.shape[1]
    return pl.pallas_call(
        _matmul_kernel,
        grid=(n // tn, t // tm),
        in_specs=[pl.BlockSpec((tm, k), lambda j, i: (i, 0)), pl.BlockSpec((k, tn), lambda j, i: (0, j))],
        out_specs=pl.BlockSpec((tm, tn), lambda j, i: (i, j)),
        out_shape=jax.ShapeDtypeStruct((t, n), F32),
        compiler_params=_params(("parallel", "arbitrary")),
    )(x, w)


def _chunk_masks(nb, sl):
    r = nb * sl
    shift = int(math.log2(sl))
    ri = lax.broadcasted_iota(jnp.int32, (r, r), 0)
    ci = lax.broadcasted_iota(jnp.int32, (r, r), 1)
    same = lax.shift_right_logical(ri, shift) == lax.shift_right_logical(ci, shift)
    return same & (ci <= ri), same & (ci < ri)


def _row_valid(nb, sl, n_valid):
    rowid = lax.broadcasted_iota(jnp.int32, (nb * sl, 1), 0)
    return (rowid & (sl - 1)) < n_valid


def _last_row_bcast(x, nb, sl):
    c = x.shape[-1]
    x3 = x.reshape(nb, sl, c)
    return jnp.broadcast_to(x3[:, sl - 1:sl, :], (nb, sl, c)).reshape(nb * sl, c)


def _gated_rmsnorm(o, w, gate):
    return o * lax.rsqrt(jnp.mean(o * o, axis=-1, keepdims=True) + EPS) * w * _silu(gate)


def _gdn_kernel(*refs, nb, sl, n_valid, has_init, mode, inv_mode, head_group):
    if has_init:
        (qkv_ref, z_ref, ba_ref, cw_ref, pv_ref, nw_ref, s0_ref, c0_ref,
         o_ref, sout_ref, s_scr, carry_scr) = refs
    else:
        (qkv_ref, z_ref, ba_ref, cw_ref, pv_ref, nw_ref,
         o_ref, sout_ref, s_scr, carry_scr) = refs
    c = pl.program_id(1)
    r = nb * sl
    dh = D_HEAD

    @pl.when(c == 0)
    def _():
        if has_init:
            s_scr[...] = s0_ref[...]
            carry_scr[...] = c0_ref[...]
        else:
            s_scr[...] = jnp.zeros_like(s_scr)
            carry_scr[...] = jnp.zeros_like(carry_scr)

    incl, strict = _chunk_masks(nb, sl)
    lmask = jnp.where(incl, 1.0, 0.0)
    offdiag = jnp.where(strict, 1.0, 0.0)
    valid = _row_valid(nb, sl, n_valid)
    masked = n_valid < sl
    rowid = lax.broadcasted_iota(jnp.int32, (r, 1), 0)
    row_seq = lax.shift_right_logical(rowid, int(math.log2(sl)))
    n_sq = int(math.log2(sl)) - 1

    ba = ba_ref[...].reshape(r, LANES)
    pv = pv_ref[...]
    beta_all = _sigmoid(ba)
    g_all = -jnp.exp(pv[0:1]) * _softplus(ba + pv[1:2])
    if masked:
        g_all = jnp.where(valid, g_all, 0.0)
    gcum = _masked_cumsum(lmask, g_all)
    gcum_t = gcum.T
    glast = _last_row_bcast(gcum, nb, sl)
    cw = cw_ref[...]
    nw = nw_ref[...]

    def conv_slice(c0):
        u = qkv_ref[:, :, c0:c0 + dh]
        prev = carry_scr[:, :, c0:c0 + dh]
        full = jnp.concatenate([prev, u], axis=1)
        acc = None
        for j in range(CONV_W):
            off = SUBLANES - (CONV_W - 1) + j
            term = full[:, off:off + sl, :] * cw[j:j + 1, c0:c0 + dh]
            acc = term if acc is None else acc + term
        return _silu(acc).reshape(r, dh)

    for h0 in range(0, N_HEADS, head_group):
        heads = range(h0, h0 + head_group)
        qs, ks, vs, bcs, gcs, gls, egs, decays = [], [], [], [], [], [], [], []
        for h in heads:
            q = conv_slice(h * dh)
            k = conv_slice(N_HEADS * dh + h * dh)
            v = conv_slice(2 * N_HEADS * dh + h * dh)
            q = q * lax.rsqrt(jnp.sum(q * q, axis=-1, keepdims=True) + EPS) * (dh ** -0.5)
            k = k * lax.rsqrt(jnp.sum(k * k, axis=-1, keepdims=True) + EPS)
            if masked:
                q = jnp.where(valid, q, 0.0)
                k = jnp.where(valid, k, 0.0)
                v = jnp.where(valid, v, 0.0)
            gc = gcum[:, N_HEADS + h:N_HEADS + h + 1]
            gr = gcum_t[N_HEADS + h:N_HEADS + h + 1, :]
            qs.append(q)
            ks.append(k)
            vs.append(v)
            bcs.append(beta_all[:, h:h + 1])
            gcs.append(gc)
            gls.append(glast[:, N_HEADS + h:N_HEADS + h + 1])
            egs.append(jnp.exp(gc))
            decays.append(jnp.exp(jnp.where(incl, gc - gr, NEG_BIG)))
        n = len(qs)
        qk_kk = [_dot_nt(jnp.concatenate([qs[i], ks[i]], axis=0), ks[i], mode) for i in range(n)]
        tm1 = [qk_kk[i][r:] * (decays[i] * offdiag) * (-bcs[i]) for i in range(n)]
        pw = list(tm1)
        for _ in range(n_sq):
            pw = [_dot(pw[i], pw[i], inv_mode) for i in range(n)]
            tm1 = [tm1[i] + pw[i] + _dot(tm1[i], pw[i], inv_mode) for i in range(n)]
        rhs = [jnp.concatenate([vs[i] * bcs[i], ks[i] * (bcs[i] * egs[i])], axis=1) for i in range(n)]
        uw = [rhs[i] + _dot(tm1[i], rhs[i], mode) for i in range(n)]
        vnew, ointer = [], []
        for i, h in enumerate(heads):
            u = uw[i][:, :dh]
            w = uw[i][:, dh:]
            qe = qs[i] * egs[i]
            vnew_parts, ointer_parts = [], []
            for b in range(nb):
                rows = slice(b * sl, (b + 1) * sl)
                ws = _dot(jnp.concatenate([w[rows], qe[rows]], axis=0), s_scr[b, h], mode)
                vnew_parts.append(u[rows] - ws[:sl])
                ointer_parts.append(ws[sl:])
            vnew.append(vnew_parts[0] if nb == 1 else jnp.concatenate(vnew_parts, axis=0))
            ointer.append(ointer_parts[0] if nb == 1 else jnp.concatenate(ointer_parts, axis=0))
        for i, h in enumerate(heads):
            attn = qk_kk[i][:r] * decays[i]
            o = ointer[i] + _dot(attn, vnew[i], mode)
            z = z_ref[:, :, h * dh:(h + 1) * dh].reshape(r, dh)
            o_ref[:, :, h * dh:(h + 1) * dh] = _gated_rmsnorm(o, nw, z).reshape(nb, sl, dh).astype(o_ref.dtype)
        for i, h in enumerate(heads):
            ktil = ks[i] * jnp.exp(gls[i] - gcs[i])
            for b in range(nb):
                kt_b = ktil if nb == 1 else jnp.where(row_seq == b, ktil, 0.0)
                gl_b = gls[i][b * sl:b * sl + 1, :]
                s_scr[b, h] = s_scr[b, h] * jnp.exp(gl_b) + _dot_tn(kt_b, vnew[i], mode)

    carry_scr[...] = qkv_ref[:, sl - SUBLANES:sl, :]

    @pl.when(c == pl.num_programs(1) - 1)
    def _():
        sout_ref[...] = s_scr[...]


def _hgrn_kernel(*refs, nb, sl, n_valid, has_init, mode, head_group):
    if has_init:
        (q_ref, f_ref, i_ref, g_ref, lb_ref, nw_ref, s0_ref, o_ref, sout_ref, s_scr) = refs
    else:
        (q_ref, f_ref, i_ref, g_ref, lb_ref, nw_ref, o_ref, sout_ref, s_scr) = refs
    c = pl.program_id(1)
    r = nb * sl
    dh = D_HEAD
    width = N_HEADS * dh

    @pl.when(c == 0)
    def _():
        if has_init:
            s_scr[...] = s0_ref[...]
        else:
            s_scr[...] = jnp.zeros_like(s_scr)

    incl, _ = _chunk_masks(nb, sl)
    lmask = jnp.where(incl, 1.0, 0.0)
    valid = _row_valid(nb, sl, n_valid)
    masked = n_valid < sl
    rowid = lax.broadcasted_iota(jnp.int32, (r, 1), 0)
    row_seq = lax.shift_right_logical(rowid, int(math.log2(sl)))
    sub = min(SUB, sl)
    nblk = r // sub
    sub_shift = int(math.log2(sub))
    ri = lax.broadcasted_iota(jnp.int32, (r, r), 0)
    ci = lax.broadcasted_iota(jnp.int32, (r, r), 1)
    same_blk = lax.shift_right_logical(ri, sub_shift) == lax.shift_right_logical(ci, sub_shift)
    diag_mask = incl & same_blk
    cross_mask = incl & jnp.logical_not(same_blk)

    lb = lb_ref[...]
    f = lb + (1.0 - lb) * _sigmoid(f_ref[...].reshape(r, width))
    lf = jnp.log(f)
    k_all = 1.0 - f
    if masked:
        lf = jnp.where(valid, lf, 0.0)
        k_all = jnp.where(valid, k_all, 0.0)
    bcum = _masked_cumsum(lmask, lf)
    nw = nw_ref[...]

    def head_inputs(h):
        cols = slice(h * dh, (h + 1) * dh)
        q = _silu(q_ref[:, :, cols].reshape(r, dh)) * (dh ** -0.5)
        v = i_ref[:, :, cols].reshape(r, dh)
        if masked:
            v = jnp.where(valid, v, 0.0)
        return q, k_all[:, cols], v, bcum[:, cols]

    def intra_attn(q, k, bh):
        bmid = jnp.broadcast_to(bh.reshape(nblk, sub, dh)[:, sub // 2:sub // 2 + 1, :],
                                (nblk, sub, dh)).reshape(r, dh)
        attn = jnp.where(diag_mask, _dot_nt(q * jnp.exp(bh - bmid), k * jnp.exp(bmid - bh), mode), 0.0)
        if sl > sub:
            parts = [jnp.zeros((sub, r), F32)]
            for blk in range(1, nblk):
                start = blk * sub
                bref = bh[start - 1:start, :]
                qc = q[start:start + sub] * jnp.exp(bh[start:start + sub] - bref)
                kc = k * jnp.exp(jnp.minimum(bref - bh, 0.0))
                parts.append(_dot_nt(qc, kc, mode))
            attn = attn + jnp.where(cross_mask, jnp.concatenate(parts, axis=0), 0.0)
        return attn

    for h0 in range(0, N_HEADS, head_group):
        heads = range(h0, h0 + head_group)
        ins = [head_inputs(h) for h in heads]
        attns = [intra_attn(q, k, bh) for (q, k, v, bh) in ins]
        for i, h in enumerate(heads):
            q, k, v, bh = ins[i]
            qe = q * jnp.exp(bh)
            ointer_parts = []
            for b in range(nb):
                rows = slice(b * sl, (b + 1) * sl)
                ointer_parts.append(_dot(qe[rows], s_scr[b, h], mode))
            ointer = ointer_parts[0] if nb == 1 else jnp.concatenate(ointer_parts, axis=0)
            o = ointer + _dot(attns[i], v, mode)
            cols = slice(h * dh, (h + 1) * dh)
            gate = g_ref[:, :, cols].reshape(r, dh)
            o_ref[:, :, cols] = _gated_rmsnorm(o, nw, gate).reshape(nb, sl, dh).astype(o_ref.dtype)
        for i, h in enumerate(heads):
            q, k, v, bh = ins[i]
            blast = _last_row_bcast(bh, nb, sl)
            ktil = k * jnp.exp(blast - bh)
            pad = [jnp.zeros((LANES - r, dh), F32)] if r < LANES else []
            tr = jnp.concatenate([blast] + pad, axis=0).T
            for b in range(nb):
                kt_b = ktil if nb == 1 else jnp.where(row_seq == b, ktil, 0.0)
                dec_col = jnp.exp(tr[:, b * sl:b * sl + 1])
                s_scr[b, h] = s_scr[b, h] * dec_col + _dot_tn(kt_b, v, mode)

    @pl.when(c == pl.num_programs(1) - 1)
    def _():
        sout_ref[...] = s_scr[...]


def _mixers(proj3, ba3, blk_off, nseq, nb, sl, n_valid, conv_w, pvec, gdn_nw, lb, hgrn_nw,
            s_gdn0, conv0, s_hgrn0, shared_init, out_dtype, mode, inv_mode):
    length = proj3.shape[1]
    has_init = s_gdn0 is not None
    width = N_HEADS * D_HEAD
    conv_ch = 3 * width
    grid = (nseq // nb, length // sl)
    state_spec = pl.BlockSpec((nb, N_HEADS, D_HEAD, D_HEAD), lambda g, c: (g, 0, 0, 0))
    state_shape = jax.ShapeDtypeStruct((nseq, N_HEADS, D_HEAD, D_HEAD), F32)
    init_idx = (lambda g: 0) if shared_init else (lambda g: g)
    init_state_spec = pl.BlockSpec((nb, N_HEADS, D_HEAD, D_HEAD), lambda g, c: (init_idx(g), 0, 0, 0))
    head_group = N_HEADS if nb == 1 else N_HEADS // 2

    def col_spec(w, idx):
        return pl.BlockSpec((nb, sl, w), lambda g, c: (g + blk_off, c, idx))

    def out_spec(w):
        return pl.BlockSpec((nb, sl, w), lambda g, c: (g, c, 0))

    def const_spec(shape):
        return pl.BlockSpec(shape, lambda g, c: (0,) * len(shape))

    gdn_in = [proj3, proj3, ba3, conv_w, pvec, gdn_nw]
    gdn_specs = [col_spec(conv_ch, 0), col_spec(width, 3), col_spec(LANES, 0),
                 const_spec(conv_w.shape), const_spec(pvec.shape), const_spec(gdn_nw.shape)]
    if has_init:
        gdn_in += [s_gdn0, conv0]
        gdn_specs += [init_state_spec, pl.BlockSpec((nb, SUBLANES, conv_ch), lambda g, c: (init_idx(g), 0, 0))]
    hgrn_in = [proj3, proj3, proj3, proj3, lb, hgrn_nw]
    hgrn_specs = [col_spec(width, 4), col_spec(width, 5), col_spec(width, 6), col_spec(width, 7),
                  const_spec(lb.shape), const_spec(hgrn_nw.shape)]
    if has_init:
        hgrn_in += [s_hgrn0]
        hgrn_specs += [init_state_spec]

    gdn_body = functools.partial(_gdn_kernel, nb=nb, sl=sl, n_valid=n_valid, has_init=has_init, mode=mode,
                                 inv_mode=inv_mode, head_group=head_group)
    hgrn_body = functools.partial(_hgrn_kernel, nb=nb, sl=sl, n_valid=n_valid, has_init=has_init, mode=mode,
                                  head_group=head_group)
    n_g, n_h = len(gdn_in), len(hgrn_in)

    def both(*refs):
        ins, outs, scr = refs[:n_g + n_h], refs[n_g + n_h:n_g + n_h + 4], refs[n_g + n_h + 4:]
        gdn_body(*ins[:n_g], outs[0], outs[1], scr[0], scr[1])
        hgrn_body(*ins[n_g:], outs[2], outs[3], scr[2])

    state_scr = pltpu.VMEM((nb, N_HEADS, D_HEAD, D_HEAD), F32)
    o_shape = jax.ShapeDtypeStruct((nseq, length, width), out_dtype)
    o_gdn, s_gdn, o_hgrn, s_hgrn = pl.pallas_call(
        both,
        grid=grid,
        in_specs=gdn_specs + hgrn_specs,
        out_specs=[out_spec(width), state_spec, out_spec(width), state_spec],
        out_shape=[o_shape, state_shape, o_shape, state_shape],
        scratch_shapes=[state_scr, pltpu.VMEM((nb, SUBLANES, conv_ch), F32), state_scr],
        compiler_params=_params(("parallel", "arbitrary")),
    )(*gdn_in, *hgrn_in)
    return o_gdn, o_hgrn, s_gdn, s_hgrn


def _outproj_router_kernel(oap_ref, obp_ref, hpp_ref, oas_ref, obs_ref, hps_ref, wo_ref, n2_ref, wr_ref, br_ref,
                           hp2_ref, xloc_ref, ri_ref, rw_ref, m8_ref, *, tm, n_p):
    i = pl.program_id(0)
    body = functools.partial(_outproj_router_tile, wo_ref=wo_ref, n2_ref=n2_ref, wr_ref=wr_ref, br_ref=br_ref,
                             hp2_ref=hp2_ref, xloc_ref=xloc_ref, ri_ref=ri_ref, rw_ref=rw_ref, m8_ref=m8_ref, tm=tm)

    @pl.when(i < n_p)
    def _():
        body(oap_ref[...], obp_ref[...], hpp_ref[...])

    @pl.when(i >= n_p)
    def _():
        body(oas_ref[...], obs_ref[...], hps_ref[...])


def _outproj_router_tile(oa, ob, hp, *, wo_ref, n2_ref, wr_ref, br_ref, hp2_ref, xloc_ref, ri_ref, rw_ref,
                         m8_ref, tm):
    half = oa.shape[-1]
    mix = _dot(oa.astype(BF16), wo_ref[:half, :]) + _dot(ob.astype(BF16), wo_ref[half:, :])
    hp2 = hp + mix
    hp2_ref[...] = hp2
    xn2 = hp2 * lax.rsqrt(jnp.mean(hp2 * hp2, axis=-1, keepdims=True) + EPS) * n2_ref[...]
    logits = _dot(xn2, wr_ref[...], "bf16x3") + br_ref[...]

    lane = lax.broadcasted_iota(jnp.int32, (tm, LANES), 1)
    lane_f = lane.astype(F32)
    far = float(4 * LANES)
    is_g = (lane >= N_EXPERTS) & (lane < N_EXPERTS + N_GROUPS)
    lg = jnp.where(is_g, logits, -jnp.inf)
    gmax = jnp.max(lg, axis=-1, keepdims=True)
    gsel = jnp.min(jnp.where(lg == gmax, lane_f, far), axis=-1, keepdims=True).astype(jnp.int32) - N_EXPERTS
    p_top = 1.0 / jnp.sum(jnp.where(is_g, jnp.exp(logits - gmax), 0.0), axis=-1, keepdims=True)
    in_grp = (lane < N_EXPERTS) & (lax.shift_right_logical(lane, 3) == gsel)
    le = jnp.where(in_grp, logits, -jnp.inf)
    m1 = jnp.max(le, axis=-1, keepdims=True)
    i1 = jnp.min(jnp.where(le == m1, lane_f, far), axis=-1, keepdims=True).astype(jnp.int32)
    le2 = jnp.where(lane == i1, -jnp.inf, le)
    m2 = jnp.max(le2, axis=-1, keepdims=True)
    i2 = jnp.min(jnp.where(le2 == m2, lane_f, far), axis=-1, keepdims=True).astype(jnp.int32)
    e2 = jnp.exp(m2 - m1)
    w1 = p_top / (1.0 + e2)
    w2 = p_top * e2 / (1.0 + e2)

    onehot = (lane == i1) | (lane == i2)
    onehot_f = jnp.where(onehot, 1.0, 0.0)
    tri = (lax.broadcasted_iota(jnp.int32, (tm, tm), 1) < lax.broadcasted_iota(jnp.int32, (tm, tm), 0))
    rank = _dot(jnp.where(tri, 1.0, 0.0).astype(BF16), onehot_f.astype(BF16))
    cnt = jnp.sum(onehot_f, axis=0, keepdims=True)
    m8 = jnp.floor((cnt + (SUBLANES - 1)) * (1.0 / SUBLANES))
    upper = (lax.broadcasted_iota(jnp.int32, (LANES, LANES), 0) < lax.broadcasted_iota(jnp.int32, (LANES, LANES), 1))
    goff = _dot(jnp.broadcast_to(m8, (SUBLANES, LANES)).astype(BF16), jnp.where(upper, 1.0, 0.0).astype(BF16))[0:1]
    local = goff * SUBLANES + rank
    lr1 = jnp.sum(jnp.where(lane == i1, rank, 0.0), axis=-1, keepdims=True)
    lr2 = jnp.sum(jnp.where(lane == i2, rank, 0.0), axis=-1, keepdims=True)
    lp1 = jnp.sum(jnp.where(lane == i1, local, 0.0), axis=-1, keepdims=True)
    lp2 = jnp.sum(jnp.where(lane == i2, local, 0.0), axis=-1, keepdims=True)
    lp_rows = jnp.where(lane == 0, lp1, jnp.where(lane == 1, lp2, -1.0)).T
    n_loc = xloc_ref.shape[0]
    row = lax.broadcasted_iota(jnp.int32, (n_loc, tm), 0).astype(F32)
    perm = (row == lp_rows[0:1, :]) | (row == lp_rows[1:2, :])
    xloc_ref[...] = _dot(jnp.where(perm, 1.0, 0.0).astype(BF16), xn2.astype(BF16))

    ri_ref[...] = jnp.where(lane == 0, i1, jnp.where(lane == 1, i2, jnp.where(
        lane == 2, lr1.astype(jnp.int32), jnp.where(lane == 3, lr2.astype(jnp.int32), 0))))
    rw_ref[...] = jnp.where(lane == 0, w1, jnp.where(lane == 1, w2, 0.0))
    m8_ref[...] = jnp.broadcast_to(m8, m8_ref.shape)


def _outproj_router(oa_p, ob_p, hp_p, oa_s, ob_s, hp_s, w_out, norm2_w, w_r, b_r, tm):
    (tp, d), ts = hp_p.shape, hp_s.shape[0]
    half = oa_p.shape[1]
    n_p, n_s = tp // tm, ts // tm
    t = tp + ts
    n_loc = _local_rows(tm)
    prow = lambda w: pl.BlockSpec((tm, w), lambda i: (jnp.minimum(i, n_p - 1), 0))
    srow = lambda w: pl.BlockSpec((tm, w), lambda i: (jnp.maximum(i - n_p, 0), 0))
    row = lambda w: pl.BlockSpec((tm, w), lambda i: (i, 0))
    const = lambda shape: pl.BlockSpec(shape, lambda i: (0,) * len(shape))
    return pl.pallas_call(
        functools.partial(_outproj_router_kernel, tm=tm, n_p=n_p),
        grid=(n_p + n_s,),
        in_specs=[prow(half), prow(half), prow(d), srow(half), srow(half), srow(d),
                  const(w_out.shape), const((1, d)), const(w_r.shape), const((1, LANES))],
        out_specs=[row(d), pl.BlockSpec((n_loc, d), lambda i: (i, 0)), row(LANES), row(LANES),
                   pl.BlockSpec((None, SUBLANES, LANES), lambda i: (i, 0, 0))],
        out_shape=[jax.ShapeDtypeStruct((t, d), F32), jax.ShapeDtypeStruct(((n_p + n_s) * n_loc, d), F32),
                   jax.ShapeDtypeStruct((t, LANES), jnp.int32), jax.ShapeDtypeStruct((t, LANES), F32),
                   jax.ShapeDtypeStruct((n_p + n_s, SUBLANES, LANES), F32)],
        compiler_params=_params(("parallel",)),
    )(oa_p, ob_p, hp_p, oa_s, ob_s, hp_s, w_out, norm2_w.reshape(1, d), w_r, b_r)


def _local_rows(tm):
    return 2 * tm + SUBLANES * N_EXPERTS


def _start_row_gather(idx_ref, base, n, src_hbm, dst, sem, queue=0):
    def issue(r, carry):
        pltpu.async_copy(src_hbm.at[pl.ds(idx_ref[base + r], 1), :], dst.at[pl.ds(r, 1), :], sem, priority=queue)
        return carry
    lax.fori_loop(0, n, issue, 0, unroll=8)


def _start_group_gather(idx_ref, base, n_groups, src_hbm, dst, sem):
    def issue(r, carry):
        src_row = pl.multiple_of(idx_ref[base + r] * SUBLANES, SUBLANES)
        dst_row = pl.multiple_of(r * SUBLANES, SUBLANES)
        pltpu.make_async_copy(src_hbm.at[pl.ds(src_row, SUBLANES), :], dst.at[pl.ds(dst_row, SUBLANES), :], sem).start()
        return carry
    lax.fori_loop(0, n_groups, issue, 0, unroll=8)


def _wait_row_gather(n, src_hbm, dst, sem):
    pltpu.make_async_copy(src_hbm.at[pl.ds(0, n), :], dst, sem).wait()


def _moe_kernel(te_ref, st_ref, nx_ref, nu_ref, x_hbm, wg_hbm, wu_hbm, wd_hbm, o_ref,
                xbuf, wg_f32, wu_f32, wd_f32, wg_scr, wu_scr, wd_scr, sems, wsems, *, tile):
    i = pl.program_id(0)
    n_used = nu_ref[0]
    slot = lax.rem(i, 2)
    weights = ((wg_hbm, wg_f32, wg_scr), (wu_hbm, wu_f32, wu_scr), (wd_hbm, wd_f32, wd_scr))

    def start_weights(e):
        for k, (w_hbm, w_f32, _) in enumerate(weights):
            pltpu.async_copy(w_hbm.at[e], w_f32, wsems.at[k], priority=1)

    @pl.when((i == 0) & (n_used > 0))
    def _():
        start_weights(te_ref[0])
        _start_group_gather(st_ref, 0, tile // SUBLANES, x_hbm, xbuf.at[0], sems.at[0])

    @pl.when(i < n_used)
    def _():
        @pl.when(i + 1 < n_used)
        def _():
            _start_group_gather(st_ref, (i + 1) * (tile // SUBLANES), tile // SUBLANES, x_hbm,
                                xbuf.at[1 - slot], sems.at[1 - slot])

        e = te_ref[i]

        @pl.when((i == 0) | (e != te_ref[jnp.maximum(i - 1, 0)]))
        def _():
            for k, (w_hbm, w_f32, w_scr) in enumerate(weights):
                pltpu.make_async_copy(w_hbm.at[0], w_f32, wsems.at[k]).wait()
                w_scr[...] = w_f32[...].astype(BF16)
            nxt = nx_ref[e]

            @pl.when(nxt < N_EXPERTS)
            def _():
                start_weights(nxt)

        _wait_row_gather(tile, x_hbm, xbuf.at[slot], sems.at[slot])
        x = xbuf[slot].astype(BF16)
        g = _dot(x, wg_scr[...])
        u = _dot(x, wu_scr[...])
        o_ref[...] = _dot((_silu(g) * u).astype(BF16), wd_scr[...])

    @pl.when(i >= n_used)
    def _():
        o_ref[...] = jnp.zeros_like(o_ref)


def _moe(xn2, w_gate, w_up, w_down, tile_expert, slot_token, next_expert, n_used, tile):
    d = xn2.shape[1]
    n_tiles = tile_expert.shape[0]
    de = w_gate.shape[2]
    hbm = pl.BlockSpec(memory_space=pl.ANY)
    grid_spec = pltpu.PrefetchScalarGridSpec(
        num_scalar_prefetch=4,
        grid=(n_tiles,),
        in_specs=[hbm, hbm, hbm, hbm],
        out_specs=pl.BlockSpec((tile, d), lambda i, te, st, nx, nu: (i, 0)),
        scratch_shapes=[pltpu.VMEM((2, tile, d), F32),
                        pltpu.VMEM((d, de), F32), pltpu.VMEM((d, de), F32), pltpu.VMEM((de, d), F32),
                        pltpu.VMEM((d, de), BF16), pltpu.VMEM((d, de), BF16), pltpu.VMEM((de, d), BF16),
                        pltpu.SemaphoreType.DMA((2,)), pltpu.SemaphoreType.DMA((3,))],
    )
    return pl.pallas_call(
        functools.partial(_moe_kernel, tile=tile),
        grid_spec=grid_spec,
        out_shape=jax.ShapeDtypeStruct((n_tiles * tile, d), F32),
        compiler_params=_params(("arbitrary",)),
    )(tile_expert, slot_token, next_expert, n_used, xn2, w_gate, w_up, w_down)


def _combine_kernel(p0_ref, p1_ref, ys_hbm, hp2_ref, rw_ref, fw_ref, o_ref, buf0, buf1, sems, *, tm, tile_off):
    i = pl.program_id(0)
    slot = lax.rem(i, 2)

    def start(step, s):
        base = (step + tile_off) * tm
        _start_row_gather(p0_ref, base, tm, ys_hbm, buf0.at[s], sems.at[0, s], queue=0)
        _start_row_gather(p1_ref, base, tm, ys_hbm, buf1.at[s], sems.at[1, s], queue=1)

    @pl.when(i == 0)
    def _():
        start(0, 0)

    @pl.when(i + 1 < pl.num_programs(0))
    def _():
        start(i + 1, 1 - slot)

    _wait_row_gather(tm, ys_hbm, buf0.at[slot], sems.at[0, slot])
    _wait_row_gather(tm, ys_hbm, buf1.at[slot], sems.at[1, slot])
    rw = rw_ref[...]
    y = hp2_ref[...] + rw[:, 0:1] * buf0[slot] + rw[:, 1:2] * buf1[slot]
    o_ref[...] = y * lax.rsqrt(jnp.mean(y * y, axis=-1, keepdims=True) + EPS) * fw_ref[...]


def _combine(ys, hp2, route_w, final_w, pos0, pos1, tm, tile_off, n_tiles):
    _, d = hp2.shape
    grid_spec = pltpu.PrefetchScalarGridSpec(
        num_scalar_prefetch=2,
        grid=(n_tiles,),
        in_specs=[pl.BlockSpec(memory_space=pl.ANY),
                  pl.BlockSpec((tm, d), lambda i, p0, p1: (i + tile_off, 0)),
                  pl.BlockSpec((tm, LANES), lambda i, p0, p1: (i + tile_off, 0)),
                  pl.BlockSpec((1, d), lambda i, p0, p1: (0, 0))],
        out_specs=pl.BlockSpec((tm, d), lambda i, p0, p1: (i, 0)),
        scratch_shapes=[pltpu.VMEM((2, tm, d), F32), pltpu.VMEM((2, tm, d), F32),
                        pltpu.SemaphoreType.DMA((2, 2))],
    )
    return pl.pallas_call(
        functools.partial(_combine_kernel, tm=tm, tile_off=tile_off),
        grid_spec=grid_spec,
        out_shape=jax.ShapeDtypeStruct((n_tiles * tm, d), F32),
        compiler_params=_params(("arbitrary",)),
    )(pos0, pos1, ys, hp2, route_w, final_w.reshape(1, d))


def kernel(x_prompt, x_sample, state_gdn, state_conv, state_hgrn, meta_tokens, norm1_w, w_in, conv_w, a_log,
           dt_bias, gdn_norm_w, lb_logits, hgrn_norm_w, w_out, norm2_w, w_router_group, b_router_group,
           w_router_expert, b_router_expert, w_gate, w_up, w_down, final_norm_w):
    bp, seq, d = x_prompt.shape
    bs, dec_seq, _ = x_sample.shape
    assert w_in.shape[0] == 1, "single-layer trunk"
    width = N_HEADS * D_HEAD
    conv_ch = 3 * width
    tile = 256
    tn = 512
    sl_s = SUBLANES
    nb_s = CHUNK // sl_s
    assert seq % CHUNK == 0 and N_META <= CHUNK and dec_seq <= sl_s and bs % nb_s == 0
    tp, ts, ts_pad = bp * seq, bs * dec_seq, bs * sl_s
    t_small = ts_pad + CHUNK
    tm = _row_tile(math.gcd(tp, ts), 256)
    tm_p = _row_tile(tp, 2048)

    xp = x_prompt.reshape(tp, d)
    x_small = jnp.concatenate([jnp.pad(x_sample, ((0, 0), (0, sl_s - dec_seq), (0, 0))).reshape(ts_pad, d),
                               jnp.zeros((CHUNK - N_META, d), F32), meta_tokens.astype(F32)], axis=0)

    wi = w_in[0]
    n_a = 4 * width // tn
    w_b = wi[:, 4 * width + 2 * N_HEADS:]
    w_ba = jnp.pad(wi[:, 4 * width:4 * width + 2 * N_HEADS], ((0, 0), (0, LANES - 2 * N_HEADS))).astype(BF16)
    n_cols = 8 * width

    xn_p = _rmsnorm(xp, norm1_w[0], BF16, _row_tile(tp, 512))
    xn_s = _rmsnorm(x_small, norm1_w[0], BF16, _row_tile(t_small, 1024))
    w_a = wi[:, :4 * width]
    proj_p = _inproj(xn_p, w_a, n_a, w_b, tm_p, tn)
    proj_s = _inproj(xn_s, w_a, n_a, w_b, t_small, tn)
    ba_p = _matmul(xn_p, w_ba, tm_p, LANES)
    ba_s = _matmul(xn_s, w_ba, t_small, LANES)

    pvec = jnp.zeros((2, LANES), F32)
    pvec = pvec.at[0, N_HEADS:2 * N_HEADS].set(a_log[0]).at[1, N_HEADS:2 * N_HEADS].set(dt_bias[0])
    lb = jnp.cumsum(jax.nn.softmax(lb_logits.astype(F32), axis=0), axis=0)[0].reshape(1, width)
    gdn_nw = gdn_norm_w[0].reshape(1, D_HEAD)
    hgrn_nw = hgrn_norm_w[0].reshape(1, D_HEAD)
    cw = conv_w[0]
    mix_args = (cw, pvec, gdn_nw, lb, hgrn_nw)

    _, _, sg_m, sh_m = _mixers(proj_s.reshape(t_small // CHUNK, CHUNK, n_cols),
                               ba_s.reshape(t_small // CHUNK, CHUNK, LANES), ts_pad // CHUNK, 1, 1, CHUNK, CHUNK,
                               *mix_args, None, None, None, False, F32, MIX_MODE, INV_MODE)
    conv_m = proj_s[t_small - SUBLANES:, :conv_ch].reshape(1, SUBLANES, conv_ch)
    oa_p, ob_p, sg_p, sh_p = _mixers(proj_p.reshape(bp, seq, n_cols), ba_p.reshape(bp, seq, LANES), 0, bp, 1,
                                     CHUNK, CHUNK, *mix_args, sg_m, conv_m, sh_m, True, BF16, MIX_MODE, INV_MODE)
    conv0 = jnp.pad(state_conv[0], ((0, 0), (SUBLANES - (CONV_W - 1), 0), (0, 0)))
    oa_s, ob_s, sg_s, sh_s = _mixers(proj_s.reshape(t_small // sl_s, sl_s, n_cols),
                                     ba_s.reshape(t_small // sl_s, sl_s, LANES), 0, bs, nb_s, sl_s, dec_seq,
                                     *mix_args, state_gdn[0], conv0, state_hgrn[0], False, F32, MIX_MODE, INV_MODE)

    w_r = jnp.concatenate([w_router_expert[0], w_router_group[0],
                           jnp.zeros((d, LANES - N_EXPERTS - N_GROUPS), F32)], axis=1)
    b_r = jnp.concatenate([b_router_expert[0], b_router_group[0],
                           jnp.zeros((LANES - N_EXPERTS - N_GROUPS,), F32)]).reshape(1, LANES)
    t = tp + ts
    hp2, xloc, route_i, route_w, tile_m8 = _outproj_router(
        oa_p.reshape(tp, width), ob_p.reshape(tp, width), xp,
        oa_s[:, :dec_seq].reshape(ts, width), ob_s[:, :dec_seq].reshape(ts, width), x_sample.reshape(ts, d),
        w_out[0].astype(BF16), norm2_w[0], w_r, b_r, tm)

    n_tt = t // tm
    gpt = tile // SUBLANES
    loc_g = _local_rows(tm) // SUBLANES
    expert_ids = jnp.arange(N_EXPERTS, dtype=jnp.int32)
    m8 = tile_m8[:, 0, :N_EXPERTS].astype(jnp.int32)
    before = jnp.cumsum(m8, axis=0) - m8
    goff = jnp.cumsum(m8, axis=1) - m8
    groups = jnp.sum(m8, axis=0)
    padded_g = (groups + gpt - 1) // gpt * gpt
    ends_g = jnp.cumsum(padded_g)
    offs_g = ends_g - padded_g
    eid = route_i[:, 0:2]
    base_rows = SUBLANES * (offs_g[None, :] + before)
    base_tok = jnp.repeat(base_rows, tm, axis=0)
    pos = jnp.sum(jnp.where(eid[:, :, None] == expert_ids, base_tok[:, None, :], 0), axis=-1) + route_i[:, 2:4]

    n_tiles = -(-(2 * t + n_tt * N_EXPERTS * (SUBLANES - 1)) // tile) + N_EXPERTS
    tile_expert = jnp.minimum(
        jnp.sum((ends_g[None, :] <= (jnp.arange(n_tiles, dtype=jnp.int32) * gpt)[:, None]).astype(jnp.int32), axis=1),
        N_EXPERTS - 1)
    n_used = (ends_g[-1] // gpt).astype(jnp.int32).reshape(1)
    q = jnp.arange(n_tiles * gpt, dtype=jnp.int32)
    e_q = jnp.repeat(tile_expert, gpt)
    sel = (e_q[:, None] == expert_ids).astype(jnp.int32)
    u = q - sel @ offs_g
    run_end = sel @ (before + m8).T
    j_q = jnp.sum((run_end <= u[:, None]).astype(jnp.int32), axis=1)
    hit = (jnp.arange(n_tt, dtype=jnp.int32) == j_q[:, None]).astype(jnp.int32)
    src_in_tile = jnp.sum(hit * (sel @ (goff - before).T), axis=1) + u
    slot_group = jnp.where(j_q < n_tt, j_q * loc_g + src_in_tile, 0).astype(jnp.int32)

    later_active = (expert_ids[None, :] > expert_ids[:, None]) & (groups[None, :] > 0)
    next_expert = jnp.min(jnp.where(later_active, expert_ids[None, :], N_EXPERTS), axis=1).astype(jnp.int32)
    ys = _moe(xloc, w_gate[0], w_up[0], w_down[0], tile_expert, slot_group, next_expert, n_used, tile)
    pos0, pos1 = pos[:, 0], pos[:, 1]
    y_prompt = _combine(ys, hp2, route_w, final_norm_w, pos0, pos1, tm, 0, tp // tm).reshape(bp, seq, d)
    y_sample = _combine(ys, hp2, route_w, final_norm_w, pos0, pos1, tm, tp // tm, ts // tm).reshape(bs, dec_seq, d)

    conv_p = proj_p.reshape(bp, seq, n_cols)[:, seq - (CONV_W - 1):, :conv_ch]
    u_s = proj_s.reshape(t_small // sl_s, sl_s, n_cols)[:bs, :dec_seq, :conv_ch]
    conv_s = jnp.concatenate([state_conv[0], u_s], axis=1)[:, dec_seq:]
    return (y_prompt, y_sample, sg_p[None], conv_p[None], sh_p[None], sg_s[None], conv_s[None], sh_s[None])
```

```python
import functools
import math

import jax
import jax.numpy as jnp
from jax import lax
from jax.experimental import pallas as pl
from jax.experimental.pallas import tpu as pltpu

F32 = jnp.float32
BF16 = jnp.bfloat16

EPS = 1e-6
N_META = 16
CONV_W = 4
N_HEADS = 8
D_HEAD = 128
N_GROUPS = 4
EXPERTS_PER_GROUP = 8
N_EXPERTS = N_GROUPS * EXPERTS_PER_GROUP

LANES = 128
SUBLANES = 8
CHUNK = 64
SUB = 16
VMEM_LIMIT = 56 * 1024 * 1024
NEG_BIG = -1e30
MIX_MODE = "bf16"
INV_MODE = "bf16"


def _sigmoid(x):
    return 0.5 * jnp.tanh(0.5 * x) + 0.5


def _silu(x):
    return x * _sigmoid(x)


def _softplus(x):
    return jnp.maximum(x, 0.0) + jnp.log1p(jnp.exp(-jnp.abs(x)))


def _split_bf16(a, pieces):
    out = []
    for _ in range(pieces - 1):
        hi = a.astype(BF16)
        out.append(hi)
        a = a - hi.astype(F32)
    out.append(a.astype(BF16))
    return out


def _mm(a, b, dims, mode):
    dg = functools.partial(lax.dot_general, dimension_numbers=(dims, ((), ())), preferred_element_type=F32)
    if mode == "bf16":
        return dg(a.astype(BF16), b.astype(BF16))
    assert mode == "bf16x3"
    ah, al = _split_bf16(a, 2)
    bh, bl = _split_bf16(b, 2)
    return dg(ah, bh) + dg(ah, bl) + dg(al, bh)


def _dot(a, b, mode="bf16"):
    return _mm(a, b, ((1,), (0,)), mode)


def _dot_nt(a, b, mode="bf16"):
    return _mm(a, b, ((1,), (1,)), mode)


def _dot_tn(a, b, mode="bf16"):
    return _mm(a, b, ((0,), (0,)), mode)


def _masked_cumsum(lmask, x):
    lm = lmask.astype(BF16)
    return sum(lax.dot_general(lm, p, (((1,), (0,)), ((), ())), preferred_element_type=F32)
               for p in _split_bf16(x, 3))


def _params(sem):
    return pltpu.CompilerParams(dimension_semantics=sem, vmem_limit_bytes=VMEM_LIMIT)


def _row_tile(n, target):
    best = max(c for c in range(16, min(n, target) + 1, 16) if n % c == 0)
    return best


def _rmsnorm_kernel(x_ref, w_ref, wn_ref, o_ref, on_ref):
    x = x_ref[...]
    ms = jnp.mean(x * x, axis=-1, keepdims=True)
    xn = (x * lax.rsqrt(ms + EPS) * w_ref[...]).astype(BF16)
    o_ref[...] = xn
    on_ref[...] = _dot(xn, wn_ref[...])


def _rmsnorm(x, w, w_narrow, tm):
    t, d = x.shape
    return pl.pallas_call(
        _rmsnorm_kernel,
        grid=(t // tm,),
        in_specs=[pl.BlockSpec((tm, d), lambda i: (i, 0)), pl.BlockSpec((1, d), lambda i: (0, 0)),
                  pl.BlockSpec(w_narrow.shape, lambda i: (0, 0))],
        out_specs=[pl.BlockSpec((tm, d), lambda i: (i, 0)), pl.BlockSpec((tm, LANES), lambda i: (i, 0))],
        out_shape=[jax.ShapeDtypeStruct((t, d), BF16), jax.ShapeDtypeStruct((t, LANES), F32)],
        compiler_params=_params(("parallel",)),
    )(x, w.reshape(1, d), w_narrow)


def _inproj_kernel(x_ref, wa_ref, wb_ref, o_ref, w_scr, *, n_a):
    j = pl.program_id(0)
    i = pl.program_id(1)

    @pl.when((i == 0) & (j < n_a))
    def _():
        w_scr[...] = wa_ref[...].astype(BF16)

    @pl.when((i == 0) & (j >= n_a))
    def _():
        w_scr[...] = wb_ref[...].astype(BF16)

    o_ref[...] = _dot(x_ref[...], w_scr[...])


def _inproj(x, w_a, n_a, w_b, tm, tn):
    t, k = x.shape
    n_b = w_b.shape[1] // tn
    return pl.pallas_call(
        functools.partial(_inproj_kernel, n_a=n_a),
        grid=(n_a + n_b, t // tm),
        in_specs=[pl.BlockSpec((tm, k), lambda j, i: (i, 0)),
                  pl.BlockSpec((k, tn), lambda j, i: (0, jnp.minimum(j, n_a - 1))),
                  pl.BlockSpec((k, tn), lambda j, i: (0, jnp.maximum(j - n_a, 0)))],
        out_specs=pl.BlockSpec((tm, tn), lambda j, i: (i, j)),
        out_shape=jax.ShapeDtypeStruct((t, (n_a + n_b) * tn), F32),
        scratch_shapes=[pltpu.VMEM((k, tn), BF16)],
        compiler_params=_params(("arbitrary", "arbitrary")),
    )(x, w_a, w_b)


def _chunk_masks(nb, sl):
    r = nb * sl
    shift = int(math.log2(sl))
    ri = lax.broadcasted_iota(jnp.int32, (r, r), 0)
    ci = lax.broadcasted_iota(jnp.int32, (r, r), 1)
    same = lax.shift_right_logical(ri, shift) == lax.shift_right_logical(ci, shift)
    return same & (ci <= ri), same & (ci < ri)


def _row_valid(nb, sl, n_valid):
    rowid = lax.broadcasted_iota(jnp.int32, (nb * sl, 1), 0)
    return (rowid & (sl - 1)) < n_valid


def _last_row_bcast(x, nb, sl):
    c = x.shape[-1]
    x3 = x.reshape(nb, sl, c)
    return jnp.broadcast_to(x3[:, sl - 1:sl, :], (nb, sl, c)).reshape(nb * sl, c)


def _gated_rmsnorm(o, w, gate):
    return o * lax.rsqrt(jnp.mean(o * o, axis=-1, keepdims=True) + EPS) * w * _silu(gate)


def _gdn_kernel(*refs, nb, sl, n_valid, has_init, mode, inv_mode, head_group):
    if has_init:
        (qkv_ref, z_ref, ba_ref, cw_ref, pv_ref, nw_ref, s0_ref, c0_ref,
         o_ref, sout_ref, s_scr, carry_scr) = refs
    else:
        (qkv_ref, z_ref, ba_ref, cw_ref, pv_ref, nw_ref,
         o_ref, sout_ref, s_scr, carry_scr) = refs
    c = pl.program_id(1)
    r = nb * sl
    dh = D_HEAD

    @pl.when(c == 0)
    def _():
        if has_init:
            s_scr[...] = s0_ref[...]
            carry_scr[...] = c0_ref[...]
        else:
            s_scr[...] = jnp.zeros_like(s_scr)
            carry_scr[...] = jnp.zeros_like(carry_scr)

    incl, strict = _chunk_masks(nb, sl)
    lmask = jnp.where(incl, 1.0, 0.0)
    offdiag = jnp.where(strict, 1.0, 0.0)
    valid = _row_valid(nb, sl, n_valid)
    masked = n_valid < sl
    rowid = lax.broadcasted_iota(jnp.int32, (r, 1), 0)
    row_seq = lax.shift_right_logical(rowid, int(math.log2(sl)))
    n_sq = int(math.log2(sl)) - 1

    ba = ba_ref[...].reshape(r, LANES)
    pv = pv_ref[...]
    beta_all = _sigmoid(ba)
    g_all = -jnp.exp(pv[0:1]) * _softplus(ba + pv[1:2])
    if masked:
        g_all = jnp.where(valid, g_all, 0.0)
    gcum = _masked_cumsum(lmask, g_all)
    gcum_t = gcum.T
    glast = _last_row_bcast(gcum, nb, sl)
    cw = cw_ref[...]
    nw = nw_ref[...]

    def conv_slice(c0):
        u = qkv_ref[:, :, c0:c0 + dh]
        prev = carry_scr[:, :, c0:c0 + dh]
        full = jnp.concatenate([prev, u], axis=1)
        acc = None
        for j in range(CONV_W):
            off = SUBLANES - (CONV_W - 1) + j
            term = full[:, off:off + sl, :] * cw[j:j + 1, c0:c0 + dh]
            acc = term if acc is None else acc + term
        return _silu(acc).reshape(r, dh)

    for h0 in range(0, N_HEADS, head_group):
        heads = range(h0, h0 + head_group)
        qs, ks, vs, bcs, gcs, gls, egs, decays = [], [], [], [], [], [], [], []
        for h in heads:
            q = conv_slice(h * dh)
            k = conv_slice(N_HEADS * dh + h * dh)
            v = conv_slice(2 * N_HEADS * dh + h * dh)
            q = q * lax.rsqrt(jnp.sum(q * q, axis=-1, keepdims=True) + EPS) * (dh ** -0.5)
            k = k * lax.rsqrt(jnp.sum(k * k, axis=-1, keepdims=True) + EPS)
            if masked:
                q = jnp.where(valid, q, 0.0)
                k = jnp.where(valid, k, 0.0)
                v = jnp.where(valid, v, 0.0)
            gc = gcum[:, N_HEADS + h:N_HEADS + h + 1]
            gr = gcum_t[N_HEADS + h:N_HEADS + h + 1, :]
            qs.append(q)
            ks.append(k)
            vs.append(v)
            bcs.append(beta_all[:, h:h + 1])
            gcs.append(gc)
            gls.append(glast[:, N_HEADS + h:N_HEADS + h + 1])
            egs.append(jnp.exp(gc))
            decays.append(jnp.exp(jnp.where(incl, gc - gr, NEG_BIG)))
        n = len(qs)
        qk_kk = [_dot_nt(jnp.concatenate([qs[i], ks[i]], axis=0), ks[i], mode) for i in range(n)]
        tm1 = [qk_kk[i][r:] * (decays[i] * offdiag) * (-bcs[i]) for i in range(n)]
        pw = list(tm1)
        for _ in range(n_sq):
            pw = [_dot(pw[i], pw[i], inv_mode) for i in range(n)]
            tm1 = [tm1[i] + pw[i] + _dot(tm1[i], pw[i], inv_mode) for i in range(n)]
        rhs = [jnp.concatenate([vs[i] * bcs[i], ks[i] * (bcs[i] * egs[i])], axis=1) for i in range(n)]
        uw = [rhs[i] + _dot(tm1[i], rhs[i], mode) for i in range(n)]
        vnew, ointer = [], []
        for i, h in enumerate(heads):
            u = uw[i][:, :dh]
            w = uw[i][:, dh:]
            qe = qs[i] * egs[i]
            vnew_parts, ointer_parts = [], []
            for b in range(nb):
                rows = slice(b * sl, (b + 1) * sl)
                ws = _dot(jnp.concatenate([w[rows], qe[rows]], axis=0), s_scr[b, h], mode)
                vnew_parts.append(u[rows] - ws[:sl])
                ointer_parts.append(ws[sl:])
            vnew.append(vnew_parts[0] if nb == 1 else jnp.concatenate(vnew_parts, axis=0))
            ointer.append(ointer_parts[0] if nb == 1 else jnp.concatenate(ointer_parts, axis=0))
        for i, h in enumerate(heads):
            attn = qk_kk[i][:r] * decays[i]
            o = ointer[i] + _dot(attn, vnew[i], mode)
            z = z_ref[:, :, h * dh:(h + 1) * dh].reshape(r, dh)
            o_ref[:, :, h * dh:(h + 1) * dh] = _gated_rmsnorm(o, nw, z).reshape(nb, sl, dh).astype(o_ref.dtype)
        for i, h in enumerate(heads):
            ktil = ks[i] * jnp.exp(gls[i] - gcs[i])
            for b in range(nb):
                kt_b = ktil if nb == 1 else jnp.where(row_seq == b, ktil, 0.0)
                gl_b = gls[i][b * sl:b * sl + 1, :]
                s_scr[b, h] = s_scr[b, h] * jnp.exp(gl_b) + _dot_tn(kt_b, vnew[i], mode)

    carry_scr[...] = qkv_ref[:, sl - SUBLANES:sl, :]

    @pl.when(c == pl.num_programs(1) - 1)
    def _():
        sout_ref[...] = s_scr[...]


def _hgrn_kernel(*refs, nb, sl, n_valid, has_init, mode, head_group):
    if has_init:
        (q_ref, f_ref, i_ref, g_ref, lb_ref, nw_ref, s0_ref, o_ref, sout_ref, s_scr) = refs
    else:
        (q_ref, f_ref, i_ref, g_ref, lb_ref, nw_ref, o_ref, sout_ref, s_scr) = refs
    c = pl.program_id(1)
    r = nb * sl
    dh = D_HEAD
    width = N_HEADS * dh

    @pl.when(c == 0)
    def _():
        if has_init:
            s_scr[...] = s0_ref[...]
        else:
            s_scr[...] = jnp.zeros_like(s_scr)

    incl, _ = _chunk_masks(nb, sl)
    lmask = jnp.where(incl, 1.0, 0.0)
    valid = _row_valid(nb, sl, n_valid)
    masked = n_valid < sl
    rowid = lax.broadcasted_iota(jnp.int32, (r, 1), 0)
    row_seq = lax.shift_right_logical(rowid, int(math.log2(sl)))
    sub = min(SUB, sl)
    nblk = r // sub
    sub_shift = int(math.log2(sub))
    ri = lax.broadcasted_iota(jnp.int32, (r, r), 0)
    ci = lax.broadcasted_iota(jnp.int32, (r, r), 1)
    same_blk = lax.shift_right_logical(ri, sub_shift) == lax.shift_right_logical(ci, sub_shift)
    diag_mask = incl & same_blk
    cross_mask = incl & jnp.logical_not(same_blk)

    lb = lb_ref[...]
    f = lb + (1.0 - lb) * _sigmoid(f_ref[...].reshape(r, width))
    lf = jnp.log(f)
    k_all = 1.0 - f
    if masked:
        lf = jnp.where(valid, lf, 0.0)
        k_all = jnp.where(valid, k_all, 0.0)
    bcum = _masked_cumsum(lmask, lf)
    nw = nw_ref[...]

    def head_inputs(h):
        cols = slice(h * dh, (h + 1) * dh)
        q = _silu(q_ref[:, :, cols].reshape(r, dh)) * (dh ** -0.5)
        v = i_ref[:, :, cols].reshape(r, dh)
        if masked:
            v = jnp.where(valid, v, 0.0)
        return q, k_all[:, cols], v, bcum[:, cols]

    def intra_attn(q, k, bh):
        bmid = jnp.broadcast_to(bh.reshape(nblk, sub, dh)[:, sub // 2:sub // 2 + 1, :],
                                (nblk, sub, dh)).reshape(r, dh)
        attn = jnp.where(diag_mask, _dot_nt(q * jnp.exp(bh - bmid), k * jnp.exp(bmid - bh), mode), 0.0)
        if sl > sub:
            parts = [jnp.zeros((sub, r), F32)]
            for blk in range(1, nblk):
                start = blk * sub
                bref = bh[start - 1:start, :]
                qc = q[start:start + sub] * jnp.exp(bh[start:start + sub] - bref)
                kc = k * jnp.exp(jnp.minimum(bref - bh, 0.0))
                parts.append(_dot_nt(qc, kc, mode))
            attn = attn + jnp.where(cross_mask, jnp.concatenate(parts, axis=0), 0.0)
        return attn

    for h0 in range(0, N_HEADS, head_group):
        heads = range(h0, h0 + head_group)
        ins = [head_inputs(h) for h in heads]
        attns = [intra_attn(q, k, bh) for (q, k, v, bh) in ins]
        for i, h in enumerate(heads):
            q, k, v, bh = ins[i]
            qe = q * jnp.exp(bh)
            ointer_parts = []
            for b in range(nb):
                rows = slice(b * sl, (b + 1) * sl)
                ointer_parts.append(_dot(qe[rows], s_scr[b, h], mode))
            ointer = ointer_parts[0] if nb == 1 else jnp.concatenate(ointer_parts, axis=0)
            o = ointer + _dot(attns[i], v, mode)
            cols = slice(h * dh, (h + 1) * dh)
            gate = g_ref[:, :, cols].reshape(r, dh)
            o_ref[:, :, cols] = _gated_rmsnorm(o, nw, gate).reshape(nb, sl, dh).astype(o_ref.dtype)
        for i, h in enumerate(heads):
            q, k, v, bh = ins[i]
            blast = _last_row_bcast(bh, nb, sl)
            ktil = k * jnp.exp(blast - bh)
            pad = [jnp.zeros((LANES - r, dh), F32)] if r < LANES else []
            tr = jnp.concatenate([blast] + pad, axis=0).T
            for b in range(nb):
                kt_b = ktil if nb == 1 else jnp.where(row_seq == b, ktil, 0.0)
                dec_col = jnp.exp(tr[:, b * sl:b * sl + 1])
                s_scr[b, h] = s_scr[b, h] * dec_col + _dot_tn(kt_b, v, mode)

    @pl.when(c == pl.num_programs(1) - 1)
    def _():
        sout_ref[...] = s_scr[...]


def _mixers(proj3, ba3, blk_off, nseq, nb, sl, n_valid, conv_w, pvec, gdn_nw, lb, hgrn_nw,
            s_gdn0, conv0, s_hgrn0, shared_init, out_dtype, mode, inv_mode):
    length = proj3.shape[1]
    has_init = s_gdn0 is not None
    width = N_HEADS * D_HEAD
    conv_ch = 3 * width
    grid = (nseq // nb, length // sl)
    state_spec = pl.BlockSpec((nb, N_HEADS, D_HEAD, D_HEAD), lambda g, c: (g, 0, 0, 0))
    state_shape = jax.ShapeDtypeStruct((nseq, N_HEADS, D_HEAD, D_HEAD), F32)
    init_idx = (lambda g: 0) if shared_init else (lambda g: g)
    init_state_spec = pl.BlockSpec((nb, N_HEADS, D_HEAD, D_HEAD), lambda g, c: (init_idx(g), 0, 0, 0))
    head_group = N_HEADS if nb == 1 else N_HEADS // 2

    def col_spec(w, idx):
        return pl.BlockSpec((nb, sl, w), lambda g, c: (g + blk_off, c, idx))

    def out_spec(w):
        return pl.BlockSpec((nb, sl, w), lambda g, c: (g, c, 0))

    def const_spec(shape):
        return pl.BlockSpec(shape, lambda g, c: (0,) * len(shape))

    gdn_in = [proj3, proj3, ba3, conv_w, pvec, gdn_nw]
    gdn_specs = [col_spec(conv_ch, 0), col_spec(width, 3), col_spec(LANES, 0),
                 const_spec(conv_w.shape), const_spec(pvec.shape), const_spec(gdn_nw.shape)]
    if has_init:
        gdn_in += [s_gdn0, conv0]
        gdn_specs += [init_state_spec, pl.BlockSpec((nb, SUBLANES, conv_ch), lambda g, c: (init_idx(g), 0, 0))]
    hgrn_in = [proj3, proj3, proj3, proj3, lb, hgrn_nw]
    hgrn_specs = [col_spec(width, 4), col_spec(width, 5), col_spec(width, 6), col_spec(width, 7),
                  const_spec(lb.shape), const_spec(hgrn_nw.shape)]
    if has_init:
        hgrn_in += [s_hgrn0]
        hgrn_specs += [init_state_spec]

    gdn_body = functools.partial(_gdn_kernel, nb=nb, sl=sl, n_valid=n_valid, has_init=has_init, mode=mode,
                                 inv_mode=inv_mode, head_group=head_group)
    hgrn_body = functools.partial(_hgrn_kernel, nb=nb, sl=sl, n_valid=n_valid, has_init=has_init, mode=mode,
                                  head_group=head_group)
    n_g, n_h = len(gdn_in), len(hgrn_in)

    def both(*refs):
        ins, outs, scr = refs[:n_g + n_h], refs[n_g + n_h:n_g + n_h + 4], refs[n_g + n_h + 4:]
        gdn_body(*ins[:n_g], outs[0], outs[1], scr[0], scr[1])
        hgrn_body(*ins[n_g:], outs[2], outs[3], scr[2])

    state_scr = pltpu.VMEM((nb, N_HEADS, D_HEAD, D_HEAD), F32)
    o_shape = jax.ShapeDtypeStruct((nseq, length, width), out_dtype)
    o_gdn, s_gdn, o_hgrn, s_hgrn = pl.pallas_call(
        both,
        grid=grid,
        in_specs=gdn_specs + hgrn_specs,
        out_specs=[out_spec(width), state_spec, out_spec(width), state_spec],
        out_shape=[o_shape, state_shape, o_shape, state_shape],
        scratch_shapes=[state_scr, pltpu.VMEM((nb, SUBLANES, conv_ch), F32), state_scr],
        compiler_params=_params(("parallel", "arbitrary")),
    )(*gdn_in, *hgrn_in)
    return o_gdn, o_hgrn, s_gdn, s_hgrn


def _outproj_router_kernel(oap_ref, obp_ref, hpp_ref, oas_ref, obs_ref, hps_ref, wo_ref, n2_ref, wr_ref, br_ref,
                           hp2_ref, xloc_ref, ri_ref, rw_ref, m8_ref, *, tm, n_p):
    i = pl.program_id(0)
    body = functools.partial(_outproj_router_tile, wo_ref=wo_ref, n2_ref=n2_ref, wr_ref=wr_ref, br_ref=br_ref,
                             hp2_ref=hp2_ref, xloc_ref=xloc_ref, ri_ref=ri_ref, rw_ref=rw_ref, m8_ref=m8_ref, tm=tm)

    @pl.when(i < n_p)
    def _():
        body(oap_ref[...], obp_ref[...], hpp_ref[...])

    @pl.when(i >= n_p)
    def _():
        body(oas_ref[...], obs_ref[...], hps_ref[...])


def _outproj_router_tile(oa, ob, hp, *, wo_ref, n2_ref, wr_ref, br_ref, hp2_ref, xloc_ref, ri_ref, rw_ref,
                         m8_ref, tm):
    half = oa.shape[-1]
    mix = _dot(oa.astype(BF16), wo_ref[:half, :]) + _dot(ob.astype(BF16), wo_ref[half:, :])
    hp2 = hp + mix
    hp2_ref[...] = hp2
    xn2 = hp2 * lax.rsqrt(jnp.mean(hp2 * hp2, axis=-1, keepdims=True) + EPS) * n2_ref[...]
    logits = _dot(xn2, wr_ref[...], "bf16x3") + br_ref[...]

    lane = lax.broadcasted_iota(jnp.int32, (tm, LANES), 1)
    lane_f = lane.astype(F32)
    far = float(4 * LANES)
    is_g = (lane >= N_EXPERTS) & (lane < N_EXPERTS + N_GROUPS)
    lg = jnp.where(is_g, logits, -jnp.inf)
    gmax = jnp.max(lg, axis=-1, keepdims=True)
    gsel = jnp.min(jnp.where(lg == gmax, lane_f, far), axis=-1, keepdims=True).astype(jnp.int32) - N_EXPERTS
    p_top = 1.0 / jnp.sum(jnp.where(is_g, jnp.exp(logits - gmax), 0.0), axis=-1, keepdims=True)
    in_grp = (lane < N_EXPERTS) & (lax.shift_right_logical(lane, 3) == gsel)
    le = jnp.where(in_grp, logits, -jnp.inf)
    m1 = jnp.max(le, axis=-1, keepdims=True)
    i1 = jnp.min(jnp.where(le == m1, lane_f, far), axis=-1, keepdims=True).astype(jnp.int32)
    le2 = jnp.where(lane == i1, -jnp.inf, le)
    m2 = jnp.max(le2, axis=-1, keepdims=True)
    i2 = jnp.min(jnp.where(le2 == m2, lane_f, far), axis=-1, keepdims=True).astype(jnp.int32)
    e2 = jnp.exp(m2 - m1)
    w1 = p_top / (1.0 + e2)
    w2 = p_top * e2 / (1.0 + e2)

    onehot = (lane == i1) | (lane == i2)
    onehot_f = jnp.where(onehot, 1.0, 0.0)
    tri = (lax.broadcasted_iota(jnp.int32, (tm, tm), 1) < lax.broadcasted_iota(jnp.int32, (tm, tm), 0))
    rank = _dot(jnp.where(tri, 1.0, 0.0).astype(BF16), onehot_f.astype(BF16))
    cnt = jnp.sum(onehot_f, axis=0, keepdims=True)
    m8 = jnp.floor((cnt + (SUBLANES - 1)) * (1.0 / SUBLANES))
    upper = (lax.broadcasted_iota(jnp.int32, (LANES, LANES), 0) < lax.broadcasted_iota(jnp.int32, (LANES, LANES), 1))
    goff = _dot(jnp.broadcast_to(m8, (SUBLANES, LANES)).astype(BF16), jnp.where(upper, 1.0, 0.0).astype(BF16))[0:1]
    local = goff * SUBLANES + rank
    lr1 = jnp.sum(jnp.where(lane == i1, rank, 0.0), axis=-1, keepdims=True)
    lr2 = jnp.sum(jnp.where(lane == i2, rank, 0.0), axis=-1, keepdims=True)
    lp1 = jnp.sum(jnp.where(lane == i1, local, 0.0), axis=-1, keepdims=True)
    lp2 = jnp.sum(jnp.where(lane == i2, local, 0.0), axis=-1, keepdims=True)
    lp_rows = jnp.where(lane == 0, lp1, jnp.where(lane == 1, lp2, -1.0)).T
    n_loc = xloc_ref.shape[0]
    row = lax.broadcasted_iota(jnp.int32, (n_loc, tm), 0).astype(F32)
    perm = (row == lp_rows[0:1, :]) | (row == lp_rows[1:2, :])
    xloc_ref[...] = _dot(jnp.where(perm, 1.0, 0.0).astype(BF16), xn2.astype(BF16))

    ri_ref[...] = jnp.where(lane == 0, i1, jnp.where(lane == 1, i2, jnp.where(
        lane == 2, lr1.astype(jnp.int32), jnp.where(lane == 3, lr2.astype(jnp.int32), 0))))
    rw_ref[...] = jnp.where(lane == 0, w1, jnp.where(lane == 1, w2, 0.0))
    m8_ref[...] = jnp.broadcast_to(m8, m8_ref.shape)


def _outproj_router(oa_p, ob_p, hp_p, oa_s, ob_s, hp_s, w_out, norm2_w, w_r, b_r, tm):
    (tp, d), ts = hp_p.shape, hp_s.shape[0]
    half = oa_p.shape[1]
    n_p, n_s = tp // tm, ts // tm
    t = tp + ts
    n_loc = _local_rows(tm)
    prow = lambda w: pl.BlockSpec((tm, w), lambda i: (jnp.minimum(i, n_p - 1), 0))
    srow = lambda w: pl.BlockSpec((tm, w), lambda i: (jnp.maximum(i - n_p, 0), 0))
    row = lambda w: pl.BlockSpec((tm, w), lambda i: (i, 0))
    const = lambda shape: pl.BlockSpec(shape, lambda i: (0,) * len(shape))
    return pl.pallas_call(
        functools.partial(_outproj_router_kernel, tm=tm, n_p=n_p),
        grid=(n_p + n_s,),
        in_specs=[prow(half), prow(half), prow(d), srow(half), srow(half), srow(d),
                  const(w_out.shape), const((1, d)), const(w_r.shape), const((1, LANES))],
        out_specs=[row(d), pl.BlockSpec((n_loc, d), lambda i: (i, 0)), row(LANES), row(LANES),
                   pl.BlockSpec((None, SUBLANES, LANES), lambda i: (i, 0, 0))],
        out_shape=[jax.ShapeDtypeStruct((t, d), F32), jax.ShapeDtypeStruct(((n_p + n_s) * n_loc, d), F32),
                   jax.ShapeDtypeStruct((t, LANES), jnp.int32), jax.ShapeDtypeStruct((t, LANES), F32),
                   jax.ShapeDtypeStruct((n_p + n_s, SUBLANES, LANES), F32)],
        compiler_params=_params(("parallel",)),
    )(oa_p, ob_p, hp_p, oa_s, ob_s, hp_s, w_out, norm2_w.reshape(1, d), w_r, b_r)


def _local_rows(tm):
    return 2 * tm + SUBLANES * N_EXPERTS


def _start_row_gather(idx_ref, base, n, src_hbm, dst, sem):
    def issue(r, carry):
        pltpu.make_async_copy(src_hbm.at[pl.ds(idx_ref[base + r], 1), :], dst.at[pl.ds(r, 1), :], sem).start()
        return carry
    lax.fori_loop(0, n, issue, 0, unroll=8)


def _start_group_gather(idx_ref, base, n_groups, src_hbm, dst, sem):
    def issue(r, carry):
        src_row = pl.multiple_of(idx_ref[base + r] * SUBLANES, SUBLANES)
        dst_row = pl.multiple_of(r * SUBLANES, SUBLANES)
        pltpu.make_async_copy(src_hbm.at[pl.ds(src_row, SUBLANES), :], dst.at[pl.ds(dst_row, SUBLANES), :], sem).start()
        return carry
    lax.fori_loop(0, n_groups, issue, 0, unroll=8)


def _wait_row_gather(n, src_hbm, dst, sem):
    pltpu.make_async_copy(src_hbm.at[pl.ds(0, n), :], dst, sem).wait()


def _moe_kernel(te_ref, st_ref, nx_ref, nu_ref, x_hbm, wg_hbm, wu_hbm, wd_hbm, o_ref,
                xbuf, wg_f32, wu_f32, wd_f32, wg_scr, wu_scr, wd_scr, sems, wsems, *, tile):
    i = pl.program_id(0)
    n_used = nu_ref[0]
    slot = lax.rem(i, 2)
    weights = ((wg_hbm, wg_f32, wg_scr), (wu_hbm, wu_f32, wu_scr), (wd_hbm, wd_f32, wd_scr))

    def start_weights(e):
        for k, (w_hbm, w_f32, _) in enumerate(weights):
            pltpu.async_copy(w_hbm.at[e], w_f32, wsems.at[k], priority=1)

    @pl.when((i == 0) & (n_used > 0))
    def _():
        start_weights(te_ref[0])
        _start_group_gather(st_ref, 0, tile // SUBLANES, x_hbm, xbuf.at[0], sems.at[0])

    @pl.when(i < n_used)
    def _():
        @pl.when(i + 1 < n_used)
        def _():
            _start_group_gather(st_ref, (i + 1) * (tile // SUBLANES), tile // SUBLANES, x_hbm,
                                xbuf.at[1 - slot], sems.at[1 - slot])

        e = te_ref[i]

        @pl.when((i == 0) | (e != te_ref[jnp.maximum(i - 1, 0)]))
        def _():
            for k, (w_hbm, w_f32, w_scr) in enumerate(weights):
                pltpu.make_async_copy(w_hbm.at[0], w_f32, wsems.at[k]).wait()
                w_scr[...] = w_f32[...].astype(BF16)
            nxt = nx_ref[e]

            @pl.when(nxt < N_EXPERTS)
            def _():
                start_weights(nxt)

        _wait_row_gather(tile, x_hbm, xbuf.at[slot], sems.at[slot])
        x = xbuf[slot].astype(BF16)
        g = _dot(x, wg_scr[...])
        u = _dot(x, wu_scr[...])
        o_ref[...] = _dot((_silu(g) * u).astype(BF16), wd_scr[...])

    @pl.when(i >= n_used)
    def _():
        o_ref[...] = jnp.zeros_like(o_ref)


def _moe(xn2, w_gate, w_up, w_down, tile_expert, slot_token, next_expert, n_used, tile):
    d = xn2.shape[1]
    n_tiles = tile_expert.shape[0]
    de = w_gate.shape[2]
    hbm = pl.BlockSpec(memory_space=pl.ANY)
    grid_spec = pltpu.PrefetchScalarGridSpec(
        num_scalar_prefetch=4,
        grid=(n_tiles,),
        in_specs=[hbm, hbm, hbm, hbm],
        out_specs=pl.BlockSpec((tile, d), lambda i, te, st, nx, nu: (i, 0)),
        scratch_shapes=[pltpu.VMEM((2, tile, d), F32),
                        pltpu.VMEM((d, de), F32), pltpu.VMEM((d, de), F32), pltpu.VMEM((de, d), F32),
                        pltpu.VMEM((d, de), BF16), pltpu.VMEM((d, de), BF16), pltpu.VMEM((de, d), BF16),
                        pltpu.SemaphoreType.DMA((2,)), pltpu.SemaphoreType.DMA((3,))],
    )
    return pl.pallas_call(
        functools.partial(_moe_kernel, tile=tile),
        grid_spec=grid_spec,
        out_shape=jax.ShapeDtypeStruct((n_tiles * tile, d), F32),
        compiler_params=_params(("arbitrary",)),
    )(tile_expert, slot_token, next_expert, n_used, xn2, w_gate, w_up, w_down)


def _combine_kernel(p0_ref, p1_ref, ys_hbm, hp2_ref, rw_ref, fw_ref, o_ref, buf0, buf1, sems, *, tm, tile_off):
    i = pl.program_id(0)
    slot = lax.rem(i, 2)

    def start(step, s):
        base = (step + tile_off) * tm
        _start_row_gather(p0_ref, base, tm, ys_hbm, buf0.at[s], sems.at[0, s])
        _start_row_gather(p1_ref, base, tm, ys_hbm, buf1.at[s], sems.at[1, s])

    @pl.when(i == 0)
    def _():
        start(0, 0)

    @pl.when(i + 1 < pl.num_programs(0))
    def _():
        start(i + 1, 1 - slot)

    _wait_row_gather(tm, ys_hbm, buf0.at[slot], sems.at[0, slot])
    _wait_row_gather(tm, ys_hbm, buf1.at[slot], sems.at[1, slot])
    rw = rw_ref[...]
    y = hp2_ref[...] + rw[:, 0:1] * buf0[slot] + rw[:, 1:2] * buf1[slot]
    o_ref[...] = y * lax.rsqrt(jnp.mean(y * y, axis=-1, keepdims=True) + EPS) * fw_ref[...]


def _combine(ys, hp2, route_w, final_w, pos0, pos1, tm, tile_off, n_tiles):
    _, d = hp2.shape
    grid_spec = pltpu.PrefetchScalarGridSpec(
        num_scalar_prefetch=2,
        grid=(n_tiles,),
        in_specs=[pl.BlockSpec(memory_space=pl.ANY),
                  pl.BlockSpec((tm, d), lambda i, p0, p1: (i + tile_off, 0)),
                  pl.BlockSpec((tm, LANES), lambda i, p0, p1: (i + tile_off, 0)),
                  pl.BlockSpec((1, d), lambda i, p0, p1: (0, 0))],
        out_specs=pl.BlockSpec((tm, d), lambda i, p0, p1: (i, 0)),
        scratch_shapes=[pltpu.VMEM((2, tm, d), F32), pltpu.VMEM((2, tm, d), F32),
                        pltpu.SemaphoreType.DMA((2, 2))],
    )
    return pl.pallas_call(
        functools.partial(_combine_kernel, tm=tm, tile_off=tile_off),
        grid_spec=grid_spec,
        out_shape=jax.ShapeDtypeStruct((n_tiles * tm, d), F32),
        compiler_params=_params(("arbitrary",)),
    )(pos0, pos1, ys, hp2, route_w, final_w.reshape(1, d))


def kernel(x_prompt, x_sample, state_gdn, state_conv, state_hgrn, meta_tokens, norm1_w, w_in, conv_w, a_log,
           dt_bias, gdn_norm_w, lb_logits, hgrn_norm_w, w_out, norm2_w, w_router_group, b_router_group,
           w_router_expert, b_router_expert, w_gate, w_up, w_down, final_norm_w):
    bp, seq, d = x_prompt.shape
    bs, dec_seq, _ = x_sample.shape
    assert w_in.shape[0] == 1, "single-layer trunk"
    width = N_HEADS * D_HEAD
    conv_ch = 3 * width
    tile = 256
    tn = 512
    sl_s = SUBLANES
    nb_s = CHUNK // sl_s
    assert seq % CHUNK == 0 and N_META <= CHUNK and dec_seq <= sl_s and bs % nb_s == 0
    tp, ts, ts_pad = bp * seq, bs * dec_seq, bs * sl_s
    t_small = ts_pad + CHUNK
    tm = _row_tile(math.gcd(tp, ts), 256)
    tm_p = _row_tile(tp, 2048)

    xp = x_prompt.reshape(tp, d)
    x_small = jnp.concatenate([jnp.pad(x_sample, ((0, 0), (0, sl_s - dec_seq), (0, 0))).reshape(ts_pad, d),
                               jnp.zeros((CHUNK - N_META, d), F32), meta_tokens.astype(F32)], axis=0)

    wi = w_in[0]
    n_a = 4 * width // tn
    w_b = wi[:, 4 * width + 2 * N_HEADS:]
    w_ba = jnp.pad(wi[:, 4 * width:4 * width + 2 * N_HEADS], ((0, 0), (0, LANES - 2 * N_HEADS))).astype(BF16)
    n_cols = 8 * width

    xn_p, ba_p = _rmsnorm(xp, norm1_w[0], w_ba, _row_tile(tp, 512))
    xn_s, ba_s = _rmsnorm(x_small, norm1_w[0], w_ba, _row_tile(t_small, 1024))
    w_a = wi[:, :4 * width]
    proj_p = _inproj(xn_p, w_a, n_a, w_b, tm_p, tn)
    proj_s = _inproj(xn_s, w_a, n_a, w_b, t_small, tn)

    pvec = jnp.zeros((2, LANES), F32)
    pvec = pvec.at[0, N_HEADS:2 * N_HEADS].set(a_log[0]).at[1, N_HEADS:2 * N_HEADS].set(dt_bias[0])
    lb = jnp.cumsum(jax.nn.softmax(lb_logits.astype(F32), axis=0), axis=0)[0].reshape(1, width)
    gdn_nw = gdn_norm_w[0].reshape(1, D_HEAD)
    hgrn_nw = hgrn_norm_w[0].reshape(1, D_HEAD)
    cw = conv_w[0]
    mix_args = (cw, pvec, gdn_nw, lb, hgrn_nw)

    _, _, sg_m, sh_m = _mixers(proj_s.reshape(t_small // CHUNK, CHUNK, n_cols),
                               ba_s.reshape(t_small // CHUNK, CHUNK, LANES), ts_pad // CHUNK, 1, 1, CHUNK, CHUNK,
                               *mix_args, None, None, None, False, F32, MIX_MODE, INV_MODE)
    conv_m = proj_s[t_small - SUBLANES:, :conv_ch].reshape(1, SUBLANES, conv_ch)
    oa_p, ob_p, sg_p, sh_p = _mixers(proj_p.reshape(bp, seq, n_cols), ba_p.reshape(bp, seq, LANES), 0, bp, 1,
                                     CHUNK, CHUNK, *mix_args, sg_m, conv_m, sh_m, True, BF16, MIX_MODE, INV_MODE)
    conv0 = jnp.pad(state_conv[0], ((0, 0), (SUBLANES - (CONV_W - 1), 0), (0, 0)))
    oa_s, ob_s, sg_s, sh_s = _mixers(proj_s.reshape(t_small // sl_s, sl_s, n_cols),
                                     ba_s.reshape(t_small // sl_s, sl_s, LANES), 0, bs, nb_s, sl_s, dec_seq,
                                     *mix_args, state_gdn[0], conv0, state_hgrn[0], False, F32, MIX_MODE, INV_MODE)

    w_r = jnp.concatenate([w_router_expert[0], w_router_group[0],
                           jnp.zeros((d, LANES - N_EXPERTS - N_GROUPS), F32)], axis=1)
    b_r = jnp.concatenate([b_router_expert[0], b_router_group[0],
                           jnp.zeros((LANES - N_EXPERTS - N_GROUPS,), F32)]).reshape(1, LANES)
    t = tp + ts
    hp2, xloc, route_i, route_w, tile_m8 = _outproj_router(
        oa_p.reshape(tp, width), ob_p.reshape(tp, width), xp,
        oa_s[:, :dec_seq].reshape(ts, width), ob_s[:, :dec_seq].reshape(ts, width), x_sample.reshape(ts, d),
        w_out[0].astype(BF16), norm2_w[0], w_r, b_r, tm)

    n_tt = t // tm
    gpt = tile // SUBLANES
    loc_g = _local_rows(tm) // SUBLANES
    expert_ids = jnp.arange(N_EXPERTS, dtype=jnp.int32)
    m8 = tile_m8[:, 0, :N_EXPERTS].astype(jnp.int32)
    before = jnp.cumsum(m8, axis=0) - m8
    goff = jnp.cumsum(m8, axis=1) - m8
    groups = jnp.sum(m8, axis=0)
    padded_g = (groups + gpt - 1) // gpt * gpt
    ends_g = jnp.cumsum(padded_g)
    offs_g = ends_g - padded_g
    eid = route_i[:, 0:2]
    base_rows = SUBLANES * (offs_g[None, :] + before)
    base_tok = jnp.repeat(base_rows, tm, axis=0)
    pos = jnp.sum(jnp.where(eid[:, :, None] == expert_ids, base_tok[:, None, :], 0), axis=-1) + route_i[:, 2:4]

    n_tiles = -(-(2 * t + n_tt * N_EXPERTS * (SUBLANES - 1)) // tile) + N_EXPERTS
    tile_expert = jnp.minimum(
        jnp.sum((ends_g[None, :] <= (jnp.arange(n_tiles, dtype=jnp.int32) * gpt)[:, None]).astype(jnp.int32), axis=1),
        N_EXPERTS - 1)
    n_used = (ends_g[-1] // gpt).astype(jnp.int32).reshape(1)
    q = jnp.arange(n_tiles * gpt, dtype=jnp.int32)
    e_q = jnp.repeat(tile_expert, gpt)
    sel = (e_q[:, None] == expert_ids).astype(jnp.int32)
    u = q - sel @ offs_g
    run_end = sel @ (before + m8).T
    j_q = jnp.sum((run_end <= u[:, None]).astype(jnp.int32), axis=1)
    hit = (jnp.arange(n_tt, dtype=jnp.int32) == j_q[:, None]).astype(jnp.int32)
    src_in_tile = jnp.sum(hit * (sel @ (goff - before).T), axis=1) + u
    slot_group = jnp.where(j_q < n_tt, j_q * loc_g + src_in_tile, 0).astype(jnp.int32)

    later_active = (expert_ids[None, :] > expert_ids[:, None]) & (groups[None, :] > 0)
    next_expert = jnp.min(jnp.where(later_active, expert_ids[None, :], N_EXPERTS), axis=1).astype(jnp.int32)
    ys = _moe(xloc, w_gate[0], w_up[0], w_down[0], tile_expert, slot_group, next_expert, n_used, tile)
    pos0, pos1 = pos[:, 0], pos[:, 1]
    y_prompt = _combine(ys, hp2, route_w, final_norm_w, pos0, pos1, tm, 0, tp // tm).reshape(bp, seq, d)
    y_sample = _combine(ys, hp2, route_w, final_norm_w, pos0, pos1, tm, tp // tm, ts // tm).reshape(bs, dec_seq, d)

    conv_p = proj_p.reshape(bp, seq, n_cols)[:, seq - (CONV_W - 1):, :conv_ch]
    u_s = proj_s.reshape(t_small // sl_s, sl_s, n_cols)[:bs, :dec_seq, :conv_ch]
    conv_s = jnp.concatenate([state_conv[0], u_s], axis=1)[:, dec_seq:]
    return (y_prompt, y_sample, sg_p[None], conv_p[None], sh_p[None], sg_s[None], conv_s[None], sh_s[None])
```
